```python
import math
import jax
import jax.numpy as jnp
from jax import lax
import numpy as np

D_MODEL = 1024
BATCH = 2
SEQ = 16384
DEPTH = 4

DA_HEAD_DIM = 128
DA_HEADS = D_MODEL // (2 * DA_HEAD_DIM)
SW_HEAD_DIM = 64
SW_HEADS = D_MODEL // SW_HEAD_DIM
SW_KV_HEADS = SW_HEADS // 8
WINDOW = 128
Q_BLOCK = 128
N_EXPERTS = 32
TOP_K = 4
D_FF = D_MODEL
SWIGLU_LIMIT = 7.0
SWIGLU_ALPHA = 1.702
MOE_BLOCK = 256
LN_EPS = 1e-5
DEEPNORM_ALPHA = (2.0 * DEPTH) ** 0.25
DEEPNORM_BETA = (8.0 * DEPTH) ** -0.25
NEG_INF = -1e30
DA_QK_COLS = DA_HEADS * 2 * DA_HEAD_DIM
DA_V_COLS = DA_HEADS * 2 * DA_HEAD_DIM
SW_Q_COLS = SW_HEADS * SW_HEAD_DIM
SW_KV_COLS = SW_KV_HEADS * SW_HEAD_DIM
D_IN = 2 * DA_QK_COLS + DA_V_COLS + SW_Q_COLS + 2 * SW_KV_COLS + 2 * D_MODEL

kernel_name = 'hybrid_diffattn_swa_moe_deepnorm'


def _alibi_slopes(n):
    return 2.0 ** (-8.0 * jnp.arange(1, n + 1, dtype=jnp.float32) / n)


def _layer_norm(x, g, b):
    xf = x.astype(jnp.float32)
    mu = jnp.mean(xf, axis=-1, keepdims=True)
    var = jnp.mean(jnp.square(xf - mu), axis=-1, keepdims=True)
    y = (xf - mu) * lax.rsqrt(var + LN_EPS) * g.astype(jnp.float32) + b.astype(jnp.float32)
    return y.astype(x.dtype)


def _diff_attention(q, k, v, lam, lam_init, subln_g):
    B, S = q.shape[0], q.shape[1]
    n_qb = S // Q_BLOCK
    slopes = _alibi_slopes(DA_HEADS)
    scale = DA_HEAD_DIM ** -0.5
    q_blocks = q.reshape(B, n_qb, Q_BLOCK, DA_HEADS, 2, DA_HEAD_DIM).transpose(1, 0, 2, 3, 4, 5)
    kpos = jnp.arange(S, dtype=jnp.int32)

    def one_block(args):
        qb, i = args
        s = jnp.einsum('bqhcd,bkhcd->bhcqk', qb, k, preferred_element_type=jnp.float32) * scale
        qpos = i * Q_BLOCK + jnp.arange(Q_BLOCK, dtype=jnp.int32)
        dist = qpos[:, None] - kpos[None, :]
        s = s - slopes[:, None, None, None] * dist.astype(jnp.float32)
        s = jnp.where(dist >= 0, s, NEG_INF)
        p = jax.nn.softmax(s, axis=-1)
        a = p[:, :, 0] - lam * p[:, :, 1]
        return jnp.einsum('bhqk,bkhe->bqhe', a.astype(v.dtype), v)

    o = lax.map(one_block, (q_blocks, jnp.arange(n_qb, dtype=jnp.int32)))
    o = o.transpose(1, 0, 2, 3, 4).reshape(B, S, DA_HEADS, 2 * DA_HEAD_DIM)
    of = o.astype(jnp.float32)
    of = of * lax.rsqrt(jnp.mean(jnp.square(of), axis=-1, keepdims=True) + LN_EPS)
    of = of * subln_g.astype(jnp.float32) * (1.0 - lam_init)
    return of.reshape(B, S, DA_HEADS * 2 * DA_HEAD_DIM).astype(v.dtype)


def _band(t, n_blocks):
    B = t.shape[0]
    tb = t.reshape(B, n_blocks, WINDOW, SW_KV_HEADS, SW_HEAD_DIM)
    prev = jnp.concatenate([jnp.zeros_like(tb[:, :1]), tb[:, :-1]], axis=1)
    return jnp.concatenate([prev, tb], axis=2)


def _sliding_window_attention(q, k, v, sinks):
    B, S = q.shape[0], q.shape[1]
    G = SW_HEADS // SW_KV_HEADS
    nb = S // WINDOW
    qb = q.reshape(B, nb, WINDOW, SW_KV_HEADS, G, SW_HEAD_DIM)
    kk = _band(k, nb)
    vv = _band(v, nb)
    s = jnp.einsum('bnqkgd,bnjkd->bnkgqj', qb, kk, preferred_element_type=jnp.float32) * (SW_HEAD_DIM ** -0.5)
    qi = jnp.arange(WINDOW, dtype=jnp.int32)
    kj = jnp.arange(2 * WINDOW, dtype=jnp.int32)
    dist = qi[:, None] + WINDOW - kj[None, :]
    key_global = jnp.arange(nb, dtype=jnp.int32)[:, None] * WINDOW + kj[None, :] - WINDOW
    valid = ((dist >= 0) & (dist < WINDOW))[None] & (key_global >= 0)[:, None, :]
    slopes = _alibi_slopes(SW_HEADS).reshape(SW_KV_HEADS, G)
    s = s - slopes[:, :, None, None] * dist.astype(jnp.float32)
    s = jnp.where(valid[:, None, None], s, NEG_INF)
    sink = jnp.broadcast_to(sinks.astype(jnp.float32).reshape(SW_KV_HEADS, G)[:, :, None, None], s.shape[:-1] + (1,))
    p = jax.nn.softmax(jnp.concatenate([s, sink], axis=-1), axis=-1)[..., :-1]
    o = jnp.einsum('bnkgqj,bnjkd->bnqkgd', p.astype(v.dtype), vv)
    return o.reshape(B, S, SW_HEADS * SW_HEAD_DIM)


def _hybrid_mixer(x, w_in, w_o, lambda_qk, subln_g, sinks, layer):
    B, S, _ = x.shape
    proj = x @ w_in
    offs = np.cumsum([DA_QK_COLS, DA_QK_COLS, DA_V_COLS, SW_Q_COLS, SW_KV_COLS, SW_KV_COLS, D_MODEL]).tolist()
    qa, ka, va, qb, kb, vb, ga, gb = jnp.split(proj, offs, axis=-1)
    qa = qa.reshape(B, S, DA_HEADS, 2, DA_HEAD_DIM)
    ka = ka.reshape(B, S, DA_HEADS, 2, DA_HEAD_DIM)
    va = va.reshape(B, S, DA_HEADS, 2 * DA_HEAD_DIM)
    lam_init = 0.8 - 0.6 * math.exp(-0.3 * layer)
    lq = lambda_qk.astype(jnp.float32)
    lam = jnp.exp(jnp.sum(lq[0] * lq[1])) - jnp.exp(jnp.sum(lq[2] * lq[3])) + lam_init
    o_a = _diff_attention(qa, ka, va, lam, lam_init, subln_g)
    o_b = _sliding_window_attention(
        qb.reshape(B, S, SW_HEADS, SW_HEAD_DIM),
        kb.reshape(B, S, SW_KV_HEADS, SW_HEAD_DIM),
        vb.reshape(B, S, SW_KV_HEADS, SW_HEAD_DIM),
        sinks)
    merged = jax.nn.sigmoid(ga) * o_a + jax.nn.sigmoid(gb) * o_b
    return merged @ w_o


def _moe(x, w_router, b_router, w_up, b_up, w_down, b_down):
    B, S, D = x.shape
    T = B * S
    TK = T * TOP_K
    xt = x.reshape(T, D)
    logits = jnp.dot(xt, w_router, preferred_element_type=jnp.float32) + b_router.astype(jnp.float32)
    top_val, top_idx = lax.top_k(logits, TOP_K)
    gates = jax.nn.softmax(top_val, axis=-1)
    flat_e = top_idx.reshape(TK).astype(jnp.int32)
    flat_tok = jnp.arange(TK, dtype=jnp.int32) // TOP_K
    flat_gate = gates.reshape(TK)
    order = jnp.argsort(flat_e)
    e_s = flat_e[order]
    tok_s = flat_tok[order]
    gate_s = flat_gate[order]
    counts = jnp.bincount(flat_e, length=N_EXPERTS).astype(jnp.int32)
    padded = (counts + MOE_BLOCK - 1) // MOE_BLOCK * MOE_BLOCK
    pad_end = jnp.cumsum(padded)
    pad_start = pad_end - padded
    start = jnp.cumsum(counts) - counts
    dest = pad_start[e_s] + jnp.arange(TK, dtype=jnp.int32) - start[e_s]
    n_blocks = (TK + N_EXPERTS * (MOE_BLOCK - 1) + MOE_BLOCK - 1) // MOE_BLOCK
    n_rows = n_blocks * MOE_BLOCK
    buf_tok = jnp.zeros((n_rows,), jnp.int32).at[dest].set(tok_s)
    buf_gate = jnp.zeros((n_rows,), jnp.float32).at[dest].set(gate_s)
    block_expert = jnp.minimum(
        jnp.searchsorted(pad_end, jnp.arange(n_blocks, dtype=jnp.int32) * MOE_BLOCK, side='right'),
        N_EXPERTS - 1).astype(jnp.int32)
    xb = xt[buf_tok].reshape(n_blocks, MOE_BLOCK, D)

    def expert_block(args):
        xblk, e = args
        h = xblk @ w_up[e] + b_up[e]
        gate = jnp.minimum(h[..., ::2], SWIGLU_LIMIT)
        up = jnp.clip(h[..., 1::2], -SWIGLU_LIMIT, SWIGLU_LIMIT)
        act = (up + 1.0) * (gate * jax.nn.sigmoid(SWIGLU_ALPHA * gate))
        return act @ w_down[e] + b_down[e]

    yb = lax.map(expert_block, (xb, block_expert)).reshape(n_rows, D)
    y = jnp.zeros((T, D), x.dtype).at[buf_tok].add(yb * buf_gate[:, None].astype(x.dtype))
    return y.reshape(B, S, D)


def setup_inputs(seed: int = 0) -> dict:
    key = jax.random.key(seed)
    ks = jax.random.split(key, 14)
    f32 = jnp.float32
    x = jax.random.normal(ks[0], (BATCH, SEQ, D_MODEL), f32)
    col_scale = jnp.concatenate([
        jnp.ones((2 * DA_QK_COLS,), f32),
        jnp.full((DA_V_COLS,), DEEPNORM_BETA, f32),
        jnp.ones((SW_Q_COLS + SW_KV_COLS,), f32),
        jnp.full((SW_KV_COLS,), DEEPNORM_BETA, f32),
        jnp.ones((2 * D_MODEL,), f32)])
    w_in = jax.random.normal(ks[1], (DEPTH, D_MODEL, D_IN), f32) * (D_MODEL ** -0.5) * col_scale
    w_o = jax.random.normal(ks[2], (DEPTH, D_MODEL, D_MODEL), f32) * (D_MODEL ** -0.5) * DEEPNORM_BETA
    lambda_qk = jax.random.normal(ks[3], (DEPTH, 4, DA_HEAD_DIM), f32) * 0.1
    subln_g = 1.0 + 0.02 * jax.random.normal(ks[4], (DEPTH, 2 * DA_HEAD_DIM), f32)
    sinks = 0.5 * jax.random.normal(ks[5], (DEPTH, SW_HEADS), f32)
    ln_g = 1.0 + 0.02 * jax.random.normal(ks[6], (DEPTH, 2, D_MODEL), f32)
    ln_b = 0.02 * jax.random.normal(ks[7], (DEPTH, 2, D_MODEL), f32)
    w_router = jax.random.normal(ks[8], (DEPTH, D_MODEL, N_EXPERTS), f32) * (D_MODEL ** -0.5)
    b_router = 0.01 * jax.random.normal(ks[9], (DEPTH, N_EXPERTS), f32)
    w_up = jax.random.normal(ks[10], (DEPTH, N_EXPERTS, D_MODEL, 2 * D_FF), f32) * (D_MODEL ** -0.5) * DEEPNORM_BETA
    b_up = 0.01 * jax.random.normal(ks[11], (DEPTH, N_EXPERTS, 2 * D_FF), f32)
    w_down = jax.random.normal(ks[12], (DEPTH, N_EXPERTS, D_FF, D_MODEL), f32) * (D_FF ** -0.5) * DEEPNORM_BETA
    b_down = 0.01 * jax.random.normal(ks[13], (DEPTH, N_EXPERTS, D_MODEL), f32)
    return {'x': x, 'w_in': w_in, 'w_o': w_o, 'lambda_qk': lambda_qk, 'subln_g': subln_g,
            'sinks': sinks, 'ln_g': ln_g, 'ln_b': ln_b, 'w_router': w_router, 'b_router': b_router,
            'w_up': w_up, 'b_up': b_up, 'w_down': w_down, 'b_down': b_down}


def reference(x, w_in, w_o, lambda_qk, subln_g, sinks, ln_g, ln_b, w_router, b_router, w_up, b_up, w_down, b_down):
    for l in range(DEPTH):
        mix = _hybrid_mixer(x, w_in[l], w_o[l], lambda_qk[l], subln_g[l], sinks[l], l)
        h = _layer_norm(DEEPNORM_ALPHA * x + mix, ln_g[l, 0], ln_b[l, 0])
        ffn = _moe(h, w_router[l], b_router[l], w_up[l], b_up[l], w_down[l], b_down[l])
        x = _layer_norm(DEEPNORM_ALPHA * h + ffn, ln_g[l, 1], ln_b[l, 1])
    return x
```

```python
import functools
import math

import jax
import jax.numpy as jnp
from jax import lax
from jax.experimental import pallas as pl
from jax.experimental.pallas import tpu as pltpu

D_MODEL = 1024
DEPTH = 4
DA_HEAD_DIM = 128
DA_HEADS = 4
DA_WIDTH = 2 * DA_HEAD_DIM
SW_HEAD_DIM = 64
SW_HEADS = 16
SW_KV_HEADS = 2
SW_GROUP = SW_HEADS // SW_KV_HEADS
WINDOW = 128
N_EXPERTS = 32
TOP_K = 4
D_FF = D_MODEL
SWIGLU_LIMIT = 7.0
SWIGLU_ALPHA = 1.702
LN_EPS = 1e-5
DEEPNORM_ALPHA = (2.0 * DEPTH) ** 0.25
NEG_INF = -1e30

QA_OFF, KA_OFF, VA_OFF = 0, 1024, 2048
QB_OFF, KVB_OFF = 3072, 4096
ATT_COLS = 4352
GATE_COLS = 2 * D_MODEL
LANES = 128

PROJ_TM = 512
ATT_TQ = 512
SWA_TB = 256
OUT_TM = 256
MOE_BLOCK = 256
LN2_TM = 256
VMEM_LIMIT = 56 * 1024 * 1024

_F32 = jnp.float32
_BF16 = jnp.bfloat16


def _params(*sem):
    return pltpu.CompilerParams(dimension_semantics=sem, vmem_limit_bytes=VMEM_LIMIT)


def _proj_kernel(x_ref, w_ref, s_ref, o_ref):
    acc = jnp.dot(x_ref[...], w_ref[0], preferred_element_type=_F32)
    o_ref[...] = (acc * s_ref[...]).astype(o_ref.dtype)


def _project(xb, w_all, layer, tn, scale, out_dtype):
    t = xb.shape[0]
    n_cols = w_all.shape[-1]
    assert n_cols % tn == 0 and t % PROJ_TM == 0
    return pl.pallas_call(
        _proj_kernel,
        grid=(n_cols // tn, t // PROJ_TM),
        in_specs=[
            pl.BlockSpec((PROJ_TM, D_MODEL), lambda j, i: (i, 0)),
            pl.BlockSpec((1, D_MODEL, tn), lambda j, i: (layer, 0, j)),
            pl.BlockSpec((1, tn), lambda j, i: (0, j)),
        ],
        out_specs=pl.BlockSpec((PROJ_TM, tn), lambda j, i: (i, j)),
        out_shape=jax.ShapeDtypeStruct((t, n_cols), out_dtype),
        compiler_params=_params("parallel", "parallel"),
        name="in_proj",
    )(xb, w_all, scale)


def _diff_attn_kernel(slope_ref, prm_ref, g_ref, q_ref, k_ref, v_ref, o_ref, acc_ref, m_ref, l_ref, *, tq):
    h = pl.program_id(1)
    i = pl.program_id(2)
    slope = slope_ref[h]
    q = q_ref[...]
    m_ref[...] = jnp.full(m_ref.shape, NEG_INF, _F32)
    l_ref[...] = jnp.zeros(l_ref.shape, _F32)
    acc_ref[...] = jnp.zeros(acc_ref.shape, _F32)

    def step(j, masked):
        start = pl.multiple_of(j * tq, tq)
        k = k_ref[pl.ds(start, tq), :]
        v = v_ref[pl.ds(start, tq), :]
        col = lax.broadcasted_iota(jnp.int32, (1, tq), 1)
        col_bias = (col + (j - i) * tq).astype(_F32) * slope
        for c in range(2):
            lo = c * DA_HEAD_DIM
            s = lax.dot_general(q[:, lo:lo + DA_HEAD_DIM], k[:, lo:lo + DA_HEAD_DIM],
                                (((1,), (1,)), ((), ())), preferred_element_type=_F32)
            s = s + col_bias
            if masked:
                rows = lax.broadcasted_iota(jnp.int32, (tq, tq), 0)
                cols = lax.broadcasted_iota(jnp.int32, (tq, tq), 1)
                s = jnp.where(cols <= rows, s, NEG_INF)
            m_old = m_ref[c]
            m_new = jnp.maximum(m_old, jnp.max(s, axis=1, keepdims=True))
            alpha = jnp.exp(m_old - m_new)
            p = jnp.exp(s - m_new)
            l_ref[c] = alpha * l_ref[c] + jnp.sum(p, axis=1, keepdims=True)
            acc_ref[c] = alpha * acc_ref[c] + jnp.dot(p.astype(_BF16), v, preferred_element_type=_F32)
            m_ref[c] = m_new

    def body(j, carry):
        step(j, False)
        return carry

    lax.fori_loop(0, i, body, 0)
    step(i, True)

    prm = prm_ref[...]
    lam_init = prm[4:5, 0:1]
    lam = (jnp.exp(jnp.sum(prm[0:1] * prm[1:2], axis=1, keepdims=True))
           - jnp.exp(jnp.sum(prm[2:3] * prm[3:4], axis=1, keepdims=True)) + lam_init)
    a = acc_ref[0] / l_ref[0] - lam * (acc_ref[1] / l_ref[1])
    ms = jnp.mean(a * a, axis=1, keepdims=True)
    o_ref[...] = (a * lax.rsqrt(ms + LN_EPS) * g_ref[...] * (1.0 - lam_init)).astype(o_ref.dtype)


def _diff_attention(att, slopes, prm, subln_g, batch, seq):
    tq = ATT_TQ
    nq = seq // tq
    t = batch * seq
    kernel = functools.partial(_diff_attn_kernel, tq=tq)
    return pl.pallas_call(
        kernel,
        grid=(batch, DA_HEADS, nq),
        in_specs=[
            pl.BlockSpec(memory_space=pltpu.SMEM),
            pl.BlockSpec((8, LANES), lambda b, h, i: (0, 0)),
            pl.BlockSpec((1, DA_WIDTH), lambda b, h, i: (0, 0)),
            pl.BlockSpec((tq, DA_WIDTH), lambda b, h, i: (b * nq + i, QA_OFF // DA_WIDTH + h)),
            pl.BlockSpec((seq, DA_WIDTH), lambda b, h, i: (b, KA_OFF // DA_WIDTH + h)),
            pl.BlockSpec((seq, DA_WIDTH), lambda b, h, i: (b, VA_OFF // DA_WIDTH + h)),
        ],
        out_specs=pl.BlockSpec((tq, DA_WIDTH), lambda b, h, i: (b * nq + i, h)),
        out_shape=jax.ShapeDtypeStruct((t, D_MODEL), _F32),
        scratch_shapes=[
            pltpu.VMEM((2, tq, DA_WIDTH), _F32),
            pltpu.VMEM((2, tq, 1), _F32),
            pltpu.VMEM((2, tq, 1), _F32),
        ],
        compiler_params=_params("parallel", "parallel", "arbitrary"),
        name="diff_attn",
    )(slopes, prm, subln_g, att, att, att)


def _swa_kernel(sink_ref, bias_ref, q_ref, kv_ref, kvp_ref, o_ref):
    i = pl.program_id(1)
    kv_all = jnp.concatenate([kvp_ref[...], kv_ref[...]], axis=0)
    col = lax.broadcasted_iota(jnp.int32, (WINDOW, 2 * WINDOW), 1)
    no_prev = jnp.logical_and(i == 0, col < WINDOW)
    for r in range(SWA_TB // WINDOW):
        win = kv_all[r * WINDOW:(r + 2) * WINDOW, :]
        heads = []
        for head in range(SW_HEADS):
            kh = head // SW_GROUP
            q = q_ref[r * WINDOW:(r + 1) * WINDOW, head * SW_HEAD_DIM:(head + 1) * SW_HEAD_DIM]
            k = win[:, kh * SW_HEAD_DIM:(kh + 1) * SW_HEAD_DIM]
            v = win[:, (SW_KV_HEADS + kh) * SW_HEAD_DIM:(SW_KV_HEADS + kh + 1) * SW_HEAD_DIM]
            s = lax.dot_general(q, k, (((1,), (1,)), ((), ())), preferred_element_type=_F32)
            bias = bias_ref[head]
            if r == 0:
                bias = jnp.where(no_prev, NEG_INF, bias)
            s = s + bias
            sink = sink_ref[head]
            m = jnp.maximum(jnp.max(s, axis=1, keepdims=True), sink)
            p = jnp.exp(s - m)
            denom = jnp.sum(p, axis=1, keepdims=True) + jnp.exp(sink - m)
            o = jnp.dot(p.astype(_BF16), v, preferred_element_type=_F32)
            heads.append(o / denom)
        o_ref[r * WINDOW:(r + 1) * WINDOW, :] = jnp.concatenate(heads, axis=1).astype(o_ref.dtype)


def _swa_bias():
    slopes = 2.0 ** (-8.0 * jnp.arange(1, SW_HEADS + 1, dtype=_F32) / SW_HEADS)
    qi = jnp.arange(WINDOW, dtype=jnp.int32)[:, None]
    kj = jnp.arange(2 * WINDOW, dtype=jnp.int32)[None, :]
    dist = qi + WINDOW - kj
    valid = (dist >= 0) & (dist < WINDOW)
    bias = -slopes[:, None, None] * dist.astype(_F32)[None]
    return jnp.where(valid[None], bias, NEG_INF)


def _sliding_window_attention(att, sinks_l, bias, batch, seq):
    tb = SWA_TB
    nb = seq // tb
    ratio = tb // WINDOW
    t = batch * seq
    kv_width = 2 * SW_KV_HEADS * SW_HEAD_DIM
    q_blk = QB_OFF // D_MODEL
    kv_blk = KVB_OFF // kv_width
    return pl.pallas_call(
        _swa_kernel,
        grid=(batch, nb),
        in_specs=[
            pl.BlockSpec(memory_space=pltpu.SMEM),
            pl.BlockSpec((SW_HEADS, WINDOW, 2 * WINDOW), lambda b, i: (0, 0, 0)),
            pl.BlockSpec((tb, D_MODEL), lambda b, i: (b * nb + i, q_blk)),
            pl.BlockSpec((tb, kv_width), lambda b, i: (b * nb + i, kv_blk)),
            pl.BlockSpec((WINDOW, kv_width), lambda b, i: (jnp.maximum((b * nb + i) * ratio - 1, 0), kv_blk)),
        ],
        out_specs=pl.BlockSpec((tb, D_MODEL), lambda b, i: (b * nb + i, 0)),
        out_shape=jax.ShapeDtypeStruct((t, D_MODEL), _F32),
        compiler_params=_params("parallel", "arbitrary"),
        name="swa",
    )(sinks_l, bias, att, att, att)


def _layer_norm(z, g, b):
    mu = jnp.mean(z, axis=1, keepdims=True)
    d = z - mu
    var = jnp.mean(d * d, axis=1, keepdims=True)
    return d * lax.rsqrt(var + LN_EPS) * g + b


def _merge_out_kernel(ga_ref, gb_ref, oa_ref, ob_ref, x_ref, wo_ref, lng_ref, lnb_ref, wr_ref, br_ref,
                      h_ref, hb_ref, idx_ref, gate_ref):
    merged = jax.nn.sigmoid(ga_ref[...]) * oa_ref[...] + jax.nn.sigmoid(gb_ref[...]) * ob_ref[...]
    y = jnp.dot(merged.astype(_BF16), wo_ref[0], preferred_element_type=_F32)
    hn = _layer_norm(DEEPNORM_ALPHA * x_ref[...] + y, lng_ref[0, 0:1, :], lnb_ref[0, 0:1, :])
    h_ref[...] = hn
    hb = hn.astype(_BF16)
    hb_ref[...] = hb
    logits = jnp.dot(hb, wr_ref[0], preferred_element_type=_F32) + br_ref[0]
    tm = logits.shape[0]
    ecol = lax.broadcasted_iota(jnp.int32, (tm, N_EXPERTS), 1)
    lane = lax.broadcasted_iota(jnp.int32, (tm, LANES), 1)
    cur = logits
    vals, idx_out = [], jnp.zeros((tm, LANES), jnp.int32)
    for k in range(TOP_K):
        mx = jnp.max(cur, axis=1, keepdims=True)
        ix = jnp.min(jnp.where(cur == mx, ecol, N_EXPERTS), axis=1, keepdims=True)
        vals.append(mx)
        idx_out = jnp.where(lane == k, ix, idx_out)
        cur = jnp.where(ecol == ix, -jnp.inf, cur)
    exps = [jnp.exp(v - vals[0]) for v in vals]
    tot = exps[0] + exps[1] + exps[2] + exps[3]
    gate_out = jnp.zeros((tm, LANES), _F32)
    for k in range(TOP_K):
        gate_out = jnp.where(lane == k, exps[k] / tot, gate_out)
    idx_ref[...] = idx_out
    gate_ref[...] = gate_out


def _merge_out(gates, o_a, o_b, x, wo_all, ln_g, ln_b, wr_all, br_all, layer):
    t = x.shape[0]
    tm = OUT_TM
    row = lambda i: (i, 0)
    return pl.pallas_call(
        _merge_out_kernel,
        grid=(t // tm,),
        in_specs=[
            pl.BlockSpec((tm, D_MODEL), lambda i: (i, 0)),
            pl.BlockSpec((tm, D_MODEL), lambda i: (i, 1)),
            pl.BlockSpec((tm, D_MODEL), row),
            pl.BlockSpec((tm, D_MODEL), row),
            pl.BlockSpec((tm, D_MODEL), row),
            pl.BlockSpec((1, D_MODEL, D_MODEL), lambda i: (layer, 0, 0)),
            pl.BlockSpec((1, 2, D_MODEL), lambda i: (layer, 0, 0)),
            pl.BlockSpec((1, 2, D_MODEL), lambda i: (layer, 0, 0)),
            pl.BlockSpec((1, D_MODEL, N_EXPERTS), lambda i: (layer, 0, 0)),
            pl.BlockSpec((1, 1, N_EXPERTS), lambda i: (layer, 0, 0)),
        ],
        out_specs=[
            pl.BlockSpec((tm, D_MODEL), row),
            pl.BlockSpec((tm, D_MODEL), row),
            pl.BlockSpec((tm, LANES), row),
            pl.BlockSpec((tm, LANES), row),
        ],
        out_shape=[
            jax.ShapeDtypeStruct((t, D_MODEL), _F32),
            jax.ShapeDtypeStruct((t, D_MODEL), _BF16),
            jax.ShapeDtypeStruct((t, LANES), jnp.int32),
            jax.ShapeDtypeStruct((t, LANES), _F32),
        ],
        compiler_params=_params("parallel"),
        name="merge_out_ln_router",
    )(gates, gates, o_a, o_b, x, wo_all, ln_g, ln_b, wr_all, br_all)


def _expert_kernel(be_ref, nused_ref, x_ref, wg_ref, wu_ref, bg_ref, bu_ref, wd_ref, bd_ref, y_ref):
    i = pl.program_id(0)

    @pl.when(i < nused_ref[0])
    def _():
        x = x_ref[...]
        gate = jnp.dot(x, wg_ref[0, 0], preferred_element_type=_F32) + bg_ref[0, 0]
        up = jnp.dot(x, wu_ref[0, 0], preferred_element_type=_F32) + bu_ref[0, 0]
        gate = jnp.minimum(gate, SWIGLU_LIMIT)
        up = jnp.clip(up, -SWIGLU_LIMIT, SWIGLU_LIMIT)
        act = (up + 1.0) * (gate * jax.nn.sigmoid(SWIGLU_ALPHA * gate))
        y = jnp.dot(act.astype(_BF16), wd_ref[0, 0], preferred_element_type=_F32) + bd_ref[0, 0]
        y_ref[...] = y.astype(y_ref.dtype)

    @pl.when(i >= nused_ref[0])
    def _():
        y_ref[...] = jnp.zeros(y_ref.shape, y_ref.dtype)


def _experts(block_expert, n_used, xb, wg_all, wu_all, bg_all, bu_all, wd_all, bd_all, layer):
    n_rows = xb.shape[0]
    n_blocks = n_rows // MOE_BLOCK
    wspec = pl.BlockSpec((1, 1, D_MODEL, D_FF), lambda i, be, nu: (layer, be[i], 0, 0))
    bspec = pl.BlockSpec((1, 1, 1, D_FF), lambda i, be, nu: (layer, be[i], 0, 0))
    grid_spec = pltpu.PrefetchScalarGridSpec(
        num_scalar_prefetch=2,
        grid=(n_blocks,),
        in_specs=[
            pl.BlockSpec((MOE_BLOCK, D_MODEL), lambda i, be, nu: (i, 0)),
            wspec, wspec, bspec, bspec, wspec, bspec,
        ],
        out_specs=pl.BlockSpec((MOE_BLOCK, D_MODEL), lambda i, be, nu: (i, 0)),
    )
    return pl.pallas_call(
        _expert_kernel,
        grid_spec=grid_spec,
        out_shape=jax.ShapeDtypeStruct((n_rows, D_MODEL), _F32),
        compiler_params=_params("arbitrary"),
        name="experts",
    )(block_expert, n_used, xb, wg_all, wu_all, bg_all, bu_all, wd_all, bd_all)


def _combine_kernel(h_ref, yg_ref, gate_ref, lng_ref, lnb_ref, x_ref, xb_ref):
    gates = gate_ref[...]
    y = gates[:, 0:1] * yg_ref[0]
    for k in range(1, TOP_K):
        y = y + gates[:, k:k + 1] * yg_ref[k]
    xn = _layer_norm(DEEPNORM_ALPHA * h_ref[...] + y, lng_ref[0, 1:2, :], lnb_ref[0, 1:2, :])
    x_ref[...] = xn
    xb_ref[...] = xn.astype(_BF16)


def _combine(h, yg, gates, ln_g, ln_b, layer):
    t = h.shape[0]
    tm = LN2_TM
    row = lambda i: (i, 0)
    return pl.pallas_call(
        _combine_kernel,
        grid=(t // tm,),
        in_specs=[
            pl.BlockSpec((tm, D_MODEL), row),
            pl.BlockSpec((TOP_K, tm, D_MODEL), lambda i: (0, i, 0)),
            pl.BlockSpec((tm, LANES), row),
            pl.BlockSpec((1, 2, D_MODEL), lambda i: (layer, 0, 0)),
            pl.BlockSpec((1, 2, D_MODEL), lambda i: (layer, 0, 0)),
        ],
        out_specs=[pl.BlockSpec((tm, D_MODEL), row), pl.BlockSpec((tm, D_MODEL), row)],
        out_shape=[jax.ShapeDtypeStruct((t, D_MODEL), _F32), jax.ShapeDtypeStruct((t, D_MODEL), _BF16)],
        compiler_params=_params("parallel"),
        name="combine_ln",
    )(h, yg, gates, ln_g, ln_b)


def _dispatch_tables(top_idx, t):
    tk = t * TOP_K
    flat_e = top_idx.reshape(tk)
    onehot = (flat_e[:, None] == jnp.arange(N_EXPERTS, dtype=jnp.int32)[None, :]).astype(jnp.int32)
    csum = jnp.cumsum(onehot, axis=0)
    rank = jnp.sum(csum * onehot, axis=1) - 1
    counts = csum[-1]
    padded = (counts + MOE_BLOCK - 1) // MOE_BLOCK * MOE_BLOCK
    pad_end = jnp.cumsum(padded)
    pad_start = pad_end - padded
    dest = pad_start[flat_e] + rank
    n_blocks = (tk + N_EXPERTS * (MOE_BLOCK - 1) + MOE_BLOCK - 1) // MOE_BLOCK
    n_rows = n_blocks * MOE_BLOCK
    flat_tok = jnp.arange(tk, dtype=jnp.int32) // TOP_K
    buf_tok = jnp.zeros((n_rows,), jnp.int32).at[dest].set(flat_tok)
    block_expert = jnp.minimum(
        jnp.searchsorted(pad_end, jnp.arange(n_blocks, dtype=jnp.int32) * MOE_BLOCK, side='right'),
        N_EXPERTS - 1).astype(jnp.int32)
    n_used = (pad_end[-1:] // MOE_BLOCK).astype(jnp.int32)
    return buf_tok, dest.reshape(t, TOP_K), block_expert, n_used


def kernel(x, w_in, w_o, lambda_qk, subln_g, sinks, ln_g, ln_b, w_router, b_router, w_up, b_up, w_down, b_down):
    batch, seq, _ = x.shape
    t = batch * seq
    assert seq % ATT_TQ == 0 and seq % SWA_TB == 0 and t % PROJ_TM == 0

    w_att_b = w_in[..., :ATT_COLS].astype(_BF16)
    w_gates_b = w_in[..., ATT_COLS:].astype(_BF16)
    w_o_b = w_o.astype(_BF16)
    w_r_b = w_router.astype(_BF16)
    w_gate_b = w_up[..., 0::2].astype(_BF16)
    w_upp_b = w_up[..., 1::2].astype(_BF16)
    b_gate = b_up[..., 0::2].reshape(DEPTH, N_EXPERTS, 1, D_FF)
    b_upp = b_up[..., 1::2].reshape(DEPTH, N_EXPERTS, 1, D_FF)
    w_down_b = w_down.astype(_BF16)
    b_down4 = b_down.reshape(DEPTH, N_EXPERTS, 1, D_MODEL)
    b_router3 = b_router.reshape(DEPTH, 1, N_EXPERTS)

    att_scale = jnp.concatenate([
        jnp.full((1024,), DA_HEAD_DIM ** -0.5, _F32), jnp.ones((2048,), _F32),
        jnp.full((1024,), SW_HEAD_DIM ** -0.5, _F32), jnp.ones((256,), _F32)]).reshape(1, ATT_COLS)
    gate_scale = jnp.ones((1, GATE_COLS), _F32)
    da_slopes = 2.0 ** (-8.0 * jnp.arange(1, DA_HEADS + 1, dtype=_F32) / DA_HEADS)
    swa_bias = _swa_bias()

    xf = x.reshape(t, D_MODEL)
    xb = xf.astype(_BF16)
    for l in range(DEPTH):
        lam_init = 0.8 - 0.6 * math.exp(-0.3 * l)
        prm = jnp.concatenate([lambda_qk[l].astype(_F32), jnp.full((4, DA_HEAD_DIM), lam_init, _F32)], axis=0)
        att = _project(xb, w_att_b, l, ATT_COLS // 2, att_scale, _BF16)
        gates = _project(xb, w_gates_b, l, GATE_COLS // 2, gate_scale, _F32)
        o_a = _diff_attention(att, da_slopes, prm, subln_g[l].reshape(1, DA_WIDTH), batch, seq)
        o_b = _sliding_window_attention(att, sinks[l], swa_bias, batch, seq)
        h, hb, top_idx, top_gate = _merge_out(gates, o_a, o_b, xf, w_o_b, ln_g, ln_b, w_r_b, b_router3, l)
        buf_tok, pos, block_expert, n_used = _dispatch_tables(top_idx[:, :TOP_K], t)
        x_rows = hb[buf_tok]
        yb = _experts(block_expert, n_used, x_rows, w_gate_b, w_upp_b, b_gate, b_upp, w_down_b, b_down4, l)
        yg = yb[pos.T]
        xf, xb = _combine(h, yg, top_gate, ln_g, ln_b, l)
    return xf.reshape(batch, seq, D_MODEL)
```

```python
import functools
import math

import jax
import jax.numpy as jnp
from jax import lax
from jax.experimental import pallas as pl
from jax.experimental.pallas import tpu as pltpu

D_MODEL = 1024
DEPTH = 4
DA_HEAD_DIM = 128
DA_HEADS = 4
DA_WIDTH = 2 * DA_HEAD_DIM
SW_HEAD_DIM = 64
SW_HEADS = 16
SW_KV_HEADS = 2
SW_GROUP = SW_HEADS // SW_KV_HEADS
WINDOW = 128
N_EXPERTS = 32
TOP_K = 4
D_FF = D_MODEL
SWIGLU_LIMIT = 7.0
SWIGLU_ALPHA = 1.702
LN_EPS = 1e-5
DEEPNORM_ALPHA = (2.0 * DEPTH) ** 0.25
NEG_INF = -1e30

QA_OFF, KA_OFF, QB_OFF, KVB_OFF = 0, 1024, 2048, 3072
ATT_COLS = 3328
GATE_COLS = 2 * D_MODEL
LANES = 128
SUBLANES_BF16 = 16
VT_ROWS = DA_WIDTH + SUBLANES_BF16
LOG2E = math.log2(math.e)
N_SLOPE_PARTS = 3

PROJ_TM = 512
ATT_TQ = 512
SWA_TB = 256
OUT_TM = 256
MOE_BLOCK = 256
DEINT_TILE = 256
LN2_TM = 256
VMEM_LIMIT = 56 * 1024 * 1024

_F32 = jnp.float32
_BF16 = jnp.bfloat16


def _params(*sem):
    return pltpu.CompilerParams(dimension_semantics=sem, vmem_limit_bytes=VMEM_LIMIT)


def _proj_kernel(x_ref, w_ref, s_ref, o_ref):
    acc = jnp.dot(x_ref[...], w_ref[0], preferred_element_type=_F32)
    o_ref[...] = (acc * s_ref[...]).astype(o_ref.dtype)


def _project(xb, w_all, layer, tn, scale, out_dtype):
    t = xb.shape[0]
    n_cols = w_all.shape[-1]
    assert n_cols % tn == 0 and t % PROJ_TM == 0
    return pl.pallas_call(
        _proj_kernel,
        grid=(n_cols // tn, t // PROJ_TM),
        in_specs=[
            pl.BlockSpec((PROJ_TM, D_MODEL), lambda j, i: (i, 0)),
            pl.BlockSpec((1, D_MODEL, tn), lambda j, i: (layer, 0, j)),
            pl.BlockSpec((1, tn), lambda j, i: (0, j)),
        ],
        out_specs=pl.BlockSpec((PROJ_TM, tn), lambda j, i: (i, j)),
        out_shape=jax.ShapeDtypeStruct((t, n_cols), out_dtype),
        compiler_params=_params("parallel", "parallel"),
        name="in_proj",
    )(xb, w_all, scale)


def _proj_vt_kernel(x_ref, wt_ref, o_ref):
    x = x_ref[...]
    ones = jnp.ones((SUBLANES_BF16, x.shape[0]), o_ref.dtype)
    for h in range(DA_HEADS):
        vt = lax.dot_general(wt_ref[0, h * DA_WIDTH:(h + 1) * DA_WIDTH, :], x,
                             (((1,), (1,)), ((), ())), preferred_element_type=_F32)
        o_ref[h * VT_ROWS:h * VT_ROWS + DA_WIDTH, :] = vt.astype(o_ref.dtype)
        o_ref[h * VT_ROWS + DA_WIDTH:(h + 1) * VT_ROWS, :] = ones


def _project_vt(xb, wt_all, layer):
    t = xb.shape[0]
    return pl.pallas_call(
        _proj_vt_kernel,
        grid=(t // PROJ_TM,),
        in_specs=[
            pl.BlockSpec((PROJ_TM, D_MODEL), lambda i: (i, 0)),
            pl.BlockSpec((1, D_MODEL, D_MODEL), lambda i: (layer, 0, 0)),
        ],
        out_specs=pl.BlockSpec((DA_HEADS * VT_ROWS, PROJ_TM), lambda i: (0, i)),
        out_shape=jax.ShapeDtypeStruct((DA_HEADS * VT_ROWS, t), _BF16),
        compiler_params=_params("parallel"),
        name="in_proj_vt",
    )(xb, wt_all)


def _diff_attn_kernel(slope_ref, prm_ref, g_ref, q_ref, k_ref, vt_ref, o_ref,
                      acc_ref, m_ref, qaug_ref, kaug_ref, *, tq):
    h = pl.program_id(1)
    i = pl.program_id(2)
    lane = lax.broadcasted_iota(jnp.int32, (tq, DA_HEAD_DIM), 1)
    sub = lax.broadcasted_iota(jnp.int32, (tq, DA_HEAD_DIM), 0)
    slope_tile = jnp.zeros((tq, DA_HEAD_DIM), _F32)
    slope = 0.0
    for part in range(N_SLOPE_PARTS):
        piece = slope_ref[h, part]
        slope = slope + piece
        slope_tile = jnp.where(lane // 2 == part, piece, slope_tile)
    pos_tile = jnp.where(lane < 2 * N_SLOPE_PARTS, jnp.where(lane % 2 == 0, sub % 256, sub // 256 * 256), 0)
    for c in range(2):
        kaug_ref[c, :, DA_HEAD_DIM:] = pos_tile.astype(_F32).astype(_BF16)
        qaug_ref[c, :, :DA_HEAD_DIM] = q_ref[:, c * DA_HEAD_DIM:(c + 1) * DA_HEAD_DIM]
        qaug_ref[c, :, DA_HEAD_DIM:] = slope_tile.astype(_BF16)
    m_ref[...] = jnp.full(m_ref.shape, NEG_INF, _F32)
    acc_ref[...] = jnp.zeros(acc_ref.shape, _F32)

    def step(j, masked):
        start = pl.multiple_of(j * tq, tq)
        vt = vt_ref[:, pl.ds(start, tq)]
        off = ((j - i) * tq).astype(_F32) * slope
        for c in range(2):
            kaug_ref[c, :, :DA_HEAD_DIM] = k_ref[pl.ds(start, tq), c * DA_HEAD_DIM:(c + 1) * DA_HEAD_DIM]
            s = lax.dot_general(kaug_ref[c], qaug_ref[c], (((1,), (1,)), ((), ())),
                                preferred_element_type=_F32)
            if masked:
                keys = lax.broadcasted_iota(jnp.int32, (tq, tq), 0)
                queries = lax.broadcasted_iota(jnp.int32, (tq, tq), 1)
                s = jnp.where(keys <= queries, s, NEG_INF)
            m_old = m_ref[c]
            m_new = jnp.maximum(m_old, jnp.max(s, axis=0, keepdims=True) + off)
            alpha = jnp.exp2(m_old - m_new)
            p = jnp.exp2(s - (m_new - off))
            acc_ref[c] = alpha * acc_ref[c] + jnp.dot(vt, p.astype(_BF16), preferred_element_type=_F32)
            m_ref[c] = m_new

    def body(j, carry):
        step(j, False)
        return carry

    lax.fori_loop(0, i, body, 0)
    step(i, True)

    prm = prm_ref[...]
    lam_init = prm[4:5, 0:1]
    lam = (jnp.exp(jnp.sum(prm[0:1] * prm[1:2], axis=1, keepdims=True))
           - jnp.exp(jnp.sum(prm[2:3] * prm[3:4], axis=1, keepdims=True)) + lam_init)
    o0 = acc_ref[0, :DA_WIDTH, :] / acc_ref[0, DA_WIDTH:DA_WIDTH + 1, :]
    o1 = acc_ref[1, :DA_WIDTH, :] / acc_ref[1, DA_WIDTH:DA_WIDTH + 1, :]
    a = o0 - lam * o1
    ms = jnp.mean(a * a, axis=0, keepdims=True)
    out = a * lax.rsqrt(ms + LN_EPS) * g_ref[...] * (1.0 - lam_init)
    o_ref[...] = out.T.astype(o_ref.dtype)


def _diff_attention(att, vt, slopes, prm, subln_g, batch, seq):
    tq = ATT_TQ
    nq = seq // tq
    t = batch * seq
    kernel = functools.partial(_diff_attn_kernel, tq=tq)
    return pl.pallas_call(
        kernel,
        grid=(batch, DA_HEADS, nq),
        in_specs=[
            pl.BlockSpec(memory_space=pltpu.SMEM),
            pl.BlockSpec((8, LANES), lambda b, h, i: (0, 0)),
            pl.BlockSpec((DA_WIDTH, 1), lambda b, h, i: (0, 0)),
            pl.BlockSpec((tq, DA_WIDTH), lambda b, h, i: (b * nq + i, QA_OFF // DA_WIDTH + h)),
            pl.BlockSpec((seq, DA_WIDTH), lambda b, h, i: (b, KA_OFF // DA_WIDTH + h)),
            pl.BlockSpec((VT_ROWS, seq), lambda b, h, i: (h, b)),
        ],
        out_specs=pl.BlockSpec((tq, DA_WIDTH), lambda b, h, i: (b * nq + i, h)),
        out_shape=jax.ShapeDtypeStruct((t, D_MODEL), _F32),
        scratch_shapes=[
            pltpu.VMEM((2, VT_ROWS, tq), _F32),
            pltpu.VMEM((2, 1, tq), _F32),
            pltpu.VMEM((2, tq, 2 * DA_HEAD_DIM), _BF16),
            pltpu.VMEM((2, tq, 2 * DA_HEAD_DIM), _BF16),
        ],
        compiler_params=_params("parallel", "parallel", "arbitrary"),
        name="diff_attn",
    )(slopes, prm, subln_g, att, att, vt)


def _swa_kernel(sink_ref, bias_ref, q_ref, kv_ref, kvp_ref, o_ref):
    i = pl.program_id(1)
    kv_all = jnp.concatenate([kvp_ref[...], kv_ref[...]], axis=0)
    col = lax.broadcasted_iota(jnp.int32, (WINDOW, 2 * WINDOW), 1)
    no_prev = jnp.logical_and(i == 0, col < WINDOW)
    for r in range(SWA_TB // WINDOW):
        win = kv_all[r * WINDOW:(r + 2) * WINDOW, :]
        heads = []
        for head in range(SW_HEADS):
            kh = head // SW_GROUP
            q = q_ref[r * WINDOW:(r + 1) * WINDOW, head * SW_HEAD_DIM:(head + 1) * SW_HEAD_DIM]
            k = win[:, kh * SW_HEAD_DIM:(kh + 1) * SW_HEAD_DIM]
            v = win[:, (SW_KV_HEADS + kh) * SW_HEAD_DIM:(SW_KV_HEADS + kh + 1) * SW_HEAD_DIM]
            s = lax.dot_general(q, k, (((1,), (1,)), ((), ())), preferred_element_type=_F32)
            bias = bias_ref[head]
            if r == 0:
                bias = jnp.where(no_prev, NEG_INF, bias)
            s = s + bias
            sink = sink_ref[head]
            m = jnp.maximum(jnp.max(s, axis=1, keepdims=True), sink)
            p = jnp.exp(s - m)
            denom = jnp.sum(p, axis=1, keepdims=True) + jnp.exp(sink - m)
            o = jnp.dot(p.astype(_BF16), v, preferred_element_type=_F32)
            heads.append(o / denom)
        o_ref[r * WINDOW:(r + 1) * WINDOW, :] = jnp.concatenate(heads, axis=1).astype(o_ref.dtype)


def _swa_bias():
    slopes = 2.0 ** (-8.0 * jnp.arange(1, SW_HEADS + 1, dtype=_F32) / SW_HEADS)
    qi = jnp.arange(WINDOW, dtype=jnp.int32)[:, None]
    kj = jnp.arange(2 * WINDOW, dtype=jnp.int32)[None, :]
    dist = qi + WINDOW - kj
    valid = (dist >= 0) & (dist < WINDOW)
    bias = -slopes[:, None, None] * dist.astype(_F32)[None]
    return jnp.where(valid[None], bias, NEG_INF)


def _sliding_window_attention(att, sinks_l, bias, batch, seq):
    tb = SWA_TB
    nb = seq // tb
    ratio = tb // WINDOW
    t = batch * seq
    kv_width = 2 * SW_KV_HEADS * SW_HEAD_DIM
    q_blk = QB_OFF // D_MODEL
    kv_blk = KVB_OFF // kv_width
    return pl.pallas_call(
        _swa_kernel,
        grid=(batch, nb),
        in_specs=[
            pl.BlockSpec(memory_space=pltpu.SMEM),
            pl.BlockSpec((SW_HEADS, WINDOW, 2 * WINDOW), lambda b, i: (0, 0, 0)),
            pl.BlockSpec((tb, D_MODEL), lambda b, i: (b * nb + i, q_blk)),
            pl.BlockSpec((tb, kv_width), lambda b, i: (b * nb + i, kv_blk)),
            pl.BlockSpec((WINDOW, kv_width), lambda b, i: (jnp.maximum((b * nb + i) * ratio - 1, 0), kv_blk)),
        ],
        out_specs=pl.BlockSpec((tb, D_MODEL), lambda b, i: (b * nb + i, 0)),
        out_shape=jax.ShapeDtypeStruct((t, D_MODEL), _F32),
        compiler_params=_params("parallel", "arbitrary"),
        name="swa",
    )(sinks_l, bias, att, att, att)


def _layer_norm(z, g, b):
    mu = jnp.mean(z, axis=1, keepdims=True)
    d = z - mu
    var = jnp.mean(d * d, axis=1, keepdims=True)
    return d * lax.rsqrt(var + LN_EPS) * g + b


def _merge_out_kernel(ga_ref, gb_ref, oa_ref, ob_ref, x_ref, wo_ref, lng_ref, lnb_ref, wr_ref, br_ref,
                      h_ref, hb_ref, idx_ref, gate_ref):
    merged = jax.nn.sigmoid(ga_ref[...]) * oa_ref[...] + jax.nn.sigmoid(gb_ref[...]) * ob_ref[...]
    y = jnp.dot(merged.astype(_BF16), wo_ref[0], preferred_element_type=_F32)
    hn = _layer_norm(DEEPNORM_ALPHA * x_ref[...] + y, lng_ref[0, 0:1, :], lnb_ref[0, 0:1, :])
    h_ref[...] = hn
    hb = hn.astype(_BF16)
    hb_ref[...] = hb
    logits = jnp.dot(hb, wr_ref[0], preferred_element_type=_F32) + br_ref[0]
    tm = logits.shape[0]
    ecol = lax.broadcasted_iota(jnp.int32, (tm, N_EXPERTS), 1)
    lane = lax.broadcasted_iota(jnp.int32, (tm, LANES), 1)
    cur = logits
    vals, idx_out = [], jnp.zeros((tm, LANES), jnp.int32)
    for k in range(TOP_K):
        mx = jnp.max(cur, axis=1, keepdims=True)
        ix = jnp.min(jnp.where(cur == mx, ecol, N_EXPERTS), axis=1, keepdims=True)
        vals.append(mx)
        idx_out = jnp.where(lane == k, ix, idx_out)
        cur = jnp.where(ecol == ix, -jnp.inf, cur)
    exps = [jnp.exp(v - vals[0]) for v in vals]
    tot = exps[0] + exps[1] + exps[2] + exps[3]
    gate_out = jnp.zeros((tm, LANES), _F32)
    for k in range(TOP_K):
        gate_out = jnp.where(lane == k, exps[k] / tot, gate_out)
    idx_ref[...] = idx_out
    gate_ref[...] = gate_out


def _merge_out(gates, o_a, o_b, x, wo_all, ln_g, ln_b, wr_all, br_all, layer):
    t = x.shape[0]
    tm = OUT_TM
    row = lambda i: (i, 0)
    return pl.pallas_call(
        _merge_out_kernel,
        grid=(t // tm,),
        in_specs=[
            pl.BlockSpec((tm, D_MODEL), lambda i: (i, 0)),
            pl.BlockSpec((tm, D_MODEL), lambda i: (i, 1)),
            pl.BlockSpec((tm, D_MODEL), row),
            pl.BlockSpec((tm, D_MODEL), row),
            pl.BlockSpec((tm, D_MODEL), row),
            pl.BlockSpec((1, D_MODEL, D_MODEL), lambda i: (layer, 0, 0)),
            pl.BlockSpec((1, 2, D_MODEL), lambda i: (layer, 0, 0)),
            pl.BlockSpec((1, 2, D_MODEL), lambda i: (layer, 0, 0)),
            pl.BlockSpec((1, D_MODEL, N_EXPERTS), lambda i: (layer, 0, 0)),
            pl.BlockSpec((1, 1, N_EXPERTS), lambda i: (layer, 0, 0)),
        ],
        out_specs=[
            pl.BlockSpec((tm, D_MODEL), row),
            pl.BlockSpec((tm, D_MODEL), row),
            pl.BlockSpec((tm, LANES), row),
            pl.BlockSpec((tm, LANES), row),
        ],
        out_shape=[
            jax.ShapeDtypeStruct((t, D_MODEL), _F32),
            jax.ShapeDtypeStruct((t, D_MODEL), _BF16),
            jax.ShapeDtypeStruct((t, LANES), jnp.int32),
            jax.ShapeDtypeStruct((t, LANES), _F32),
        ],
        compiler_params=_params("parallel"),
        name="merge_out_ln_router",
    )(gates, gates, o_a, o_b, x, wo_all, ln_g, ln_b, wr_all, br_all)


def _expert_kernel(be_ref, nused_ref, x_ref, wup_ref, bg_ref, bu_ref, wd_ref, bd_ref, y_ref,
                   wg_scr, wu_scr, wd_scr):
    i = pl.program_id(0)
    changed = jnp.logical_or(i == 0, be_ref[i] != be_ref[jnp.maximum(i - 1, 0)])

    @pl.when(jnp.logical_and(changed, i < nused_ref[0]))
    def _():
        half = DEINT_TILE // 2
        r = lax.broadcasted_iota(jnp.int32, (DEINT_TILE, DEINT_TILE), 0)
        c = lax.broadcasted_iota(jnp.int32, (DEINT_TILE, DEINT_TILE), 1)
        sel = jnp.where(r == jnp.where(c < half, 2 * c, 2 * (c - half) + 1), 1.0, 0.0).astype(_BF16)
        for t in range(2 * D_FF // DEINT_TILE):
            w = wup_ref[0, 0, :, t * DEINT_TILE:(t + 1) * DEINT_TILE].astype(_BF16)
            de = jnp.dot(w, sel, preferred_element_type=_F32).astype(_BF16)
            wg_scr[:, t * half:(t + 1) * half] = de[:, :half]
            wu_scr[:, t * half:(t + 1) * half] = de[:, half:]
        wd_scr[...] = wd_ref[0, 0].astype(_BF16)

    @pl.when(i < nused_ref[0])
    def _():
        x = x_ref[...]
        gate = jnp.dot(x, wg_scr[...], preferred_element_type=_F32) + bg_ref[0, 0]
        up = jnp.dot(x, wu_scr[...], preferred_element_type=_F32) + bu_ref[0, 0]
        gate = jnp.minimum(gate, SWIGLU_LIMIT)
        up = jnp.clip(up, -SWIGLU_LIMIT, SWIGLU_LIMIT)
        act = (up + 1.0) * (gate * jax.nn.sigmoid(SWIGLU_ALPHA * gate))
        y = jnp.dot(act.astype(_BF16), wd_scr[...], preferred_element_type=_F32) + bd_ref[0, 0]
        y_ref[...] = y.astype(y_ref.dtype)

    @pl.when(i >= nused_ref[0])
    def _():
        y_ref[...] = jnp.zeros(y_ref.shape, y_ref.dtype)


def _experts(block_expert, n_used, xb, w_up, bg_all, bu_all, w_down, bd_all, layer):
    n_rows = xb.shape[0]
    n_blocks = n_rows // MOE_BLOCK
    pick = lambda i, be, nu: (layer, be[i], 0, 0)
    bspec = pl.BlockSpec((1, 1, 1, D_FF), pick)
    grid_spec = pltpu.PrefetchScalarGridSpec(
        num_scalar_prefetch=2,
        grid=(n_blocks,),
        in_specs=[
            pl.BlockSpec((MOE_BLOCK, D_MODEL), lambda i, be, nu: (i, 0)),
            pl.BlockSpec((1, 1, D_MODEL, 2 * D_FF), pick),
            bspec, bspec,
            pl.BlockSpec((1, 1, D_FF, D_MODEL), pick),
            pl.BlockSpec((1, 1, 1, D_MODEL), pick),
        ],
        out_specs=pl.BlockSpec((MOE_BLOCK, D_MODEL), lambda i, be, nu: (i, 0)),
        scratch_shapes=[
            pltpu.VMEM((D_MODEL, D_FF), _BF16),
            pltpu.VMEM((D_MODEL, D_FF), _BF16),
            pltpu.VMEM((D_FF, D_MODEL), _BF16),
        ],
    )
    return pl.pallas_call(
        _expert_kernel,
        grid_spec=grid_spec,
        out_shape=jax.ShapeDtypeStruct((n_rows, D_MODEL), _F32),
        compiler_params=_params("arbitrary"),
        name="experts",
    )(block_expert, n_used, xb, w_up, bg_all, bu_all, w_down, bd_all)


def _combine_kernel(h_ref, yg_ref, gate_ref, lng_ref, lnb_ref, x_ref, xb_ref):
    gates = gate_ref[...]
    y = gates[:, 0:1] * yg_ref[0]
    for k in range(1, TOP_K):
        y = y + gates[:, k:k + 1] * yg_ref[k]
    xn = _layer_norm(DEEPNORM_ALPHA * h_ref[...] + y, lng_ref[0, 1:2, :], lnb_ref[0, 1:2, :])
    x_ref[...] = xn
    xb_ref[...] = xn.astype(_BF16)


def _combine(h, yg, gates, ln_g, ln_b, layer):
    t = h.shape[0]
    tm = LN2_TM
    row = lambda i: (i, 0)
    return pl.pallas_call(
        _combine_kernel,
        grid=(t // tm,),
        in_specs=[
            pl.BlockSpec((tm, D_MODEL), row),
            pl.BlockSpec((TOP_K, tm, D_MODEL), lambda i: (0, i, 0)),
            pl.BlockSpec((tm, LANES), row),
            pl.BlockSpec((1, 2, D_MODEL), lambda i: (layer, 0, 0)),
            pl.BlockSpec((1, 2, D_MODEL), lambda i: (layer, 0, 0)),
        ],
        out_specs=[pl.BlockSpec((tm, D_MODEL), row), pl.BlockSpec((tm, D_MODEL), row)],
        out_shape=[jax.ShapeDtypeStruct((t, D_MODEL), _F32), jax.ShapeDtypeStruct((t, D_MODEL), _BF16)],
        compiler_params=_params("parallel"),
        name="combine_ln",
    )(h, yg, gates, ln_g, ln_b)


def _dispatch_tables(top_idx, t):
    tk = t * TOP_K
    flat_e = top_idx.reshape(tk)
    onehot = (flat_e[:, None] == jnp.arange(N_EXPERTS, dtype=jnp.int32)[None, :]).astype(jnp.int32)
    csum = jnp.cumsum(onehot, axis=0)
    rank = jnp.sum(csum * onehot, axis=1) - 1
    counts = csum[-1]
    padded = (counts + MOE_BLOCK - 1) // MOE_BLOCK * MOE_BLOCK
    pad_end = jnp.cumsum(padded)
    pad_start = pad_end - padded
    dest = pad_start[flat_e] + rank
    n_blocks = (tk + N_EXPERTS * (MOE_BLOCK - 1) + MOE_BLOCK - 1) // MOE_BLOCK
    n_rows = n_blocks * MOE_BLOCK
    flat_tok = jnp.arange(tk, dtype=jnp.int32) // TOP_K
    buf_tok = jnp.zeros((n_rows,), jnp.int32).at[dest].set(flat_tok)
    block_expert = jnp.minimum(
        jnp.searchsorted(pad_end, jnp.arange(n_blocks, dtype=jnp.int32) * MOE_BLOCK, side='right'),
        N_EXPERTS - 1).astype(jnp.int32)
    n_used = (pad_end[-1:] // MOE_BLOCK).astype(jnp.int32)
    return buf_tok, dest.reshape(t, TOP_K), block_expert, n_used


def kernel(x, w_in, w_o, lambda_qk, subln_g, sinks, ln_g, ln_b, w_router, b_router, w_up, b_up, w_down, b_down):
    batch, seq, _ = x.shape
    t = batch * seq
    assert seq % ATT_TQ == 0 and seq % SWA_TB == 0 and t % PROJ_TM == 0

    va_lo, va_hi = 2 * D_MODEL, 3 * D_MODEL
    gates_lo = w_in.shape[-1] - GATE_COLS
    w_att_b = jnp.concatenate([w_in[..., :va_lo], w_in[..., va_hi:gates_lo]], axis=-1).astype(_BF16)
    w_vt_b = jnp.swapaxes(w_in[..., va_lo:va_hi], 1, 2).astype(_BF16)
    w_gates_b = w_in[..., gates_lo:].astype(_BF16)
    w_o_b = w_o.astype(_BF16)
    w_r_b = w_router.astype(_BF16)
    b_gate = b_up[..., 0::2].reshape(DEPTH, N_EXPERTS, 1, D_FF)
    b_upp = b_up[..., 1::2].reshape(DEPTH, N_EXPERTS, 1, D_FF)
    b_down4 = b_down.reshape(DEPTH, N_EXPERTS, 1, D_MODEL)
    b_router3 = b_router.reshape(DEPTH, 1, N_EXPERTS)

    att_scale = jnp.concatenate([
        jnp.full((D_MODEL,), DA_HEAD_DIM ** -0.5 * LOG2E, _F32), jnp.ones((D_MODEL,), _F32),
        jnp.full((D_MODEL,), SW_HEAD_DIM ** -0.5, _F32),
        jnp.ones((2 * SW_KV_HEADS * SW_HEAD_DIM,), _F32)]).reshape(1, ATT_COLS)
    gate_scale = jnp.ones((1, GATE_COLS), _F32)
    rest = 2.0 ** (-8.0 * jnp.arange(1, DA_HEADS + 1, dtype=_F32) / DA_HEADS) * LOG2E
    pieces = []
    for _ in range(N_SLOPE_PARTS):
        piece = rest.astype(_BF16).astype(_F32)
        pieces.append(piece)
        rest = rest - piece
    da_slopes = jnp.stack(pieces, axis=1)
    swa_bias = _swa_bias()

    xf = x.reshape(t, D_MODEL)
    xb = xf.astype(_BF16)
    for l in range(DEPTH):
        lam_init = 0.8 - 0.6 * math.exp(-0.3 * l)
        prm = jnp.concatenate([lambda_qk[l].astype(_F32), jnp.full((4, DA_HEAD_DIM), lam_init, _F32)], axis=0)
        att = _project(xb, w_att_b, l, ATT_COLS // 2, att_scale, _BF16)
        gates = _project(xb, w_gates_b, l, GATE_COLS // 2, gate_scale, _F32)
        vt = _project_vt(xb, w_vt_b, l)
        o_a = _diff_attention(att, vt, da_slopes, prm, subln_g[l].reshape(DA_WIDTH, 1), batch, seq)
        o_b = _sliding_window_attention(att, sinks[l], swa_bias, batch, seq)
        h, hb, top_idx, top_gate = _merge_out(gates, o_a, o_b, xf, w_o_b, ln_g, ln_b, w_r_b, b_router3, l)
        buf_tok, pos, block_expert, n_used = _dispatch_tables(top_idx[:, :TOP_K], t)
        x_rows = hb[buf_tok]
        yb = _experts(block_expert, n_used, x_rows, w_up, b_gate, b_upp, w_down, b_down4, l)
        yg = yb[pos.T]
        xf, xb = _combine(h, yg, top_gate, ln_g, ln_b, l)
    return xf.reshape(batch, seq, D_MODEL)
```

```python
import functools
import math

import jax
import jax.numpy as jnp
from jax import lax
from jax.experimental import pallas as pl
from jax.experimental.pallas import tpu as pltpu

D_MODEL = 1024
DEPTH = 4
DA_HEAD_DIM = 128
DA_HEADS = 4
DA_WIDTH = 2 * DA_HEAD_DIM
SW_HEAD_DIM = 64
SW_HEADS = 16
SW_KV_HEADS = 2
SW_GROUP = SW_HEADS // SW_KV_HEADS
WINDOW = 128
N_EXPERTS = 32
TOP_K = 4
D_FF = D_MODEL
SWIGLU_LIMIT = 7.0
SWIGLU_ALPHA = 1.702
LN_EPS = 1e-5
DEEPNORM_ALPHA = (2.0 * DEPTH) ** 0.25
NEG_INF = -1e30

QA_OFF, KA_OFF, QB_OFF, KVB_OFF = 0, 1024, 2048, 3072
ATT_COLS = 3328
GATE_COLS = 2 * D_MODEL
LANES = 128
SUBLANES_BF16 = 16
VT_ROWS = DA_WIDTH + SUBLANES_BF16
LOG2E = math.log2(math.e)
N_SLOPE_PARTS = 3

PROJ_TM = 512
ATT_TQ = 512
SWA_TB = 256
OUT_TM = 256
MOE_BLOCK = 256
DEINT_TILE = 256
LN2_TM = 256
VMEM_LIMIT = 56 * 1024 * 1024

_F32 = jnp.float32
_BF16 = jnp.bfloat16


def _params(*sem):
    return pltpu.CompilerParams(dimension_semantics=sem, vmem_limit_bytes=VMEM_LIMIT)


def _proj_kernel(x_ref, w_ref, s_ref, o_ref):
    acc = jnp.dot(x_ref[...], w_ref[0], preferred_element_type=_F32)
    o_ref[...] = (acc * s_ref[...]).astype(o_ref.dtype)


def _project(xb, w_all, layer, tn, scale, out_dtype):
    t = xb.shape[0]
    n_cols = w_all.shape[-1]
    assert n_cols % tn == 0 and t % PROJ_TM == 0
    return pl.pallas_call(
        _proj_kernel,
        grid=(n_cols // tn, t // PROJ_TM),
        in_specs=[
            pl.BlockSpec((PROJ_TM, D_MODEL), lambda j, i: (i, 0)),
            pl.BlockSpec((1, D_MODEL, tn), lambda j, i: (layer, 0, j)),
            pl.BlockSpec((1, tn), lambda j, i: (0, j)),
        ],
        out_specs=pl.BlockSpec((PROJ_TM, tn), lambda j, i: (i, j)),
        out_shape=jax.ShapeDtypeStruct((t, n_cols), out_dtype),
        compiler_params=_params("parallel", "parallel"),
        name="in_proj",
    )(xb, w_all, scale)


def _proj_vt_kernel(x_ref, wt_ref, o_ref):
    x = x_ref[...]
    ones = jnp.ones((SUBLANES_BF16, x.shape[0]), o_ref.dtype)
    for h in range(DA_HEADS):
        vt = lax.dot_general(wt_ref[0, h * DA_WIDTH:(h + 1) * DA_WIDTH, :], x,
                             (((1,), (1,)), ((), ())), preferred_element_type=_F32)
        o_ref[h * VT_ROWS:h * VT_ROWS + DA_WIDTH, :] = vt.astype(o_ref.dtype)
        o_ref[h * VT_ROWS + DA_WIDTH:(h + 1) * VT_ROWS, :] = ones


def _project_vt(xb, wt_all, layer):
    t = xb.shape[0]
    return pl.pallas_call(
        _proj_vt_kernel,
        grid=(t // PROJ_TM,),
        in_specs=[
            pl.BlockSpec((PROJ_TM, D_MODEL), lambda i: (i, 0)),
            pl.BlockSpec((1, D_MODEL, D_MODEL), lambda i: (layer, 0, 0)),
        ],
        out_specs=pl.BlockSpec((DA_HEADS * VT_ROWS, PROJ_TM), lambda i: (0, i)),
        out_shape=jax.ShapeDtypeStruct((DA_HEADS * VT_ROWS, t), _BF16),
        compiler_params=_params("parallel"),
        name="in_proj_vt",
    )(xb, wt_all)


def _diff_attn_kernel(slope_ref, prm_ref, g_ref, q_ref, k_ref, vt_ref, o_ref,
                      acc_ref, m_ref, qaug_ref, kaug_ref, s_ref, smax_ref, *, tq):
    h = pl.program_id(1)
    i = pl.program_id(2)
    lane = lax.broadcasted_iota(jnp.int32, (tq, DA_HEAD_DIM), 1)
    sub = lax.broadcasted_iota(jnp.int32, (tq, DA_HEAD_DIM), 0)
    slope_tile = jnp.zeros((tq, DA_HEAD_DIM), _F32)
    slope = 0.0
    for part in range(N_SLOPE_PARTS):
        piece = slope_ref[h, part]
        slope = slope + piece
        slope_tile = jnp.where(lane // 2 == part, piece, slope_tile)
    pos_tile = jnp.where(lane < 2 * N_SLOPE_PARTS, jnp.where(lane % 2 == 0, sub % 256, sub // 256 * 256), 0)
    for c in range(2):
        kaug_ref[c, :, DA_HEAD_DIM:] = pos_tile.astype(_F32).astype(_BF16)
        qaug_ref[c, :, :DA_HEAD_DIM] = q_ref[:, c * DA_HEAD_DIM:(c + 1) * DA_HEAD_DIM]
        qaug_ref[c, :, DA_HEAD_DIM:] = slope_tile.astype(_BF16)
    m_ref[...] = jnp.full(m_ref.shape, NEG_INF, _F32)
    acc_ref[...] = jnp.zeros(acc_ref.shape, _F32)

    def scores(j):
        start = pl.multiple_of(j * tq, tq)
        for c in range(2):
            kaug_ref[c, :, :DA_HEAD_DIM] = k_ref[pl.ds(start, tq), c * DA_HEAD_DIM:(c + 1) * DA_HEAD_DIM]
            s = lax.dot_general(kaug_ref[c], qaug_ref[c], (((1,), (1,)), ((), ())),
                                preferred_element_type=_F32)
            s_ref[c] = s
            smax_ref[c] = jnp.max(s, axis=0, keepdims=True)

    def accumulate(j, masked):
        start = pl.multiple_of(j * tq, tq)
        vt = vt_ref[:, pl.ds(start, tq)]
        off = ((j - i) * tq).astype(_F32) * slope
        for c in range(2):
            s = s_ref[c]
            if masked:
                keys = lax.broadcasted_iota(jnp.int32, (tq, tq), 0)
                queries = lax.broadcasted_iota(jnp.int32, (tq, tq), 1)
                s = jnp.where(keys <= queries, s, NEG_INF)
                smax = jnp.max(s, axis=0, keepdims=True)
            else:
                smax = smax_ref[c]
            m_old = m_ref[c]
            m_new = jnp.maximum(m_old, smax + off)
            alpha = jnp.exp2(m_old - m_new)
            p = jnp.exp2(s - (m_new - off))
            acc_ref[c] = alpha * acc_ref[c] + jnp.dot(vt, p.astype(_BF16), preferred_element_type=_F32)
            m_ref[c] = m_new

    def body(j, carry):
        accumulate(j, False)
        scores(j + 1)
        return carry

    scores(0)
    lax.fori_loop(0, i, body, 0)
    accumulate(i, True)

    prm = prm_ref[...]
    lam_init = prm[4:5, 0:1]
    lam = (jnp.exp(jnp.sum(prm[0:1] * prm[1:2], axis=1, keepdims=True))
           - jnp.exp(jnp.sum(prm[2:3] * prm[3:4], axis=1, keepdims=True)) + lam_init)
    o0 = acc_ref[0, :DA_WIDTH, :] / acc_ref[0, DA_WIDTH:DA_WIDTH + 1, :]
    o1 = acc_ref[1, :DA_WIDTH, :] / acc_ref[1, DA_WIDTH:DA_WIDTH + 1, :]
    a = o0 - lam * o1
    ms = jnp.mean(a * a, axis=0, keepdims=True)
    out = a * lax.rsqrt(ms + LN_EPS) * g_ref[...] * (1.0 - lam_init)
    o_ref[...] = out.T.astype(o_ref.dtype)


def _diff_attention(att, vt, slopes, prm, subln_g, batch, seq):
    tq = ATT_TQ
    nq = seq // tq
    t = batch * seq
    kernel = functools.partial(_diff_attn_kernel, tq=tq)
    return pl.pallas_call(
        kernel,
        grid=(batch, DA_HEADS, nq),
        in_specs=[
            pl.BlockSpec(memory_space=pltpu.SMEM),
            pl.BlockSpec((8, LANES), lambda b, h, i: (0, 0)),
            pl.BlockSpec((DA_WIDTH, 1), lambda b, h, i: (0, 0)),
            pl.BlockSpec((tq, DA_WIDTH), lambda b, h, i: (b * nq + i, QA_OFF // DA_WIDTH + h)),
            pl.BlockSpec((seq, DA_WIDTH), lambda b, h, i: (b, KA_OFF // DA_WIDTH + h)),
            pl.BlockSpec((VT_ROWS, seq), lambda b, h, i: (h, b)),
        ],
        out_specs=pl.BlockSpec((tq, DA_WIDTH), lambda b, h, i: (b * nq + i, h)),
        out_shape=jax.ShapeDtypeStruct((t, D_MODEL), _F32),
        scratch_shapes=[
            pltpu.VMEM((2, VT_ROWS, tq), _F32),
            pltpu.VMEM((2, 1, tq), _F32),
            pltpu.VMEM((2, tq, 2 * DA_HEAD_DIM), _BF16),
            pltpu.VMEM((2, tq, 2 * DA_HEAD_DIM), _BF16),
            pltpu.VMEM((2, tq, tq), _F32),
            pltpu.VMEM((2, 1, tq), _F32),
        ],
        compiler_params=_params("parallel", "parallel", "arbitrary"),
        name="diff_attn",
    )(slopes, prm, subln_g, att, att, vt)


def _swa_kernel(sink_ref, bias_ref, q_ref, kv_ref, kvp_ref, o_ref):
    i = pl.program_id(1)
    kv_all = jnp.concatenate([kvp_ref[...], kv_ref[...]], axis=0)
    col = lax.broadcasted_iota(jnp.int32, (WINDOW, 2 * WINDOW), 1)
    no_prev = jnp.logical_and(i == 0, col < WINDOW)
    for r in range(SWA_TB // WINDOW):
        win = kv_all[r * WINDOW:(r + 2) * WINDOW, :]
        heads = []
        for head in range(SW_HEADS):
            kh = head // SW_GROUP
            q = q_ref[r * WINDOW:(r + 1) * WINDOW, head * SW_HEAD_DIM:(head + 1) * SW_HEAD_DIM]
            k = win[:, kh * SW_HEAD_DIM:(kh + 1) * SW_HEAD_DIM]
            v = win[:, (SW_KV_HEADS + kh) * SW_HEAD_DIM:(SW_KV_HEADS + kh + 1) * SW_HEAD_DIM]
            s = lax.dot_general(q, k, (((1,), (1,)), ((), ())), preferred_element_type=_F32)
            bias = bias_ref[head]
            if r == 0:
                bias = jnp.where(no_prev, NEG_INF, bias)
            s = s + bias
            sink = sink_ref[head]
            m = jnp.maximum(jnp.max(s, axis=1, keepdims=True), sink)
            p = jnp.exp(s - m)
            denom = jnp.sum(p, axis=1, keepdims=True) + jnp.exp(sink - m)
            o = jnp.dot(p.astype(_BF16), v, preferred_element_type=_F32)
            heads.append(o / denom)
        o_ref[r * WINDOW:(r + 1) * WINDOW, :] = jnp.concatenate(heads, axis=1).astype(o_ref.dtype)


def _swa_bias():
    slopes = 2.0 ** (-8.0 * jnp.arange(1, SW_HEADS + 1, dtype=_F32) / SW_HEADS)
    qi = jnp.arange(WINDOW, dtype=jnp.int32)[:, None]
    kj = jnp.arange(2 * WINDOW, dtype=jnp.int32)[None, :]
    dist = qi + WINDOW - kj
    valid = (dist >= 0) & (dist < WINDOW)
    bias = -slopes[:, None, None] * dist.astype(_F32)[None]
    return jnp.where(valid[None], bias, NEG_INF)


def _sliding_window_attention(att, sinks_l, bias, batch, seq):
    tb = SWA_TB
    nb = seq // tb
    ratio = tb // WINDOW
    t = batch * seq
    kv_width = 2 * SW_KV_HEADS * SW_HEAD_DIM
    q_blk = QB_OFF // D_MODEL
    kv_blk = KVB_OFF // kv_width
    return pl.pallas_call(
        _swa_kernel,
        grid=(batch, nb),
        in_specs=[
            pl.BlockSpec(memory_space=pltpu.SMEM),
            pl.BlockSpec((SW_HEADS, WINDOW, 2 * WINDOW), lambda b, i: (0, 0, 0)),
            pl.BlockSpec((tb, D_MODEL), lambda b, i: (b * nb + i, q_blk)),
            pl.BlockSpec((tb, kv_width), lambda b, i: (b * nb + i, kv_blk)),
            pl.BlockSpec((WINDOW, kv_width), lambda b, i: (jnp.maximum((b * nb + i) * ratio - 1, 0), kv_blk)),
        ],
        out_specs=pl.BlockSpec((tb, D_MODEL), lambda b, i: (b * nb + i, 0)),
        out_shape=jax.ShapeDtypeStruct((t, D_MODEL), _F32),
        compiler_params=_params("parallel", "arbitrary"),
        name="swa",
    )(sinks_l, bias, att, att, att)


def _layer_norm(z, g, b):
    mu = jnp.mean(z, axis=1, keepdims=True)
    d = z - mu
    var = jnp.mean(d * d, axis=1, keepdims=True)
    return d * lax.rsqrt(var + LN_EPS) * g + b


def _merge_out_kernel(ga_ref, gb_ref, oa_ref, ob_ref, x_ref, wo_ref, lng_ref, lnb_ref, wr_ref, br_ref,
                      h_ref, hb_ref, idx_ref, gate_ref):
    merged = jax.nn.sigmoid(ga_ref[...]) * oa_ref[...] + jax.nn.sigmoid(gb_ref[...]) * ob_ref[...]
    y = jnp.dot(merged.astype(_BF16), wo_ref[0], preferred_element_type=_F32)
    hn = _layer_norm(DEEPNORM_ALPHA * x_ref[...] + y, lng_ref[0, 0:1, :], lnb_ref[0, 0:1, :])
    h_ref[...] = hn
    hb = hn.astype(_BF16)
    hb_ref[...] = hb
    logits = jnp.dot(hb, wr_ref[0], preferred_element_type=_F32) + br_ref[0]
    tm = logits.shape[0]
    ecol = lax.broadcasted_iota(jnp.int32, (tm, N_EXPERTS), 1)
    lane = lax.broadcasted_iota(jnp.int32, (tm, LANES), 1)
    cur = logits
    vals, idx_out = [], jnp.zeros((tm, LANES), jnp.int32)
    for k in range(TOP_K):
        mx = jnp.max(cur, axis=1, keepdims=True)
        ix = jnp.min(jnp.where(cur == mx, ecol, N_EXPERTS), axis=1, keepdims=True)
        vals.append(mx)
        idx_out = jnp.where(lane == k, ix, idx_out)
        cur = jnp.where(ecol == ix, -jnp.inf, cur)
    exps = [jnp.exp(v - vals[0]) for v in vals]
    tot = exps[0] + exps[1] + exps[2] + exps[3]
    gate_out = jnp.zeros((tm, LANES), _F32)
    for k in range(TOP_K):
        gate_out = jnp.where(lane == k, exps[k] / tot, gate_out)
    idx_ref[...] = idx_out
    gate_ref[...] = gate_out


def _merge_out(gates, o_a, o_b, x, wo_all, ln_g, ln_b, wr_all, br_all, layer):
    t = x.shape[0]
    tm = OUT_TM
    row = lambda i: (i, 0)
    return pl.pallas_call(
        _merge_out_kernel,
        grid=(t // tm,),
        in_specs=[
            pl.BlockSpec((tm, D_MODEL), lambda i: (i, 0)),
            pl.BlockSpec((tm, D_MODEL), lambda i: (i, 1)),
            pl.BlockSpec((tm, D_MODEL), row),
            pl.BlockSpec((tm, D_MODEL), row),
            pl.BlockSpec((tm, D_MODEL), row),
            pl.BlockSpec((1, D_MODEL, D_MODEL), lambda i: (layer, 0, 0)),
            pl.BlockSpec((1, 2, D_MODEL), lambda i: (layer, 0, 0)),
            pl.BlockSpec((1, 2, D_MODEL), lambda i: (layer, 0, 0)),
            pl.BlockSpec((1, D_MODEL, N_EXPERTS), lambda i: (layer, 0, 0)),
            pl.BlockSpec((1, 1, N_EXPERTS), lambda i: (layer, 0, 0)),
        ],
        out_specs=[
            pl.BlockSpec((tm, D_MODEL), row),
            pl.BlockSpec((tm, D_MODEL), row),
            pl.BlockSpec((tm, LANES), row),
            pl.BlockSpec((tm, LANES), row),
        ],
        out_shape=[
            jax.ShapeDtypeStruct((t, D_MODEL), _F32),
            jax.ShapeDtypeStruct((t, D_MODEL), _BF16),
            jax.ShapeDtypeStruct((t, LANES), jnp.int32),
            jax.ShapeDtypeStruct((t, LANES), _F32),
        ],
        compiler_params=_params("parallel"),
        name="merge_out_ln_router",
    )(gates, gates, o_a, o_b, x, wo_all, ln_g, ln_b, wr_all, br_all)


def _expert_kernel(be_ref, nused_ref, x_ref, wup_ref, bg_ref, bu_ref, wd_ref, bd_ref, y_ref,
                   wg_scr, wu_scr, wd_scr):
    i = pl.program_id(0)
    changed = jnp.logical_or(i == 0, be_ref[i] != be_ref[jnp.maximum(i - 1, 0)])

    @pl.when(jnp.logical_and(changed, i < nused_ref[0]))
    def _():
        half = DEINT_TILE // 2
        r = lax.broadcasted_iota(jnp.int32, (DEINT_TILE, DEINT_TILE), 0)
        c = lax.broadcasted_iota(jnp.int32, (DEINT_TILE, DEINT_TILE), 1)
        sel = jnp.where(r == jnp.where(c < half, 2 * c, 2 * (c - half) + 1), 1.0, 0.0).astype(_BF16)
        for t in range(2 * D_FF // DEINT_TILE):
            w = wup_ref[0, 0, :, t * DEINT_TILE:(t + 1) * DEINT_TILE].astype(_BF16)
            de = jnp.dot(w, sel, preferred_element_type=_F32).astype(_BF16)
            wg_scr[:, t * half:(t + 1) * half] = de[:, :half]
            wu_scr[:, t * half:(t + 1) * half] = de[:, half:]
        wd_scr[...] = wd_ref[0, 0].astype(_BF16)

    @pl.when(i < nused_ref[0])
    def _():
        x = x_ref[...]
        gate = jnp.dot(x, wg_scr[...], preferred_element_type=_F32) + bg_ref[0, 0]
        up = jnp.dot(x, wu_scr[...], preferred_element_type=_F32) + bu_ref[0, 0]
        gate = jnp.minimum(gate, SWIGLU_LIMIT)
        up = jnp.clip(up, -SWIGLU_LIMIT, SWIGLU_LIMIT)
        act = (up + 1.0) * (gate * jax.nn.sigmoid(SWIGLU_ALPHA * gate))
        y = jnp.dot(act.astype(_BF16), wd_scr[...], preferred_element_type=_F32) + bd_ref[0, 0]
        y_ref[...] = y.astype(y_ref.dtype)

    @pl.when(i >= nused_ref[0])
    def _():
        y_ref[...] = jnp.zeros(y_ref.shape, y_ref.dtype)


def _experts(block_expert, n_used, xb, w_up, bg_all, bu_all, w_down, bd_all, layer):
    n_rows = xb.shape[0]
    n_blocks = n_rows // MOE_BLOCK
    pick = lambda i, be, nu: (layer, be[i], 0, 0)
    bspec = pl.BlockSpec((1, 1, 1, D_FF), pick)
    grid_spec = pltpu.PrefetchScalarGridSpec(
        num_scalar_prefetch=2,
        grid=(n_blocks,),
        in_specs=[
            pl.BlockSpec((MOE_BLOCK, D_MODEL), lambda i, be, nu: (i, 0)),
            pl.BlockSpec((1, 1, D_MODEL, 2 * D_FF), pick),
            bspec, bspec,
            pl.BlockSpec((1, 1, D_FF, D_MODEL), pick),
            pl.BlockSpec((1, 1, 1, D_MODEL), pick),
        ],
        out_specs=pl.BlockSpec((MOE_BLOCK, D_MODEL), lambda i, be, nu: (i, 0)),
        scratch_shapes=[
            pltpu.VMEM((D_MODEL, D_FF), _BF16),
            pltpu.VMEM((D_MODEL, D_FF), _BF16),
            pltpu.VMEM((D_FF, D_MODEL), _BF16),
        ],
    )
    return pl.pallas_call(
        _expert_kernel,
        grid_spec=grid_spec,
        out_shape=jax.ShapeDtypeStruct((n_rows, D_MODEL), _F32),
        compiler_params=_params("arbitrary"),
        name="experts",
    )(block_expert, n_used, xb, w_up, bg_all, bu_all, w_down, bd_all)


def _combine_kernel(h_ref, yg_ref, gate_ref, lng_ref, lnb_ref, x_ref, xb_ref):
    gates = gate_ref[...]
    y = gates[:, 0:1] * yg_ref[0]
    for k in range(1, TOP_K):
        y = y + gates[:, k:k + 1] * yg_ref[k]
    xn = _layer_norm(DEEPNORM_ALPHA * h_ref[...] + y, lng_ref[0, 1:2, :], lnb_ref[0, 1:2, :])
    x_ref[...] = xn
    xb_ref[...] = xn.astype(_BF16)


def _combine(h, yg, gates, ln_g, ln_b, layer):
    t = h.shape[0]
    tm = LN2_TM
    row = lambda i: (i, 0)
    return pl.pallas_call(
        _combine_kernel,
        grid=(t // tm,),
        in_specs=[
            pl.BlockSpec((tm, D_MODEL), row),
            pl.BlockSpec((TOP_K, tm, D_MODEL), lambda i: (0, i, 0)),
            pl.BlockSpec((tm, LANES), row),
            pl.BlockSpec((1, 2, D_MODEL), lambda i: (layer, 0, 0)),
            pl.BlockSpec((1, 2, D_MODEL), lambda i: (layer, 0, 0)),
        ],
        out_specs=[pl.BlockSpec((tm, D_MODEL), row), pl.BlockSpec((tm, D_MODEL), row)],
        out_shape=[jax.ShapeDtypeStruct((t, D_MODEL), _F32), jax.ShapeDtypeStruct((t, D_MODEL), _BF16)],
        compiler_params=_params("parallel"),
        name="combine_ln",
    )(h, yg, gates, ln_g, ln_b)


def _dispatch_tables(top_idx, t):
    tk = t * TOP_K
    flat_e = top_idx.reshape(tk)
    onehot = (flat_e[:, None] == jnp.arange(N_EXPERTS, dtype=jnp.int32)[None, :]).astype(jnp.int32)
    csum = jnp.cumsum(onehot, axis=0)
    rank = jnp.sum(csum * onehot, axis=1) - 1
    counts = csum[-1]
    padded = (counts + MOE_BLOCK - 1) // MOE_BLOCK * MOE_BLOCK
    pad_end = jnp.cumsum(padded)
    pad_start = pad_end - padded
    dest = pad_start[flat_e] + rank
    n_blocks = (tk + N_EXPERTS * (MOE_BLOCK - 1) + MOE_BLOCK - 1) // MOE_BLOCK
    n_rows = n_blocks * MOE_BLOCK
    flat_tok = jnp.arange(tk, dtype=jnp.int32) // TOP_K
    buf_tok = jnp.zeros((n_rows,), jnp.int32).at[dest].set(flat_tok)
    block_expert = jnp.minimum(
        jnp.searchsorted(pad_end, jnp.arange(n_blocks, dtype=jnp.int32) * MOE_BLOCK, side='right'),
        N_EXPERTS - 1).astype(jnp.int32)
    n_used = (pad_end[-1:] // MOE_BLOCK).astype(jnp.int32)
    return buf_tok, dest.reshape(t, TOP_K), block_expert, n_used


def kernel(x, w_in, w_o, lambda_qk, subln_g, sinks, ln_g, ln_b, w_router, b_router, w_up, b_up, w_down, b_down):
    batch, seq, _ = x.shape
    t = batch * seq
    assert seq % ATT_TQ == 0 and seq % SWA_TB == 0 and t % PROJ_TM == 0

    va_lo, va_hi = 2 * D_MODEL, 3 * D_MODEL
    gates_lo = w_in.shape[-1] - GATE_COLS
    w_att_b = jnp.concatenate([w_in[..., :va_lo], w_in[..., va_hi:gates_lo]], axis=-1).astype(_BF16)
    w_vt_b = jnp.swapaxes(w_in[..., va_lo:va_hi], 1, 2).astype(_BF16)
    w_gates_b = w_in[..., gates_lo:].astype(_BF16)
    w_o_b = w_o.astype(_BF16)
    w_r_b = w_router.astype(_BF16)
    b_gate = b_up[..., 0::2].reshape(DEPTH, N_EXPERTS, 1, D_FF)
    b_upp = b_up[..., 1::2].reshape(DEPTH, N_EXPERTS, 1, D_FF)
    b_down4 = b_down.reshape(DEPTH, N_EXPERTS, 1, D_MODEL)
    b_router3 = b_router.reshape(DEPTH, 1, N_EXPERTS)

    att_scale = jnp.concatenate([
        jnp.full((D_MODEL,), DA_HEAD_DIM ** -0.5 * LOG2E, _F32), jnp.ones((D_MODEL,), _F32),
        jnp.full((D_MODEL,), SW_HEAD_DIM ** -0.5, _F32),
        jnp.ones((2 * SW_KV_HEADS * SW_HEAD_DIM,), _F32)]).reshape(1, ATT_COLS)
    gate_scale = jnp.ones((1, GATE_COLS), _F32)
    rest = 2.0 ** (-8.0 * jnp.arange(1, DA_HEADS + 1, dtype=_F32) / DA_HEADS) * LOG2E
    pieces = []
    for _ in range(N_SLOPE_PARTS):
        piece = rest.astype(_BF16).astype(_F32)
        pieces.append(piece)
        rest = rest - piece
    da_slopes = jnp.stack(pieces, axis=1)
    swa_bias = _swa_bias()

    xf = x.reshape(t, D_MODEL)
    xb = xf.astype(_BF16)
    for l in range(DEPTH):
        lam_init = 0.8 - 0.6 * math.exp(-0.3 * l)
        prm = jnp.concatenate([lambda_qk[l].astype(_F32), jnp.full((4, DA_HEAD_DIM), lam_init, _F32)], axis=0)
        att = _project(xb, w_att_b, l, ATT_COLS // 2, att_scale, _BF16)
        gates = _project(xb, w_gates_b, l, GATE_COLS // 2, gate_scale, _F32)
        vt = _project_vt(xb, w_vt_b, l)
        o_a = _diff_attention(att, vt, da_slopes, prm, subln_g[l].reshape(DA_WIDTH, 1), batch, seq)
        o_b = _sliding_window_attention(att, sinks[l], swa_bias, batch, seq)
        h, hb, top_idx, top_gate = _merge_out(gates, o_a, o_b, xf, w_o_b, ln_g, ln_b, w_r_b, b_router3, l)
        buf_tok, pos, block_expert, n_used = _dispatch_tables(top_idx[:, :TOP_K], t)
        x_rows = hb[buf_tok]
        yb = _experts(block_expert, n_used, x_rows, w_up, b_gate, b_upp, w_down, b_down4, l)
        yg = yb[pos.T]
        xf, xb = _combine(h, yg, top_gate, ln_g, ln_b, l)
    return xf.reshape(batch, seq, D_MODEL)
```

```python
import functools
import math

import jax
import jax.numpy as jnp
from jax import lax
from jax.experimental import pallas as pl
from jax.experimental.pallas import tpu as pltpu

D_MODEL = 1024
DEPTH = 4
DA_HEAD_DIM = 128
DA_HEADS = 4
DA_WIDTH = 2 * DA_HEAD_DIM
SW_HEAD_DIM = 64
SW_HEADS = 16
SW_KV_HEADS = 2
SW_GROUP = SW_HEADS // SW_KV_HEADS
WINDOW = 128
N_EXPERTS = 32
TOP_K = 4
D_FF = D_MODEL
SWIGLU_LIMIT = 7.0
SWIGLU_ALPHA = 1.702
LN_EPS = 1e-5
DEEPNORM_ALPHA = (2.0 * DEPTH) ** 0.25
NEG_INF = -1e30

QA_OFF, KA_OFF, QB_OFF, KVB_OFF = 0, 1024, 2048, 3072
ATT_COLS = 3328
GATE_COLS = 2 * D_MODEL
LANES = 128
SUBLANES_BF16 = 16
VT_ROWS = DA_WIDTH + SUBLANES_BF16
LOG2E = math.log2(math.e)
N_SLOPE_PARTS = 3

PROJ_TM = 512
ATT_TQ = 512
SWA_TB = 256
OUT_TM = 256
MOE_BLOCK = 256
DEINT_TILE = 256
LN2_TM = 256
VMEM_LIMIT = 56 * 1024 * 1024

_F32 = jnp.float32
_BF16 = jnp.bfloat16


def _params(*sem):
    return pltpu.CompilerParams(dimension_semantics=sem, vmem_limit_bytes=VMEM_LIMIT)


def _proj_kernel(x_ref, w_ref, s_ref, o_ref):
    acc = jnp.dot(x_ref[...], w_ref[0], preferred_element_type=_F32)
    o_ref[...] = (acc * s_ref[...]).astype(o_ref.dtype)


def _project(xb, w_all, layer, tn, scale, out_dtype):
    t = xb.shape[0]
    n_cols = w_all.shape[-1]
    assert n_cols % tn == 0 and t % PROJ_TM == 0
    return pl.pallas_call(
        _proj_kernel,
        grid=(n_cols // tn, t // PROJ_TM),
        in_specs=[
            pl.BlockSpec((PROJ_TM, D_MODEL), lambda j, i: (i, 0)),
            pl.BlockSpec((1, D_MODEL, tn), lambda j, i: (layer, 0, j)),
            pl.BlockSpec((1, tn), lambda j, i: (0, j)),
        ],
        out_specs=pl.BlockSpec((PROJ_TM, tn), lambda j, i: (i, j)),
        out_shape=jax.ShapeDtypeStruct((t, n_cols), out_dtype),
        compiler_params=_params("parallel", "parallel"),
        name="in_proj",
    )(xb, w_all, scale)


def _proj_vt_kernel(x_ref, wt_ref, o_ref):
    x = x_ref[...]
    ones = jnp.ones((SUBLANES_BF16, x.shape[0]), o_ref.dtype)
    for h in range(DA_HEADS):
        vt = lax.dot_general(wt_ref[0, h * DA_WIDTH:(h + 1) * DA_WIDTH, :], x,
                             (((1,), (1,)), ((), ())), preferred_element_type=_F32)
        o_ref[h * VT_ROWS:h * VT_ROWS + DA_WIDTH, :] = vt.astype(o_ref.dtype)
        o_ref[h * VT_ROWS + DA_WIDTH:(h + 1) * VT_ROWS, :] = ones


def _project_vt(xb, wt_all, layer):
    t = xb.shape[0]
    return pl.pallas_call(
        _proj_vt_kernel,
        grid=(t // PROJ_TM,),
        in_specs=[
            pl.BlockSpec((PROJ_TM, D_MODEL), lambda i: (i, 0)),
            pl.BlockSpec((1, D_MODEL, D_MODEL), lambda i: (layer, 0, 0)),
        ],
        out_specs=pl.BlockSpec((DA_HEADS * VT_ROWS, PROJ_TM), lambda i: (0, i)),
        out_shape=jax.ShapeDtypeStruct((DA_HEADS * VT_ROWS, t), _BF16),
        compiler_params=_params("parallel"),
        name="in_proj_vt",
    )(xb, wt_all)


def _diff_attn_kernel(slope_ref, prm_ref, g_ref, q_ref, k_ref, vt_ref, o_ref,
                      acc_ref, m_ref, qaug_ref, kaug_ref, s_ref, smax_ref, p_ref, alpha_ref, *, tq):
    h = pl.program_id(1)
    i = pl.program_id(2)
    lane = lax.broadcasted_iota(jnp.int32, (tq, DA_HEAD_DIM), 1)
    sub = lax.broadcasted_iota(jnp.int32, (tq, DA_HEAD_DIM), 0)
    slope_tile = jnp.zeros((tq, DA_HEAD_DIM), _F32)
    slope = 0.0
    for part in range(N_SLOPE_PARTS):
        piece = slope_ref[h, part]
        slope = slope + piece
        slope_tile = jnp.where(lane // 2 == part, piece, slope_tile)
    pos_tile = jnp.where(lane < 2 * N_SLOPE_PARTS, jnp.where(lane % 2 == 0, sub % 256, sub // 256 * 256), 0)
    for c in range(2):
        kaug_ref[c, :, DA_HEAD_DIM:] = pos_tile.astype(_F32).astype(_BF16)
        qaug_ref[c, :, :DA_HEAD_DIM] = q_ref[:, c * DA_HEAD_DIM:(c + 1) * DA_HEAD_DIM]
        qaug_ref[c, :, DA_HEAD_DIM:] = slope_tile.astype(_BF16)
    m_ref[...] = jnp.full(m_ref.shape, NEG_INF, _F32)
    acc_ref[...] = jnp.zeros(acc_ref.shape, _F32)

    def scores(j):
        start = pl.multiple_of(j * tq, tq)
        for c in range(2):
            kaug_ref[c, :, :DA_HEAD_DIM] = k_ref[pl.ds(start, tq), c * DA_HEAD_DIM:(c + 1) * DA_HEAD_DIM]
            s = lax.dot_general(kaug_ref[c], qaug_ref[c], (((1,), (1,)), ((), ())),
                                preferred_element_type=_F32)
            s_ref[c] = s
            smax_ref[c] = jnp.max(s, axis=0, keepdims=True)

    def weights(j, masked):
        off = ((j - i) * tq).astype(_F32) * slope
        for c in range(2):
            s = s_ref[c]
            if masked:
                keys = lax.broadcasted_iota(jnp.int32, (tq, tq), 0)
                queries = lax.broadcasted_iota(jnp.int32, (tq, tq), 1)
                s = jnp.where(keys <= queries, s, NEG_INF)
                smax = jnp.max(s, axis=0, keepdims=True)
            else:
                smax = smax_ref[c]
            m_old = m_ref[c]
            m_new = jnp.maximum(m_old, smax + off)
            alpha_ref[c] = jnp.exp2(m_old - m_new)
            p_ref[c] = jnp.exp2(s - (m_new - off)).astype(_BF16)
            m_ref[c] = m_new

    def accumulate(j):
        start = pl.multiple_of(j * tq, tq)
        vt = vt_ref[:, pl.ds(start, tq)]
        for c in range(2):
            acc_ref[c] = alpha_ref[c] * acc_ref[c] + jnp.dot(vt, p_ref[c], preferred_element_type=_F32)

    def body(j, carry):
        accumulate(j)
        weights(j + 1, False)
        scores(j + 2)
        return carry

    scores(0)

    @pl.when(i >= 1)
    def _():
        weights(0, False)
        scores(1)
        lax.fori_loop(0, i - 1, body, 0)
        accumulate(i - 1)

    weights(i, True)
    accumulate(i)

    prm = prm_ref[...]
    lam_init = prm[4:5, 0:1]
    lam = (jnp.exp(jnp.sum(prm[0:1] * prm[1:2], axis=1, keepdims=True))
           - jnp.exp(jnp.sum(prm[2:3] * prm[3:4], axis=1, keepdims=True)) + lam_init)
    o0 = acc_ref[0, :DA_WIDTH, :] / acc_ref[0, DA_WIDTH:DA_WIDTH + 1, :]
    o1 = acc_ref[1, :DA_WIDTH, :] / acc_ref[1, DA_WIDTH:DA_WIDTH + 1, :]
    a = o0 - lam * o1
    ms = jnp.mean(a * a, axis=0, keepdims=True)
    out = a * lax.rsqrt(ms + LN_EPS) * g_ref[...] * (1.0 - lam_init)
    o_ref[...] = out.T.astype(o_ref.dtype)


def _diff_attention(att, vt, slopes, prm, subln_g, batch, seq):
    tq = ATT_TQ
    nq = seq // tq
    t = batch * seq
    kernel = functools.partial(_diff_attn_kernel, tq=tq)
    return pl.pallas_call(
        kernel,
        grid=(batch, DA_HEADS, nq),
        in_specs=[
            pl.BlockSpec(memory_space=pltpu.SMEM),
            pl.BlockSpec((8, LANES), lambda b, h, i: (0, 0)),
            pl.BlockSpec((DA_WIDTH, 1), lambda b, h, i: (0, 0)),
            pl.BlockSpec((tq, DA_WIDTH), lambda b, h, i: (b * nq + i, QA_OFF // DA_WIDTH + h)),
            pl.BlockSpec((seq, DA_WIDTH), lambda b, h, i: (b, KA_OFF // DA_WIDTH + h)),
            pl.BlockSpec((VT_ROWS, seq), lambda b, h, i: (h, b)),
        ],
        out_specs=pl.BlockSpec((tq, DA_WIDTH), lambda b, h, i: (b * nq + i, h)),
        out_shape=jax.ShapeDtypeStruct((t, D_MODEL), _F32),
        scratch_shapes=[
            pltpu.VMEM((2, VT_ROWS, tq), _F32),
            pltpu.VMEM((2, 1, tq), _F32),
            pltpu.VMEM((2, tq, 2 * DA_HEAD_DIM), _BF16),
            pltpu.VMEM((2, tq, 2 * DA_HEAD_DIM), _BF16),
            pltpu.VMEM((2, tq, tq), _F32),
            pltpu.VMEM((2, 1, tq), _F32),
            pltpu.VMEM((2, tq, tq), _BF16),
            pltpu.VMEM((2, 1, tq), _F32),
        ],
        compiler_params=_params("parallel", "parallel", "arbitrary"),
        name="diff_attn",
    )(slopes, prm, subln_g, att, att, vt)


def _swa_kernel(sink_ref, bias_ref, q_ref, kv_ref, kvp_ref, o_ref):
    i = pl.program_id(1)
    kv_all = jnp.concatenate([kvp_ref[...], kv_ref[...]], axis=0)
    col = lax.broadcasted_iota(jnp.int32, (WINDOW, 2 * WINDOW), 1)
    no_prev = jnp.logical_and(i == 0, col < WINDOW)
    for r in range(SWA_TB // WINDOW):
        win = kv_all[r * WINDOW:(r + 2) * WINDOW, :]
        heads = []
        for head in range(SW_HEADS):
            kh = head // SW_GROUP
            q = q_ref[r * WINDOW:(r + 1) * WINDOW, head * SW_HEAD_DIM:(head + 1) * SW_HEAD_DIM]
            k = win[:, kh * SW_HEAD_DIM:(kh + 1) * SW_HEAD_DIM]
            v = win[:, (SW_KV_HEADS + kh) * SW_HEAD_DIM:(SW_KV_HEADS + kh + 1) * SW_HEAD_DIM]
            s = lax.dot_general(q, k, (((1,), (1,)), ((), ())), preferred_element_type=_F32)
            bias = bias_ref[head]
            if r == 0:
                bias = jnp.where(no_prev, NEG_INF, bias)
            s = s + bias
            sink = sink_ref[head]
            m = jnp.maximum(jnp.max(s, axis=1, keepdims=True), sink)
            p = jnp.exp(s - m)
            denom = jnp.sum(p, axis=1, keepdims=True) + jnp.exp(sink - m)
            o = jnp.dot(p.astype(_BF16), v, preferred_element_type=_F32)
            heads.append(o / denom)
        o_ref[r * WINDOW:(r + 1) * WINDOW, :] = jnp.concatenate(heads, axis=1).astype(o_ref.dtype)


def _swa_bias():
    slopes = 2.0 ** (-8.0 * jnp.arange(1, SW_HEADS + 1, dtype=_F32) / SW_HEADS)
    qi = jnp.arange(WINDOW, dtype=jnp.int32)[:, None]
    kj = jnp.arange(2 * WINDOW, dtype=jnp.int32)[None, :]
    dist = qi + WINDOW - kj
    valid = (dist >= 0) & (dist < WINDOW)
    bias = -slopes[:, None, None] * dist.astype(_F32)[None]
    return jnp.where(valid[None], bias, NEG_INF)


def _sliding_window_attention(att, sinks_l, bias, batch, seq):
    tb = SWA_TB
    nb = seq // tb
    ratio = tb // WINDOW
    t = batch * seq
    kv_width = 2 * SW_KV_HEADS * SW_HEAD_DIM
    q_blk = QB_OFF // D_MODEL
    kv_blk = KVB_OFF // kv_width
    return pl.pallas_call(
        _swa_kernel,
        grid=(batch, nb),
        in_specs=[
            pl.BlockSpec(memory_space=pltpu.SMEM),
            pl.BlockSpec((SW_HEADS, WINDOW, 2 * WINDOW), lambda b, i: (0, 0, 0)),
            pl.BlockSpec((tb, D_MODEL), lambda b, i: (b * nb + i, q_blk)),
            pl.BlockSpec((tb, kv_width), lambda b, i: (b * nb + i, kv_blk)),
            pl.BlockSpec((WINDOW, kv_width), lambda b, i: (jnp.maximum((b * nb + i) * ratio - 1, 0), kv_blk)),
        ],
        out_specs=pl.BlockSpec((tb, D_MODEL), lambda b, i: (b * nb + i, 0)),
        out_shape=jax.ShapeDtypeStruct((t, D_MODEL), _F32),
        compiler_params=_params("parallel", "arbitrary"),
        name="swa",
    )(sinks_l, bias, att, att, att)


def _layer_norm(z, g, b):
    mu = jnp.mean(z, axis=1, keepdims=True)
    d = z - mu
    var = jnp.mean(d * d, axis=1, keepdims=True)
    return d * lax.rsqrt(var + LN_EPS) * g + b


def _merge_out_kernel(ga_ref, gb_ref, oa_ref, ob_ref, x_ref, wo_ref, lng_ref, lnb_ref, wr_ref, br_ref,
                      h_ref, hb_ref, idx_ref, gate_ref):
    merged = jax.nn.sigmoid(ga_ref[...]) * oa_ref[...] + jax.nn.sigmoid(gb_ref[...]) * ob_ref[...]
    y = jnp.dot(merged.astype(_BF16), wo_ref[0], preferred_element_type=_F32)
    hn = _layer_norm(DEEPNORM_ALPHA * x_ref[...] + y, lng_ref[0, 0:1, :], lnb_ref[0, 0:1, :])
    h_ref[...] = hn
    hb = hn.astype(_BF16)
    hb_ref[...] = hb
    logits = jnp.dot(hb, wr_ref[0], preferred_element_type=_F32) + br_ref[0]
    tm = logits.shape[0]
    ecol = lax.broadcasted_iota(jnp.int32, (tm, N_EXPERTS), 1)
    lane = lax.broadcasted_iota(jnp.int32, (tm, LANES), 1)
    cur = logits
    vals, idx_out = [], jnp.zeros((tm, LANES), jnp.int32)
    for k in range(TOP_K):
        mx = jnp.max(cur, axis=1, keepdims=True)
        ix = jnp.min(jnp.where(cur == mx, ecol, N_EXPERTS), axis=1, keepdims=True)
        vals.append(mx)
        idx_out = jnp.where(lane == k, ix, idx_out)
        cur = jnp.where(ecol == ix, -jnp.inf, cur)
    exps = [jnp.exp(v - vals[0]) for v in vals]
    tot = exps[0] + exps[1] + exps[2] + exps[3]
    gate_out = jnp.zeros((tm, LANES), _F32)
    for k in range(TOP_K):
        gate_out = jnp.where(lane == k, exps[k] / tot, gate_out)
    idx_ref[...] = idx_out
    gate_ref[...] = gate_out


def _merge_out(gates, o_a, o_b, x, wo_all, ln_g, ln_b, wr_all, br_all, layer):
    t = x.shape[0]
    tm = OUT_TM
    row = lambda i: (i, 0)
    return pl.pallas_call(
        _merge_out_kernel,
        grid=(t // tm,),
        in_specs=[
            pl.BlockSpec((tm, D_MODEL), lambda i: (i, 0)),
            pl.BlockSpec((tm, D_MODEL), lambda i: (i, 1)),
            pl.BlockSpec((tm, D_MODEL), row),
            pl.BlockSpec((tm, D_MODEL), row),
            pl.BlockSpec((tm, D_MODEL), row),
            pl.BlockSpec((1, D_MODEL, D_MODEL), lambda i: (layer, 0, 0)),
            pl.BlockSpec((1, 2, D_MODEL), lambda i: (layer, 0, 0)),
            pl.BlockSpec((1, 2, D_MODEL), lambda i: (layer, 0, 0)),
            pl.BlockSpec((1, D_MODEL, N_EXPERTS), lambda i: (layer, 0, 0)),
            pl.BlockSpec((1, 1, N_EXPERTS), lambda i: (layer, 0, 0)),
        ],
        out_specs=[
            pl.BlockSpec((tm, D_MODEL), row),
            pl.BlockSpec((tm, D_MODEL), row),
            pl.BlockSpec((tm, LANES), row),
            pl.BlockSpec((tm, LANES), row),
        ],
        out_shape=[
            jax.ShapeDtypeStruct((t, D_MODEL), _F32),
            jax.ShapeDtypeStruct((t, D_MODEL), _BF16),
            jax.ShapeDtypeStruct((t, LANES), jnp.int32),
            jax.ShapeDtypeStruct((t, LANES), _F32),
        ],
        compiler_params=_params("parallel"),
        name="merge_out_ln_router",
    )(gates, gates, o_a, o_b, x, wo_all, ln_g, ln_b, wr_all, br_all)


def _expert_kernel(be_ref, nused_ref, x_ref, wup_ref, bg_ref, bu_ref, wd_ref, bd_ref, y_ref,
                   wg_scr, wu_scr, wd_scr):
    i = pl.program_id(0)
    changed = jnp.logical_or(i == 0, be_ref[i] != be_ref[jnp.maximum(i - 1, 0)])

    @pl.when(jnp.logical_and(changed, i < nused_ref[0]))
    def _():
        half = DEINT_TILE // 2
        r = lax.broadcasted_iota(jnp.int32, (DEINT_TILE, DEINT_TILE), 0)
        c = lax.broadcasted_iota(jnp.int32, (DEINT_TILE, DEINT_TILE), 1)
        sel = jnp.where(r == jnp.where(c < half, 2 * c, 2 * (c - half) + 1), 1.0, 0.0).astype(_BF16)
        for t in range(2 * D_FF // DEINT_TILE):
            w = wup_ref[0, 0, :, t * DEINT_TILE:(t + 1) * DEINT_TILE].astype(_BF16)
            de = jnp.dot(w, sel, preferred_element_type=_F32).astype(_BF16)
            wg_scr[:, t * half:(t + 1) * half] = de[:, :half]
            wu_scr[:, t * half:(t + 1) * half] = de[:, half:]
        wd_scr[...] = wd_ref[0, 0].astype(_BF16)

    @pl.when(i < nused_ref[0])
    def _():
        x = x_ref[...]
        gate = jnp.dot(x, wg_scr[...], preferred_element_type=_F32) + bg_ref[0, 0]
        up = jnp.dot(x, wu_scr[...], preferred_element_type=_F32) + bu_ref[0, 0]
        gate = jnp.minimum(gate, SWIGLU_LIMIT)
        up = jnp.clip(up, -SWIGLU_LIMIT, SWIGLU_LIMIT)
        act = (up + 1.0) * (gate * jax.nn.sigmoid(SWIGLU_ALPHA * gate))
        y = jnp.dot(act.astype(_BF16), wd_scr[...], preferred_element_type=_F32) + bd_ref[0, 0]
        y_ref[...] = y.astype(y_ref.dtype)

    @pl.when(i >= nused_ref[0])
    def _():
        y_ref[...] = jnp.zeros(y_ref.shape, y_ref.dtype)


def _experts(block_expert, n_used, xb, w_up, bg_all, bu_all, w_down, bd_all, layer):
    n_rows = xb.shape[0]
    n_blocks = n_rows // MOE_BLOCK
    pick = lambda i, be, nu: (layer, be[i], 0, 0)
    bspec = pl.BlockSpec((1, 1, 1, D_FF), pick)
    grid_spec = pltpu.PrefetchScalarGridSpec(
        num_scalar_prefetch=2,
        grid=(n_blocks,),
        in_specs=[
            pl.BlockSpec((MOE_BLOCK, D_MODEL), lambda i, be, nu: (i, 0)),
            pl.BlockSpec((1, 1, D_MODEL, 2 * D_FF), pick),
            bspec, bspec,
            pl.BlockSpec((1, 1, D_FF, D_MODEL), pick),
            pl.BlockSpec((1, 1, 1, D_MODEL), pick),
        ],
        out_specs=pl.BlockSpec((MOE_BLOCK, D_MODEL), lambda i, be, nu: (i, 0)),
        scratch_shapes=[
            pltpu.VMEM((D_MODEL, D_FF), _BF16),
            pltpu.VMEM((D_MODEL, D_FF), _BF16),
            pltpu.VMEM((D_FF, D_MODEL), _BF16),
        ],
    )
    return pl.pallas_call(
        _expert_kernel,
        grid_spec=grid_spec,
        out_shape=jax.ShapeDtypeStruct((n_rows, D_MODEL), _F32),
        compiler_params=_params("arbitrary"),
        name="experts",
    )(block_expert, n_used, xb, w_up, bg_all, bu_all, w_down, bd_all)


def _combine_kernel(h_ref, yg_ref, gate_ref, lng_ref, lnb_ref, x_ref, xb_ref):
    gates = gate_ref[...]
    y = gates[:, 0:1] * yg_ref[0]
    for k in range(1, TOP_K):
        y = y + gates[:, k:k + 1] * yg_ref[k]
    xn = _layer_norm(DEEPNORM_ALPHA * h_ref[...] + y, lng_ref[0, 1:2, :], lnb_ref[0, 1:2, :])
    x_ref[...] = xn
    xb_ref[...] = xn.astype(_BF16)


def _combine(h, yg, gates, ln_g, ln_b, layer):
    t = h.shape[0]
    tm = LN2_TM
    row = lambda i: (i, 0)
    return pl.pallas_call(
        _combine_kernel,
        grid=(t // tm,),
        in_specs=[
            pl.BlockSpec((tm, D_MODEL), row),
            pl.BlockSpec((TOP_K, tm, D_MODEL), lambda i: (0, i, 0)),
            pl.BlockSpec((tm, LANES), row),
            pl.BlockSpec((1, 2, D_MODEL), lambda i: (layer, 0, 0)),
            pl.BlockSpec((1, 2, D_MODEL), lambda i: (layer, 0, 0)),
        ],
        out_specs=[pl.BlockSpec((tm, D_MODEL), row), pl.BlockSpec((tm, D_MODEL), row)],
        out_shape=[jax.ShapeDtypeStruct((t, D_MODEL), _F32), jax.ShapeDtypeStruct((t, D_MODEL), _BF16)],
        compiler_params=_params("parallel"),
        name="combine_ln",
    )(h, yg, gates, ln_g, ln_b)


def _dispatch_tables(top_idx, t):
    tk = t * TOP_K
    flat_e = top_idx.reshape(tk)
    onehot = (flat_e[:, None] == jnp.arange(N_EXPERTS, dtype=jnp.int32)[None, :]).astype(jnp.int32)
    csum = jnp.cumsum(onehot, axis=0)
    rank = jnp.sum(csum * onehot, axis=1) - 1
    counts = csum[-1]
    padded = (counts + MOE_BLOCK - 1) // MOE_BLOCK * MOE_BLOCK
    pad_end = jnp.cumsum(padded)
    pad_start = pad_end - padded
    dest = pad_start[flat_e] + rank
    n_blocks = (tk + N_EXPERTS * (MOE_BLOCK - 1) + MOE_BLOCK - 1) // MOE_BLOCK
    n_rows = n_blocks * MOE_BLOCK
    flat_tok = jnp.arange(tk, dtype=jnp.int32) // TOP_K
    buf_tok = jnp.zeros((n_rows,), jnp.int32).at[dest].set(flat_tok)
    block_expert = jnp.minimum(
        jnp.searchsorted(pad_end, jnp.arange(n_blocks, dtype=jnp.int32) * MOE_BLOCK, side='right'),
        N_EXPERTS - 1).astype(jnp.int32)
    n_used = (pad_end[-1:] // MOE_BLOCK).astype(jnp.int32)
    return buf_tok, dest.reshape(t, TOP_K), block_expert, n_used


def kernel(x, w_in, w_o, lambda_qk, subln_g, sinks, ln_g, ln_b, w_router, b_router, w_up, b_up, w_down, b_down):
    batch, seq, _ = x.shape
    t = batch * seq
    assert seq % ATT_TQ == 0 and seq % SWA_TB == 0 and t % PROJ_TM == 0

    va_lo, va_hi = 2 * D_MODEL, 3 * D_MODEL
    gates_lo = w_in.shape[-1] - GATE_COLS
    w_att_b = jnp.concatenate([w_in[..., :va_lo], w_in[..., va_hi:gates_lo]], axis=-1).astype(_BF16)
    w_vt_b = jnp.swapaxes(w_in[..., va_lo:va_hi], 1, 2).astype(_BF16)
    w_gates_b = w_in[..., gates_lo:].astype(_BF16)
    w_o_b = w_o.astype(_BF16)
    w_r_b = w_router.astype(_BF16)
    b_gate = b_up[..., 0::2].reshape(DEPTH, N_EXPERTS, 1, D_FF)
    b_upp = b_up[..., 1::2].reshape(DEPTH, N_EXPERTS, 1, D_FF)
    b_down4 = b_down.reshape(DEPTH, N_EXPERTS, 1, D_MODEL)
    b_router3 = b_router.reshape(DEPTH, 1, N_EXPERTS)

    att_scale = jnp.concatenate([
        jnp.full((D_MODEL,), DA_HEAD_DIM ** -0.5 * LOG2E, _F32), jnp.ones((D_MODEL,), _F32),
        jnp.full((D_MODEL,), SW_HEAD_DIM ** -0.5, _F32),
        jnp.ones((2 * SW_KV_HEADS * SW_HEAD_DIM,), _F32)]).reshape(1, ATT_COLS)
    gate_scale = jnp.ones((1, GATE_COLS), _F32)
    rest = 2.0 ** (-8.0 * jnp.arange(1, DA_HEADS + 1, dtype=_F32) / DA_HEADS) * LOG2E
    pieces = []
    for _ in range(N_SLOPE_PARTS):
        piece = rest.astype(_BF16).astype(_F32)
        pieces.append(piece)
        rest = rest - piece
    da_slopes = jnp.stack(pieces, axis=1)
    swa_bias = _swa_bias()

    xf = x.reshape(t, D_MODEL)
    xb = xf.astype(_BF16)
    for l in range(DEPTH):
        lam_init = 0.8 - 0.6 * math.exp(-0.3 * l)
        prm = jnp.concatenate([lambda_qk[l].astype(_F32), jnp.full((4, DA_HEAD_DIM), lam_init, _F32)], axis=0)
        att = _project(xb, w_att_b, l, ATT_COLS // 2, att_scale, _BF16)
        gates = _project(xb, w_gates_b, l, GATE_COLS // 2, gate_scale, _F32)
        vt = _project_vt(xb, w_vt_b, l)
        o_a = _diff_attention(att, vt, da_slopes, prm, subln_g[l].reshape(DA_WIDTH, 1), batch, seq)
        o_b = _sliding_window_attention(att, sinks[l], swa_bias, batch, seq)
        h, hb, top_idx, top_gate = _merge_out(gates, o_a, o_b, xf, w_o_b, ln_g, ln_b, w_r_b, b_router3, l)
        buf_tok, pos, block_expert, n_used = _dispatch_tables(top_idx[:, :TOP_K], t)
        x_rows = hb[buf_tok]
        yb = _experts(block_expert, n_used, x_rows, w_up, b_gate, b_upp, w_down, b_down4, l)
        yg = yb[pos.T]
        xf, xb = _combine(h, yg, top_gate, ln_g, ln_b, l)
    return xf.reshape(batch, seq, D_MODEL)
```

```python
import functools
import math

import jax
import jax.numpy as jnp
from jax import lax
from jax.experimental import pallas as pl
from jax.experimental.pallas import tpu as pltpu

D_MODEL = 1024
DEPTH = 4
DA_HEAD_DIM = 128
DA_HEADS = 4
DA_WIDTH = 2 * DA_HEAD_DIM
SW_HEAD_DIM = 64
SW_HEADS = 16
SW_KV_HEADS = 2
SW_GROUP = SW_HEADS // SW_KV_HEADS
WINDOW = 128
N_EXPERTS = 32
TOP_K = 4
D_FF = D_MODEL
SWIGLU_LIMIT = 7.0
SWIGLU_ALPHA = 1.702
LN_EPS = 1e-5
DEEPNORM_ALPHA = (2.0 * DEPTH) ** 0.25
NEG_INF = -1e30

QA_OFF, KA_OFF, QB_OFF, KVB_OFF = 0, 1024, 2048, 3072
ATT_COLS = 3328
GATE_COLS = 2 * D_MODEL
LANES = 128
SUBLANES_BF16 = 16
VT_ROWS = DA_WIDTH + SUBLANES_BF16
LOG2E = math.log2(math.e)
N_SLOPE_PARTS = 3

PROJ_TM = 512
ATT_TQ = 512
SWA_TB = 256
OUT_TM = 256
MOE_BLOCK = 256
DEINT_TILE = 256
LN2_TM = 256
VMEM_LIMIT = 56 * 1024 * 1024

_F32 = jnp.float32
_BF16 = jnp.bfloat16


def _params(*sem):
    return pltpu.CompilerParams(dimension_semantics=sem, vmem_limit_bytes=VMEM_LIMIT)


def _proj_kernel(x_ref, w_ref, s_ref, o_ref):
    acc = jnp.dot(x_ref[...], w_ref[0], preferred_element_type=_F32)
    o_ref[...] = (acc * s_ref[...]).astype(o_ref.dtype)


def _project(xb, w_all, layer, tn, scale, out_dtype):
    t = xb.shape[0]
    n_cols = w_all.shape[-1]
    assert n_cols % tn == 0 and t % PROJ_TM == 0
    return pl.pallas_call(
        _proj_kernel,
        grid=(n_cols // tn, t // PROJ_TM),
        in_specs=[
            pl.BlockSpec((PROJ_TM, D_MODEL), lambda j, i: (i, 0)),
            pl.BlockSpec((1, D_MODEL, tn), lambda j, i: (layer, 0, j)),
            pl.BlockSpec((1, tn), lambda j, i: (0, j)),
        ],
        out_specs=pl.BlockSpec((PROJ_TM, tn), lambda j, i: (i, j)),
        out_shape=jax.ShapeDtypeStruct((t, n_cols), out_dtype),
        compiler_params=_params("parallel", "parallel"),
        name="in_proj",
    )(xb, w_all, scale)


def _proj_vt_kernel(x_ref, wt_ref, o_ref):
    x = x_ref[...]
    ones = jnp.ones((SUBLANES_BF16, x.shape[0]), o_ref.dtype)
    for h in range(DA_HEADS):
        vt = lax.dot_general(wt_ref[0, h * DA_WIDTH:(h + 1) * DA_WIDTH, :], x,
                             (((1,), (1,)), ((), ())), preferred_element_type=_F32)
        o_ref[h * VT_ROWS:h * VT_ROWS + DA_WIDTH, :] = vt.astype(o_ref.dtype)
        o_ref[h * VT_ROWS + DA_WIDTH:(h + 1) * VT_ROWS, :] = ones


def _project_vt(xb, wt_all, layer):
    t = xb.shape[0]
    return pl.pallas_call(
        _proj_vt_kernel,
        grid=(t // PROJ_TM,),
        in_specs=[
            pl.BlockSpec((PROJ_TM, D_MODEL), lambda i: (i, 0)),
            pl.BlockSpec((1, D_MODEL, D_MODEL), lambda i: (layer, 0, 0)),
        ],
        out_specs=pl.BlockSpec((DA_HEADS * VT_ROWS, PROJ_TM), lambda i: (0, i)),
        out_shape=jax.ShapeDtypeStruct((DA_HEADS * VT_ROWS, t), _BF16),
        compiler_params=_params("parallel"),
        name="in_proj_vt",
    )(xb, wt_all)


def _diff_attn_kernel(slope_ref, prm_ref, g_ref, q_ref, k_ref, vt_ref, o_ref,
                      acc_ref, m_ref, qaug_ref, kaug_ref, s_ref, smax_ref, p_ref, alpha_ref, *, tq):
    h = pl.program_id(1)
    i = pl.program_id(2)
    lane = lax.broadcasted_iota(jnp.int32, (tq, DA_HEAD_DIM), 1)
    sub = lax.broadcasted_iota(jnp.int32, (tq, DA_HEAD_DIM), 0)
    slope_tile = jnp.zeros((tq, DA_HEAD_DIM), _F32)
    slope = 0.0
    for part in range(N_SLOPE_PARTS):
        piece = slope_ref[h, part]
        slope = slope + piece
        slope_tile = jnp.where(lane // 2 == part, piece, slope_tile)
    pos_tile = jnp.where(lane < 2 * N_SLOPE_PARTS, jnp.where(lane % 2 == 0, sub % 256, sub // 256 * 256), 0)
    for c in range(2):
        kaug_ref[c, :, DA_HEAD_DIM:] = pos_tile.astype(_F32).astype(_BF16)
        qaug_ref[c, :, :DA_HEAD_DIM] = q_ref[:, c * DA_HEAD_DIM:(c + 1) * DA_HEAD_DIM]
        qaug_ref[c, :, DA_HEAD_DIM:] = slope_tile.astype(_BF16)
    m_ref[...] = jnp.full(m_ref.shape, NEG_INF, _F32)
    acc_ref[...] = jnp.zeros(acc_ref.shape, _F32)

    def scores(j):
        start = pl.multiple_of(j * tq, tq)
        for c in range(2):
            kaug_ref[c, :, :DA_HEAD_DIM] = k_ref[pl.ds(start, tq), c * DA_HEAD_DIM:(c + 1) * DA_HEAD_DIM]
            s = lax.dot_general(kaug_ref[c], qaug_ref[c], (((1,), (1,)), ((), ())),
                                preferred_element_type=_F32)
            s_ref[c] = s
            smax_ref[c] = jnp.max(s, axis=0, keepdims=True)

    def weights(j, masked):
        off = ((j - i) * tq).astype(_F32) * slope
        for c in range(2):
            s = s_ref[c]
            if masked:
                keys = lax.broadcasted_iota(jnp.int32, (tq, tq), 0)
                queries = lax.broadcasted_iota(jnp.int32, (tq, tq), 1)
                s = jnp.where(keys <= queries, s, NEG_INF)
                smax = jnp.max(s, axis=0, keepdims=True)
            else:
                smax = smax_ref[c]
            m_old = m_ref[c]
            m_new = jnp.maximum(m_old, smax + off)
            alpha_ref[c] = jnp.exp2(m_old - m_new)
            p_ref[c] = jnp.exp2(s - (m_new - off)).astype(_BF16)
            m_ref[c] = m_new

    def accumulate(j):
        start = pl.multiple_of(j * tq, tq)
        vt = vt_ref[:, pl.ds(start, tq)]
        for c in range(2):
            acc_ref[c] = alpha_ref[c] * acc_ref[c] + jnp.dot(vt, p_ref[c], preferred_element_type=_F32)

    def body(j, carry):
        accumulate(j)
        weights(j + 1, False)
        scores(j + 2)
        return carry

    scores(0)

    @pl.when(i >= 1)
    def _():
        weights(0, False)
        scores(1)
        lax.fori_loop(0, i - 1, body, 0)
        accumulate(i - 1)

    weights(i, True)
    accumulate(i)

    prm = prm_ref[...]
    lam_init = prm[4:5, 0:1]
    lam = (jnp.exp(jnp.sum(prm[0:1] * prm[1:2], axis=1, keepdims=True))
           - jnp.exp(jnp.sum(prm[2:3] * prm[3:4], axis=1, keepdims=True)) + lam_init)
    o0 = acc_ref[0, :DA_WIDTH, :] / acc_ref[0, DA_WIDTH:DA_WIDTH + 1, :]
    o1 = acc_ref[1, :DA_WIDTH, :] / acc_ref[1, DA_WIDTH:DA_WIDTH + 1, :]
    a = o0 - lam * o1
    ms = jnp.mean(a * a, axis=0, keepdims=True)
    out = a * lax.rsqrt(ms + LN_EPS) * g_ref[...] * (1.0 - lam_init)
    o_ref[...] = out.T.astype(o_ref.dtype)


def _diff_attention(att, vt, slopes, prm, subln_g, batch, seq):
    tq = ATT_TQ
    nq = seq // tq
    t = batch * seq
    kernel = functools.partial(_diff_attn_kernel, tq=tq)
    return pl.pallas_call(
        kernel,
        grid=(batch, DA_HEADS, nq),
        in_specs=[
            pl.BlockSpec(memory_space=pltpu.SMEM),
            pl.BlockSpec((8, LANES), lambda b, h, i: (0, 0)),
            pl.BlockSpec((DA_WIDTH, 1), lambda b, h, i: (0, 0)),
            pl.BlockSpec((tq, DA_WIDTH), lambda b, h, i: (b * nq + i, QA_OFF // DA_WIDTH + h)),
            pl.BlockSpec((seq, DA_WIDTH), lambda b, h, i: (b, KA_OFF // DA_WIDTH + h)),
            pl.BlockSpec((VT_ROWS, seq), lambda b, h, i: (h, b)),
        ],
        out_specs=pl.BlockSpec((tq, DA_WIDTH), lambda b, h, i: (b * nq + i, h)),
        out_shape=jax.ShapeDtypeStruct((t, D_MODEL), _BF16),
        scratch_shapes=[
            pltpu.VMEM((2, VT_ROWS, tq), _F32),
            pltpu.VMEM((2, 1, tq), _F32),
            pltpu.VMEM((2, tq, 2 * DA_HEAD_DIM), _BF16),
            pltpu.VMEM((2, tq, 2 * DA_HEAD_DIM), _BF16),
            pltpu.VMEM((2, tq, tq), _F32),
            pltpu.VMEM((2, 1, tq), _F32),
            pltpu.VMEM((2, tq, tq), _BF16),
            pltpu.VMEM((2, 1, tq), _F32),
        ],
        compiler_params=_params("parallel", "parallel", "arbitrary"),
        name="diff_attn",
    )(slopes, prm, subln_g, att, att, vt)


def _swa_kernel(sink_ref, bias_ref, q_ref, kv_ref, kvp_ref, o_ref):
    i = pl.program_id(1)
    kv_all = jnp.concatenate([kvp_ref[...], kv_ref[...]], axis=0)
    col = lax.broadcasted_iota(jnp.int32, (WINDOW, 2 * WINDOW), 1)
    no_prev = jnp.logical_and(i == 0, col < WINDOW)
    for r in range(SWA_TB // WINDOW):
        win = kv_all[r * WINDOW:(r + 2) * WINDOW, :]
        heads = []
        for head in range(SW_HEADS):
            kh = head // SW_GROUP
            q = q_ref[r * WINDOW:(r + 1) * WINDOW, head * SW_HEAD_DIM:(head + 1) * SW_HEAD_DIM]
            k = win[:, kh * SW_HEAD_DIM:(kh + 1) * SW_HEAD_DIM]
            v = win[:, (SW_KV_HEADS + kh) * SW_HEAD_DIM:(SW_KV_HEADS + kh + 1) * SW_HEAD_DIM]
            s = lax.dot_general(q, k, (((1,), (1,)), ((), ())), preferred_element_type=_F32)
            bias = bias_ref[head]
            if r == 0:
                bias = jnp.where(no_prev, NEG_INF, bias)
            s = s + bias
            sink = sink_ref[head]
            m = jnp.maximum(jnp.max(s, axis=1, keepdims=True), sink)
            p = jnp.exp(s - m)
            denom = jnp.sum(p, axis=1, keepdims=True) + jnp.exp(sink - m)
            o = jnp.dot(p.astype(_BF16), v, preferred_element_type=_F32)
            heads.append(o / denom)
        o_ref[r * WINDOW:(r + 1) * WINDOW, :] = jnp.concatenate(heads, axis=1).astype(o_ref.dtype)


def _swa_bias():
    slopes = 2.0 ** (-8.0 * jnp.arange(1, SW_HEADS + 1, dtype=_F32) / SW_HEADS)
    qi = jnp.arange(WINDOW, dtype=jnp.int32)[:, None]
    kj = jnp.arange(2 * WINDOW, dtype=jnp.int32)[None, :]
    dist = qi + WINDOW - kj
    valid = (dist >= 0) & (dist < WINDOW)
    bias = -slopes[:, None, None] * dist.astype(_F32)[None]
    return jnp.where(valid[None], bias, NEG_INF)


def _sliding_window_attention(att, sinks_l, bias, batch, seq):
    tb = SWA_TB
    nb = seq // tb
    ratio = tb // WINDOW
    t = batch * seq
    kv_width = 2 * SW_KV_HEADS * SW_HEAD_DIM
    q_blk = QB_OFF // D_MODEL
    kv_blk = KVB_OFF // kv_width
    return pl.pallas_call(
        _swa_kernel,
        grid=(batch, nb),
        in_specs=[
            pl.BlockSpec(memory_space=pltpu.SMEM),
            pl.BlockSpec((SW_HEADS, WINDOW, 2 * WINDOW), lambda b, i: (0, 0, 0)),
            pl.BlockSpec((tb, D_MODEL), lambda b, i: (b * nb + i, q_blk)),
            pl.BlockSpec((tb, kv_width), lambda b, i: (b * nb + i, kv_blk)),
            pl.BlockSpec((WINDOW, kv_width), lambda b, i: (jnp.maximum((b * nb + i) * ratio - 1, 0), kv_blk)),
        ],
        out_specs=pl.BlockSpec((tb, D_MODEL), lambda b, i: (b * nb + i, 0)),
        out_shape=jax.ShapeDtypeStruct((t, D_MODEL), _BF16),
        compiler_params=_params("parallel", "arbitrary"),
        name="swa",
    )(sinks_l, bias, att, att, att)


def _layer_norm(z, g, b):
    mu = jnp.mean(z, axis=1, keepdims=True)
    d = z - mu
    var = jnp.mean(d * d, axis=1, keepdims=True)
    return d * lax.rsqrt(var + LN_EPS) * g + b


def _merge_out_kernel(ga_ref, gb_ref, oa_ref, ob_ref, x_ref, wo_ref, lng_ref, lnb_ref, wr_ref, br_ref,
                      h_ref, hb_ref, idx_ref, gate_ref, cnt_ref, run_ref):
    @pl.when(pl.program_id(0) == 0)
    def _():
        run_ref[...] = jnp.zeros(run_ref.shape, _F32)

    merged = (jax.nn.sigmoid(ga_ref[...].astype(_F32)) * oa_ref[...].astype(_F32)
              + jax.nn.sigmoid(gb_ref[...].astype(_F32)) * ob_ref[...].astype(_F32))
    y = jnp.dot(merged.astype(_BF16), wo_ref[0], preferred_element_type=_F32)
    hn = _layer_norm(DEEPNORM_ALPHA * x_ref[...] + y, lng_ref[0, 0:1, :], lnb_ref[0, 0:1, :])
    h_ref[...] = hn
    hb = hn.astype(_BF16)
    hb_ref[...] = hb
    logits = jnp.dot(hb, wr_ref[0], preferred_element_type=_F32) + br_ref[0]
    tm = logits.shape[0]
    ecol = lax.broadcasted_iota(jnp.int32, (tm, N_EXPERTS), 1)
    lane = lax.broadcasted_iota(jnp.int32, (tm, LANES), 1)
    cur = logits
    vals, picks, idx_out = [], [], jnp.zeros((tm, LANES), jnp.int32)
    for k in range(TOP_K):
        mx = jnp.max(cur, axis=1, keepdims=True)
        ix = jnp.min(jnp.where(cur == mx, ecol, N_EXPERTS), axis=1, keepdims=True)
        vals.append(mx)
        picks.append(ecol == ix)
        idx_out = jnp.where(lane == k, ix, idx_out)
        cur = jnp.where(picks[k], -jnp.inf, cur)
    exps = [jnp.exp(v - vals[0]) for v in vals]
    tot = exps[0] + exps[1] + exps[2] + exps[3]
    gate_out = jnp.zeros((tm, LANES), _F32)
    for k in range(TOP_K):
        gate_out = jnp.where(lane == k, exps[k] / tot, gate_out)
    gate_ref[...] = gate_out

    chosen = jnp.zeros((tm, N_EXPERTS), _F32)
    for k in range(TOP_K):
        chosen = chosen + jnp.where(picks[k], 1.0, 0.0)
    r = lax.broadcasted_iota(jnp.int32, (tm, tm), 0)
    c = lax.broadcasted_iota(jnp.int32, (tm, tm), 1)
    earlier = jnp.where(c < r, 1.0, 0.0).astype(_BF16)
    before = jnp.dot(earlier, chosen.astype(_BF16), preferred_element_type=_F32) + run_ref[...]
    for k in range(TOP_K):
        rank = jnp.sum(jnp.where(picks[k], before, 0.0), axis=1, keepdims=True)
        idx_out = jnp.where(lane == TOP_K + k, rank.astype(jnp.int32), idx_out)
    idx_ref[...] = idx_out
    total = run_ref[...] + jnp.sum(chosen, axis=0, keepdims=True)
    run_ref[...] = total
    cnt_ref[...] = jnp.broadcast_to(total, cnt_ref.shape).astype(jnp.int32)


def _merge_out(gates, o_a, o_b, x, wo_all, ln_g, ln_b, wr_all, br_all, layer):
    t = x.shape[0]
    tm = OUT_TM
    row = lambda i: (i, 0)
    return pl.pallas_call(
        _merge_out_kernel,
        grid=(t // tm,),
        in_specs=[
            pl.BlockSpec((tm, D_MODEL), lambda i: (i, 0)),
            pl.BlockSpec((tm, D_MODEL), lambda i: (i, 1)),
            pl.BlockSpec((tm, D_MODEL), row),
            pl.BlockSpec((tm, D_MODEL), row),
            pl.BlockSpec((tm, D_MODEL), row),
            pl.BlockSpec((1, D_MODEL, D_MODEL), lambda i: (layer, 0, 0)),
            pl.BlockSpec((1, 2, D_MODEL), lambda i: (layer, 0, 0)),
            pl.BlockSpec((1, 2, D_MODEL), lambda i: (layer, 0, 0)),
            pl.BlockSpec((1, D_MODEL, N_EXPERTS), lambda i: (layer, 0, 0)),
            pl.BlockSpec((1, 1, N_EXPERTS), lambda i: (layer, 0, 0)),
        ],
        out_specs=[
            pl.BlockSpec((tm, D_MODEL), row),
            pl.BlockSpec((tm, D_MODEL), row),
            pl.BlockSpec((tm, LANES), row),
            pl.BlockSpec((tm, LANES), row),
            pl.BlockSpec((8, N_EXPERTS), lambda i: (0, 0)),
        ],
        out_shape=[
            jax.ShapeDtypeStruct((t, D_MODEL), _F32),
            jax.ShapeDtypeStruct((t, D_MODEL), _BF16),
            jax.ShapeDtypeStruct((t, LANES), jnp.int32),
            jax.ShapeDtypeStruct((t, LANES), _F32),
            jax.ShapeDtypeStruct((8, N_EXPERTS), jnp.int32),
        ],
        scratch_shapes=[pltpu.VMEM((1, N_EXPERTS), _F32)],
        compiler_params=_params("arbitrary"),
        name="merge_out_ln_router",
    )(gates, gates, o_a, o_b, x, wo_all, ln_g, ln_b, wr_all, br_all)


def _expert_kernel(be_ref, nused_ref, x_ref, wup_ref, bg_ref, bu_ref, wd_ref, bd_ref, y_ref,
                   wg_scr, wu_scr, wd_scr):
    i = pl.program_id(0)
    changed = jnp.logical_or(i == 0, be_ref[i] != be_ref[jnp.maximum(i - 1, 0)])

    @pl.when(jnp.logical_and(changed, i < nused_ref[0]))
    def _():
        half = DEINT_TILE // 2
        r = lax.broadcasted_iota(jnp.int32, (DEINT_TILE, DEINT_TILE), 0)
        c = lax.broadcasted_iota(jnp.int32, (DEINT_TILE, DEINT_TILE), 1)
        sel = jnp.where(r == jnp.where(c < half, 2 * c, 2 * (c - half) + 1), 1.0, 0.0).astype(_BF16)
        for t in range(2 * D_FF // DEINT_TILE):
            w = wup_ref[0, 0, :, t * DEINT_TILE:(t + 1) * DEINT_TILE].astype(_BF16)
            de = jnp.dot(w, sel, preferred_element_type=_F32).astype(_BF16)
            wg_scr[:, t * half:(t + 1) * half] = de[:, :half]
            wu_scr[:, t * half:(t + 1) * half] = de[:, half:]
        wd_scr[...] = wd_ref[0, 0].astype(_BF16)

    @pl.when(i < nused_ref[0])
    def _():
        x = x_ref[...]
        gate = jnp.dot(x, wg_scr[...], preferred_element_type=_F32) + bg_ref[0, 0]
        up = jnp.dot(x, wu_scr[...], preferred_element_type=_F32) + bu_ref[0, 0]
        gate = jnp.minimum(gate, SWIGLU_LIMIT)
        up = jnp.clip(up, -SWIGLU_LIMIT, SWIGLU_LIMIT)
        act = (up + 1.0) * (gate * jax.nn.sigmoid(SWIGLU_ALPHA * gate))
        y = jnp.dot(act.astype(_BF16), wd_scr[...], preferred_element_type=_F32) + bd_ref[0, 0]
        y_ref[...] = y.astype(y_ref.dtype)

    @pl.when(i >= nused_ref[0])
    def _():
        y_ref[...] = jnp.zeros(y_ref.shape, y_ref.dtype)


def _experts(block_expert, n_used, xb, w_up, bg_all, bu_all, w_down, bd_all, layer):
    n_rows = xb.shape[0]
    n_blocks = n_rows // MOE_BLOCK
    pick = lambda i, be, nu: (layer, be[i], 0, 0)
    bspec = pl.BlockSpec((1, 1, 1, D_FF), pick)
    grid_spec = pltpu.PrefetchScalarGridSpec(
        num_scalar_prefetch=2,
        grid=(n_blocks,),
        in_specs=[
            pl.BlockSpec((MOE_BLOCK, D_MODEL), lambda i, be, nu: (i, 0)),
            pl.BlockSpec((1, 1, D_MODEL, 2 * D_FF), pick),
            bspec, bspec,
            pl.BlockSpec((1, 1, D_FF, D_MODEL), pick),
            pl.BlockSpec((1, 1, 1, D_MODEL), pick),
        ],
        out_specs=pl.BlockSpec((MOE_BLOCK, D_MODEL), lambda i, be, nu: (i, 0)),
        scratch_shapes=[
            pltpu.VMEM((D_MODEL, D_FF), _BF16),
            pltpu.VMEM((D_MODEL, D_FF), _BF16),
            pltpu.VMEM((D_FF, D_MODEL), _BF16),
        ],
    )
    return pl.pallas_call(
        _expert_kernel,
        grid_spec=grid_spec,
        out_shape=jax.ShapeDtypeStruct((n_rows, D_MODEL), _BF16),
        compiler_params=_params("arbitrary"),
        name="experts",
    )(block_expert, n_used, xb, w_up, bg_all, bu_all, w_down, bd_all)


def _combine_kernel(h_ref, yg_ref, gate_ref, lng_ref, lnb_ref, x_ref, xb_ref):
    gates = gate_ref[...]
    y = gates[:, 0:1] * yg_ref[0].astype(_F32)
    for k in range(1, TOP_K):
        y = y + gates[:, k:k + 1] * yg_ref[k].astype(_F32)
    xn = _layer_norm(DEEPNORM_ALPHA * h_ref[...] + y, lng_ref[0, 1:2, :], lnb_ref[0, 1:2, :])
    x_ref[...] = xn
    xb_ref[...] = xn.astype(_BF16)


def _combine(h, yg, gates, ln_g, ln_b, layer):
    t = h.shape[0]
    tm = LN2_TM
    row = lambda i: (i, 0)
    return pl.pallas_call(
        _combine_kernel,
        grid=(t // tm,),
        in_specs=[
            pl.BlockSpec((tm, D_MODEL), row),
            pl.BlockSpec((TOP_K, tm, D_MODEL), lambda i: (0, i, 0)),
            pl.BlockSpec((tm, LANES), row),
            pl.BlockSpec((1, 2, D_MODEL), lambda i: (layer, 0, 0)),
            pl.BlockSpec((1, 2, D_MODEL), lambda i: (layer, 0, 0)),
        ],
        out_specs=[pl.BlockSpec((tm, D_MODEL), row), pl.BlockSpec((tm, D_MODEL), row)],
        out_shape=[jax.ShapeDtypeStruct((t, D_MODEL), _F32), jax.ShapeDtypeStruct((t, D_MODEL), _BF16)],
        compiler_params=_params("parallel"),
        name="combine_ln",
    )(h, yg, gates, ln_g, ln_b)


def _dispatch_tables(route, counts, t):
    tk = t * TOP_K
    experts = route[:, :TOP_K]
    ranks = route[:, TOP_K:2 * TOP_K]
    padded = (counts + MOE_BLOCK - 1) // MOE_BLOCK * MOE_BLOCK
    pad_end = jnp.cumsum(padded)
    pad_start = pad_end - padded
    start = jnp.cumsum(counts) - counts
    expert_ids = jnp.arange(N_EXPERTS, dtype=jnp.int32)
    dest = jnp.sum(jnp.where(experts[..., None] == expert_ids, pad_start, 0), axis=-1) + ranks
    n_blocks = (tk + N_EXPERTS * (MOE_BLOCK - 1) + MOE_BLOCK - 1) // MOE_BLOCK
    block_expert = jnp.minimum(
        jnp.searchsorted(pad_end, jnp.arange(n_blocks, dtype=jnp.int32) * MOE_BLOCK, side='right'),
        N_EXPERTS - 1).astype(jnp.int32)
    n_used = (pad_end[-1:] // MOE_BLOCK).astype(jnp.int32)
    order = jnp.argsort(experts.reshape(tk))
    in_block = jnp.arange(MOE_BLOCK, dtype=jnp.int32)[None, :]
    r = (jnp.arange(n_blocks, dtype=jnp.int32) * MOE_BLOCK - pad_start[block_expert])[:, None] + in_block
    valid = r < counts[block_expert][:, None]
    src = jnp.where(valid, start[block_expert][:, None] + r, 0).reshape(n_blocks * MOE_BLOCK)
    buf_tok = jnp.where(valid.reshape(-1), order[src] // TOP_K, 0)
    return buf_tok, dest, block_expert, n_used


def kernel(x, w_in, w_o, lambda_qk, subln_g, sinks, ln_g, ln_b, w_router, b_router, w_up, b_up, w_down, b_down):
    batch, seq, _ = x.shape
    t = batch * seq
    assert seq % ATT_TQ == 0 and seq % SWA_TB == 0 and t % PROJ_TM == 0

    va_lo, va_hi = 2 * D_MODEL, 3 * D_MODEL
    gates_lo = w_in.shape[-1] - GATE_COLS
    w_att_b = jnp.concatenate([w_in[..., :va_lo], w_in[..., va_hi:gates_lo]], axis=-1).astype(_BF16)
    w_vt_b = jnp.swapaxes(w_in[..., va_lo:va_hi], 1, 2).astype(_BF16)
    w_gates_b = w_in[..., gates_lo:].astype(_BF16)
    w_o_b = w_o.astype(_BF16)
    w_r_b = w_router.astype(_BF16)
    b_gate = b_up[..., 0::2].reshape(DEPTH, N_EXPERTS, 1, D_FF)
    b_upp = b_up[..., 1::2].reshape(DEPTH, N_EXPERTS, 1, D_FF)
    b_down4 = b_down.reshape(DEPTH, N_EXPERTS, 1, D_MODEL)
    b_router3 = b_router.reshape(DEPTH, 1, N_EXPERTS)

    att_scale = jnp.concatenate([
        jnp.full((D_MODEL,), DA_HEAD_DIM ** -0.5 * LOG2E, _F32), jnp.ones((D_MODEL,), _F32),
        jnp.full((D_MODEL,), SW_HEAD_DIM ** -0.5, _F32),
        jnp.ones((2 * SW_KV_HEADS * SW_HEAD_DIM,), _F32)]).reshape(1, ATT_COLS)
    gate_scale = jnp.ones((1, GATE_COLS), _F32)
    rest = 2.0 ** (-8.0 * jnp.arange(1, DA_HEADS + 1, dtype=_F32) / DA_HEADS) * LOG2E
    pieces = []
    for _ in range(N_SLOPE_PARTS):
        piece = rest.astype(_BF16).astype(_F32)
        pieces.append(piece)
        rest = rest - piece
    da_slopes = jnp.stack(pieces, axis=1)
    swa_bias = _swa_bias()

    xf = x.reshape(t, D_MODEL)
    xb = xf.astype(_BF16)
    for l in range(DEPTH):
        lam_init = 0.8 - 0.6 * math.exp(-0.3 * l)
        prm = jnp.concatenate([lambda_qk[l].astype(_F32), jnp.full((4, DA_HEAD_DIM), lam_init, _F32)], axis=0)
        att = _project(xb, w_att_b, l, ATT_COLS // 2, att_scale, _BF16)
        gates = _project(xb, w_gates_b, l, GATE_COLS // 2, gate_scale, _BF16)
        vt = _project_vt(xb, w_vt_b, l)
        o_a = _diff_attention(att, vt, da_slopes, prm, subln_g[l].reshape(DA_WIDTH, 1), batch, seq)
        o_b = _sliding_window_attention(att, sinks[l], swa_bias, batch, seq)
        h, hb, route, top_gate, counts = _merge_out(gates, o_a, o_b, xf, w_o_b, ln_g, ln_b, w_r_b, b_router3, l)
        buf_tok, pos, block_expert, n_used = _dispatch_tables(route, counts[0], t)
        x_rows = hb[buf_tok]
        yb = _experts(block_expert, n_used, x_rows, w_up, b_gate, b_upp, w_down, b_down4, l)
        yg = yb[pos.T]
        xf, xb = _combine(h, yg, top_gate, ln_g, ln_b, l)
    return xf.reshape(batch, seq, D_MODEL)
```

```python
import functools
import math

import jax
import jax.numpy as jnp
from jax import lax
from jax.experimental import pallas as pl
from jax.experimental.pallas import tpu as pltpu

D_MODEL = 1024
DEPTH = 4
DA_HEAD_DIM = 128
DA_HEADS = 4
DA_WIDTH = 2 * DA_HEAD_DIM
SW_HEAD_DIM = 64
SW_HEADS = 16
SW_KV_HEADS = 2
SW_GROUP = SW_HEADS // SW_KV_HEADS
WINDOW = 128
N_EXPERTS = 32
TOP_K = 4
D_FF = D_MODEL
SWIGLU_LIMIT = 7.0
SWIGLU_ALPHA = 1.702
LN_EPS = 1e-5
DEEPNORM_ALPHA = (2.0 * DEPTH) ** 0.25
NEG_INF = -1e30

QA_OFF, KA_OFF, QB_OFF, KVB_OFF = 0, 1024, 2048, 3072
ATT_COLS = 3328
GATE_COLS = 2 * D_MODEL
LANES = 128
SUBLANES_BF16 = 16
VT_ROWS = DA_WIDTH + SUBLANES_BF16
LOG2E = math.log2(math.e)
N_SLOPE_PARTS = 3

PROJ_TM = 512
ATT_TQ = 1024
ATT_TK = 512
SWA_TB = 256
OUT_TM = 256
MOE_BLOCK = 256
DEINT_TILE = 256
LN2_TM = 256
VMEM_LIMIT = 56 * 1024 * 1024

_F32 = jnp.float32
_BF16 = jnp.bfloat16


def _params(*sem):
    return pltpu.CompilerParams(dimension_semantics=sem, vmem_limit_bytes=VMEM_LIMIT)


def _proj_kernel(x_ref, w_ref, s_ref, o_ref):
    acc = jnp.dot(x_ref[...], w_ref[0], preferred_element_type=_F32)
    o_ref[...] = (acc * s_ref[...]).astype(o_ref.dtype)


def _project(xb, w_all, layer, tn, scale, out_dtype):
    t = xb.shape[0]
    n_cols = w_all.shape[-1]
    assert n_cols % tn == 0 and t % PROJ_TM == 0
    return pl.pallas_call(
        _proj_kernel,
        grid=(n_cols // tn, t // PROJ_TM),
        in_specs=[
            pl.BlockSpec((PROJ_TM, D_MODEL), lambda j, i: (i, 0)),
            pl.BlockSpec((1, D_MODEL, tn), lambda j, i: (layer, 0, j)),
            pl.BlockSpec((1, tn), lambda j, i: (0, j)),
        ],
        out_specs=pl.BlockSpec((PROJ_TM, tn), lambda j, i: (i, j)),
        out_shape=jax.ShapeDtypeStruct((t, n_cols), out_dtype),
        compiler_params=_params("parallel", "parallel"),
        name="in_proj",
    )(xb, w_all, scale)


def _proj_vt_kernel(x_ref, wt_ref, o_ref):
    x = x_ref[...]
    ones = jnp.ones((SUBLANES_BF16, x.shape[0]), o_ref.dtype)
    for h in range(DA_HEADS):
        vt = lax.dot_general(wt_ref[0, h * DA_WIDTH:(h + 1) * DA_WIDTH, :], x,
                             (((1,), (1,)), ((), ())), preferred_element_type=_F32)
        o_ref[h * VT_ROWS:h * VT_ROWS + DA_WIDTH, :] = vt.astype(o_ref.dtype)
        o_ref[h * VT_ROWS + DA_WIDTH:(h + 1) * VT_ROWS, :] = ones


def _project_vt(xb, wt_all, layer):
    t = xb.shape[0]
    return pl.pallas_call(
        _proj_vt_kernel,
        grid=(t // PROJ_TM,),
        in_specs=[
            pl.BlockSpec((PROJ_TM, D_MODEL), lambda i: (i, 0)),
            pl.BlockSpec((1, D_MODEL, D_MODEL), lambda i: (layer, 0, 0)),
        ],
        out_specs=pl.BlockSpec((DA_HEADS * VT_ROWS, PROJ_TM), lambda i: (0, i)),
        out_shape=jax.ShapeDtypeStruct((DA_HEADS * VT_ROWS, t), _BF16),
        compiler_params=_params("parallel"),
        name="in_proj_vt",
    )(xb, wt_all)


def _diff_attn_kernel(slope_ref, prm_ref, g_ref, q_ref, k_ref, vt_ref, o_ref,
                      acc_ref, m_ref, qaug_ref, kaug_ref, s_ref, smax_ref, p_ref, alpha_ref, *, tq, tk):
    h = pl.program_id(1)
    i = pl.program_id(2)
    diag = tq // tk
    n_full = i * diag
    lane = lax.broadcasted_iota(jnp.int32, (tq, DA_HEAD_DIM), 1)
    slope_tile = jnp.zeros((tq, DA_HEAD_DIM), _F32)
    slope = 0.0
    for part in range(N_SLOPE_PARTS):
        piece = slope_ref[h, part]
        slope = slope + piece
        slope_tile = jnp.where(lane // 2 == part, piece, slope_tile)
    klane = lax.broadcasted_iota(jnp.int32, (tk, DA_HEAD_DIM), 1)
    kpos = lax.broadcasted_iota(jnp.int32, (tk, DA_HEAD_DIM), 0)
    pos_tile = jnp.where(klane < 2 * N_SLOPE_PARTS, jnp.where(klane % 2 == 0, kpos % 256, kpos // 256 * 256), 0)
    for c in range(2):
        kaug_ref[c, :, DA_HEAD_DIM:] = pos_tile.astype(_F32).astype(_BF16)
        qaug_ref[c, :, :DA_HEAD_DIM] = q_ref[:, c * DA_HEAD_DIM:(c + 1) * DA_HEAD_DIM]
        qaug_ref[c, :, DA_HEAD_DIM:] = slope_tile.astype(_BF16)
    m_ref[...] = jnp.full(m_ref.shape, NEG_INF, _F32)
    acc_ref[...] = jnp.zeros(acc_ref.shape, _F32)

    def scores(j):
        start = pl.multiple_of(j * tk, tk)
        for c in range(2):
            kaug_ref[c, :, :DA_HEAD_DIM] = k_ref[pl.ds(start, tk), c * DA_HEAD_DIM:(c + 1) * DA_HEAD_DIM]
            s = lax.dot_general(kaug_ref[c], qaug_ref[c], (((1,), (1,)), ((), ())),
                                preferred_element_type=_F32)
            s_ref[c] = s
            smax_ref[c] = jnp.max(s, axis=0, keepdims=True)

    def weights(j, boundary):
        off = (j * tk - i * tq).astype(_F32) * slope
        for c in range(2):
            s = s_ref[c]
            if boundary is None:
                smax = smax_ref[c]
            else:
                keys = lax.broadcasted_iota(jnp.int32, (tk, tq), 0) + boundary * tk
                queries = lax.broadcasted_iota(jnp.int32, (tk, tq), 1)
                s = jnp.where(keys <= queries, s, NEG_INF)
                smax = jnp.max(s, axis=0, keepdims=True)
            m_old = m_ref[c]
            m_new = jnp.maximum(m_old, smax + off)
            alpha_ref[c] = jnp.exp2(m_old - m_new)
            p_ref[c] = jnp.exp2(s - (m_new - off)).astype(_BF16)
            m_ref[c] = m_new

    def accumulate(j):
        start = pl.multiple_of(j * tk, tk)
        vt = vt_ref[:, pl.ds(start, tk)]
        for c in range(2):
            acc_ref[c] = alpha_ref[c] * acc_ref[c] + jnp.dot(vt, p_ref[c], preferred_element_type=_F32)

    def body(j, carry):
        accumulate(j)
        weights(j + 1, None)
        scores(j + 2)
        return carry

    scores(0)

    @pl.when(i >= 1)
    def _():
        weights(0, None)
        scores(1)
        lax.fori_loop(0, n_full - 1, body, 0)
        accumulate(n_full - 1)

    for d in range(diag):
        weights(n_full + d, d)
        if d + 1 < diag:
            scores(n_full + d + 1)
        accumulate(n_full + d)

    prm = prm_ref[...]
    lam_init = prm[4:5, 0:1]
    lam = (jnp.exp(jnp.sum(prm[0:1] * prm[1:2], axis=1, keepdims=True))
           - jnp.exp(jnp.sum(prm[2:3] * prm[3:4], axis=1, keepdims=True)) + lam_init)
    o0 = acc_ref[0, :DA_WIDTH, :] / acc_ref[0, DA_WIDTH:DA_WIDTH + 1, :]
    o1 = acc_ref[1, :DA_WIDTH, :] / acc_ref[1, DA_WIDTH:DA_WIDTH + 1, :]
    a = o0 - lam * o1
    ms = jnp.mean(a * a, axis=0, keepdims=True)
    out = a * lax.rsqrt(ms + LN_EPS) * g_ref[...] * (1.0 - lam_init)
    o_ref[...] = out.T.astype(o_ref.dtype)


def _diff_attention(att, vt, slopes, prm, subln_g, batch, seq):
    tq, tk = ATT_TQ, ATT_TK
    assert tq % tk == 0 and seq % tq == 0
    nq = seq // tq
    t = batch * seq
    kernel = functools.partial(_diff_attn_kernel, tq=tq, tk=tk)
    return pl.pallas_call(
        kernel,
        grid=(batch, DA_HEADS, nq),
        in_specs=[
            pl.BlockSpec(memory_space=pltpu.SMEM),
            pl.BlockSpec((8, LANES), lambda b, h, i: (0, 0)),
            pl.BlockSpec((DA_WIDTH, 1), lambda b, h, i: (0, 0)),
            pl.BlockSpec((tq, DA_WIDTH), lambda b, h, i: (b * nq + i, QA_OFF // DA_WIDTH + h)),
            pl.BlockSpec((seq, DA_WIDTH), lambda b, h, i: (b, KA_OFF // DA_WIDTH + h)),
            pl.BlockSpec((VT_ROWS, seq), lambda b, h, i: (h, b)),
        ],
        out_specs=pl.BlockSpec((tq, DA_WIDTH), lambda b, h, i: (b * nq + i, h)),
        out_shape=jax.ShapeDtypeStruct((t, D_MODEL), _BF16),
        scratch_shapes=[
            pltpu.VMEM((2, VT_ROWS, tq), _F32),
            pltpu.VMEM((2, 1, tq), _F32),
            pltpu.VMEM((2, tq, 2 * DA_HEAD_DIM), _BF16),
            pltpu.VMEM((2, tk, 2 * DA_HEAD_DIM), _BF16),
            pltpu.VMEM((2, tk, tq), _F32),
            pltpu.VMEM((2, 1, tq), _F32),
            pltpu.VMEM((2, tk, tq), _BF16),
            pltpu.VMEM((2, 1, tq), _F32),
        ],
        compiler_params=_params("parallel", "parallel", "arbitrary"),
        name="diff_attn",
    )(slopes, prm, subln_g, att, att, vt)


def _swa_kernel(sink_ref, bias_ref, q_ref, kv_ref, kvp_ref, o_ref):
    i = pl.program_id(1)
    kv_all = jnp.concatenate([kvp_ref[...], kv_ref[...]], axis=0)
    col = lax.broadcasted_iota(jnp.int32, (WINDOW, 2 * WINDOW), 1)
    no_prev = jnp.logical_and(i == 0, col < WINDOW)
    for r in range(SWA_TB // WINDOW):
        win = kv_all[r * WINDOW:(r + 2) * WINDOW, :]
        heads = []
        for head in range(SW_HEADS):
            kh = head // SW_GROUP
            q = q_ref[r * WINDOW:(r + 1) * WINDOW, head * SW_HEAD_DIM:(head + 1) * SW_HEAD_DIM]
            k = win[:, kh * SW_HEAD_DIM:(kh + 1) * SW_HEAD_DIM]
            v = win[:, (SW_KV_HEADS + kh) * SW_HEAD_DIM:(SW_KV_HEADS + kh + 1) * SW_HEAD_DIM]
            s = lax.dot_general(q, k, (((1,), (1,)), ((), ())), preferred_element_type=_F32)
            bias = bias_ref[head]
            if r == 0:
                bias = jnp.where(no_prev, NEG_INF, bias)
            s = s + bias
            sink = sink_ref[head]
            m = jnp.maximum(jnp.max(s, axis=1, keepdims=True), sink)
            p = jnp.exp(s - m)
            denom = jnp.sum(p, axis=1, keepdims=True) + jnp.exp(sink - m)
            o = jnp.dot(p.astype(_BF16), v, preferred_element_type=_F32)
            heads.append(o / denom)
        o_ref[r * WINDOW:(r + 1) * WINDOW, :] = jnp.concatenate(heads, axis=1).astype(o_ref.dtype)


def _swa_bias():
    slopes = 2.0 ** (-8.0 * jnp.arange(1, SW_HEADS + 1, dtype=_F32) / SW_HEADS)
    qi = jnp.arange(WINDOW, dtype=jnp.int32)[:, None]
    kj = jnp.arange(2 * WINDOW, dtype=jnp.int32)[None, :]
    dist = qi + WINDOW - kj
    valid = (dist >= 0) & (dist < WINDOW)
    bias = -slopes[:, None, None] * dist.astype(_F32)[None]
    return jnp.where(valid[None], bias, NEG_INF)


def _sliding_window_attention(att, sinks_l, bias, batch, seq):
    tb = SWA_TB
    nb = seq // tb
    ratio = tb // WINDOW
    t = batch * seq
    kv_width = 2 * SW_KV_HEADS * SW_HEAD_DIM
    q_blk = QB_OFF // D_MODEL
    kv_blk = KVB_OFF // kv_width
    return pl.pallas_call(
        _swa_kernel,
        grid=(batch, nb),
        in_specs=[
            pl.BlockSpec(memory_space=pltpu.SMEM),
            pl.BlockSpec((SW_HEADS, WINDOW, 2 * WINDOW), lambda b, i: (0, 0, 0)),
            pl.BlockSpec((tb, D_MODEL), lambda b, i: (b * nb + i, q_blk)),
            pl.BlockSpec((tb, kv_width), lambda b, i: (b * nb + i, kv_blk)),
            pl.BlockSpec((WINDOW, kv_width), lambda b, i: (jnp.maximum((b * nb + i) * ratio - 1, 0), kv_blk)),
        ],
        out_specs=pl.BlockSpec((tb, D_MODEL), lambda b, i: (b * nb + i, 0)),
        out_shape=jax.ShapeDtypeStruct((t, D_MODEL), _BF16),
        compiler_params=_params("parallel", "arbitrary"),
        name="swa",
    )(sinks_l, bias, att, att, att)


def _layer_norm(z, g, b):
    mu = jnp.mean(z, axis=1, keepdims=True)
    d = z - mu
    var = jnp.mean(d * d, axis=1, keepdims=True)
    return d * lax.rsqrt(var + LN_EPS) * g + b


def _merge_out_kernel(ga_ref, gb_ref, oa_ref, ob_ref, x_ref, wo_ref, lng_ref, lnb_ref, wr_ref, br_ref,
                      h_ref, hb_ref, idx_ref, gate_ref, cnt_ref, run_ref):
    @pl.when(pl.program_id(0) == 0)
    def _():
        run_ref[...] = jnp.zeros(run_ref.shape, _F32)

    merged = (jax.nn.sigmoid(ga_ref[...].astype(_F32)) * oa_ref[...].astype(_F32)
              + jax.nn.sigmoid(gb_ref[...].astype(_F32)) * ob_ref[...].astype(_F32))
    y = jnp.dot(merged.astype(_BF16), wo_ref[0], preferred_element_type=_F32)
    hn = _layer_norm(DEEPNORM_ALPHA * x_ref[...] + y, lng_ref[0, 0:1, :], lnb_ref[0, 0:1, :])
    h_ref[...] = hn
    hb = hn.astype(_BF16)
    hb_ref[...] = hb
    logits = jnp.dot(hb, wr_ref[0], preferred_element_type=_F32) + br_ref[0]
    tm = logits.shape[0]
    ecol = lax.broadcasted_iota(jnp.int32, (tm, N_EXPERTS), 1)
    lane = lax.broadcasted_iota(jnp.int32, (tm, LANES), 1)
    cur = logits
    vals, picks, idx_out = [], [], jnp.zeros((tm, LANES), jnp.int32)
    for k in range(TOP_K):
        mx = jnp.max(cur, axis=1, keepdims=True)
        ix = jnp.min(jnp.where(cur == mx, ecol, N_EXPERTS), axis=1, keepdims=True)
        vals.append(mx)
        picks.append(ecol == ix)
        idx_out = jnp.where(lane == k, ix, idx_out)
        cur = jnp.where(picks[k], -jnp.inf, cur)
    exps = [jnp.exp(v - vals[0]) for v in vals]
    tot = exps[0] + exps[1] + exps[2] + exps[3]
    gate_out = jnp.zeros((tm, LANES), _F32)
    for k in range(TOP_K):
        gate_out = jnp.where(lane == k, exps[k] / tot, gate_out)
    gate_ref[...] = gate_out

    chosen = jnp.zeros((tm, N_EXPERTS), _F32)
    for k in range(TOP_K):
        chosen = chosen + jnp.where(picks[k], 1.0, 0.0)
    r = lax.broadcasted_iota(jnp.int32, (tm, tm), 0)
    c = lax.broadcasted_iota(jnp.int32, (tm, tm), 1)
    earlier = jnp.where(c < r, 1.0, 0.0).astype(_BF16)
    before = jnp.dot(earlier, chosen.astype(_BF16), preferred_element_type=_F32) + run_ref[...]
    for k in range(TOP_K):
        rank = jnp.sum(jnp.where(picks[k], before, 0.0), axis=1, keepdims=True)
        idx_out = jnp.where(lane == TOP_K + k, rank.astype(jnp.int32), idx_out)
    idx_ref[...] = idx_out
    total = run_ref[...] + jnp.sum(chosen, axis=0, keepdims=True)
    run_ref[...] = total
    cnt_ref[...] = jnp.broadcast_to(total, cnt_ref.shape).astype(jnp.int32)


def _merge_out(gates, o_a, o_b, x, wo_all, ln_g, ln_b, wr_all, br_all, layer):
    t = x.shape[0]
    tm = OUT_TM
    row = lambda i: (i, 0)
    return pl.pallas_call(
        _merge_out_kernel,
        grid=(t // tm,),
        in_specs=[
            pl.BlockSpec((tm, D_MODEL), lambda i: (i, 0)),
            pl.BlockSpec((tm, D_MODEL), lambda i: (i, 1)),
            pl.BlockSpec((tm, D_MODEL), row),
            pl.BlockSpec((tm, D_MODEL), row),
            pl.BlockSpec((tm, D_MODEL), row),
            pl.BlockSpec((1, D_MODEL, D_MODEL), lambda i: (layer, 0, 0)),
            pl.BlockSpec((1, 2, D_MODEL), lambda i: (layer, 0, 0)),
            pl.BlockSpec((1, 2, D_MODEL), lambda i: (layer, 0, 0)),
            pl.BlockSpec((1, D_MODEL, N_EXPERTS), lambda i: (layer, 0, 0)),
            pl.BlockSpec((1, 1, N_EXPERTS), lambda i: (layer, 0, 0)),
        ],
        out_specs=[
            pl.BlockSpec((tm, D_MODEL), row),
            pl.BlockSpec((tm, D_MODEL), row),
            pl.BlockSpec((tm, LANES), row),
            pl.BlockSpec((tm, LANES), row),
            pl.BlockSpec((8, N_EXPERTS), lambda i: (0, 0)),
        ],
        out_shape=[
            jax.ShapeDtypeStruct((t, D_MODEL), _F32),
            jax.ShapeDtypeStruct((t, D_MODEL), _BF16),
            jax.ShapeDtypeStruct((t, LANES), jnp.int32),
            jax.ShapeDtypeStruct((t, LANES), _F32),
            jax.ShapeDtypeStruct((8, N_EXPERTS), jnp.int32),
        ],
        scratch_shapes=[pltpu.VMEM((1, N_EXPERTS), _F32)],
        compiler_params=_params("arbitrary"),
        name="merge_out_ln_router",
    )(gates, gates, o_a, o_b, x, wo_all, ln_g, ln_b, wr_all, br_all)


def _expert_kernel(be_ref, nused_ref, x_ref, wup_ref, bg_ref, bu_ref, wd_ref, bd_ref, y_ref,
                   wg_scr, wu_scr, wd_scr):
    i = pl.program_id(0)
    changed = jnp.logical_or(i == 0, be_ref[i] != be_ref[jnp.maximum(i - 1, 0)])

    @pl.when(jnp.logical_and(changed, i < nused_ref[0]))
    def _():
        half = DEINT_TILE // 2
        r = lax.broadcasted_iota(jnp.int32, (DEINT_TILE, DEINT_TILE), 0)
        c = lax.broadcasted_iota(jnp.int32, (DEINT_TILE, DEINT_TILE), 1)
        sel = jnp.where(r == jnp.where(c < half, 2 * c, 2 * (c - half) + 1), 1.0, 0.0).astype(_BF16)
        for t in range(2 * D_FF // DEINT_TILE):
            w = wup_ref[0, 0, :, t * DEINT_TILE:(t + 1) * DEINT_TILE].astype(_BF16)
            de = jnp.dot(w, sel, preferred_element_type=_F32).astype(_BF16)
            wg_scr[:, t * half:(t + 1) * half] = de[:, :half]
            wu_scr[:, t * half:(t + 1) * half] = de[:, half:]
        wd_scr[...] = wd_ref[0, 0].astype(_BF16)

    @pl.when(i < nused_ref[0])
    def _():
        x = x_ref[...]
        gate = jnp.dot(x, wg_scr[...], preferred_element_type=_F32) + bg_ref[0, 0]
        up = jnp.dot(x, wu_scr[...], preferred_element_type=_F32) + bu_ref[0, 0]
        gate = jnp.minimum(gate, SWIGLU_LIMIT)
        up = jnp.clip(up, -SWIGLU_LIMIT, SWIGLU_LIMIT)
        act = (up + 1.0) * (gate * jax.nn.sigmoid(SWIGLU_ALPHA * gate))
        y = jnp.dot(act.astype(_BF16), wd_scr[...], preferred_element_type=_F32) + bd_ref[0, 0]
        y_ref[...] = y.astype(y_ref.dtype)

    @pl.when(i >= nused_ref[0])
    def _():
        y_ref[...] = jnp.zeros(y_ref.shape, y_ref.dtype)


def _experts(block_expert, n_used, xb, w_up, bg_all, bu_all, w_down, bd_all, layer):
    n_rows = xb.shape[0]
    n_blocks = n_rows // MOE_BLOCK
    pick = lambda i, be, nu: (layer, be[i], 0, 0)
    bspec = pl.BlockSpec((1, 1, 1, D_FF), pick)
    grid_spec = pltpu.PrefetchScalarGridSpec(
        num_scalar_prefetch=2,
        grid=(n_blocks,),
        in_specs=[
            pl.BlockSpec((MOE_BLOCK, D_MODEL), lambda i, be, nu: (i, 0)),
            pl.BlockSpec((1, 1, D_MODEL, 2 * D_FF), pick),
            bspec, bspec,
            pl.BlockSpec((1, 1, D_FF, D_MODEL), pick),
            pl.BlockSpec((1, 1, 1, D_MODEL), pick),
        ],
        out_specs=pl.BlockSpec((MOE_BLOCK, D_MODEL), lambda i, be, nu: (i, 0)),
        scratch_shapes=[
            pltpu.VMEM((D_MODEL, D_FF), _BF16),
            pltpu.VMEM((D_MODEL, D_FF), _BF16),
            pltpu.VMEM((D_FF, D_MODEL), _BF16),
        ],
    )
    return pl.pallas_call(
        _expert_kernel,
        grid_spec=grid_spec,
        out_shape=jax.ShapeDtypeStruct((n_rows, D_MODEL), _BF16),
        compiler_params=_params("arbitrary"),
        name="experts",
    )(block_expert, n_used, xb, w_up, bg_all, bu_all, w_down, bd_all)


def _combine_kernel(h_ref, yg_ref, gate_ref, lng_ref, lnb_ref, x_ref, xb_ref):
    gates = gate_ref[...]
    y = gates[:, 0:1] * yg_ref[0].astype(_F32)
    for k in range(1, TOP_K):
        y = y + gates[:, k:k + 1] * yg_ref[k].astype(_F32)
    xn = _layer_norm(DEEPNORM_ALPHA * h_ref[...] + y, lng_ref[0, 1:2, :], lnb_ref[0, 1:2, :])
    x_ref[...] = xn
    xb_ref[...] = xn.astype(_BF16)


def _combine(h, yg, gates, ln_g, ln_b, layer):
    t = h.shape[0]
    tm = LN2_TM
    row = lambda i: (i, 0)
    return pl.pallas_call(
        _combine_kernel,
        grid=(t // tm,),
        in_specs=[
            pl.BlockSpec((tm, D_MODEL), row),
            pl.BlockSpec((TOP_K, tm, D_MODEL), lambda i: (0, i, 0)),
            pl.BlockSpec((tm, LANES), row),
            pl.BlockSpec((1, 2, D_MODEL), lambda i: (layer, 0, 0)),
            pl.BlockSpec((1, 2, D_MODEL), lambda i: (layer, 0, 0)),
        ],
        out_specs=[pl.BlockSpec((tm, D_MODEL), row), pl.BlockSpec((tm, D_MODEL), row)],
        out_shape=[jax.ShapeDtypeStruct((t, D_MODEL), _F32), jax.ShapeDtypeStruct((t, D_MODEL), _BF16)],
        compiler_params=_params("parallel"),
        name="combine_ln",
    )(h, yg, gates, ln_g, ln_b)


def _dispatch_tables(route, counts, t):
    tk = t * TOP_K
    experts = route[:, :TOP_K]
    ranks = route[:, TOP_K:2 * TOP_K]
    padded = (counts + MOE_BLOCK - 1) // MOE_BLOCK * MOE_BLOCK
    pad_end = jnp.cumsum(padded)
    pad_start = pad_end - padded
    start = jnp.cumsum(counts) - counts
    expert_ids = jnp.arange(N_EXPERTS, dtype=jnp.int32)
    dest = jnp.sum(jnp.where(experts[..., None] == expert_ids, pad_start, 0), axis=-1) + ranks
    n_blocks = (tk + N_EXPERTS * (MOE_BLOCK - 1) + MOE_BLOCK - 1) // MOE_BLOCK
    block_start = jnp.arange(n_blocks, dtype=jnp.int32) * MOE_BLOCK
    block_expert = jnp.minimum(
        jnp.sum((pad_end[None, :] <= block_start[:, None]).astype(jnp.int32), axis=1), N_EXPERTS - 1)
    n_used = (pad_end[-1:] // MOE_BLOCK).astype(jnp.int32)
    order = jnp.argsort(experts.reshape(tk))
    in_block = jnp.arange(MOE_BLOCK, dtype=jnp.int32)[None, :]
    r = (block_start - pad_start[block_expert])[:, None] + in_block
    valid = r < counts[block_expert][:, None]
    src = jnp.where(valid, start[block_expert][:, None] + r, 0).reshape(n_blocks * MOE_BLOCK)
    filler = jnp.arange(n_blocks * MOE_BLOCK, dtype=jnp.int32) % t
    buf_tok = jnp.where(valid.reshape(-1), order[src] // TOP_K, filler)
    return buf_tok, dest, block_expert, n_used


def kernel(x, w_in, w_o, lambda_qk, subln_g, sinks, ln_g, ln_b, w_router, b_router, w_up, b_up, w_down, b_down):
    batch, seq, _ = x.shape
    t = batch * seq
    assert seq % ATT_TQ == 0 and seq % SWA_TB == 0 and t % PROJ_TM == 0

    va_lo, va_hi = 2 * D_MODEL, 3 * D_MODEL
    gates_lo = w_in.shape[-1] - GATE_COLS
    w_att_b = jnp.concatenate([w_in[..., :va_lo], w_in[..., va_hi:gates_lo]], axis=-1).astype(_BF16)
    w_vt_b = jnp.swapaxes(w_in[..., va_lo:va_hi], 1, 2).astype(_BF16)
    w_gates_b = w_in[..., gates_lo:].astype(_BF16)
    w_o_b = w_o.astype(_BF16)
    w_r_b = w_router.astype(_BF16)
    b_gate = b_up[..., 0::2].reshape(DEPTH, N_EXPERTS, 1, D_FF)
    b_upp = b_up[..., 1::2].reshape(DEPTH, N_EXPERTS, 1, D_FF)
    b_down4 = b_down.reshape(DEPTH, N_EXPERTS, 1, D_MODEL)
    b_router3 = b_router.reshape(DEPTH, 1, N_EXPERTS)

    att_scale = jnp.concatenate([
        jnp.full((D_MODEL,), DA_HEAD_DIM ** -0.5 * LOG2E, _F32), jnp.ones((D_MODEL,), _F32),
        jnp.full((D_MODEL,), SW_HEAD_DIM ** -0.5, _F32),
        jnp.ones((2 * SW_KV_HEADS * SW_HEAD_DIM,), _F32)]).reshape(1, ATT_COLS)
    gate_scale = jnp.ones((1, GATE_COLS), _F32)
    rest = 2.0 ** (-8.0 * jnp.arange(1, DA_HEADS + 1, dtype=_F32) / DA_HEADS) * LOG2E
    pieces = []
    for _ in range(N_SLOPE_PARTS):
        piece = rest.astype(_BF16).astype(_F32)
        pieces.append(piece)
        rest = rest - piece
    da_slopes = jnp.stack(pieces, axis=1)
    swa_bias = _swa_bias()

    xf = x.reshape(t, D_MODEL)
    xb = xf.astype(_BF16)
    for l in range(DEPTH):
        lam_init = 0.8 - 0.6 * math.exp(-0.3 * l)
        prm = jnp.concatenate([lambda_qk[l].astype(_F32), jnp.full((4, DA_HEAD_DIM), lam_init, _F32)], axis=0)
        att = _project(xb, w_att_b, l, ATT_COLS // 2, att_scale, _BF16)
        gates = _project(xb, w_gates_b, l, GATE_COLS // 2, gate_scale, _BF16)
        vt = _project_vt(xb, w_vt_b, l)
        o_a = _diff_attention(att, vt, da_slopes, prm, subln_g[l].reshape(DA_WIDTH, 1), batch, seq)
        o_b = _sliding_window_attention(att, sinks[l], swa_bias, batch, seq)
        h, hb, route, top_gate, counts = _merge_out(gates, o_a, o_b, xf, w_o_b, ln_g, ln_b, w_r_b, b_router3, l)
        buf_tok, pos, block_expert, n_used = _dispatch_tables(route, counts[0], t)
        x_rows = hb[buf_tok]
        yb = _experts(block_expert, n_used, x_rows, w_up, b_gate, b_upp, w_down, b_down4, l)
        yg = yb[pos.T]
        xf, xb = _combine(h, yg, top_gate, ln_g, ln_b, l)
    return xf.reshape(batch, seq, D_MODEL)
```

```python
import functools
import math

import jax
import jax.numpy as jnp
from jax import lax
from jax.experimental import pallas as pl
from jax.experimental.pallas import tpu as pltpu

D_MODEL = 1024
DEPTH = 4
DA_HEAD_DIM = 128
DA_HEADS = 4
DA_WIDTH = 2 * DA_HEAD_DIM
SW_HEAD_DIM = 64
SW_HEADS = 16
SW_KV_HEADS = 2
SW_GROUP = SW_HEADS // SW_KV_HEADS
WINDOW = 128
N_EXPERTS = 32
TOP_K = 4
D_FF = D_MODEL
SWIGLU_LIMIT = 7.0
SWIGLU_ALPHA = 1.702
LN_EPS = 1e-5
DEEPNORM_ALPHA = (2.0 * DEPTH) ** 0.25
NEG_INF = -1e30

QA_OFF, KA_OFF, QB_OFF, KVB_OFF = 0, 1024, 2048, 3072
ATT_COLS = 3328
GATE_COLS = 2 * D_MODEL
LANES = 128
SUBLANES_BF16 = 16
VT_ROWS = DA_WIDTH + SUBLANES_BF16
VTB_ROWS = SW_HEAD_DIM + SUBLANES_BF16
LOG2E = math.log2(math.e)
N_SLOPE_PARTS = 3

PROJ_TM = 512
ATT_TQ = 1024
ATT_TK = 512
SWA_TB = 256
OUT_TM = 256
MOE_BLOCK = 256
DEINT_TILE = 256
LN2_TM = 256
VMEM_LIMIT = 56 * 1024 * 1024

_F32 = jnp.float32
_BF16 = jnp.bfloat16


def _params(*sem):
    return pltpu.CompilerParams(dimension_semantics=sem, vmem_limit_bytes=VMEM_LIMIT)


def _proj_kernel(x_ref, w_ref, s_ref, o_ref):
    acc = jnp.dot(x_ref[...], w_ref[0], preferred_element_type=_F32)
    o_ref[...] = (acc * s_ref[...]).astype(o_ref.dtype)


def _project(xb, w_all, layer, tn, scale, out_dtype):
    t = xb.shape[0]
    n_cols = w_all.shape[-1]
    assert n_cols % tn == 0 and t % PROJ_TM == 0
    return pl.pallas_call(
        _proj_kernel,
        grid=(n_cols // tn, t // PROJ_TM),
        in_specs=[
            pl.BlockSpec((PROJ_TM, D_MODEL), lambda j, i: (i, 0)),
            pl.BlockSpec((1, D_MODEL, tn), lambda j, i: (layer, 0, j)),
            pl.BlockSpec((1, tn), lambda j, i: (0, j)),
        ],
        out_specs=pl.BlockSpec((PROJ_TM, tn), lambda j, i: (i, j)),
        out_shape=jax.ShapeDtypeStruct((t, n_cols), out_dtype),
        compiler_params=_params("parallel", "parallel"),
        name="in_proj",
    )(xb, w_all, scale)


def _proj_vt_kernel(x_ref, wt_ref, oa_ref, ob_ref):
    x = x_ref[...]
    nt = (((1,), (1,)), ((), ()))
    ones = jnp.ones((SUBLANES_BF16, x.shape[0]), oa_ref.dtype)
    for h in range(DA_HEADS):
        vt = lax.dot_general(wt_ref[0, h * DA_WIDTH:(h + 1) * DA_WIDTH, :], x, nt, preferred_element_type=_F32)
        oa_ref[h * VT_ROWS:h * VT_ROWS + DA_WIDTH, :] = vt.astype(oa_ref.dtype)
        oa_ref[h * VT_ROWS + DA_WIDTH:(h + 1) * VT_ROWS, :] = ones
    vb = lax.dot_general(wt_ref[0, D_MODEL:, :], x, nt, preferred_element_type=_F32)
    for kh in range(SW_KV_HEADS):
        ob_ref[kh * VTB_ROWS:kh * VTB_ROWS + SW_HEAD_DIM, :] = (
            vb[kh * SW_HEAD_DIM:(kh + 1) * SW_HEAD_DIM, :].astype(ob_ref.dtype))
        ob_ref[kh * VTB_ROWS + SW_HEAD_DIM:(kh + 1) * VTB_ROWS, :] = ones


def _project_vt(xb, wt_all, layer):
    t = xb.shape[0]
    n_b = SW_KV_HEADS * SW_HEAD_DIM
    return pl.pallas_call(
        _proj_vt_kernel,
        grid=(t // PROJ_TM,),
        in_specs=[
            pl.BlockSpec((PROJ_TM, D_MODEL), lambda i: (i, 0)),
            pl.BlockSpec((1, D_MODEL + n_b, D_MODEL), lambda i: (layer, 0, 0)),
        ],
        out_specs=[pl.BlockSpec((DA_HEADS * VT_ROWS, PROJ_TM), lambda i: (0, i)),
                   pl.BlockSpec((SW_KV_HEADS * VTB_ROWS, PROJ_TM), lambda i: (0, i))],
        out_shape=[jax.ShapeDtypeStruct((DA_HEADS * VT_ROWS, t), _BF16),
                   jax.ShapeDtypeStruct((SW_KV_HEADS * VTB_ROWS, t), _BF16)],
        compiler_params=_params("parallel"),
        name="in_proj_vt",
    )(xb, wt_all)


def _diff_attn_kernel(slope_ref, prm_ref, g_ref, q_ref, k_ref, vt_ref, o_ref,
                      acc_ref, m_ref, qaug_ref, kaug_ref, s_ref, smax_ref, p_ref, alpha_ref, *, tq, tk):
    h = pl.program_id(1)
    i = pl.program_id(2)
    diag = tq // tk
    n_full = i * diag
    lane = lax.broadcasted_iota(jnp.int32, (tq, DA_HEAD_DIM), 1)
    slope_tile = jnp.zeros((tq, DA_HEAD_DIM), _F32)
    slope = 0.0
    for part in range(N_SLOPE_PARTS):
        piece = slope_ref[h, part]
        slope = slope + piece
        slope_tile = jnp.where(lane // 2 == part, piece, slope_tile)
    klane = lax.broadcasted_iota(jnp.int32, (tk, DA_HEAD_DIM), 1)
    kpos = lax.broadcasted_iota(jnp.int32, (tk, DA_HEAD_DIM), 0)
    pos_tile = jnp.where(klane < 2 * N_SLOPE_PARTS, jnp.where(klane % 2 == 0, kpos % 256, kpos // 256 * 256), 0)
    for c in range(2):
        kaug_ref[c, :, DA_HEAD_DIM:] = pos_tile.astype(_F32).astype(_BF16)
        qaug_ref[c, :, :DA_HEAD_DIM] = q_ref[:, c * DA_HEAD_DIM:(c + 1) * DA_HEAD_DIM]
        qaug_ref[c, :, DA_HEAD_DIM:] = slope_tile.astype(_BF16)
    m_ref[...] = jnp.full(m_ref.shape, NEG_INF, _F32)
    acc_ref[...] = jnp.zeros(acc_ref.shape, _F32)

    def scores(j):
        start = pl.multiple_of(j * tk, tk)
        for c in range(2):
            kaug_ref[c, :, :DA_HEAD_DIM] = k_ref[pl.ds(start, tk), c * DA_HEAD_DIM:(c + 1) * DA_HEAD_DIM]
            s = lax.dot_general(kaug_ref[c], qaug_ref[c], (((1,), (1,)), ((), ())),
                                preferred_element_type=_F32)
            s_ref[c] = s
            smax_ref[c] = jnp.max(s, axis=0, keepdims=True)

    def weights(j, boundary):
        off = (j * tk - i * tq).astype(_F32) * slope
        for c in range(2):
            s = s_ref[c]
            if boundary is None:
                smax = smax_ref[c]
            else:
                keys = lax.broadcasted_iota(jnp.int32, (tk, tq), 0) + boundary * tk
                queries = lax.broadcasted_iota(jnp.int32, (tk, tq), 1)
                s = jnp.where(keys <= queries, s, NEG_INF)
                smax = jnp.max(s, axis=0, keepdims=True)
            m_old = m_ref[c]
            m_new = jnp.maximum(m_old, smax + off)
            alpha_ref[c] = jnp.exp2(m_old - m_new)
            p_ref[c] = jnp.exp2(s - (m_new - off)).astype(_BF16)
            m_ref[c] = m_new

    def accumulate(j):
        start = pl.multiple_of(j * tk, tk)
        vt = vt_ref[:, pl.ds(start, tk)]
        for c in range(2):
            acc_ref[c] = alpha_ref[c] * acc_ref[c] + jnp.dot(vt, p_ref[c], preferred_element_type=_F32)

    def body(j, carry):
        accumulate(j)
        weights(j + 1, None)
        scores(j + 2)
        return carry

    scores(0)

    @pl.when(i >= 1)
    def _():
        weights(0, None)
        scores(1)
        lax.fori_loop(0, n_full - 1, body, 0)
        accumulate(n_full - 1)

    for d in range(diag):
        weights(n_full + d, d)
        if d + 1 < diag:
            scores(n_full + d + 1)
        accumulate(n_full + d)

    prm = prm_ref[...]
    lam_init = prm[4:5, 0:1]
    lam = (jnp.exp(jnp.sum(prm[0:1] * prm[1:2], axis=1, keepdims=True))
           - jnp.exp(jnp.sum(prm[2:3] * prm[3:4], axis=1, keepdims=True)) + lam_init)
    o0 = acc_ref[0, :DA_WIDTH, :] / acc_ref[0, DA_WIDTH:DA_WIDTH + 1, :]
    o1 = acc_ref[1, :DA_WIDTH, :] / acc_ref[1, DA_WIDTH:DA_WIDTH + 1, :]
    a = o0 - lam * o1
    ms = jnp.mean(a * a, axis=0, keepdims=True)
    out = a * lax.rsqrt(ms + LN_EPS) * g_ref[...] * (1.0 - lam_init)
    o_ref[...] = out.T.astype(o_ref.dtype)


def _diff_attention(att, vt, slopes, prm, subln_g, batch, seq):
    tq, tk = ATT_TQ, ATT_TK
    assert tq % tk == 0 and seq % tq == 0
    nq = seq // tq
    t = batch * seq
    kernel = functools.partial(_diff_attn_kernel, tq=tq, tk=tk)
    return pl.pallas_call(
        kernel,
        grid=(batch, DA_HEADS, nq),
        in_specs=[
            pl.BlockSpec(memory_space=pltpu.SMEM),
            pl.BlockSpec((8, LANES), lambda b, h, i: (0, 0)),
            pl.BlockSpec((DA_WIDTH, 1), lambda b, h, i: (0, 0)),
            pl.BlockSpec((tq, DA_WIDTH), lambda b, h, i: (b * nq + i, QA_OFF // DA_WIDTH + h)),
            pl.BlockSpec((seq, DA_WIDTH), lambda b, h, i: (b, KA_OFF // DA_WIDTH + h)),
            pl.BlockSpec((VT_ROWS, seq), lambda b, h, i: (h, b)),
        ],
        out_specs=pl.BlockSpec((tq, DA_WIDTH), lambda b, h, i: (b * nq + i, h)),
        out_shape=jax.ShapeDtypeStruct((t, D_MODEL), _BF16),
        scratch_shapes=[
            pltpu.VMEM((2, VT_ROWS, tq), _F32),
            pltpu.VMEM((2, 1, tq), _F32),
            pltpu.VMEM((2, tq, 2 * DA_HEAD_DIM), _BF16),
            pltpu.VMEM((2, tk, 2 * DA_HEAD_DIM), _BF16),
            pltpu.VMEM((2, tk, tq), _F32),
            pltpu.VMEM((2, 1, tq), _F32),
            pltpu.VMEM((2, tk, tq), _BF16),
            pltpu.VMEM((2, 1, tq), _F32),
        ],
        compiler_params=_params("parallel", "parallel", "arbitrary"),
        name="diff_attn",
    )(slopes, prm, subln_g, att, att, vt)


def _swa_kernel(sink_ref, bias_ref, q_ref, kv_ref, kvp_ref, vt_ref, vtp_ref, o_ref):
    i = pl.program_id(1)
    first = (i == 0).astype(jnp.int32)
    nt = (((1,), (1,)), ((), ()))
    kv_all = jnp.concatenate([kvp_ref[...], kv_ref[...]], axis=0)
    vt_all = jnp.concatenate([vtp_ref[...], vt_ref[...]], axis=1)
    for r in range(SWA_TB // WINDOW):
        heads = []
        for kh in range(SW_KV_HEADS):
            kwin = kv_all[r * WINDOW:(r + 2) * WINDOW, kh * SW_HEAD_DIM:(kh + 1) * SW_HEAD_DIM]
            vwin = vt_all[kh * VTB_ROWS:(kh + 1) * VTB_ROWS, r * WINDOW:(r + 2) * WINDOW]
            qs = jnp.concatenate(
                [q_ref[r * WINDOW:(r + 1) * WINDOW, (kh * SW_GROUP + g) * SW_HEAD_DIM:(kh * SW_GROUP + g + 1) * SW_HEAD_DIM]
                 for g in range(SW_GROUP)], axis=0)
            s = lax.dot_general(kwin, qs, nt, preferred_element_type=_F32)
            bias = bias_ref[2 * kh + first] if r == 0 else bias_ref[2 * kh]
            s = s + bias
            sink = sink_ref[kh]
            m = jnp.maximum(jnp.max(s, axis=0, keepdims=True), sink)
            p = jnp.exp2(s - m)
            ot = jnp.dot(vwin, p.astype(_BF16), preferred_element_type=_F32)
            denom = ot[SW_HEAD_DIM:SW_HEAD_DIM + 1, :] + jnp.exp2(sink - m)
            ot = ot[:SW_HEAD_DIM, :] / denom
            for g in range(SW_GROUP):
                heads.append(ot[:, g * WINDOW:(g + 1) * WINDOW].T)
        o_ref[r * WINDOW:(r + 1) * WINDOW, :] = jnp.concatenate(heads, axis=1).astype(o_ref.dtype)


def _swa_bias():
    slopes = 2.0 ** (-8.0 * jnp.arange(1, SW_HEADS + 1, dtype=_F32) / SW_HEADS)
    kj = jnp.arange(2 * WINDOW, dtype=jnp.int32)[:, None]
    qi = jnp.arange(WINDOW, dtype=jnp.int32)[None, :]
    dist = qi + WINDOW - kj
    valid = (dist >= 0) & (dist < WINDOW)
    tables = []
    for kh in range(SW_KV_HEADS):
        for has_prev in (True, False):
            ok = valid if has_prev else valid & (kj >= WINDOW)
            per_head = [jnp.where(ok, -slopes[kh * SW_GROUP + g] * LOG2E * dist.astype(_F32), NEG_INF)
                        for g in range(SW_GROUP)]
            tables.append(jnp.concatenate(per_head, axis=1))
    return jnp.stack(tables)


def _sliding_window_attention(att, vtb, sink_rows, bias, batch, seq):
    tb = SWA_TB
    nb = seq // tb
    ratio = tb // WINDOW
    t = batch * seq
    kv_width = 2 * SW_KV_HEADS * SW_HEAD_DIM
    n_b = SW_KV_HEADS * VTB_ROWS
    q_blk = QB_OFF // D_MODEL
    kv_blk = KVB_OFF // kv_width
    prev = lambda b, i: jnp.maximum((b * nb + i) * ratio - 1, 0)
    return pl.pallas_call(
        _swa_kernel,
        grid=(batch, nb),
        in_specs=[
            pl.BlockSpec((SW_KV_HEADS, 1, SW_GROUP * WINDOW), lambda b, i: (0, 0, 0)),
            pl.BlockSpec((2 * SW_KV_HEADS, 2 * WINDOW, SW_GROUP * WINDOW), lambda b, i: (0, 0, 0)),
            pl.BlockSpec((tb, D_MODEL), lambda b, i: (b * nb + i, q_blk)),
            pl.BlockSpec((tb, kv_width), lambda b, i: (b * nb + i, kv_blk)),
            pl.BlockSpec((WINDOW, kv_width), lambda b, i: (prev(b, i), kv_blk)),
            pl.BlockSpec((n_b, tb), lambda b, i: (0, b * nb + i)),
            pl.BlockSpec((n_b, WINDOW), lambda b, i: (0, prev(b, i))),
        ],
        out_specs=pl.BlockSpec((tb, D_MODEL), lambda b, i: (b * nb + i, 0)),
        out_shape=jax.ShapeDtypeStruct((t, D_MODEL), _BF16),
        compiler_params=_params("parallel", "arbitrary"),
        name="swa",
    )(sink_rows, bias, att, att, att, vtb, vtb)


def _layer_norm(z, g, b):
    mu = jnp.mean(z, axis=1, keepdims=True)
    d = z - mu
    var = jnp.mean(d * d, axis=1, keepdims=True)
    return d * lax.rsqrt(var + LN_EPS) * g + b


def _merge_out_kernel(ga_ref, gb_ref, oa_ref, ob_ref, x_ref, wo_ref, lng_ref, lnb_ref, wr_ref, br_ref,
                      h_ref, hb_ref, idx_ref, gate_ref, cnt_ref, run_ref):
    @pl.when(pl.program_id(0) == 0)
    def _():
        run_ref[...] = jnp.zeros(run_ref.shape, _F32)

    merged = (jax.nn.sigmoid(ga_ref[...].astype(_F32)) * oa_ref[...].astype(_F32)
              + jax.nn.sigmoid(gb_ref[...].astype(_F32)) * ob_ref[...].astype(_F32))
    y = jnp.dot(merged.astype(_BF16), wo_ref[0], preferred_element_type=_F32)
    hn = _layer_norm(DEEPNORM_ALPHA * x_ref[...] + y, lng_ref[0, 0:1, :], lnb_ref[0, 0:1, :])
    h_ref[...] = hn
    hb = hn.astype(_BF16)
    hb_ref[...] = hb
    logits = jnp.dot(hb, wr_ref[0], preferred_element_type=_F32) + br_ref[0]
    tm = logits.shape[0]
    ecol = lax.broadcasted_iota(jnp.int32, (tm, N_EXPERTS), 1)
    lane = lax.broadcasted_iota(jnp.int32, (tm, LANES), 1)
    cur = logits
    vals, picks, idx_out = [], [], jnp.zeros((tm, LANES), jnp.int32)
    for k in range(TOP_K):
        mx = jnp.max(cur, axis=1, keepdims=True)
        ix = jnp.min(jnp.where(cur == mx, ecol, N_EXPERTS), axis=1, keepdims=True)
        vals.append(mx)
        picks.append(ecol == ix)
        idx_out = jnp.where(lane == k, ix, idx_out)
        cur = jnp.where(picks[k], -jnp.inf, cur)
    exps = [jnp.exp(v - vals[0]) for v in vals]
    tot = exps[0] + exps[1] + exps[2] + exps[3]
    gate_out = jnp.zeros((tm, LANES), _F32)
    for k in range(TOP_K):
        gate_out = jnp.where(lane == k, exps[k] / tot, gate_out)
    gate_ref[...] = gate_out

    chosen = jnp.zeros((tm, N_EXPERTS), _F32)
    for k in range(TOP_K):
        chosen = chosen + jnp.where(picks[k], 1.0, 0.0)
    r = lax.broadcasted_iota(jnp.int32, (tm, tm), 0)
    c = lax.broadcasted_iota(jnp.int32, (tm, tm), 1)
    earlier = jnp.where(c < r, 1.0, 0.0).astype(_BF16)
    before = jnp.dot(earlier, chosen.astype(_BF16), preferred_element_type=_F32) + run_ref[...]
    for k in range(TOP_K):
        rank = jnp.sum(jnp.where(picks[k], before, 0.0), axis=1, keepdims=True)
        idx_out = jnp.where(lane == TOP_K + k, rank.astype(jnp.int32), idx_out)
    idx_ref[...] = idx_out
    total = run_ref[...] + jnp.sum(chosen, axis=0, keepdims=True)
    run_ref[...] = total
    cnt_ref[...] = jnp.broadcast_to(total, cnt_ref.shape).astype(jnp.int32)


def _merge_out(gates, o_a, o_b, x, wo_all, ln_g, ln_b, wr_all, br_all, layer):
    t = x.shape[0]
    tm = OUT_TM
    row = lambda i: (i, 0)
    return pl.pallas_call(
        _merge_out_kernel,
        grid=(t // tm,),
        in_specs=[
            pl.BlockSpec((tm, D_MODEL), lambda i: (i, 0)),
            pl.BlockSpec((tm, D_MODEL), lambda i: (i, 1)),
            pl.BlockSpec((tm, D_MODEL), row),
            pl.BlockSpec((tm, D_MODEL), row),
            pl.BlockSpec((tm, D_MODEL), row),
            pl.BlockSpec((1, D_MODEL, D_MODEL), lambda i: (layer, 0, 0)),
            pl.BlockSpec((1, 2, D_MODEL), lambda i: (layer, 0, 0)),
            pl.BlockSpec((1, 2, D_MODEL), lambda i: (layer, 0, 0)),
            pl.BlockSpec((1, D_MODEL, N_EXPERTS), lambda i: (layer, 0, 0)),
            pl.BlockSpec((1, 1, N_EXPERTS), lambda i: (layer, 0, 0)),
        ],
        out_specs=[
            pl.BlockSpec((tm, D_MODEL), row),
            pl.BlockSpec((tm, D_MODEL), row),
            pl.BlockSpec((tm, LANES), row),
            pl.BlockSpec((tm, LANES), row),
            pl.BlockSpec((8, N_EXPERTS), lambda i: (0, 0)),
        ],
        out_shape=[
            jax.ShapeDtypeStruct((t, D_MODEL), _F32),
            jax.ShapeDtypeStruct((t, D_MODEL), _BF16),
            jax.ShapeDtypeStruct((t, LANES), jnp.int32),
            jax.ShapeDtypeStruct((t, LANES), _F32),
            jax.ShapeDtypeStruct((8, N_EXPERTS), jnp.int32),
        ],
        scratch_shapes=[pltpu.VMEM((1, N_EXPERTS), _F32)],
        compiler_params=_params("arbitrary"),
        name="merge_out_ln_router",
    )(gates, gates, o_a, o_b, x, wo_all, ln_g, ln_b, wr_all, br_all)


def _expert_kernel(be_ref, nused_ref, x_ref, wup_ref, bg_ref, bu_ref, wd_ref, bd_ref, y_ref,
                   wg_scr, wu_scr, wd_scr):
    i = pl.program_id(0)
    changed = jnp.logical_or(i == 0, be_ref[i] != be_ref[jnp.maximum(i - 1, 0)])

    @pl.when(jnp.logical_and(changed, i < nused_ref[0]))
    def _():
        half = DEINT_TILE // 2
        r = lax.broadcasted_iota(jnp.int32, (DEINT_TILE, DEINT_TILE), 0)
        c = lax.broadcasted_iota(jnp.int32, (DEINT_TILE, DEINT_TILE), 1)
        sel = jnp.where(r == jnp.where(c < half, 2 * c, 2 * (c - half) + 1), 1.0, 0.0).astype(_BF16)
        for t in range(2 * D_FF // DEINT_TILE):
            w = wup_ref[0, 0, :, t * DEINT_TILE:(t + 1) * DEINT_TILE].astype(_BF16)
            de = jnp.dot(w, sel, preferred_element_type=_F32).astype(_BF16)
            wg_scr[:, t * half:(t + 1) * half] = de[:, :half]
            wu_scr[:, t * half:(t + 1) * half] = de[:, half:]
        wd_scr[...] = wd_ref[0, 0].astype(_BF16)

    @pl.when(i < nused_ref[0])
    def _():
        x = x_ref[...]
        gate = jnp.dot(x, wg_scr[...], preferred_element_type=_F32) + bg_ref[0, 0]
        up = jnp.dot(x, wu_scr[...], preferred_element_type=_F32) + bu_ref[0, 0]
        gate = jnp.minimum(gate, SWIGLU_LIMIT)
        up = jnp.clip(up, -SWIGLU_LIMIT, SWIGLU_LIMIT)
        act = (up + 1.0) * (gate * jax.nn.sigmoid(SWIGLU_ALPHA * gate))
        y = jnp.dot(act.astype(_BF16), wd_scr[...], preferred_element_type=_F32) + bd_ref[0, 0]
        y_ref[...] = y.astype(y_ref.dtype)

    @pl.when(i >= nused_ref[0])
    def _():
        y_ref[...] = jnp.zeros(y_ref.shape, y_ref.dtype)


def _experts(block_expert, n_used, xb, w_up, bg_all, bu_all, w_down, bd_all, layer):
    n_rows = xb.shape[0]
    n_blocks = n_rows // MOE_BLOCK
    pick = lambda i, be, nu: (layer, be[i], 0, 0)
    bspec = pl.BlockSpec((1, 1, 1, D_FF), pick)
    grid_spec = pltpu.PrefetchScalarGridSpec(
        num_scalar_prefetch=2,
        grid=(n_blocks,),
        in_specs=[
            pl.BlockSpec((MOE_BLOCK, D_MODEL), lambda i, be, nu: (i, 0)),
            pl.BlockSpec((1, 1, D_MODEL, 2 * D_FF), pick),
            bspec, bspec,
            pl.BlockSpec((1, 1, D_FF, D_MODEL), pick),
            pl.BlockSpec((1, 1, 1, D_MODEL), pick),
        ],
        out_specs=pl.BlockSpec((MOE_BLOCK, D_MODEL), lambda i, be, nu: (i, 0)),
        scratch_shapes=[
            pltpu.VMEM((D_MODEL, D_FF), _BF16),
            pltpu.VMEM((D_MODEL, D_FF), _BF16),
            pltpu.VMEM((D_FF, D_MODEL), _BF16),
        ],
    )
    return pl.pallas_call(
        _expert_kernel,
        grid_spec=grid_spec,
        out_shape=jax.ShapeDtypeStruct((n_rows, D_MODEL), _BF16),
        compiler_params=_params("arbitrary"),
        name="experts",
    )(block_expert, n_used, xb, w_up, bg_all, bu_all, w_down, bd_all)


def _combine_kernel(h_ref, yg_ref, gate_ref, lng_ref, lnb_ref, x_ref, xb_ref):
    gates = gate_ref[...]
    y = gates[:, 0:1] * yg_ref[0].astype(_F32)
    for k in range(1, TOP_K):
        y = y + gates[:, k:k + 1] * yg_ref[k].astype(_F32)
    xn = _layer_norm(DEEPNORM_ALPHA * h_ref[...] + y, lng_ref[0, 1:2, :], lnb_ref[0, 1:2, :])
    x_ref[...] = xn
    xb_ref[...] = xn.astype(_BF16)


def _combine(h, yg, gates, ln_g, ln_b, layer):
    t = h.shape[0]
    tm = LN2_TM
    row = lambda i: (i, 0)
    return pl.pallas_call(
        _combine_kernel,
        grid=(t // tm,),
        in_specs=[
            pl.BlockSpec((tm, D_MODEL), row),
            pl.BlockSpec((TOP_K, tm, D_MODEL), lambda i: (0, i, 0)),
            pl.BlockSpec((tm, LANES), row),
            pl.BlockSpec((1, 2, D_MODEL), lambda i: (layer, 0, 0)),
            pl.BlockSpec((1, 2, D_MODEL), lambda i: (layer, 0, 0)),
        ],
        out_specs=[pl.BlockSpec((tm, D_MODEL), row), pl.BlockSpec((tm, D_MODEL), row)],
        out_shape=[jax.ShapeDtypeStruct((t, D_MODEL), _F32), jax.ShapeDtypeStruct((t, D_MODEL), _BF16)],
        compiler_params=_params("parallel"),
        name="combine_ln",
    )(h, yg, gates, ln_g, ln_b)


def _dispatch_tables(route, counts, t):
    tk = t * TOP_K
    experts = route[:, :TOP_K]
    ranks = route[:, TOP_K:2 * TOP_K]
    padded = (counts + MOE_BLOCK - 1) // MOE_BLOCK * MOE_BLOCK
    pad_end = jnp.cumsum(padded)
    pad_start = pad_end - padded
    start = jnp.cumsum(counts) - counts
    expert_ids = jnp.arange(N_EXPERTS, dtype=jnp.int32)
    dest = jnp.sum(jnp.where(experts[..., None] == expert_ids, pad_start, 0), axis=-1) + ranks
    n_blocks = (tk + N_EXPERTS * (MOE_BLOCK - 1) + MOE_BLOCK - 1) // MOE_BLOCK
    block_start = jnp.arange(n_blocks, dtype=jnp.int32) * MOE_BLOCK
    block_expert = jnp.minimum(
        jnp.sum((pad_end[None, :] <= block_start[:, None]).astype(jnp.int32), axis=1), N_EXPERTS - 1)
    n_used = (pad_end[-1:] // MOE_BLOCK).astype(jnp.int32)
    order = jnp.argsort(experts.reshape(tk))
    in_block = jnp.arange(MOE_BLOCK, dtype=jnp.int32)[None, :]
    r = (block_start - pad_start[block_expert])[:, None] + in_block
    valid = r < counts[block_expert][:, None]
    src = jnp.where(valid, start[block_expert][:, None] + r, 0).reshape(n_blocks * MOE_BLOCK)
    filler = jnp.arange(n_blocks * MOE_BLOCK, dtype=jnp.int32) % t
    buf_tok = jnp.where(valid.reshape(-1), order[src] // TOP_K, filler)
    return buf_tok, dest, block_expert, n_used


def kernel(x, w_in, w_o, lambda_qk, subln_g, sinks, ln_g, ln_b, w_router, b_router, w_up, b_up, w_down, b_down):
    batch, seq, _ = x.shape
    t = batch * seq
    assert seq % ATT_TQ == 0 and seq % SWA_TB == 0 and t % PROJ_TM == 0

    va_lo, va_hi = 2 * D_MODEL, 3 * D_MODEL
    gates_lo = w_in.shape[-1] - GATE_COLS
    w_att_b = jnp.concatenate([w_in[..., :va_lo], w_in[..., va_hi:gates_lo]], axis=-1).astype(_BF16)
    vb_lo = gates_lo - SW_KV_HEADS * SW_HEAD_DIM
    w_vt_b = jnp.swapaxes(jnp.concatenate([w_in[..., va_lo:va_hi], w_in[..., vb_lo:gates_lo]], axis=-1),
                          1, 2).astype(_BF16)
    w_gates_b = w_in[..., gates_lo:].astype(_BF16)
    w_o_b = w_o.astype(_BF16)
    w_r_b = w_router.astype(_BF16)
    b_gate = b_up[..., 0::2].reshape(DEPTH, N_EXPERTS, 1, D_FF)
    b_upp = b_up[..., 1::2].reshape(DEPTH, N_EXPERTS, 1, D_FF)
    b_down4 = b_down.reshape(DEPTH, N_EXPERTS, 1, D_MODEL)
    b_router3 = b_router.reshape(DEPTH, 1, N_EXPERTS)

    att_scale = jnp.concatenate([
        jnp.full((D_MODEL,), DA_HEAD_DIM ** -0.5 * LOG2E, _F32), jnp.ones((D_MODEL,), _F32),
        jnp.full((D_MODEL,), SW_HEAD_DIM ** -0.5 * LOG2E, _F32),
        jnp.ones((2 * SW_KV_HEADS * SW_HEAD_DIM,), _F32)]).reshape(1, ATT_COLS)
    gate_scale = jnp.ones((1, GATE_COLS), _F32)
    rest = 2.0 ** (-8.0 * jnp.arange(1, DA_HEADS + 1, dtype=_F32) / DA_HEADS) * LOG2E
    pieces = []
    for _ in range(N_SLOPE_PARTS):
        piece = rest.astype(_BF16).astype(_F32)
        pieces.append(piece)
        rest = rest - piece
    da_slopes = jnp.stack(pieces, axis=1)
    swa_bias = _swa_bias()

    xf = x.reshape(t, D_MODEL)
    xb = xf.astype(_BF16)
    for l in range(DEPTH):
        lam_init = 0.8 - 0.6 * math.exp(-0.3 * l)
        prm = jnp.concatenate([lambda_qk[l].astype(_F32), jnp.full((4, DA_HEAD_DIM), lam_init, _F32)], axis=0)
        att = _project(xb, w_att_b, l, ATT_COLS // 2, att_scale, _BF16)
        gates = _project(xb, w_gates_b, l, GATE_COLS // 2, gate_scale, _BF16)
        vt, vtb = _project_vt(xb, w_vt_b, l)
        o_a = _diff_attention(att, vt, da_slopes, prm, subln_g[l].reshape(DA_WIDTH, 1), batch, seq)
        sink_rows = jnp.repeat(sinks[l].astype(_F32).reshape(SW_KV_HEADS, SW_GROUP) * LOG2E, WINDOW, axis=1)
        o_b = _sliding_window_attention(att, vtb, sink_rows.reshape(SW_KV_HEADS, 1, SW_GROUP * WINDOW),
                                        swa_bias, batch, seq)
        h, hb, route, top_gate, counts = _merge_out(gates, o_a, o_b, xf, w_o_b, ln_g, ln_b, w_r_b, b_router3, l)
        buf_tok, pos, block_expert, n_used = _dispatch_tables(route, counts[0], t)
        x_rows = hb[buf_tok]
        yb = _experts(block_expert, n_used, x_rows, w_up, b_gate, b_upp, w_down, b_down4, l)
        yg = yb[pos.T]
        xf, xb = _combine(h, yg, top_gate, ln_g, ln_b, l)
    return xf.reshape(batch, seq, D_MODEL)
```

```python
import functools
import math

import jax
import jax.numpy as jnp
from jax import lax
from jax.experimental import pallas as pl
from jax.experimental.pallas import tpu as pltpu

D_MODEL = 1024
DEPTH = 4
DA_HEAD_DIM = 128
DA_HEADS = 4
DA_WIDTH = 2 * DA_HEAD_DIM
SW_HEAD_DIM = 64
SW_HEADS = 16
SW_KV_HEADS = 2
SW_GROUP = SW_HEADS // SW_KV_HEADS
WINDOW = 128
N_EXPERTS = 32
TOP_K = 4
D_FF = D_MODEL
SWIGLU_LIMIT = 7.0
SWIGLU_ALPHA = 1.702
LN_EPS = 1e-5
DEEPNORM_ALPHA = (2.0 * DEPTH) ** 0.25
NEG_INF = -1e30

QA_OFF, KA_OFF, QB_OFF, KVB_OFF = 0, 1024, 2048, 3072
ATT_COLS = 3328
GATE_COLS = 2 * D_MODEL
LANES = 128
SUBLANES_BF16 = 16
VT_ROWS = DA_WIDTH + SUBLANES_BF16
VTB_ROWS = SW_HEAD_DIM + SUBLANES_BF16
LOG2E = math.log2(math.e)
N_SLOPE_PARTS = 3

PROJ_TM = 1024
ATT_TQ = 1024
ATT_TK = 512
SWA_TB = 256
OUT_TM = 256
MOE_BLOCK = 256
DEINT_TILE = 256
LN2_TM = 256
VMEM_LIMIT = 56 * 1024 * 1024

_F32 = jnp.float32
_BF16 = jnp.bfloat16


def _params(*sem):
    return pltpu.CompilerParams(dimension_semantics=sem, vmem_limit_bytes=VMEM_LIMIT)


def _proj_kernel(x_ref, w_ref, s_ref, o_ref):
    acc = jnp.dot(x_ref[...], w_ref[0], preferred_element_type=_F32)
    o_ref[...] = (acc * s_ref[...]).astype(o_ref.dtype)


def _project(xb, w_all, layer, tn, scale, out_dtype):
    t = xb.shape[0]
    n_cols = w_all.shape[-1]
    assert n_cols % tn == 0 and t % PROJ_TM == 0
    return pl.pallas_call(
        _proj_kernel,
        grid=(n_cols // tn, t // PROJ_TM),
        in_specs=[
            pl.BlockSpec((PROJ_TM, D_MODEL), lambda j, i: (i, 0)),
            pl.BlockSpec((1, D_MODEL, tn), lambda j, i: (layer, 0, j)),
            pl.BlockSpec((1, tn), lambda j, i: (0, j)),
        ],
        out_specs=pl.BlockSpec((PROJ_TM, tn), lambda j, i: (i, j)),
        out_shape=jax.ShapeDtypeStruct((t, n_cols), out_dtype),
        compiler_params=_params("parallel", "parallel"),
        name="in_proj",
    )(xb, w_all, scale)


def _proj_vt_kernel(x_ref, wt_ref, oa_ref, ob_ref):
    x = x_ref[...]
    nt = (((1,), (1,)), ((), ()))
    ones = jnp.ones((SUBLANES_BF16, x.shape[0]), oa_ref.dtype)
    for h in range(DA_HEADS):
        vt = lax.dot_general(wt_ref[0, h * DA_WIDTH:(h + 1) * DA_WIDTH, :], x, nt, preferred_element_type=_F32)
        oa_ref[h * VT_ROWS:h * VT_ROWS + DA_WIDTH, :] = vt.astype(oa_ref.dtype)
        oa_ref[h * VT_ROWS + DA_WIDTH:(h + 1) * VT_ROWS, :] = ones
    vb = lax.dot_general(wt_ref[0, D_MODEL:, :], x, nt, preferred_element_type=_F32)
    for kh in range(SW_KV_HEADS):
        ob_ref[kh * VTB_ROWS:kh * VTB_ROWS + SW_HEAD_DIM, :] = (
            vb[kh * SW_HEAD_DIM:(kh + 1) * SW_HEAD_DIM, :].astype(ob_ref.dtype))
        ob_ref[kh * VTB_ROWS + SW_HEAD_DIM:(kh + 1) * VTB_ROWS, :] = ones


def _project_vt(xb, wt_all, layer):
    t = xb.shape[0]
    n_b = SW_KV_HEADS * SW_HEAD_DIM
    return pl.pallas_call(
        _proj_vt_kernel,
        grid=(t // PROJ_TM,),
        in_specs=[
            pl.BlockSpec((PROJ_TM, D_MODEL), lambda i: (i, 0)),
            pl.BlockSpec((1, D_MODEL + n_b, D_MODEL), lambda i: (layer, 0, 0)),
        ],
        out_specs=[pl.BlockSpec((DA_HEADS * VT_ROWS, PROJ_TM), lambda i: (0, i)),
                   pl.BlockSpec((SW_KV_HEADS * VTB_ROWS, PROJ_TM), lambda i: (0, i))],
        out_shape=[jax.ShapeDtypeStruct((DA_HEADS * VT_ROWS, t), _BF16),
                   jax.ShapeDtypeStruct((SW_KV_HEADS * VTB_ROWS, t), _BF16)],
        compiler_params=_params("parallel"),
        name="in_proj_vt",
    )(xb, wt_all)


def _diff_attn_kernel(slope_ref, prm_ref, g_ref, q_ref, k_ref, vt_ref, o_ref,
                      acc_ref, m_ref, qaug_ref, kaug_ref, s_ref, smax_ref, p_ref, alpha_ref, *, tq, tk):
    h = pl.program_id(1)
    i = pl.program_id(2)
    diag = tq // tk
    n_full = i * diag
    lane = lax.broadcasted_iota(jnp.int32, (tq, DA_HEAD_DIM), 1)
    slope_tile = jnp.zeros((tq, DA_HEAD_DIM), _F32)
    slope = 0.0
    for part in range(N_SLOPE_PARTS):
        piece = slope_ref[h, part]
        slope = slope + piece
        slope_tile = jnp.where(lane // 2 == part, piece, slope_tile)
    klane = lax.broadcasted_iota(jnp.int32, (tk, DA_HEAD_DIM), 1)
    kpos = lax.broadcasted_iota(jnp.int32, (tk, DA_HEAD_DIM), 0)
    pos_tile = jnp.where(klane < 2 * N_SLOPE_PARTS, jnp.where(klane % 2 == 0, kpos % 256, kpos // 256 * 256), 0)
    for c in range(2):
        kaug_ref[c, :, DA_HEAD_DIM:] = pos_tile.astype(_F32).astype(_BF16)
        qaug_ref[c, :, :DA_HEAD_DIM] = q_ref[:, c * DA_HEAD_DIM:(c + 1) * DA_HEAD_DIM]
        qaug_ref[c, :, DA_HEAD_DIM:] = slope_tile.astype(_BF16)
    m_ref[...] = jnp.full(m_ref.shape, NEG_INF, _F32)
    acc_ref[...] = jnp.zeros(acc_ref.shape, _F32)

    def scores(j):
        start = pl.multiple_of(j * tk, tk)
        for c in range(2):
            kaug_ref[c, :, :DA_HEAD_DIM] = k_ref[pl.ds(start, tk), c * DA_HEAD_DIM:(c + 1) * DA_HEAD_DIM]
            s = lax.dot_general(kaug_ref[c], qaug_ref[c], (((1,), (1,)), ((), ())),
                                preferred_element_type=_F32)
            s_ref[c] = s
            smax_ref[c] = jnp.max(s, axis=0, keepdims=True)

    def weights(j, boundary):
        off = (j * tk - i * tq).astype(_F32) * slope
        for c in range(2):
            s = s_ref[c]
            if boundary is None:
                smax = smax_ref[c]
            else:
                keys = lax.broadcasted_iota(jnp.int32, (tk, tq), 0) + boundary * tk
                queries = lax.broadcasted_iota(jnp.int32, (tk, tq), 1)
                s = jnp.where(keys <= queries, s, NEG_INF)
                smax = jnp.max(s, axis=0, keepdims=True)
            m_old = m_ref[c]
            m_new = jnp.maximum(m_old, smax + off)
            alpha_ref[c] = jnp.exp2(m_old - m_new)
            p_ref[c] = jnp.exp2(s - (m_new - off)).astype(_BF16)
            m_ref[c] = m_new

    def accumulate(j):
        start = pl.multiple_of(j * tk, tk)
        vt = vt_ref[:, pl.ds(start, tk)]
        for c in range(2):
            acc_ref[c] = alpha_ref[c] * acc_ref[c] + jnp.dot(vt, p_ref[c], preferred_element_type=_F32)

    def body(j, carry):
        accumulate(j)
        weights(j + 1, None)
        scores(j + 2)
        return carry

    def enter_boundary():
        weights(n_full, 0)
        if diag > 1:
            scores(n_full + 1)

    scores(0)

    @pl.when(i >= 1)
    def _():
        weights(0, None)
        scores(1)
        lax.fori_loop(0, n_full - 1, body, 0)
        accumulate(n_full - 1)
        enter_boundary()

    @pl.when(i == 0)
    def _():
        enter_boundary()

    for d in range(diag):
        accumulate(n_full + d)
        if d + 1 < diag:
            weights(n_full + d + 1, d + 1)
            if d + 2 < diag:
                scores(n_full + d + 2)

    prm = prm_ref[...]
    lam_init = prm[4:5, 0:1]
    lam = (jnp.exp(jnp.sum(prm[0:1] * prm[1:2], axis=1, keepdims=True))
           - jnp.exp(jnp.sum(prm[2:3] * prm[3:4], axis=1, keepdims=True)) + lam_init)
    o0 = acc_ref[0, :DA_WIDTH, :] / acc_ref[0, DA_WIDTH:DA_WIDTH + 1, :]
    o1 = acc_ref[1, :DA_WIDTH, :] / acc_ref[1, DA_WIDTH:DA_WIDTH + 1, :]
    a = o0 - lam * o1
    ms = jnp.mean(a * a, axis=0, keepdims=True)
    out = a * lax.rsqrt(ms + LN_EPS) * g_ref[...] * (1.0 - lam_init)
    o_ref[...] = out.T.astype(o_ref.dtype)


def _diff_attention(att, vt, slopes, prm, subln_g, batch, seq):
    tq, tk = ATT_TQ, ATT_TK
    assert tq % tk == 0 and seq % tq == 0
    nq = seq // tq
    t = batch * seq
    kernel = functools.partial(_diff_attn_kernel, tq=tq, tk=tk)
    return pl.pallas_call(
        kernel,
        grid=(batch, DA_HEADS, nq),
        in_specs=[
            pl.BlockSpec(memory_space=pltpu.SMEM),
            pl.BlockSpec((8, LANES), lambda b, h, i: (0, 0)),
            pl.BlockSpec((DA_WIDTH, 1), lambda b, h, i: (0, 0)),
            pl.BlockSpec((tq, DA_WIDTH), lambda b, h, i: (b * nq + i, QA_OFF // DA_WIDTH + h)),
            pl.BlockSpec((seq, DA_WIDTH), lambda b, h, i: (b, KA_OFF // DA_WIDTH + h)),
            pl.BlockSpec((VT_ROWS, seq), lambda b, h, i: (h, b)),
        ],
        out_specs=pl.BlockSpec((tq, DA_WIDTH), lambda b, h, i: (b * nq + i, h)),
        out_shape=jax.ShapeDtypeStruct((t, D_MODEL), _BF16),
        scratch_shapes=[
            pltpu.VMEM((2, VT_ROWS, tq), _F32),
            pltpu.VMEM((2, 1, tq), _F32),
            pltpu.VMEM((2, tq, 2 * DA_HEAD_DIM), _BF16),
            pltpu.VMEM((2, tk, 2 * DA_HEAD_DIM), _BF16),
            pltpu.VMEM((2, tk, tq), _F32),
            pltpu.VMEM((2, 1, tq), _F32),
            pltpu.VMEM((2, tk, tq), _BF16),
            pltpu.VMEM((2, 1, tq), _F32),
        ],
        compiler_params=_params("parallel", "parallel", "arbitrary"),
        name="diff_attn",
    )(slopes, prm, subln_g, att, att, vt)


def _swa_kernel(sink_ref, bias_ref, q_ref, kv_ref, kvp_ref, vt_ref, vtp_ref, o_ref):
    i = pl.program_id(1)
    first = (i == 0).astype(jnp.int32)
    nt = (((1,), (1,)), ((), ()))
    kv_all = jnp.concatenate([kvp_ref[...], kv_ref[...]], axis=0)
    vt_all = jnp.concatenate([vtp_ref[...], vt_ref[...]], axis=1)
    for r in range(SWA_TB // WINDOW):
        heads = []
        for kh in range(SW_KV_HEADS):
            kwin = kv_all[r * WINDOW:(r + 2) * WINDOW, kh * SW_HEAD_DIM:(kh + 1) * SW_HEAD_DIM]
            vwin = vt_all[kh * VTB_ROWS:(kh + 1) * VTB_ROWS, r * WINDOW:(r + 2) * WINDOW]
            qs = jnp.concatenate(
                [q_ref[r * WINDOW:(r + 1) * WINDOW, (kh * SW_GROUP + g) * SW_HEAD_DIM:(kh * SW_GROUP + g + 1) * SW_HEAD_DIM]
                 for g in range(SW_GROUP)], axis=0)
            s = lax.dot_general(kwin, qs, nt, preferred_element_type=_F32)
            bias = bias_ref[2 * kh + first] if r == 0 else bias_ref[2 * kh]
            s = s + bias
            sink = sink_ref[kh]
            m = jnp.maximum(jnp.max(s, axis=0, keepdims=True), sink)
            p = jnp.exp2(s - m)
            ot = jnp.dot(vwin, p.astype(_BF16), preferred_element_type=_F32)
            denom = ot[SW_HEAD_DIM:SW_HEAD_DIM + 1, :] + jnp.exp2(sink - m)
            ot = ot[:SW_HEAD_DIM, :] / denom
            for g in range(SW_GROUP):
                heads.append(ot[:, g * WINDOW:(g + 1) * WINDOW].T)
        o_ref[r * WINDOW:(r + 1) * WINDOW, :] = jnp.concatenate(heads, axis=1).astype(o_ref.dtype)


def _swa_bias():
    slopes = 2.0 ** (-8.0 * jnp.arange(1, SW_HEADS + 1, dtype=_F32) / SW_HEADS)
    kj = jnp.arange(2 * WINDOW, dtype=jnp.int32)[:, None]
    qi = jnp.arange(WINDOW, dtype=jnp.int32)[None, :]
    dist = qi + WINDOW - kj
    valid = (dist >= 0) & (dist < WINDOW)
    tables = []
    for kh in range(SW_KV_HEADS):
        for has_prev in (True, False):
            ok = valid if has_prev else valid & (kj >= WINDOW)
            per_head = [jnp.where(ok, -slopes[kh * SW_GROUP + g] * LOG2E * dist.astype(_F32), NEG_INF)
                        for g in range(SW_GROUP)]
            tables.append(jnp.concatenate(per_head, axis=1))
    return jnp.stack(tables)


def _sliding_window_attention(att, vtb, sink_rows, bias, batch, seq):
    tb = SWA_TB
    nb = seq // tb
    ratio = tb // WINDOW
    t = batch * seq
    kv_width = 2 * SW_KV_HEADS * SW_HEAD_DIM
    n_b = SW_KV_HEADS * VTB_ROWS
    q_blk = QB_OFF // D_MODEL
    kv_blk = KVB_OFF // kv_width
    prev = lambda b, i: jnp.maximum((b * nb + i) * ratio - 1, 0)
    return pl.pallas_call(
        _swa_kernel,
        grid=(batch, nb),
        in_specs=[
            pl.BlockSpec((SW_KV_HEADS, 1, SW_GROUP * WINDOW), lambda b, i: (0, 0, 0)),
            pl.BlockSpec((2 * SW_KV_HEADS, 2 * WINDOW, SW_GROUP * WINDOW), lambda b, i: (0, 0, 0)),
            pl.BlockSpec((tb, D_MODEL), lambda b, i: (b * nb + i, q_blk)),
            pl.BlockSpec((tb, kv_width), lambda b, i: (b * nb + i, kv_blk)),
            pl.BlockSpec((WINDOW, kv_width), lambda b, i: (prev(b, i), kv_blk)),
            pl.BlockSpec((n_b, tb), lambda b, i: (0, b * nb + i)),
            pl.BlockSpec((n_b, WINDOW), lambda b, i: (0, prev(b, i))),
        ],
        out_specs=pl.BlockSpec((tb, D_MODEL), lambda b, i: (b * nb + i, 0)),
        out_shape=jax.ShapeDtypeStruct((t, D_MODEL), _BF16),
        compiler_params=_params("parallel", "arbitrary"),
        name="swa",
    )(sink_rows, bias, att, att, att, vtb, vtb)


def _layer_norm(z, g, b):
    mu = jnp.mean(z, axis=1, keepdims=True)
    d = z - mu
    var = jnp.mean(d * d, axis=1, keepdims=True)
    return d * lax.rsqrt(var + LN_EPS) * g + b


def _merge_out_kernel(ga_ref, gb_ref, oa_ref, ob_ref, x_ref, wo_ref, lng_ref, lnb_ref, wr_ref, br_ref,
                      h_ref, hb_ref, idx_ref, gate_ref, cnt_ref, run_ref):
    @pl.when(pl.program_id(0) == 0)
    def _():
        run_ref[...] = jnp.zeros(run_ref.shape, _F32)

    merged = (jax.nn.sigmoid(ga_ref[...].astype(_F32)) * oa_ref[...].astype(_F32)
              + jax.nn.sigmoid(gb_ref[...].astype(_F32)) * ob_ref[...].astype(_F32))
    y = jnp.dot(merged.astype(_BF16), wo_ref[0], preferred_element_type=_F32)
    hn = _layer_norm(DEEPNORM_ALPHA * x_ref[...] + y, lng_ref[0, 0:1, :], lnb_ref[0, 0:1, :])
    h_ref[...] = hn
    hb = hn.astype(_BF16)
    hb_ref[...] = hb
    logits = lax.dot_general(wr_ref[0], hb, (((1,), (1,)), ((), ())),
                             preferred_element_type=_F32) + br_ref[0]
    tm = logits.shape[1]
    erow = lax.broadcasted_iota(jnp.int32, (N_EXPERTS, tm), 0)
    cur = logits
    vals, picks, id_rows = [], [], []
    for k in range(TOP_K):
        mx = jnp.max(cur, axis=0, keepdims=True)
        ix = jnp.min(jnp.where(cur == mx, erow, N_EXPERTS), axis=0, keepdims=True)
        vals.append(mx)
        picks.append(erow == ix)
        id_rows.append(ix)
        cur = jnp.where(picks[k], -jnp.inf, cur)
    exps = [jnp.exp(v - vals[0]) for v in vals]
    tot = exps[0] + exps[1] + exps[2] + exps[3]
    gate_rows = [e / tot for e in exps] + [jnp.zeros((LANES - TOP_K, tm), _F32)]
    gate_ref[...] = jnp.concatenate(gate_rows, axis=0).T

    chosen = jnp.zeros((N_EXPERTS, tm), _F32)
    for k in range(TOP_K):
        chosen = chosen + jnp.where(picks[k], 1.0, 0.0)
    r = lax.broadcasted_iota(jnp.int32, (tm, tm), 0)
    c = lax.broadcasted_iota(jnp.int32, (tm, tm), 1)
    earlier = jnp.where(r < c, 1.0, 0.0).astype(_BF16)
    before = jnp.dot(chosen.astype(_BF16), earlier, preferred_element_type=_F32) + run_ref[...]
    rank_rows = [jnp.sum(jnp.where(picks[k], before, 0.0), axis=0, keepdims=True).astype(jnp.int32)
                 for k in range(TOP_K)]
    idx_ref[...] = jnp.concatenate(id_rows + rank_rows, axis=0)
    total = run_ref[...] + jnp.sum(chosen, axis=1, keepdims=True)
    run_ref[...] = total
    cnt_ref[...] = jnp.broadcast_to(total, cnt_ref.shape).astype(jnp.int32)


def _merge_out(gates, o_a, o_b, x, wo_all, ln_g, ln_b, wr_all, br_all, layer):
    t = x.shape[0]
    tm = OUT_TM
    row = lambda i: (i, 0)
    return pl.pallas_call(
        _merge_out_kernel,
        grid=(t // tm,),
        in_specs=[
            pl.BlockSpec((tm, D_MODEL), lambda i: (i, 0)),
            pl.BlockSpec((tm, D_MODEL), lambda i: (i, 1)),
            pl.BlockSpec((tm, D_MODEL), row),
            pl.BlockSpec((tm, D_MODEL), row),
            pl.BlockSpec((tm, D_MODEL), row),
            pl.BlockSpec((1, D_MODEL, D_MODEL), lambda i: (layer, 0, 0)),
            pl.BlockSpec((1, 2, D_MODEL), lambda i: (layer, 0, 0)),
            pl.BlockSpec((1, 2, D_MODEL), lambda i: (layer, 0, 0)),
            pl.BlockSpec((1, N_EXPERTS, D_MODEL), lambda i: (layer, 0, 0)),
            pl.BlockSpec((1, N_EXPERTS, 1), lambda i: (layer, 0, 0)),
        ],
        out_specs=[
            pl.BlockSpec((tm, D_MODEL), row),
            pl.BlockSpec((tm, D_MODEL), row),
            pl.BlockSpec((2 * TOP_K, tm), lambda i: (0, i)),
            pl.BlockSpec((tm, LANES), row),
            pl.BlockSpec((N_EXPERTS, LANES), lambda i: (0, 0)),
        ],
        out_shape=[
            jax.ShapeDtypeStruct((t, D_MODEL), _F32),
            jax.ShapeDtypeStruct((t, D_MODEL), _BF16),
            jax.ShapeDtypeStruct((2 * TOP_K, t), jnp.int32),
            jax.ShapeDtypeStruct((t, LANES), _F32),
            jax.ShapeDtypeStruct((N_EXPERTS, LANES), jnp.int32),
        ],
        scratch_shapes=[pltpu.VMEM((N_EXPERTS, 1), _F32)],
        compiler_params=_params("arbitrary"),
        name="merge_out_ln_router",
    )(gates, gates, o_a, o_b, x, wo_all, ln_g, ln_b, wr_all, br_all)


def _expert_kernel(be_ref, nused_ref, x_ref, wup_ref, bg_ref, bu_ref, wd_ref, bd_ref, y_ref,
                   wg_scr, wu_scr, wd_scr):
    i = pl.program_id(0)
    changed = jnp.logical_or(i == 0, be_ref[i] != be_ref[jnp.maximum(i - 1, 0)])

    @pl.when(jnp.logical_and(changed, i < nused_ref[0]))
    def _():
        half = DEINT_TILE // 2
        r = lax.broadcasted_iota(jnp.int32, (DEINT_TILE, DEINT_TILE), 0)
        c = lax.broadcasted_iota(jnp.int32, (DEINT_TILE, DEINT_TILE), 1)
        sel = jnp.where(r == jnp.where(c < half, 2 * c, 2 * (c - half) + 1), 1.0, 0.0).astype(_BF16)
        for t in range(2 * D_FF // DEINT_TILE):
            w = wup_ref[0, 0, :, t * DEINT_TILE:(t + 1) * DEINT_TILE].astype(_BF16)
            de = jnp.dot(w, sel, preferred_element_type=_F32).astype(_BF16)
            wg_scr[:, t * half:(t + 1) * half] = de[:, :half]
            wu_scr[:, t * half:(t + 1) * half] = de[:, half:]
        wd_scr[...] = wd_ref[0, 0].astype(_BF16)

    @pl.when(i < nused_ref[0])
    def _():
        x = x_ref[...]
        gate = jnp.dot(x, wg_scr[...], preferred_element_type=_F32) + bg_ref[0, 0]
        up = jnp.dot(x, wu_scr[...], preferred_element_type=_F32) + bu_ref[0, 0]
        gate = jnp.minimum(gate, SWIGLU_LIMIT)
        up = jnp.clip(up, -SWIGLU_LIMIT, SWIGLU_LIMIT)
        act = (up + 1.0) * (gate * jax.nn.sigmoid(SWIGLU_ALPHA * gate))
        y = jnp.dot(act.astype(_BF16), wd_scr[...], preferred_element_type=_F32) + bd_ref[0, 0]
        y_ref[...] = y.astype(y_ref.dtype)

    @pl.when(i >= nused_ref[0])
    def _():
        y_ref[...] = jnp.zeros(y_ref.shape, y_ref.dtype)


def _experts(block_expert, n_used, xb, w_up, bg_all, bu_all, w_down, bd_all, layer):
    n_rows = xb.shape[0]
    n_blocks = n_rows // MOE_BLOCK
    pick = lambda i, be, nu: (layer, be[i], 0, 0)
    bspec = pl.BlockSpec((1, 1, 1, D_FF), pick)
    grid_spec = pltpu.PrefetchScalarGridSpec(
        num_scalar_prefetch=2,
        grid=(n_blocks,),
        in_specs=[
            pl.BlockSpec((MOE_BLOCK, D_MODEL), lambda i, be, nu: (i, 0)),
            pl.BlockSpec((1, 1, D_MODEL, 2 * D_FF), pick),
            bspec, bspec,
            pl.BlockSpec((1, 1, D_FF, D_MODEL), pick),
            pl.BlockSpec((1, 1, 1, D_MODEL), pick),
        ],
        out_specs=pl.BlockSpec((MOE_BLOCK, D_MODEL), lambda i, be, nu: (i, 0)),
        scratch_shapes=[
            pltpu.VMEM((D_MODEL, D_FF), _BF16),
            pltpu.VMEM((D_MODEL, D_FF), _BF16),
            pltpu.VMEM((D_FF, D_MODEL), _BF16),
        ],
    )
    return pl.pallas_call(
        _expert_kernel,
        grid_spec=grid_spec,
        out_shape=jax.ShapeDtypeStruct((n_rows, D_MODEL), _BF16),
        compiler_params=_params("arbitrary"),
        name="experts",
    )(block_expert, n_used, xb, w_up, bg_all, bu_all, w_down, bd_all)


def _combine_kernel(h_ref, yg_ref, gate_ref, lng_ref, lnb_ref, x_ref, xb_ref):
    gates = gate_ref[...]
    y = gates[:, 0:1] * yg_ref[0].astype(_F32)
    for k in range(1, TOP_K):
        y = y + gates[:, k:k + 1] * yg_ref[k].astype(_F32)
    xn = _layer_norm(DEEPNORM_ALPHA * h_ref[...] + y, lng_ref[0, 1:2, :], lnb_ref[0, 1:2, :])
    x_ref[...] = xn
    xb_ref[...] = xn.astype(_BF16)


def _combine(h, yg, gates, ln_g, ln_b, layer):
    t = h.shape[0]
    tm = LN2_TM
    row = lambda i: (i, 0)
    return pl.pallas_call(
        _combine_kernel,
        grid=(t // tm,),
        in_specs=[
            pl.BlockSpec((tm, D_MODEL), row),
            pl.BlockSpec((TOP_K, tm, D_MODEL), lambda i: (0, i, 0)),
            pl.BlockSpec((tm, LANES), row),
            pl.BlockSpec((1, 2, D_MODEL), lambda i: (layer, 0, 0)),
            pl.BlockSpec((1, 2, D_MODEL), lambda i: (layer, 0, 0)),
        ],
        out_specs=[pl.BlockSpec((tm, D_MODEL), row), pl.BlockSpec((tm, D_MODEL), row)],
        out_shape=[jax.ShapeDtypeStruct((t, D_MODEL), _F32), jax.ShapeDtypeStruct((t, D_MODEL), _BF16)],
        compiler_params=_params("parallel"),
        name="combine_ln",
    )(h, yg, gates, ln_g, ln_b)


def _dispatch_tables(route, counts, t):
    tk = t * TOP_K
    experts = route[:TOP_K]
    ranks = route[TOP_K:]
    padded = (counts + MOE_BLOCK - 1) // MOE_BLOCK * MOE_BLOCK
    pad_end = jnp.cumsum(padded)
    pad_start = pad_end - padded
    start = jnp.cumsum(counts) - counts
    expert_ids = jnp.arange(N_EXPERTS, dtype=jnp.int32)
    dest = jnp.sum(jnp.where(experts[..., None] == expert_ids, pad_start, 0), axis=-1) + ranks
    n_blocks = (tk + N_EXPERTS * (MOE_BLOCK - 1) + MOE_BLOCK - 1) // MOE_BLOCK
    block_start = jnp.arange(n_blocks, dtype=jnp.int32) * MOE_BLOCK
    block_expert = jnp.minimum(
        jnp.sum((pad_end[None, :] <= block_start[:, None]).astype(jnp.int32), axis=1), N_EXPERTS - 1)
    n_used = (pad_end[-1:] // MOE_BLOCK).astype(jnp.int32)
    order = jnp.argsort(experts.T.reshape(tk))
    in_block = jnp.arange(MOE_BLOCK, dtype=jnp.int32)[None, :]
    r = (block_start - pad_start[block_expert])[:, None] + in_block
    valid = r < counts[block_expert][:, None]
    src = jnp.where(valid, start[block_expert][:, None] + r, 0).reshape(n_blocks * MOE_BLOCK)
    filler = jnp.arange(n_blocks * MOE_BLOCK, dtype=jnp.int32) % t
    buf_tok = jnp.where(valid.reshape(-1), order[src] // TOP_K, filler)
    return buf_tok, dest, block_expert, n_used


def kernel(x, w_in, w_o, lambda_qk, subln_g, sinks, ln_g, ln_b, w_router, b_router, w_up, b_up, w_down, b_down):
    batch, seq, _ = x.shape
    t = batch * seq
    assert seq % ATT_TQ == 0 and seq % SWA_TB == 0 and t % PROJ_TM == 0

    va_lo, va_hi = 2 * D_MODEL, 3 * D_MODEL
    gates_lo = w_in.shape[-1] - GATE_COLS
    w_att_b = jnp.concatenate([w_in[..., :va_lo], w_in[..., va_hi:gates_lo]], axis=-1).astype(_BF16)
    vb_lo = gates_lo - SW_KV_HEADS * SW_HEAD_DIM
    w_vt_b = jnp.swapaxes(jnp.concatenate([w_in[..., va_lo:va_hi], w_in[..., vb_lo:gates_lo]], axis=-1),
                          1, 2).astype(_BF16)
    w_gates_b = w_in[..., gates_lo:].astype(_BF16)
    w_o_b = w_o.astype(_BF16)
    w_r_b = jnp.swapaxes(w_router, 1, 2).astype(_BF16)
    b_gate = b_up[..., 0::2].reshape(DEPTH, N_EXPERTS, 1, D_FF)
    b_upp = b_up[..., 1::2].reshape(DEPTH, N_EXPERTS, 1, D_FF)
    b_down4 = b_down.reshape(DEPTH, N_EXPERTS, 1, D_MODEL)
    b_router3 = b_router.astype(_F32).reshape(DEPTH, N_EXPERTS, 1)

    att_scale = jnp.concatenate([
        jnp.full((D_MODEL,), DA_HEAD_DIM ** -0.5 * LOG2E, _F32), jnp.ones((D_MODEL,), _F32),
        jnp.full((D_MODEL,), SW_HEAD_DIM ** -0.5 * LOG2E, _F32),
        jnp.ones((2 * SW_KV_HEADS * SW_HEAD_DIM,), _F32)]).reshape(1, ATT_COLS)
    gate_scale = jnp.ones((1, GATE_COLS), _F32)
    rest = 2.0 ** (-8.0 * jnp.arange(1, DA_HEADS + 1, dtype=_F32) / DA_HEADS) * LOG2E
    pieces = []
    for _ in range(N_SLOPE_PARTS):
        piece = rest.astype(_BF16).astype(_F32)
        pieces.append(piece)
        rest = rest - piece
    da_slopes = jnp.stack(pieces, axis=1)
    swa_bias = _swa_bias()

    xf = x.reshape(t, D_MODEL)
    xb = xf.astype(_BF16)
    for l in range(DEPTH):
        lam_init = 0.8 - 0.6 * math.exp(-0.3 * l)
        prm = jnp.concatenate([lambda_qk[l].astype(_F32), jnp.full((4, DA_HEAD_DIM), lam_init, _F32)], axis=0)
        att = _project(xb, w_att_b, l, ATT_COLS // 2, att_scale, _BF16)
        gates = _project(xb, w_gates_b, l, GATE_COLS // 2, gate_scale, _BF16)
        vt, vtb = _project_vt(xb, w_vt_b, l)
        o_a = _diff_attention(att, vt, da_slopes, prm, subln_g[l].reshape(DA_WIDTH, 1), batch, seq)
        sink_rows = jnp.repeat(sinks[l].astype(_F32).reshape(SW_KV_HEADS, SW_GROUP) * LOG2E, WINDOW, axis=1)
        o_b = _sliding_window_attention(att, vtb, sink_rows.reshape(SW_KV_HEADS, 1, SW_GROUP * WINDOW),
                                        swa_bias, batch, seq)
        h, hb, route, top_gate, counts = _merge_out(gates, o_a, o_b, xf, w_o_b, ln_g, ln_b, w_r_b, b_router3, l)
        buf_tok, pos, block_expert, n_used = _dispatch_tables(route, counts[:, 0], t)
        x_rows = hb[buf_tok]
        yb = _experts(block_expert, n_used, x_rows, w_up, b_gate, b_upp, w_down, b_down4, l)
        yg = yb[pos]
        xf, xb = _combine(h, yg, top_gate, ln_g, ln_b, l)
    return xf.reshape(batch, seq, D_MODEL)
```

```python
import functools
import math

import jax
import jax.numpy as jnp
from jax import lax
from jax.experimental import pallas as pl
from jax.experimental.pallas import tpu as pltpu
from jax.experimental.pallas import tpu_sc as plsc

D_MODEL = 1024
DEPTH = 4
DA_HEAD_DIM = 128
DA_HEADS = 4
DA_WIDTH = 2 * DA_HEAD_DIM
SW_HEAD_DIM = 64
SW_HEADS = 16
SW_KV_HEADS = 2
SW_GROUP = SW_HEADS // SW_KV_HEADS
WINDOW = 128
N_EXPERTS = 32
TOP_K = 4
D_FF = D_MODEL
SWIGLU_LIMIT = 7.0
SWIGLU_ALPHA = 1.702
LN_EPS = 1e-5
DEEPNORM_ALPHA = (2.0 * DEPTH) ** 0.25
NEG_INF = -1e30

QA_OFF, KA_OFF, QB_OFF, KVB_OFF = 0, 1024, 2048, 3072
ATT_COLS = 3328
GATE_COLS = 2 * D_MODEL
LANES = 128
HALF_D = D_MODEL // 2
SC_CORES = 2
SC_SUBCORES = 16
GATHER_CHUNK = 64
SUBLANES_BF16 = 16
VT_ROWS = DA_WIDTH + SUBLANES_BF16
VTB_ROWS = SW_HEAD_DIM + SUBLANES_BF16
LOG2E = math.log2(math.e)
N_SLOPE_PARTS = 3

PROJ_TM = 1024
ATT_TQ = 1024
ATT_TK = 512
SWA_TB = 256
OUT_TM = 256
MOE_BLOCK = 256
DEINT_TILE = 256
LN2_TM = 256
VMEM_LIMIT = 56 * 1024 * 1024

_F32 = jnp.float32
_BF16 = jnp.bfloat16


def _params(*sem):
    return pltpu.CompilerParams(dimension_semantics=sem, vmem_limit_bytes=VMEM_LIMIT)


def _proj_kernel(x_ref, w_ref, s_ref, o_ref):
    acc = jnp.dot(x_ref[...], w_ref[0], preferred_element_type=_F32)
    o_ref[...] = (acc * s_ref[...]).astype(o_ref.dtype)


def _project(xb, w_all, layer, tn, scale, out_dtype):
    t = xb.shape[0]
    n_cols = w_all.shape[-1]
    assert n_cols % tn == 0 and t % PROJ_TM == 0
    return pl.pallas_call(
        _proj_kernel,
        grid=(n_cols // tn, t // PROJ_TM),
        in_specs=[
            pl.BlockSpec((PROJ_TM, D_MODEL), lambda j, i: (i, 0)),
            pl.BlockSpec((1, D_MODEL, tn), lambda j, i: (layer, 0, j)),
            pl.BlockSpec((1, tn), lambda j, i: (0, j)),
        ],
        out_specs=pl.BlockSpec((PROJ_TM, tn), lambda j, i: (i, j)),
        out_shape=jax.ShapeDtypeStruct((t, n_cols), out_dtype),
        compiler_params=_params("parallel", "parallel"),
        name="in_proj",
    )(xb, w_all, scale)


def _proj_vt_kernel(x_ref, wt_ref, oa_ref, ob_ref):
    x = x_ref[...]
    nt = (((1,), (1,)), ((), ()))
    ones = jnp.ones((SUBLANES_BF16, x.shape[0]), oa_ref.dtype)
    for h in range(DA_HEADS):
        vt = lax.dot_general(wt_ref[0, h * DA_WIDTH:(h + 1) * DA_WIDTH, :], x, nt, preferred_element_type=_F32)
        oa_ref[h * VT_ROWS:h * VT_ROWS + DA_WIDTH, :] = vt.astype(oa_ref.dtype)
        oa_ref[h * VT_ROWS + DA_WIDTH:(h + 1) * VT_ROWS, :] = ones
    vb = lax.dot_general(wt_ref[0, D_MODEL:, :], x, nt, preferred_element_type=_F32)
    for kh in range(SW_KV_HEADS):
        ob_ref[kh * VTB_ROWS:kh * VTB_ROWS + SW_HEAD_DIM, :] = (
            vb[kh * SW_HEAD_DIM:(kh + 1) * SW_HEAD_DIM, :].astype(ob_ref.dtype))
        ob_ref[kh * VTB_ROWS + SW_HEAD_DIM:(kh + 1) * VTB_ROWS, :] = ones


def _project_vt(xb, wt_all, layer):
    t = xb.shape[0]
    n_b = SW_KV_HEADS * SW_HEAD_DIM
    return pl.pallas_call(
        _proj_vt_kernel,
        grid=(t // PROJ_TM,),
        in_specs=[
            pl.BlockSpec((PROJ_TM, D_MODEL), lambda i: (i, 0)),
            pl.BlockSpec((1, D_MODEL + n_b, D_MODEL), lambda i: (layer, 0, 0)),
        ],
        out_specs=[pl.BlockSpec((DA_HEADS * VT_ROWS, PROJ_TM), lambda i: (0, i)),
                   pl.BlockSpec((SW_KV_HEADS * VTB_ROWS, PROJ_TM), lambda i: (0, i))],
        out_shape=[jax.ShapeDtypeStruct((DA_HEADS * VT_ROWS, t), _BF16),
                   jax.ShapeDtypeStruct((SW_KV_HEADS * VTB_ROWS, t), _BF16)],
        compiler_params=_params("parallel"),
        name="in_proj_vt",
    )(xb, wt_all)


def _diff_attn_kernel(slope_ref, prm_ref, g_ref, q_ref, k_ref, vt_ref, o_ref,
                      acc_ref, m_ref, qaug_ref, kaug_ref, s_ref, smax_ref, p_ref, alpha_ref, *, tq, tk):
    h = pl.program_id(1)
    i = pl.program_id(2)
    diag = tq // tk
    n_full = i * diag
    lane = lax.broadcasted_iota(jnp.int32, (tq, DA_HEAD_DIM), 1)
    slope_tile = jnp.zeros((tq, DA_HEAD_DIM), _F32)
    slope = 0.0
    for part in range(N_SLOPE_PARTS):
        piece = slope_ref[h, part]
        slope = slope + piece
        slope_tile = jnp.where(lane // 2 == part, piece, slope_tile)
    klane = lax.broadcasted_iota(jnp.int32, (tk, DA_HEAD_DIM), 1)
    kpos = lax.broadcasted_iota(jnp.int32, (tk, DA_HEAD_DIM), 0)
    pos_tile = jnp.where(klane < 2 * N_SLOPE_PARTS, jnp.where(klane % 2 == 0, kpos % 256, kpos // 256 * 256), 0)
    for c in range(2):
        kaug_ref[c, :, DA_HEAD_DIM:] = pos_tile.astype(_F32).astype(_BF16)
        qaug_ref[c, :, :DA_HEAD_DIM] = q_ref[:, c * DA_HEAD_DIM:(c + 1) * DA_HEAD_DIM]
        qaug_ref[c, :, DA_HEAD_DIM:] = slope_tile.astype(_BF16)
    m_ref[...] = jnp.full(m_ref.shape, NEG_INF, _F32)
    acc_ref[...] = jnp.zeros(acc_ref.shape, _F32)

    def scores(j):
        start = pl.multiple_of(j * tk, tk)
        for c in range(2):
            kaug_ref[c, :, :DA_HEAD_DIM] = k_ref[pl.ds(start, tk), c * DA_HEAD_DIM:(c + 1) * DA_HEAD_DIM]
            s = lax.dot_general(kaug_ref[c], qaug_ref[c], (((1,), (1,)), ((), ())),
                                preferred_element_type=_F32)
            s_ref[c] = s
            smax_ref[c] = jnp.max(s, axis=0, keepdims=True)

    def weights(j, boundary):
        off = (j * tk - i * tq).astype(_F32) * slope
        for c in range(2):
            s = s_ref[c]
            if boundary is None:
                smax = smax_ref[c]
            else:
                keys = lax.broadcasted_iota(jnp.int32, (tk, tq), 0) + boundary * tk
                queries = lax.broadcasted_iota(jnp.int32, (tk, tq), 1)
                s = jnp.where(keys <= queries, s, NEG_INF)
                smax = jnp.max(s, axis=0, keepdims=True)
            m_old = m_ref[c]
            m_new = jnp.maximum(m_old, smax + off)
            alpha_ref[c] = jnp.exp2(m_old - m_new)
            p_ref[c] = jnp.exp2(s - (m_new - off)).astype(_BF16)
            m_ref[c] = m_new

    def accumulate(j):
        start = pl.multiple_of(j * tk, tk)
        vt = vt_ref[:, pl.ds(start, tk)]
        for c in range(2):
            acc_ref[c] = alpha_ref[c] * acc_ref[c] + jnp.dot(vt, p_ref[c], preferred_element_type=_F32)

    def body(j, carry):
        accumulate(j)
        weights(j + 1, None)
        scores(j + 2)
        return carry

    def enter_boundary():
        weights(n_full, 0)
        if diag > 1:
            scores(n_full + 1)

    scores(0)

    @pl.when(i >= 1)
    def _():
        weights(0, None)
        scores(1)
        lax.fori_loop(0, n_full - 1, body, 0)
        accumulate(n_full - 1)
        enter_boundary()

    @pl.when(i == 0)
    def _():
        enter_boundary()

    for d in range(diag):
        accumulate(n_full + d)
        if d + 1 < diag:
            weights(n_full + d + 1, d + 1)
            if d + 2 < diag:
                scores(n_full + d + 2)

    prm = prm_ref[...]
    lam_init = prm[4:5, 0:1]
    lam = (jnp.exp(jnp.sum(prm[0:1] * prm[1:2], axis=1, keepdims=True))
           - jnp.exp(jnp.sum(prm[2:3] * prm[3:4], axis=1, keepdims=True)) + lam_init)
    o0 = acc_ref[0, :DA_WIDTH, :] / acc_ref[0, DA_WIDTH:DA_WIDTH + 1, :]
    o1 = acc_ref[1, :DA_WIDTH, :] / acc_ref[1, DA_WIDTH:DA_WIDTH + 1, :]
    a = o0 - lam * o1
    ms = jnp.mean(a * a, axis=0, keepdims=True)
    out = a * lax.rsqrt(ms + LN_EPS) * g_ref[...] * (1.0 - lam_init)
    o_ref[...] = out.T.astype(o_ref.dtype)


def _diff_attention(att, vt, slopes, prm, subln_g, batch, seq):
    tq, tk = ATT_TQ, ATT_TK
    assert tq % tk == 0 and seq % tq == 0
    nq = seq // tq
    t = batch * seq
    kernel = functools.partial(_diff_attn_kernel, tq=tq, tk=tk)
    return pl.pallas_call(
        kernel,
        grid=(batch, DA_HEADS, nq),
        in_specs=[
            pl.BlockSpec(memory_space=pltpu.SMEM),
            pl.BlockSpec((8, LANES), lambda b, h, i: (0, 0)),
            pl.BlockSpec((DA_WIDTH, 1), lambda b, h, i: (0, 0)),
            pl.BlockSpec((tq, DA_WIDTH), lambda b, h, i: (b * nq + i, QA_OFF // DA_WIDTH + h)),
            pl.BlockSpec((seq, DA_WIDTH), lambda b, h, i: (b, KA_OFF // DA_WIDTH + h)),
            pl.BlockSpec((VT_ROWS, seq), lambda b, h, i: (h, b)),
        ],
        out_specs=pl.BlockSpec((tq, DA_WIDTH), lambda b, h, i: (b * nq + i, h)),
        out_shape=jax.ShapeDtypeStruct((t, D_MODEL), _BF16),
        scratch_shapes=[
            pltpu.VMEM((2, VT_ROWS, tq), _F32),
            pltpu.VMEM((2, 1, tq), _F32),
            pltpu.VMEM((2, tq, 2 * DA_HEAD_DIM), _BF16),
            pltpu.VMEM((2, tk, 2 * DA_HEAD_DIM), _BF16),
            pltpu.VMEM((2, tk, tq), _F32),
            pltpu.VMEM((2, 1, tq), _F32),
            pltpu.VMEM((2, tk, tq), _BF16),
            pltpu.VMEM((2, 1, tq), _F32),
        ],
        compiler_params=_params("parallel", "parallel", "arbitrary"),
        name="diff_attn",
    )(slopes, prm, subln_g, att, att, vt)


def _swa_kernel(sink_ref, bias_ref, q_ref, kv_ref, kvp_ref, vt_ref, vtp_ref, o_ref):
    i = pl.program_id(1)
    first = (i == 0).astype(jnp.int32)
    nt = (((1,), (1,)), ((), ()))
    kv_all = jnp.concatenate([kvp_ref[...], kv_ref[...]], axis=0)
    vt_all = jnp.concatenate([vtp_ref[...], vt_ref[...]], axis=1)
    for r in range(SWA_TB // WINDOW):
        heads = []
        for kh in range(SW_KV_HEADS):
            kwin = kv_all[r * WINDOW:(r + 2) * WINDOW, kh * SW_HEAD_DIM:(kh + 1) * SW_HEAD_DIM]
            vwin = vt_all[kh * VTB_ROWS:(kh + 1) * VTB_ROWS, r * WINDOW:(r + 2) * WINDOW]
            qs = jnp.concatenate(
                [q_ref[r * WINDOW:(r + 1) * WINDOW, (kh * SW_GROUP + g) * SW_HEAD_DIM:(kh * SW_GROUP + g + 1) * SW_HEAD_DIM]
                 for g in range(SW_GROUP)], axis=0)
            s = lax.dot_general(kwin, qs, nt, preferred_element_type=_F32)
            bias = bias_ref[2 * kh + first] if r == 0 else bias_ref[2 * kh]
            s = s + bias
            sink = sink_ref[kh]
            m = jnp.maximum(jnp.max(s, axis=0, keepdims=True), sink)
            p = jnp.exp2(s - m)
            ot = jnp.dot(vwin, p.astype(_BF16), preferred_element_type=_F32)
            denom = ot[SW_HEAD_DIM:SW_HEAD_DIM + 1, :] + jnp.exp2(sink - m)
            ot = ot[:SW_HEAD_DIM, :] / denom
            for g in range(SW_GROUP):
                heads.append(ot[:, g * WINDOW:(g + 1) * WINDOW].T)
        o_ref[r * WINDOW:(r + 1) * WINDOW, :] = jnp.concatenate(heads, axis=1).astype(o_ref.dtype)


def _swa_bias():
    slopes = 2.0 ** (-8.0 * jnp.arange(1, SW_HEADS + 1, dtype=_F32) / SW_HEADS)
    kj = jnp.arange(2 * WINDOW, dtype=jnp.int32)[:, None]
    qi = jnp.arange(WINDOW, dtype=jnp.int32)[None, :]
    dist = qi + WINDOW - kj
    valid = (dist >= 0) & (dist < WINDOW)
    tables = []
    for kh in range(SW_KV_HEADS):
        for has_prev in (True, False):
            ok = valid if has_prev else valid & (kj >= WINDOW)
            per_head = [jnp.where(ok, -slopes[kh * SW_GROUP + g] * LOG2E * dist.astype(_F32), NEG_INF)
                        for g in range(SW_GROUP)]
            tables.append(jnp.concatenate(per_head, axis=1))
    return jnp.stack(tables)


def _sliding_window_attention(att, vtb, sink_rows, bias, batch, seq):
    tb = SWA_TB
    nb = seq // tb
    ratio = tb // WINDOW
    t = batch * seq
    kv_width = 2 * SW_KV_HEADS * SW_HEAD_DIM
    n_b = SW_KV_HEADS * VTB_ROWS
    q_blk = QB_OFF // D_MODEL
    kv_blk = KVB_OFF // kv_width
    prev = lambda b, i: jnp.maximum((b * nb + i) * ratio - 1, 0)
    return pl.pallas_call(
        _swa_kernel,
        grid=(batch, nb),
        in_specs=[
            pl.BlockSpec((SW_KV_HEADS, 1, SW_GROUP * WINDOW), lambda b, i: (0, 0, 0)),
            pl.BlockSpec((2 * SW_KV_HEADS, 2 * WINDOW, SW_GROUP * WINDOW), lambda b, i: (0, 0, 0)),
            pl.BlockSpec((tb, D_MODEL), lambda b, i: (b * nb + i, q_blk)),
            pl.BlockSpec((tb, kv_width), lambda b, i: (b * nb + i, kv_blk)),
            pl.BlockSpec((WINDOW, kv_width), lambda b, i: (prev(b, i), kv_blk)),
            pl.BlockSpec((n_b, tb), lambda b, i: (0, b * nb + i)),
            pl.BlockSpec((n_b, WINDOW), lambda b, i: (0, prev(b, i))),
        ],
        out_specs=pl.BlockSpec((tb, D_MODEL), lambda b, i: (b * nb + i, 0)),
        out_shape=jax.ShapeDtypeStruct((t, D_MODEL), _BF16),
        compiler_params=_params("parallel", "arbitrary"),
        name="swa",
    )(sink_rows, bias, att, att, att, vtb, vtb)


def _pack_rows(x):
    bits = lax.bitcast_convert_type(x.astype(_BF16).astype(_F32), jnp.uint32)
    return lax.bitcast_convert_type(bits[:, :HALF_D] | (bits[:, HALF_D:] >> 16), jnp.int32)


def _unpack_rows(words):
    bits = lax.bitcast_convert_type(words, jnp.uint32)
    hi = lax.bitcast_convert_type(bits & jnp.uint32(0xFFFF0000), _F32)
    lo = lax.bitcast_convert_type(bits << 16, _F32)
    return hi, lo


def _gather_rows(table, idx):
    n_idx = idx.shape[0]
    width = table.shape[1]
    workers = SC_CORES * SC_SUBCORES
    assert n_idx % (workers * GATHER_CHUNK) == 0
    per_worker = n_idx // workers
    mesh = plsc.VectorSubcoreMesh(core_axis_name="c", subcore_axis_name="s",
                                  num_cores=SC_CORES, num_subcores=SC_SUBCORES)

    @functools.partial(
        pl.kernel, mesh=mesh,
        out_type=jax.ShapeDtypeStruct((n_idx, width), table.dtype),
        scratch_types=[
            pltpu.VMEM((GATHER_CHUNK,), jnp.int32),
            pltpu.VMEM((GATHER_CHUNK, width), table.dtype),
            pltpu.SemaphoreType.DMA,
        ],
        name="gather_rows",
    )
    def gather(table_hbm, idx_hbm, out_hbm, idx_v, rows_v, sem):
        base = (lax.axis_index("s") * SC_CORES + lax.axis_index("c")) * per_worker

        @pl.loop(0, per_worker // GATHER_CHUNK)
        def _(g):
            off = base + g * GATHER_CHUNK
            pltpu.sync_copy(idx_hbm.at[pl.ds(off, GATHER_CHUNK)], idx_v)
            pltpu.async_copy(table_hbm.at[idx_v], rows_v, sem).wait()
            pltpu.sync_copy(rows_v, out_hbm.at[pl.ds(off, GATHER_CHUNK)])

    return gather(table, idx)


def _layer_norm(z, g, b):
    mu = jnp.mean(z, axis=1, keepdims=True)
    d = z - mu
    var = jnp.mean(d * d, axis=1, keepdims=True)
    return d * lax.rsqrt(var + LN_EPS) * g + b


def _merge_out_kernel(ga_ref, gb_ref, oa_ref, ob_ref, x_ref, wo_ref, lng_ref, lnb_ref, wr_ref, br_ref,
                      h_ref, hp_ref, idx_ref, gate_ref, cnt_ref, run_ref):
    @pl.when(pl.program_id(0) == 0)
    def _():
        run_ref[...] = jnp.zeros(run_ref.shape, _F32)

    merged = (jax.nn.sigmoid(ga_ref[...].astype(_F32)) * oa_ref[...].astype(_F32)
              + jax.nn.sigmoid(gb_ref[...].astype(_F32)) * ob_ref[...].astype(_F32))
    y = jnp.dot(merged.astype(_BF16), wo_ref[0], preferred_element_type=_F32)
    hn = _layer_norm(DEEPNORM_ALPHA * x_ref[...] + y, lng_ref[0, 0:1, :], lnb_ref[0, 0:1, :])
    h_ref[...] = hn
    hb = hn.astype(_BF16)
    hp_ref[...] = _pack_rows(hn)
    logits = lax.dot_general(wr_ref[0], hb, (((1,), (1,)), ((), ())),
                             preferred_element_type=_F32) + br_ref[0]
    tm = logits.shape[1]
    erow = lax.broadcasted_iota(jnp.int32, (N_EXPERTS, tm), 0)
    cur = logits
    vals, picks, id_rows = [], [], []
    for k in range(TOP_K):
        mx = jnp.max(cur, axis=0, keepdims=True)
        ix = jnp.min(jnp.where(cur == mx, erow, N_EXPERTS), axis=0, keepdims=True)
        vals.append(mx)
        picks.append(erow == ix)
        id_rows.append(ix)
        cur = jnp.where(picks[k], -jnp.inf, cur)
    exps = [jnp.exp(v - vals[0]) for v in vals]
    tot = exps[0] + exps[1] + exps[2] + exps[3]
    gate_rows = [e / tot for e in exps] + [jnp.zeros((LANES - TOP_K, tm), _F32)]
    gate_ref[...] = jnp.concatenate(gate_rows, axis=0).T

    chosen = jnp.zeros((N_EXPERTS, tm), _F32)
    for k in range(TOP_K):
        chosen = chosen + jnp.where(picks[k], 1.0, 0.0)
    r = lax.broadcasted_iota(jnp.int32, (tm, tm), 0)
    c = lax.broadcasted_iota(jnp.int32, (tm, tm), 1)
    earlier = jnp.where(r < c, 1.0, 0.0).astype(_BF16)
    before = jnp.dot(chosen.astype(_BF16), earlier, preferred_element_type=_F32) + run_ref[...]
    rank_rows = [jnp.sum(jnp.where(picks[k], before, 0.0), axis=0, keepdims=True).astype(jnp.int32)
                 for k in range(TOP_K)]
    idx_ref[...] = jnp.concatenate(id_rows + rank_rows, axis=0)
    total = run_ref[...] + jnp.sum(chosen, axis=1, keepdims=True)
    run_ref[...] = total
    cnt_ref[...] = jnp.broadcast_to(total, cnt_ref.shape).astype(jnp.int32)


def _merge_out(gates, o_a, o_b, x, wo_all, ln_g, ln_b, wr_all, br_all, layer):
    t = x.shape[0]
    tm = OUT_TM
    row = lambda i: (i, 0)
    return pl.pallas_call(
        _merge_out_kernel,
        grid=(t // tm,),
        in_specs=[
            pl.BlockSpec((tm, D_MODEL), lambda i: (i, 0)),
            pl.BlockSpec((tm, D_MODEL), lambda i: (i, 1)),
            pl.BlockSpec((tm, D_MODEL), row),
            pl.BlockSpec((tm, D_MODEL), row),
            pl.BlockSpec((tm, D_MODEL), row),
            pl.BlockSpec((1, D_MODEL, D_MODEL), lambda i: (layer, 0, 0)),
            pl.BlockSpec((1, 2, D_MODEL), lambda i: (layer, 0, 0)),
            pl.BlockSpec((1, 2, D_MODEL), lambda i: (layer, 0, 0)),
            pl.BlockSpec((1, N_EXPERTS, D_MODEL), lambda i: (layer, 0, 0)),
            pl.BlockSpec((1, N_EXPERTS, 1), lambda i: (layer, 0, 0)),
        ],
        out_specs=[
            pl.BlockSpec((tm, D_MODEL), row),
            pl.BlockSpec((tm, HALF_D), row),
            pl.BlockSpec((2 * TOP_K, tm), lambda i: (0, i)),
            pl.BlockSpec((tm, LANES), row),
            pl.BlockSpec((N_EXPERTS, LANES), lambda i: (0, 0)),
        ],
        out_shape=[
            jax.ShapeDtypeStruct((t, D_MODEL), _F32),
            jax.ShapeDtypeStruct((t, HALF_D), jnp.int32),
            jax.ShapeDtypeStruct((2 * TOP_K, t), jnp.int32),
            jax.ShapeDtypeStruct((t, LANES), _F32),
            jax.ShapeDtypeStruct((N_EXPERTS, LANES), jnp.int32),
        ],
        scratch_shapes=[pltpu.VMEM((N_EXPERTS, 1), _F32)],
        compiler_params=_params("arbitrary"),
        name="merge_out_ln_router",
    )(gates, gates, o_a, o_b, x, wo_all, ln_g, ln_b, wr_all, br_all)


def _expert_kernel(be_ref, nused_ref, x_ref, wup_ref, bg_ref, bu_ref, wd_ref, bd_ref, y_ref,
                   wg_scr, wu_scr, wd_scr):
    i = pl.program_id(0)
    changed = jnp.logical_or(i == 0, be_ref[i] != be_ref[jnp.maximum(i - 1, 0)])

    @pl.when(jnp.logical_and(changed, i < nused_ref[0]))
    def _():
        half = DEINT_TILE // 2
        r = lax.broadcasted_iota(jnp.int32, (DEINT_TILE, DEINT_TILE), 0)
        c = lax.broadcasted_iota(jnp.int32, (DEINT_TILE, DEINT_TILE), 1)
        sel = jnp.where(r == jnp.where(c < half, 2 * c, 2 * (c - half) + 1), 1.0, 0.0).astype(_BF16)
        for t in range(2 * D_FF // DEINT_TILE):
            w = wup_ref[0, 0, :, t * DEINT_TILE:(t + 1) * DEINT_TILE].astype(_BF16)
            de = jnp.dot(w, sel, preferred_element_type=_F32).astype(_BF16)
            wg_scr[:, t * half:(t + 1) * half] = de[:, :half]
            wu_scr[:, t * half:(t + 1) * half] = de[:, half:]
        wd_scr[...] = wd_ref[0, 0].astype(_BF16)

    @pl.when(i < nused_ref[0])
    def _():
        hi, lo = _unpack_rows(x_ref[...])
        x = jnp.concatenate([hi.astype(_BF16), lo.astype(_BF16)], axis=1)
        gate = jnp.dot(x, wg_scr[...], preferred_element_type=_F32) + bg_ref[0, 0]
        up = jnp.dot(x, wu_scr[...], preferred_element_type=_F32) + bu_ref[0, 0]
        gate = jnp.minimum(gate, SWIGLU_LIMIT)
        up = jnp.clip(up, -SWIGLU_LIMIT, SWIGLU_LIMIT)
        act = (up + 1.0) * (gate * jax.nn.sigmoid(SWIGLU_ALPHA * gate))
        y = jnp.dot(act.astype(_BF16), wd_scr[...], preferred_element_type=_F32) + bd_ref[0, 0]
        y_ref[...] = _pack_rows(y)

    @pl.when(i >= nused_ref[0])
    def _():
        y_ref[...] = jnp.zeros(y_ref.shape, y_ref.dtype)


def _experts(block_expert, n_used, xb, w_up, bg_all, bu_all, w_down, bd_all, layer):
    n_rows = xb.shape[0]
    n_blocks = n_rows // MOE_BLOCK
    pick = lambda i, be, nu: (layer, be[i], 0, 0)
    bspec = pl.BlockSpec((1, 1, 1, D_FF), pick)
    grid_spec = pltpu.PrefetchScalarGridSpec(
        num_scalar_prefetch=2,
        grid=(n_blocks,),
        in_specs=[
            pl.BlockSpec((MOE_BLOCK, HALF_D), lambda i, be, nu: (i, 0)),
            pl.BlockSpec((1, 1, D_MODEL, 2 * D_FF), pick),
            bspec, bspec,
            pl.BlockSpec((1, 1, D_FF, D_MODEL), pick),
            pl.BlockSpec((1, 1, 1, D_MODEL), pick),
        ],
        out_specs=pl.BlockSpec((MOE_BLOCK, HALF_D), lambda i, be, nu: (i, 0)),
        scratch_shapes=[
            pltpu.VMEM((D_MODEL, D_FF), _BF16),
            pltpu.VMEM((D_MODEL, D_FF), _BF16),
            pltpu.VMEM((D_FF, D_MODEL), _BF16),
        ],
    )
    return pl.pallas_call(
        _expert_kernel,
        grid_spec=grid_spec,
        out_shape=jax.ShapeDtypeStruct((n_rows, HALF_D), jnp.int32),
        compiler_params=_params("arbitrary"),
        name="experts",
    )(block_expert, n_used, xb, w_up, bg_all, bu_all, w_down, bd_all)


def _combine_kernel(h_ref, yg_ref, gate_ref, lng_ref, lnb_ref, x_ref, xb_ref):
    gates = gate_ref[...]
    y = None
    for k in range(TOP_K):
        hi, lo = _unpack_rows(yg_ref[k])
        yk = gates[:, k:k + 1] * jnp.concatenate([hi, lo], axis=1)
        y = yk if y is None else y + yk
    xn = _layer_norm(DEEPNORM_ALPHA * h_ref[...] + y, lng_ref[0, 1:2, :], lnb_ref[0, 1:2, :])
    x_ref[...] = xn
    xb_ref[...] = xn.astype(_BF16)


def _combine(h, yg, gates, ln_g, ln_b, layer):
    t = h.shape[0]
    tm = LN2_TM
    row = lambda i: (i, 0)
    return pl.pallas_call(
        _combine_kernel,
        grid=(t // tm,),
        in_specs=[
            pl.BlockSpec((tm, D_MODEL), row),
            pl.BlockSpec((TOP_K, tm, HALF_D), lambda i: (0, i, 0)),
            pl.BlockSpec((tm, LANES), row),
            pl.BlockSpec((1, 2, D_MODEL), lambda i: (layer, 0, 0)),
            pl.BlockSpec((1, 2, D_MODEL), lambda i: (layer, 0, 0)),
        ],
        out_specs=[pl.BlockSpec((tm, D_MODEL), row), pl.BlockSpec((tm, D_MODEL), row)],
        out_shape=[jax.ShapeDtypeStruct((t, D_MODEL), _F32), jax.ShapeDtypeStruct((t, D_MODEL), _BF16)],
        compiler_params=_params("parallel"),
        name="combine_ln",
    )(h, yg, gates, ln_g, ln_b)


def _dispatch_tables(route, counts, t):
    tk = t * TOP_K
    experts = route[:TOP_K]
    ranks = route[TOP_K:]
    padded = (counts + MOE_BLOCK - 1) // MOE_BLOCK * MOE_BLOCK
    pad_end = jnp.cumsum(padded)
    pad_start = pad_end - padded
    start = jnp.cumsum(counts) - counts
    expert_ids = jnp.arange(N_EXPERTS, dtype=jnp.int32)
    dest = jnp.sum(jnp.where(experts[..., None] == expert_ids, pad_start, 0), axis=-1) + ranks
    n_blocks = (tk + N_EXPERTS * (MOE_BLOCK - 1) + MOE_BLOCK - 1) // MOE_BLOCK
    block_start = jnp.arange(n_blocks, dtype=jnp.int32) * MOE_BLOCK
    block_expert = jnp.minimum(
        jnp.sum((pad_end[None, :] <= block_start[:, None]).astype(jnp.int32), axis=1), N_EXPERTS - 1)
    n_used = (pad_end[-1:] // MOE_BLOCK).astype(jnp.int32)
    order = jnp.argsort(experts.T.reshape(tk))
    in_block = jnp.arange(MOE_BLOCK, dtype=jnp.int32)[None, :]
    r = (block_start - pad_start[block_expert])[:, None] + in_block
    valid = r < counts[block_expert][:, None]
    src = jnp.where(valid, start[block_expert][:, None] + r, 0).reshape(n_blocks * MOE_BLOCK)
    filler = jnp.arange(n_blocks * MOE_BLOCK, dtype=jnp.int32) % t
    buf_tok = jnp.where(valid.reshape(-1), order[src] // TOP_K, filler)
    return buf_tok, dest, block_expert, n_used


def kernel(x, w_in, w_o, lambda_qk, subln_g, sinks, ln_g, ln_b, w_router, b_router, w_up, b_up, w_down, b_down):
    batch, seq, _ = x.shape
    t = batch * seq
    assert seq % ATT_TQ == 0 and seq % SWA_TB == 0 and t % PROJ_TM == 0

    va_lo, va_hi = 2 * D_MODEL, 3 * D_MODEL
    gates_lo = w_in.shape[-1] - GATE_COLS
    w_att_b = jnp.concatenate([w_in[..., :va_lo], w_in[..., va_hi:gates_lo]], axis=-1).astype(_BF16)
    vb_lo = gates_lo - SW_KV_HEADS * SW_HEAD_DIM
    w_vt_b = jnp.swapaxes(jnp.concatenate([w_in[..., va_lo:va_hi], w_in[..., vb_lo:gates_lo]], axis=-1),
                          1, 2).astype(_BF16)
    w_gates_b = w_in[..., gates_lo:].astype(_BF16)
    w_o_b = w_o.astype(_BF16)
    w_r_b = jnp.swapaxes(w_router, 1, 2).astype(_BF16)
    b_gate = b_up[..., 0::2].reshape(DEPTH, N_EXPERTS, 1, D_FF)
    b_upp = b_up[..., 1::2].reshape(DEPTH, N_EXPERTS, 1, D_FF)
    b_down4 = b_down.reshape(DEPTH, N_EXPERTS, 1, D_MODEL)
    b_router3 = b_router.astype(_F32).reshape(DEPTH, N_EXPERTS, 1)

    att_scale = jnp.concatenate([
        jnp.full((D_MODEL,), DA_HEAD_DIM ** -0.5 * LOG2E, _F32), jnp.ones((D_MODEL,), _F32),
        jnp.full((D_MODEL,), SW_HEAD_DIM ** -0.5 * LOG2E, _F32),
        jnp.ones((2 * SW_KV_HEADS * SW_HEAD_DIM,), _F32)]).reshape(1, ATT_COLS)
    gate_scale = jnp.ones((1, GATE_COLS), _F32)
    rest = 2.0 ** (-8.0 * jnp.arange(1, DA_HEADS + 1, dtype=_F32) / DA_HEADS) * LOG2E
    pieces = []
    for _ in range(N_SLOPE_PARTS):
        piece = rest.astype(_BF16).astype(_F32)
        pieces.append(piece)
        rest = rest - piece
    da_slopes = jnp.stack(pieces, axis=1)
    swa_bias = _swa_bias()

    xf = x.reshape(t, D_MODEL)
    xb = xf.astype(_BF16)
    for l in range(DEPTH):
        lam_init = 0.8 - 0.6 * math.exp(-0.3 * l)
        prm = jnp.concatenate([lambda_qk[l].astype(_F32), jnp.full((4, DA_HEAD_DIM), lam_init, _F32)], axis=0)
        att = _project(xb, w_att_b, l, ATT_COLS // 2, att_scale, _BF16)
        gates = _project(xb, w_gates_b, l, GATE_COLS // 2, gate_scale, _BF16)
        vt, vtb = _project_vt(xb, w_vt_b, l)
        o_a = _diff_attention(att, vt, da_slopes, prm, subln_g[l].reshape(DA_WIDTH, 1), batch, seq)
        sink_rows = jnp.repeat(sinks[l].astype(_F32).reshape(SW_KV_HEADS, SW_GROUP) * LOG2E, WINDOW, axis=1)
        o_b = _sliding_window_attention(att, vtb, sink_rows.reshape(SW_KV_HEADS, 1, SW_GROUP * WINDOW),
                                        swa_bias, batch, seq)
        h, hp, route, top_gate, counts = _merge_out(gates, o_a, o_b, xf, w_o_b, ln_g, ln_b, w_r_b, b_router3, l)
        buf_tok, pos, block_expert, n_used = _dispatch_tables(route, counts[:, 0], t)
        x_rows = _gather_rows(hp, buf_tok)
        yb = _experts(block_expert, n_used, x_rows, w_up, b_gate, b_upp, w_down, b_down4, l)
        yg = _gather_rows(yb, pos.reshape(TOP_K * t)).reshape(TOP_K, t, HALF_D)
        xf, xb = _combine(h, yg, top_gate, ln_g, ln_b, l)
    return xf.reshape(batch, seq, D_MODEL)
```

```python
import functools
import math

import jax
import jax.numpy as jnp
from jax import lax
from jax.experimental import pallas as pl
from jax.experimental.pallas import tpu as pltpu
from jax.experimental.pallas import tpu_sc as plsc

D_MODEL = 1024
DEPTH = 4
DA_HEAD_DIM = 128
DA_HEADS = 4
DA_WIDTH = 2 * DA_HEAD_DIM
SW_HEAD_DIM = 64
SW_HEADS = 16
SW_KV_HEADS = 2
SW_GROUP = SW_HEADS // SW_KV_HEADS
WINDOW = 128
N_EXPERTS = 32
TOP_K = 4
D_FF = D_MODEL
SWIGLU_LIMIT = 7.0
SWIGLU_ALPHA = 1.702
LN_EPS = 1e-5
DEEPNORM_ALPHA = (2.0 * DEPTH) ** 0.25
NEG_INF = -1e30

QA_OFF, KA_OFF, QB_OFF, KVB_OFF = 0, 1024, 2048, 3072
ATT_COLS = 3328
GATE_COLS = 2 * D_MODEL
LANES = 128
HALF_D = D_MODEL // 2
SC_CORES = 2
SC_SUBCORES = 16
GATHER_CHUNK = 64
SUBLANES_BF16 = 16
VT_ROWS = DA_WIDTH + SUBLANES_BF16
VTB_ROWS = SW_HEAD_DIM + SUBLANES_BF16
LOG2E = math.log2(math.e)
N_SLOPE_PARTS = 3

PROJ_TM = 1024
ATT_TQ = 1024
ATT_TK = 512
SWA_TB = 256
OUT_TM = 256
MOE_BLOCK = 256
DEINT_TILE = 256
LN2_TM = 256
VMEM_LIMIT = 56 * 1024 * 1024

_F32 = jnp.float32
_BF16 = jnp.bfloat16


def _params(*sem):
    return pltpu.CompilerParams(dimension_semantics=sem, vmem_limit_bytes=VMEM_LIMIT)


def _proj_kernel(x_ref, w_ref, s_ref, o_ref):
    acc = jnp.dot(x_ref[...], w_ref[0], preferred_element_type=_F32)
    o_ref[...] = (acc * s_ref[...]).astype(o_ref.dtype)


def _project(xb, w_all, layer, tn, scale, out_dtype):
    t = xb.shape[0]
    n_cols = w_all.shape[-1]
    assert n_cols % tn == 0 and t % PROJ_TM == 0
    return pl.pallas_call(
        _proj_kernel,
        grid=(n_cols // tn, t // PROJ_TM),
        in_specs=[
            pl.BlockSpec((PROJ_TM, D_MODEL), lambda j, i: (i, 0)),
            pl.BlockSpec((1, D_MODEL, tn), lambda j, i: (layer, 0, j)),
            pl.BlockSpec((1, tn), lambda j, i: (0, j)),
        ],
        out_specs=pl.BlockSpec((PROJ_TM, tn), lambda j, i: (i, j)),
        out_shape=jax.ShapeDtypeStruct((t, n_cols), out_dtype),
        compiler_params=_params("parallel", "parallel"),
        name="in_proj",
    )(xb, w_all, scale)


def _proj_vt_kernel(x_ref, wt_ref, oa_ref, ob_ref):
    x = x_ref[...]
    nt = (((1,), (1,)), ((), ()))
    ones = jnp.ones((SUBLANES_BF16, x.shape[0]), oa_ref.dtype)
    for h in range(DA_HEADS):
        vt = lax.dot_general(wt_ref[0, h * DA_WIDTH:(h + 1) * DA_WIDTH, :], x, nt, preferred_element_type=_F32)
        oa_ref[h * VT_ROWS:h * VT_ROWS + DA_WIDTH, :] = vt.astype(oa_ref.dtype)
        oa_ref[h * VT_ROWS + DA_WIDTH:(h + 1) * VT_ROWS, :] = ones
    vb = lax.dot_general(wt_ref[0, D_MODEL:, :], x, nt, preferred_element_type=_F32)
    for kh in range(SW_KV_HEADS):
        ob_ref[kh * VTB_ROWS:kh * VTB_ROWS + SW_HEAD_DIM, :] = (
            vb[kh * SW_HEAD_DIM:(kh + 1) * SW_HEAD_DIM, :].astype(ob_ref.dtype))
        ob_ref[kh * VTB_ROWS + SW_HEAD_DIM:(kh + 1) * VTB_ROWS, :] = ones


def _project_vt(xb, wt_all, layer):
    t = xb.shape[0]
    n_b = SW_KV_HEADS * SW_HEAD_DIM
    return pl.pallas_call(
        _proj_vt_kernel,
        grid=(t // PROJ_TM,),
        in_specs=[
            pl.BlockSpec((PROJ_TM, D_MODEL), lambda i: (i, 0)),
            pl.BlockSpec((1, D_MODEL + n_b, D_MODEL), lambda i: (layer, 0, 0)),
        ],
        out_specs=[pl.BlockSpec((DA_HEADS * VT_ROWS, PROJ_TM), lambda i: (0, i)),
                   pl.BlockSpec((SW_KV_HEADS * VTB_ROWS, PROJ_TM), lambda i: (0, i))],
        out_shape=[jax.ShapeDtypeStruct((DA_HEADS * VT_ROWS, t), _BF16),
                   jax.ShapeDtypeStruct((SW_KV_HEADS * VTB_ROWS, t), _BF16)],
        compiler_params=_params("parallel"),
        name="in_proj_vt",
    )(xb, wt_all)


def _diff_attn_kernel(slope_ref, prm_ref, g_ref, q_ref, k_ref, vt_ref, o_ref,
                      acc_ref, m_ref, qaug_ref, kaug_ref, s_ref, smax_ref, p_ref, alpha_ref, *, tq, tk):
    h = pl.program_id(1)
    i = pl.program_id(2)
    diag = tq // tk
    n_full = i * diag
    lane = lax.broadcasted_iota(jnp.int32, (tq, DA_HEAD_DIM), 1)
    slope_tile = jnp.zeros((tq, DA_HEAD_DIM), _F32)
    slope = 0.0
    for part in range(N_SLOPE_PARTS):
        piece = slope_ref[h, part]
        slope = slope + piece
        slope_tile = jnp.where(lane // 2 == part, piece, slope_tile)
    klane = lax.broadcasted_iota(jnp.int32, (tk, DA_HEAD_DIM), 1)
    kpos = lax.broadcasted_iota(jnp.int32, (tk, DA_HEAD_DIM), 0)
    pos_tile = jnp.where(klane < 2 * N_SLOPE_PARTS, jnp.where(klane % 2 == 0, kpos % 256, kpos // 256 * 256), 0)
    for c in range(2):
        kaug_ref[c, :, DA_HEAD_DIM:] = pos_tile.astype(_F32).astype(_BF16)
        qaug_ref[c, :, :DA_HEAD_DIM] = q_ref[:, c * DA_HEAD_DIM:(c + 1) * DA_HEAD_DIM]
        qaug_ref[c, :, DA_HEAD_DIM:] = slope_tile.astype(_BF16)
    m_ref[...] = jnp.full(m_ref.shape, NEG_INF, _F32)
    acc_ref[...] = jnp.zeros(acc_ref.shape, _F32)

    def scores(j):
        start = pl.multiple_of(j * tk, tk)
        for c in range(2):
            kaug_ref[c, :, :DA_HEAD_DIM] = k_ref[pl.ds(start, tk), c * DA_HEAD_DIM:(c + 1) * DA_HEAD_DIM]
            s = lax.dot_general(kaug_ref[c], qaug_ref[c], (((1,), (1,)), ((), ())),
                                preferred_element_type=_F32)
            s_ref[c] = s
            smax_ref[c] = jnp.max(s, axis=0, keepdims=True)

    def weights(j, boundary):
        off = (j * tk - i * tq).astype(_F32) * slope
        for c in range(2):
            s = s_ref[c]
            if boundary is None:
                smax = smax_ref[c]
            else:
                keys = lax.broadcasted_iota(jnp.int32, (tk, tq), 0) + boundary * tk
                queries = lax.broadcasted_iota(jnp.int32, (tk, tq), 1)
                s = jnp.where(keys <= queries, s, NEG_INF)
                smax = jnp.max(s, axis=0, keepdims=True)
            m_old = m_ref[c]
            m_new = jnp.maximum(m_old, smax + off)
            alpha_ref[c] = jnp.exp2(m_old - m_new)
            p_ref[c] = jnp.exp2(s - (m_new - off)).astype(_BF16)
            m_ref[c] = m_new

    def accumulate(j):
        start = pl.multiple_of(j * tk, tk)
        vt = vt_ref[:, pl.ds(start, tk)]
        for c in range(2):
            acc_ref[c] = alpha_ref[c] * acc_ref[c] + jnp.dot(vt, p_ref[c], preferred_element_type=_F32)

    def body(j, carry):
        accumulate(j)
        weights(j + 1, None)
        scores(j + 2)
        return carry

    def enter_boundary():
        weights(n_full, 0)
        if diag > 1:
            scores(n_full + 1)

    scores(0)

    @pl.when(i >= 1)
    def _():
        weights(0, None)
        scores(1)
        lax.fori_loop(0, n_full - 1, body, 0)
        accumulate(n_full - 1)
        enter_boundary()

    @pl.when(i == 0)
    def _():
        enter_boundary()

    for d in range(diag):
        accumulate(n_full + d)
        if d + 1 < diag:
            weights(n_full + d + 1, d + 1)
            if d + 2 < diag:
                scores(n_full + d + 2)

    prm = prm_ref[...]
    lam_init = prm[4:5, 0:1]
    lam = (jnp.exp(jnp.sum(prm[0:1] * prm[1:2], axis=1, keepdims=True))
           - jnp.exp(jnp.sum(prm[2:3] * prm[3:4], axis=1, keepdims=True)) + lam_init)
    o0 = acc_ref[0, :DA_WIDTH, :] / acc_ref[0, DA_WIDTH:DA_WIDTH + 1, :]
    o1 = acc_ref[1, :DA_WIDTH, :] / acc_ref[1, DA_WIDTH:DA_WIDTH + 1, :]
    a = o0 - lam * o1
    ms = jnp.mean(a * a, axis=0, keepdims=True)
    out = a * lax.rsqrt(ms + LN_EPS) * g_ref[...] * (1.0 - lam_init)
    o_ref[...] = out.T.astype(o_ref.dtype)


def _diff_attention(att, vt, slopes, prm, subln_g, batch, seq):
    tq, tk = ATT_TQ, ATT_TK
    assert tq % tk == 0 and seq % tq == 0
    nq = seq // tq
    t = batch * seq
    kernel = functools.partial(_diff_attn_kernel, tq=tq, tk=tk)
    return pl.pallas_call(
        kernel,
        grid=(batch, DA_HEADS, nq),
        in_specs=[
            pl.BlockSpec(memory_space=pltpu.SMEM),
            pl.BlockSpec((8, LANES), lambda b, h, i: (0, 0)),
            pl.BlockSpec((DA_WIDTH, 1), lambda b, h, i: (0, 0)),
            pl.BlockSpec((tq, DA_WIDTH), lambda b, h, i: (b * nq + i, QA_OFF // DA_WIDTH + h)),
            pl.BlockSpec((seq, DA_WIDTH), lambda b, h, i: (b, KA_OFF // DA_WIDTH + h)),
            pl.BlockSpec((VT_ROWS, seq), lambda b, h, i: (h, b)),
        ],
        out_specs=pl.BlockSpec((tq, DA_WIDTH), lambda b, h, i: (b * nq + i, h)),
        out_shape=jax.ShapeDtypeStruct((t, D_MODEL), _BF16),
        scratch_shapes=[
            pltpu.VMEM((2, VT_ROWS, tq), _F32),
            pltpu.VMEM((2, 1, tq), _F32),
            pltpu.VMEM((2, tq, 2 * DA_HEAD_DIM), _BF16),
            pltpu.VMEM((2, tk, 2 * DA_HEAD_DIM), _BF16),
            pltpu.VMEM((2, tk, tq), _F32),
            pltpu.VMEM((2, 1, tq), _F32),
            pltpu.VMEM((2, tk, tq), _BF16),
            pltpu.VMEM((2, 1, tq), _F32),
        ],
        compiler_params=_params("parallel", "parallel", "arbitrary"),
        name="diff_attn",
    )(slopes, prm, subln_g, att, att, vt)


def _swa_kernel(sink_ref, bias_ref, q_ref, kv_ref, kvp_ref, vt_ref, vtp_ref, o_ref):
    i = pl.program_id(1)
    first = (i == 0).astype(jnp.int32)
    nt = (((1,), (1,)), ((), ()))
    kv_all = jnp.concatenate([kvp_ref[...], kv_ref[...]], axis=0)
    vt_all = jnp.concatenate([vtp_ref[...], vt_ref[...]], axis=1)
    for r in range(SWA_TB // WINDOW):
        heads = []
        for kh in range(SW_KV_HEADS):
            kwin = kv_all[r * WINDOW:(r + 2) * WINDOW, kh * SW_HEAD_DIM:(kh + 1) * SW_HEAD_DIM]
            vwin = vt_all[kh * VTB_ROWS:(kh + 1) * VTB_ROWS, r * WINDOW:(r + 2) * WINDOW]
            qs = jnp.concatenate(
                [q_ref[r * WINDOW:(r + 1) * WINDOW, (kh * SW_GROUP + g) * SW_HEAD_DIM:(kh * SW_GROUP + g + 1) * SW_HEAD_DIM]
                 for g in range(SW_GROUP)], axis=0)
            s = lax.dot_general(kwin, qs, nt, preferred_element_type=_F32)
            bias = bias_ref[2 * kh + first] if r == 0 else bias_ref[2 * kh]
            s = s + bias
            sink = sink_ref[kh]
            m = jnp.maximum(jnp.max(s, axis=0, keepdims=True), sink)
            p = jnp.exp2(s - m)
            ot = jnp.dot(vwin, p.astype(_BF16), preferred_element_type=_F32)
            denom = ot[SW_HEAD_DIM:SW_HEAD_DIM + 1, :] + jnp.exp2(sink - m)
            ot = ot[:SW_HEAD_DIM, :] / denom
            for g in range(SW_GROUP):
                heads.append(ot[:, g * WINDOW:(g + 1) * WINDOW].T)
        o_ref[r * WINDOW:(r + 1) * WINDOW, :] = jnp.concatenate(heads, axis=1).astype(o_ref.dtype)


def _swa_bias():
    slopes = 2.0 ** (-8.0 * jnp.arange(1, SW_HEADS + 1, dtype=_F32) / SW_HEADS)
    kj = jnp.arange(2 * WINDOW, dtype=jnp.int32)[:, None]
    qi = jnp.arange(WINDOW, dtype=jnp.int32)[None, :]
    dist = qi + WINDOW - kj
    valid = (dist >= 0) & (dist < WINDOW)
    tables = []
    for kh in range(SW_KV_HEADS):
        for has_prev in (True, False):
            ok = valid if has_prev else valid & (kj >= WINDOW)
            per_head = [jnp.where(ok, -slopes[kh * SW_GROUP + g] * LOG2E * dist.astype(_F32), NEG_INF)
                        for g in range(SW_GROUP)]
            tables.append(jnp.concatenate(per_head, axis=1))
    return jnp.stack(tables)


def _sliding_window_attention(att, vtb, sink_rows, bias, batch, seq):
    tb = SWA_TB
    nb = seq // tb
    ratio = tb // WINDOW
    t = batch * seq
    kv_width = 2 * SW_KV_HEADS * SW_HEAD_DIM
    n_b = SW_KV_HEADS * VTB_ROWS
    q_blk = QB_OFF // D_MODEL
    kv_blk = KVB_OFF // kv_width
    prev = lambda b, i: jnp.maximum((b * nb + i) * ratio - 1, 0)
    return pl.pallas_call(
        _swa_kernel,
        grid=(batch, nb),
        in_specs=[
            pl.BlockSpec((SW_KV_HEADS, 1, SW_GROUP * WINDOW), lambda b, i: (0, 0, 0)),
            pl.BlockSpec((2 * SW_KV_HEADS, 2 * WINDOW, SW_GROUP * WINDOW), lambda b, i: (0, 0, 0)),
            pl.BlockSpec((tb, D_MODEL), lambda b, i: (b * nb + i, q_blk)),
            pl.BlockSpec((tb, kv_width), lambda b, i: (b * nb + i, kv_blk)),
            pl.BlockSpec((WINDOW, kv_width), lambda b, i: (prev(b, i), kv_blk)),
            pl.BlockSpec((n_b, tb), lambda b, i: (0, b * nb + i)),
            pl.BlockSpec((n_b, WINDOW), lambda b, i: (0, prev(b, i))),
        ],
        out_specs=pl.BlockSpec((tb, D_MODEL), lambda b, i: (b * nb + i, 0)),
        out_shape=jax.ShapeDtypeStruct((t, D_MODEL), _BF16),
        compiler_params=_params("parallel", "arbitrary"),
        name="swa",
    )(sink_rows, bias, att, att, att, vtb, vtb)


def _pack_rows(x):
    bits = lax.bitcast_convert_type(x.astype(_BF16).astype(_F32), jnp.uint32)
    return lax.bitcast_convert_type(bits[:, :HALF_D] | (bits[:, HALF_D:] >> 16), jnp.int32)


def _unpack_rows(words):
    bits = lax.bitcast_convert_type(words, jnp.uint32)
    hi = lax.bitcast_convert_type(bits & jnp.uint32(0xFFFF0000), _F32)
    lo = lax.bitcast_convert_type(bits << 16, _F32)
    return hi, lo


def _gather_rows(table, idx):
    n_idx = idx.shape[0]
    width = table.shape[1]
    workers = SC_CORES * SC_SUBCORES
    assert n_idx % (workers * 2 * GATHER_CHUNK) == 0
    per_worker = n_idx // workers
    n_chunks = per_worker // GATHER_CHUNK
    mesh = plsc.VectorSubcoreMesh(core_axis_name="c", subcore_axis_name="s",
                                  num_cores=SC_CORES, num_subcores=SC_SUBCORES)

    @functools.partial(
        pl.kernel, mesh=mesh,
        out_type=jax.ShapeDtypeStruct((n_idx, width), table.dtype),
        scratch_types=[
            pltpu.VMEM((per_worker,), jnp.int32),
            pltpu.VMEM((2, GATHER_CHUNK, width), table.dtype),
            pltpu.SemaphoreType.DMA((2,)),
            pltpu.SemaphoreType.DMA((2,)),
        ],
        name="gather_rows",
    )
    def gather(table_hbm, idx_hbm, out_hbm, idx_v, rows_v, fetch_sem, store_sem):
        base = (lax.axis_index("s") * SC_CORES + lax.axis_index("c")) * per_worker
        pltpu.sync_copy(idx_hbm.at[pl.ds(base, per_worker)], idx_v)

        def fetch(g, buf):
            chunk = idx_v.at[pl.ds(g * GATHER_CHUNK, GATHER_CHUNK)]
            return pltpu.make_async_copy(table_hbm.at[chunk], rows_v.at[buf], fetch_sem.at[buf])

        def store(g, buf):
            dst = out_hbm.at[pl.ds(base + g * GATHER_CHUNK, GATHER_CHUNK)]
            return pltpu.make_async_copy(rows_v.at[buf], dst, store_sem.at[buf])

        fetch(0, 0).start()

        @pl.loop(0, n_chunks, step=2)
        def _(g0):
            for buf in range(2):
                g = g0 + buf
                fetch(g, buf).wait()

                @pl.when(g >= 1)
                def _():
                    store(g - 1, 1 - buf).wait()

                @pl.when(g + 1 < n_chunks)
                def _():
                    fetch(g + 1, 1 - buf).start()

                store(g, buf).start()

        store(n_chunks - 1, 1).wait()

    return gather(table, idx)


def _layer_norm(z, g, b):
    mu = jnp.mean(z, axis=1, keepdims=True)
    d = z - mu
    var = jnp.mean(d * d, axis=1, keepdims=True)
    return d * lax.rsqrt(var + LN_EPS) * g + b


def _merge_out_kernel(ga_ref, gb_ref, oa_ref, ob_ref, x_ref, wo_ref, lng_ref, lnb_ref, wr_ref, br_ref,
                      h_ref, hp_ref, idx_ref, gate_ref, cnt_ref, run_ref):
    @pl.when(pl.program_id(0) == 0)
    def _():
        run_ref[...] = jnp.zeros(run_ref.shape, _F32)

    merged = (jax.nn.sigmoid(ga_ref[...].astype(_F32)) * oa_ref[...].astype(_F32)
              + jax.nn.sigmoid(gb_ref[...].astype(_F32)) * ob_ref[...].astype(_F32))
    y = jnp.dot(merged.astype(_BF16), wo_ref[0], preferred_element_type=_F32)
    hn = _layer_norm(DEEPNORM_ALPHA * x_ref[...] + y, lng_ref[0, 0:1, :], lnb_ref[0, 0:1, :])
    h_ref[...] = hn
    hb = hn.astype(_BF16)
    hp_ref[...] = _pack_rows(hn)
    logits = lax.dot_general(wr_ref[0], hb, (((1,), (1,)), ((), ())),
                             preferred_element_type=_F32) + br_ref[0]
    tm = logits.shape[1]
    erow = lax.broadcasted_iota(jnp.int32, (N_EXPERTS, tm), 0)
    cur = logits
    vals, picks, id_rows = [], [], []
    for k in range(TOP_K):
        mx = jnp.max(cur, axis=0, keepdims=True)
        ix = jnp.min(jnp.where(cur == mx, erow, N_EXPERTS), axis=0, keepdims=True)
        vals.append(mx)
        picks.append(erow == ix)
        id_rows.append(ix)
        cur = jnp.where(picks[k], -jnp.inf, cur)
    exps = [jnp.exp(v - vals[0]) for v in vals]
    tot = exps[0] + exps[1] + exps[2] + exps[3]
    gate_rows = [e / tot for e in exps] + [jnp.zeros((LANES - TOP_K, tm), _F32)]
    gate_ref[...] = jnp.concatenate(gate_rows, axis=0).T

    chosen = jnp.zeros((N_EXPERTS, tm), _F32)
    for k in range(TOP_K):
        chosen = chosen + jnp.where(picks[k], 1.0, 0.0)
    r = lax.broadcasted_iota(jnp.int32, (tm, tm), 0)
    c = lax.broadcasted_iota(jnp.int32, (tm, tm), 1)
    earlier = jnp.where(r < c, 1.0, 0.0).astype(_BF16)
    before = jnp.dot(chosen.astype(_BF16), earlier, preferred_element_type=_F32) + run_ref[...]
    rank_rows = [jnp.sum(jnp.where(picks[k], before, 0.0), axis=0, keepdims=True).astype(jnp.int32)
                 for k in range(TOP_K)]
    idx_ref[...] = jnp.concatenate(id_rows + rank_rows, axis=0)
    total = run_ref[...] + jnp.sum(chosen, axis=1, keepdims=True)
    run_ref[...] = total
    cnt_ref[...] = jnp.broadcast_to(total, cnt_ref.shape).astype(jnp.int32)


def _merge_out(gates, o_a, o_b, x, wo_all, ln_g, ln_b, wr_all, br_all, layer):
    t = x.shape[0]
    tm = OUT_TM
    row = lambda i: (i, 0)
    return pl.pallas_call(
        _merge_out_kernel,
        grid=(t // tm,),
        in_specs=[
            pl.BlockSpec((tm, D_MODEL), lambda i: (i, 0)),
            pl.BlockSpec((tm, D_MODEL), lambda i: (i, 1)),
            pl.BlockSpec((tm, D_MODEL), row),
            pl.BlockSpec((tm, D_MODEL), row),
            pl.BlockSpec((tm, D_MODEL), row),
            pl.BlockSpec((1, D_MODEL, D_MODEL), lambda i: (layer, 0, 0)),
            pl.BlockSpec((1, 2, D_MODEL), lambda i: (layer, 0, 0)),
            pl.BlockSpec((1, 2, D_MODEL), lambda i: (layer, 0, 0)),
            pl.BlockSpec((1, N_EXPERTS, D_MODEL), lambda i: (layer, 0, 0)),
            pl.BlockSpec((1, N_EXPERTS, 1), lambda i: (layer, 0, 0)),
        ],
        out_specs=[
            pl.BlockSpec((tm, D_MODEL), row),
            pl.BlockSpec((tm, HALF_D), row),
            pl.BlockSpec((2 * TOP_K, tm), lambda i: (0, i)),
            pl.BlockSpec((tm, LANES), row),
            pl.BlockSpec((N_EXPERTS, LANES), lambda i: (0, 0)),
        ],
        out_shape=[
            jax.ShapeDtypeStruct((t, D_MODEL), _F32),
            jax.ShapeDtypeStruct((t, HALF_D), jnp.int32),
            jax.ShapeDtypeStruct((2 * TOP_K, t), jnp.int32),
            jax.ShapeDtypeStruct((t, LANES), _F32),
            jax.ShapeDtypeStruct((N_EXPERTS, LANES), jnp.int32),
        ],
        scratch_shapes=[pltpu.VMEM((N_EXPERTS, 1), _F32)],
        compiler_params=_params("arbitrary"),
        name="merge_out_ln_router",
    )(gates, gates, o_a, o_b, x, wo_all, ln_g, ln_b, wr_all, br_all)


def _expert_kernel(be_ref, nused_ref, x_ref, wup_ref, bg_ref, bu_ref, wd_ref, bd_ref, y_ref,
                   wg_scr, wu_scr, wd_scr):
    i = pl.program_id(0)
    changed = jnp.logical_or(i == 0, be_ref[i] != be_ref[jnp.maximum(i - 1, 0)])

    @pl.when(jnp.logical_and(changed, i < nused_ref[0]))
    def _():
        half = DEINT_TILE // 2
        r = lax.broadcasted_iota(jnp.int32, (DEINT_TILE, DEINT_TILE), 0)
        c = lax.broadcasted_iota(jnp.int32, (DEINT_TILE, DEINT_TILE), 1)
        sel = jnp.where(r == jnp.where(c < half, 2 * c, 2 * (c - half) + 1), 1.0, 0.0).astype(_BF16)
        for t in range(2 * D_FF // DEINT_TILE):
            w = wup_ref[0, 0, :, t * DEINT_TILE:(t + 1) * DEINT_TILE].astype(_BF16)
            de = jnp.dot(w, sel, preferred_element_type=_F32).astype(_BF16)
            wg_scr[:, t * half:(t + 1) * half] = de[:, :half]
            wu_scr[:, t * half:(t + 1) * half] = de[:, half:]
        wd_scr[...] = wd_ref[0, 0].astype(_BF16)

    @pl.when(i < nused_ref[0])
    def _():
        hi, lo = _unpack_rows(x_ref[...])
        x = jnp.concatenate([hi.astype(_BF16), lo.astype(_BF16)], axis=1)
        gate = jnp.dot(x, wg_scr[...], preferred_element_type=_F32) + bg_ref[0, 0]
        up = jnp.dot(x, wu_scr[...], preferred_element_type=_F32) + bu_ref[0, 0]
        gate = jnp.minimum(gate, SWIGLU_LIMIT)
        up = jnp.clip(up, -SWIGLU_LIMIT, SWIGLU_LIMIT)
        act = (up + 1.0) * (gate * jax.nn.sigmoid(SWIGLU_ALPHA * gate))
        y = jnp.dot(act.astype(_BF16), wd_scr[...], preferred_element_type=_F32) + bd_ref[0, 0]
        y_ref[...] = _pack_rows(y)

    @pl.when(i >= nused_ref[0])
    def _():
        y_ref[...] = jnp.zeros(y_ref.shape, y_ref.dtype)


def _experts(block_expert, n_used, xb, w_up, bg_all, bu_all, w_down, bd_all, layer):
    n_rows = xb.shape[0]
    n_blocks = n_rows // MOE_BLOCK
    pick = lambda i, be, nu: (layer, be[i], 0, 0)
    bspec = pl.BlockSpec((1, 1, 1, D_FF), pick)
    grid_spec = pltpu.PrefetchScalarGridSpec(
        num_scalar_prefetch=2,
        grid=(n_blocks,),
        in_specs=[
            pl.BlockSpec((MOE_BLOCK, HALF_D), lambda i, be, nu: (i, 0)),
            pl.BlockSpec((1, 1, D_MODEL, 2 * D_FF), pick),
            bspec, bspec,
            pl.BlockSpec((1, 1, D_FF, D_MODEL), pick),
            pl.BlockSpec((1, 1, 1, D_MODEL), pick),
        ],
        out_specs=pl.BlockSpec((MOE_BLOCK, HALF_D), lambda i, be, nu: (i, 0)),
        scratch_shapes=[
            pltpu.VMEM((D_MODEL, D_FF), _BF16),
            pltpu.VMEM((D_MODEL, D_FF), _BF16),
            pltpu.VMEM((D_FF, D_MODEL), _BF16),
        ],
    )
    return pl.pallas_call(
        _expert_kernel,
        grid_spec=grid_spec,
        out_shape=jax.ShapeDtypeStruct((n_rows, HALF_D), jnp.int32),
        compiler_params=_params("arbitrary"),
        name="experts",
    )(block_expert, n_used, xb, w_up, bg_all, bu_all, w_down, bd_all)


def _combine_kernel(h_ref, yg_ref, gate_ref, lng_ref, lnb_ref, x_ref, xb_ref):
    gates = gate_ref[...]
    y = None
    for k in range(TOP_K):
        hi, lo = _unpack_rows(yg_ref[k])
        yk = gates[:, k:k + 1] * jnp.concatenate([hi, lo], axis=1)
        y = yk if y is None else y + yk
    xn = _layer_norm(DEEPNORM_ALPHA * h_ref[...] + y, lng_ref[0, 1:2, :], lnb_ref[0, 1:2, :])
    x_ref[...] = xn
    xb_ref[...] = xn.astype(_BF16)


def _combine(h, yg, gates, ln_g, ln_b, layer):
    t = h.shape[0]
    tm = LN2_TM
    row = lambda i: (i, 0)
    return pl.pallas_call(
        _combine_kernel,
        grid=(t // tm,),
        in_specs=[
            pl.BlockSpec((tm, D_MODEL), row),
            pl.BlockSpec((TOP_K, tm, HALF_D), lambda i: (0, i, 0)),
            pl.BlockSpec((tm, LANES), row),
            pl.BlockSpec((1, 2, D_MODEL), lambda i: (layer, 0, 0)),
            pl.BlockSpec((1, 2, D_MODEL), lambda i: (layer, 0, 0)),
        ],
        out_specs=[pl.BlockSpec((tm, D_MODEL), row), pl.BlockSpec((tm, D_MODEL), row)],
        out_shape=[jax.ShapeDtypeStruct((t, D_MODEL), _F32), jax.ShapeDtypeStruct((t, D_MODEL), _BF16)],
        compiler_params=_params("parallel"),
        name="combine_ln",
    )(h, yg, gates, ln_g, ln_b)


def _dispatch_tables(route, counts, t):
    tk = t * TOP_K
    experts = route[:TOP_K]
    ranks = route[TOP_K:]
    padded = (counts + MOE_BLOCK - 1) // MOE_BLOCK * MOE_BLOCK
    pad_end = jnp.cumsum(padded)
    pad_start = pad_end - padded
    start = jnp.cumsum(counts) - counts
    expert_ids = jnp.arange(N_EXPERTS, dtype=jnp.int32)
    dest = jnp.sum(jnp.where(experts[..., None] == expert_ids, pad_start, 0), axis=-1) + ranks
    n_blocks = (tk + N_EXPERTS * (MOE_BLOCK - 1) + MOE_BLOCK - 1) // MOE_BLOCK
    block_start = jnp.arange(n_blocks, dtype=jnp.int32) * MOE_BLOCK
    block_expert = jnp.minimum(
        jnp.sum((pad_end[None, :] <= block_start[:, None]).astype(jnp.int32), axis=1), N_EXPERTS - 1)
    n_used = (pad_end[-1:] // MOE_BLOCK).astype(jnp.int32)
    order = jnp.argsort(experts.T.reshape(tk))
    in_block = jnp.arange(MOE_BLOCK, dtype=jnp.int32)[None, :]
    r = (block_start - pad_start[block_expert])[:, None] + in_block
    valid = r < counts[block_expert][:, None]
    src = jnp.where(valid, start[block_expert][:, None] + r, 0).reshape(n_blocks * MOE_BLOCK)
    filler = jnp.arange(n_blocks * MOE_BLOCK, dtype=jnp.int32) % t
    buf_tok = jnp.where(valid.reshape(-1), order[src] // TOP_K, filler)
    return buf_tok, dest, block_expert, n_used


def kernel(x, w_in, w_o, lambda_qk, subln_g, sinks, ln_g, ln_b, w_router, b_router, w_up, b_up, w_down, b_down):
    batch, seq, _ = x.shape
    t = batch * seq
    assert seq % ATT_TQ == 0 and seq % SWA_TB == 0 and t % PROJ_TM == 0

    va_lo, va_hi = 2 * D_MODEL, 3 * D_MODEL
    gates_lo = w_in.shape[-1] - GATE_COLS
    w_att_b = jnp.concatenate([w_in[..., :va_lo], w_in[..., va_hi:gates_lo]], axis=-1).astype(_BF16)
    vb_lo = gates_lo - SW_KV_HEADS * SW_HEAD_DIM
    w_vt_b = jnp.swapaxes(jnp.concatenate([w_in[..., va_lo:va_hi], w_in[..., vb_lo:gates_lo]], axis=-1),
                          1, 2).astype(_BF16)
    w_gates_b = w_in[..., gates_lo:].astype(_BF16)
    w_o_b = w_o.astype(_BF16)
    w_r_b = jnp.swapaxes(w_router, 1, 2).astype(_BF16)
    b_gate = b_up[..., 0::2].reshape(DEPTH, N_EXPERTS, 1, D_FF)
    b_upp = b_up[..., 1::2].reshape(DEPTH, N_EXPERTS, 1, D_FF)
    b_down4 = b_down.reshape(DEPTH, N_EXPERTS, 1, D_MODEL)
    b_router3 = b_router.astype(_F32).reshape(DEPTH, N_EXPERTS, 1)

    att_scale = jnp.concatenate([
        jnp.full((D_MODEL,), DA_HEAD_DIM ** -0.5 * LOG2E, _F32), jnp.ones((D_MODEL,), _F32),
        jnp.full((D_MODEL,), SW_HEAD_DIM ** -0.5 * LOG2E, _F32),
        jnp.ones((2 * SW_KV_HEADS * SW_HEAD_DIM,), _F32)]).reshape(1, ATT_COLS)
    gate_scale = jnp.ones((1, GATE_COLS), _F32)
    rest = 2.0 ** (-8.0 * jnp.arange(1, DA_HEADS + 1, dtype=_F32) / DA_HEADS) * LOG2E
    pieces = []
    for _ in range(N_SLOPE_PARTS):
        piece = rest.astype(_BF16).astype(_F32)
        pieces.append(piece)
        rest = rest - piece
    da_slopes = jnp.stack(pieces, axis=1)
    swa_bias = _swa_bias()

    xf = x.reshape(t, D_MODEL)
    xb = xf.astype(_BF16)
    for l in range(DEPTH):
        lam_init = 0.8 - 0.6 * math.exp(-0.3 * l)
        prm = jnp.concatenate([lambda_qk[l].astype(_F32), jnp.full((4, DA_HEAD_DIM), lam_init, _F32)], axis=0)
        att = _project(xb, w_att_b, l, ATT_COLS // 2, att_scale, _BF16)
        gates = _project(xb, w_gates_b, l, GATE_COLS // 2, gate_scale, _BF16)
        vt, vtb = _project_vt(xb, w_vt_b, l)
        o_a = _diff_attention(att, vt, da_slopes, prm, subln_g[l].reshape(DA_WIDTH, 1), batch, seq)
        sink_rows = jnp.repeat(sinks[l].astype(_F32).reshape(SW_KV_HEADS, SW_GROUP) * LOG2E, WINDOW, axis=1)
        o_b = _sliding_window_attention(att, vtb, sink_rows.reshape(SW_KV_HEADS, 1, SW_GROUP * WINDOW),
                                        swa_bias, batch, seq)
        h, hp, route, top_gate, counts = _merge_out(gates, o_a, o_b, xf, w_o_b, ln_g, ln_b, w_r_b, b_router3, l)
        buf_tok, pos, block_expert, n_used = _dispatch_tables(route, counts[:, 0], t)
        x_rows = _gather_rows(hp, buf_tok)
        yb = _experts(block_expert, n_used, x_rows, w_up, b_gate, b_upp, w_down, b_down4, l)
        yg = _gather_rows(yb, pos.reshape(TOP_K * t)).reshape(TOP_K, t, HALF_D)
        xf, xb = _combine(h, yg, top_gate, ln_g, ln_b, l)
    return xf.reshape(batch, seq, D_MODEL)
```

```python
import functools
import math

import jax
import jax.numpy as jnp
from jax import lax
from jax.experimental import pallas as pl
from jax.experimental.pallas import tpu as pltpu
from jax.experimental.pallas import tpu_sc as plsc

D_MODEL = 1024
DEPTH = 4
DA_HEAD_DIM = 128
DA_HEADS = 4
DA_WIDTH = 2 * DA_HEAD_DIM
SW_HEAD_DIM = 64
SW_HEADS = 16
SW_KV_HEADS = 2
SW_GROUP = SW_HEADS // SW_KV_HEADS
WINDOW = 128
N_EXPERTS = 32
TOP_K = 4
D_FF = D_MODEL
SWIGLU_LIMIT = 7.0
SWIGLU_ALPHA = 1.702
LN_EPS = 1e-5
DEEPNORM_ALPHA = (2.0 * DEPTH) ** 0.25
NEG_INF = -1e30

QA_OFF, KA_OFF, QB_OFF, KVB_OFF = 0, 1024, 2048, 3072
ATT_COLS = 3328
GATE_COLS = 2 * D_MODEL
LANES = 128
HALF_D = D_MODEL // 2
SC_CORES = 2
SC_SUBCORES = 16
GATHER_CHUNK = 64
SUBLANES_BF16 = 16
VT_ROWS = DA_WIDTH + SUBLANES_BF16
VTB_ROWS = SW_HEAD_DIM + SUBLANES_BF16
LOG2E = math.log2(math.e)
N_SLOPE_PARTS = 3

PROJ_TM = 1024
ATT_TQ = 1024
ATT_TK = 512
SWA_TB = 256
OUT_TM = 256
MOE_BLOCK = 256
MOE_PARTS = 2
DEINT_TILE = 256
LN2_TM = 256
VMEM_LIMIT = 56 * 1024 * 1024

_F32 = jnp.float32
_BF16 = jnp.bfloat16


def _params(*sem):
    return pltpu.CompilerParams(dimension_semantics=sem, vmem_limit_bytes=VMEM_LIMIT)


def _proj_kernel(x_ref, w_ref, s_ref, o_ref):
    acc = jnp.dot(x_ref[...], w_ref[0], preferred_element_type=_F32)
    o_ref[...] = (acc * s_ref[...]).astype(o_ref.dtype)


def _project(xb, w_all, layer, tn, scale, out_dtype):
    t = xb.shape[0]
    n_cols = w_all.shape[-1]
    assert n_cols % tn == 0 and t % PROJ_TM == 0
    return pl.pallas_call(
        _proj_kernel,
        grid=(n_cols // tn, t // PROJ_TM),
        in_specs=[
            pl.BlockSpec((PROJ_TM, D_MODEL), lambda j, i: (i, 0)),
            pl.BlockSpec((1, D_MODEL, tn), lambda j, i: (layer, 0, j)),
            pl.BlockSpec((1, tn), lambda j, i: (0, j)),
        ],
        out_specs=pl.BlockSpec((PROJ_TM, tn), lambda j, i: (i, j)),
        out_shape=jax.ShapeDtypeStruct((t, n_cols), out_dtype),
        compiler_params=_params("parallel", "parallel"),
        name="in_proj",
    )(xb, w_all, scale)


def _proj_vt_kernel(x_ref, wt_ref, oa_ref, ob_ref):
    x = x_ref[...]
    nt = (((1,), (1,)), ((), ()))
    ones = jnp.ones((SUBLANES_BF16, x.shape[0]), oa_ref.dtype)
    for h in range(DA_HEADS):
        vt = lax.dot_general(wt_ref[0, h * DA_WIDTH:(h + 1) * DA_WIDTH, :], x, nt, preferred_element_type=_F32)
        oa_ref[h * VT_ROWS:h * VT_ROWS + DA_WIDTH, :] = vt.astype(oa_ref.dtype)
        oa_ref[h * VT_ROWS + DA_WIDTH:(h + 1) * VT_ROWS, :] = ones
    vb = lax.dot_general(wt_ref[0, D_MODEL:, :], x, nt, preferred_element_type=_F32)
    for kh in range(SW_KV_HEADS):
        ob_ref[kh * VTB_ROWS:kh * VTB_ROWS + SW_HEAD_DIM, :] = (
            vb[kh * SW_HEAD_DIM:(kh + 1) * SW_HEAD_DIM, :].astype(ob_ref.dtype))
        ob_ref[kh * VTB_ROWS + SW_HEAD_DIM:(kh + 1) * VTB_ROWS, :] = ones


def _project_vt(xb, wt_all, layer):
    t = xb.shape[0]
    n_b = SW_KV_HEADS * SW_HEAD_DIM
    return pl.pallas_call(
        _proj_vt_kernel,
        grid=(t // PROJ_TM,),
        in_specs=[
            pl.BlockSpec((PROJ_TM, D_MODEL), lambda i: (i, 0)),
            pl.BlockSpec((1, D_MODEL + n_b, D_MODEL), lambda i: (layer, 0, 0)),
        ],
        out_specs=[pl.BlockSpec((DA_HEADS * VT_ROWS, PROJ_TM), lambda i: (0, i)),
                   pl.BlockSpec((SW_KV_HEADS * VTB_ROWS, PROJ_TM), lambda i: (0, i))],
        out_shape=[jax.ShapeDtypeStruct((DA_HEADS * VT_ROWS, t), _BF16),
                   jax.ShapeDtypeStruct((SW_KV_HEADS * VTB_ROWS, t), _BF16)],
        compiler_params=_params("parallel"),
        name="in_proj_vt",
    )(xb, wt_all)


def _diff_attn_kernel(slope_ref, prm_ref, g_ref, q_ref, k_ref, vt_ref, o_ref,
                      acc_ref, m_ref, qaug_ref, kaug_ref, s_ref, smax_ref, p_ref, alpha_ref, *, tq, tk):
    h = pl.program_id(1)
    i = pl.program_id(2)
    diag = tq // tk
    n_full = i * diag
    lane = lax.broadcasted_iota(jnp.int32, (tq, DA_HEAD_DIM), 1)
    slope_tile = jnp.zeros((tq, DA_HEAD_DIM), _F32)
    slope = 0.0
    for part in range(N_SLOPE_PARTS):
        piece = slope_ref[h, part]
        slope = slope + piece
        slope_tile = jnp.where(lane // 2 == part, piece, slope_tile)
    klane = lax.broadcasted_iota(jnp.int32, (tk, DA_HEAD_DIM), 1)
    kpos = lax.broadcasted_iota(jnp.int32, (tk, DA_HEAD_DIM), 0)
    pos_tile = jnp.where(klane < 2 * N_SLOPE_PARTS, jnp.where(klane % 2 == 0, kpos % 256, kpos // 256 * 256), 0)
    for c in range(2):
        kaug_ref[c, :, DA_HEAD_DIM:] = pos_tile.astype(_F32).astype(_BF16)
        qaug_ref[c, :, :DA_HEAD_DIM] = q_ref[:, c * DA_HEAD_DIM:(c + 1) * DA_HEAD_DIM]
        qaug_ref[c, :, DA_HEAD_DIM:] = slope_tile.astype(_BF16)
    m_ref[...] = jnp.full(m_ref.shape, NEG_INF, _F32)
    acc_ref[...] = jnp.zeros(acc_ref.shape, _F32)

    def scores(j):
        start = pl.multiple_of(j * tk, tk)
        for c in range(2):
            kaug_ref[c, :, :DA_HEAD_DIM] = k_ref[pl.ds(start, tk), c * DA_HEAD_DIM:(c + 1) * DA_HEAD_DIM]
            s = lax.dot_general(kaug_ref[c], qaug_ref[c], (((1,), (1,)), ((), ())),
                                preferred_element_type=_F32)
            s_ref[c] = s
            smax_ref[c] = jnp.max(s, axis=0, keepdims=True)

    def weights(j, boundary):
        off = (j * tk - i * tq).astype(_F32) * slope
        for c in range(2):
            s = s_ref[c]
            if boundary is None:
                smax = smax_ref[c]
            else:
                keys = lax.broadcasted_iota(jnp.int32, (tk, tq), 0) + boundary * tk
                queries = lax.broadcasted_iota(jnp.int32, (tk, tq), 1)
                s = jnp.where(keys <= queries, s, NEG_INF)
                smax = jnp.max(s, axis=0, keepdims=True)
            m_old = m_ref[c]
            m_new = jnp.maximum(m_old, smax + off)
            alpha_ref[c] = jnp.exp2(m_old - m_new)
            p_ref[c] = jnp.exp2(s - (m_new - off)).astype(_BF16)
            m_ref[c] = m_new

    def accumulate(j):
        start = pl.multiple_of(j * tk, tk)
        vt = vt_ref[:, pl.ds(start, tk)]
        for c in range(2):
            acc_ref[c] = alpha_ref[c] * acc_ref[c] + jnp.dot(vt, p_ref[c], preferred_element_type=_F32)

    def body(j, carry):
        accumulate(j)
        weights(j + 1, None)
        scores(j + 2)
        return carry

    def enter_boundary():
        weights(n_full, 0)
        if diag > 1:
            scores(n_full + 1)

    scores(0)

    @pl.when(i >= 1)
    def _():
        weights(0, None)
        scores(1)
        lax.fori_loop(0, n_full - 1, body, 0)
        accumulate(n_full - 1)
        enter_boundary()

    @pl.when(i == 0)
    def _():
        enter_boundary()

    for d in range(diag):
        accumulate(n_full + d)
        if d + 1 < diag:
            weights(n_full + d + 1, d + 1)
            if d + 2 < diag:
                scores(n_full + d + 2)

    prm = prm_ref[...]
    lam_init = prm[4:5, 0:1]
    lam = (jnp.exp(jnp.sum(prm[0:1] * prm[1:2], axis=1, keepdims=True))
           - jnp.exp(jnp.sum(prm[2:3] * prm[3:4], axis=1, keepdims=True)) + lam_init)
    o0 = acc_ref[0, :DA_WIDTH, :] / acc_ref[0, DA_WIDTH:DA_WIDTH + 1, :]
    o1 = acc_ref[1, :DA_WIDTH, :] / acc_ref[1, DA_WIDTH:DA_WIDTH + 1, :]
    a = o0 - lam * o1
    ms = jnp.mean(a * a, axis=0, keepdims=True)
    out = a * lax.rsqrt(ms + LN_EPS) * g_ref[...] * (1.0 - lam_init)
    o_ref[...] = out.T.astype(o_ref.dtype)


def _diff_attention(att, vt, slopes, prm, subln_g, batch, seq):
    tq, tk = ATT_TQ, ATT_TK
    assert tq % tk == 0 and seq % tq == 0
    nq = seq // tq
    t = batch * seq
    kernel = functools.partial(_diff_attn_kernel, tq=tq, tk=tk)
    return pl.pallas_call(
        kernel,
        grid=(batch, DA_HEADS, nq),
        in_specs=[
            pl.BlockSpec(memory_space=pltpu.SMEM),
            pl.BlockSpec((8, LANES), lambda b, h, i: (0, 0)),
            pl.BlockSpec((DA_WIDTH, 1), lambda b, h, i: (0, 0)),
            pl.BlockSpec((tq, DA_WIDTH), lambda b, h, i: (b * nq + i, QA_OFF // DA_WIDTH + h)),
            pl.BlockSpec((seq, DA_WIDTH), lambda b, h, i: (b, KA_OFF // DA_WIDTH + h)),
            pl.BlockSpec((VT_ROWS, seq), lambda b, h, i: (h, b)),
        ],
        out_specs=pl.BlockSpec((tq, DA_WIDTH), lambda b, h, i: (b * nq + i, h)),
        out_shape=jax.ShapeDtypeStruct((t, D_MODEL), _BF16),
        scratch_shapes=[
            pltpu.VMEM((2, VT_ROWS, tq), _F32),
            pltpu.VMEM((2, 1, tq), _F32),
            pltpu.VMEM((2, tq, 2 * DA_HEAD_DIM), _BF16),
            pltpu.VMEM((2, tk, 2 * DA_HEAD_DIM), _BF16),
            pltpu.VMEM((2, tk, tq), _F32),
            pltpu.VMEM((2, 1, tq), _F32),
            pltpu.VMEM((2, tk, tq), _BF16),
            pltpu.VMEM((2, 1, tq), _F32),
        ],
        compiler_params=_params("parallel", "parallel", "arbitrary"),
        name="diff_attn",
    )(slopes, prm, subln_g, att, att, vt)


def _swa_kernel(sink_ref, bias_ref, q_ref, kv_ref, kvp_ref, vt_ref, vtp_ref, o_ref):
    i = pl.program_id(1)
    first = (i == 0).astype(jnp.int32)
    nt = (((1,), (1,)), ((), ()))
    kv_all = jnp.concatenate([kvp_ref[...], kv_ref[...]], axis=0)
    vt_all = jnp.concatenate([vtp_ref[...], vt_ref[...]], axis=1)
    for r in range(SWA_TB // WINDOW):
        heads = []
        for kh in range(SW_KV_HEADS):
            kwin = kv_all[r * WINDOW:(r + 2) * WINDOW, kh * SW_HEAD_DIM:(kh + 1) * SW_HEAD_DIM]
            vwin = vt_all[kh * VTB_ROWS:(kh + 1) * VTB_ROWS, r * WINDOW:(r + 2) * WINDOW]
            qs = jnp.concatenate(
                [q_ref[r * WINDOW:(r + 1) * WINDOW, (kh * SW_GROUP + g) * SW_HEAD_DIM:(kh * SW_GROUP + g + 1) * SW_HEAD_DIM]
                 for g in range(SW_GROUP)], axis=0)
            s = lax.dot_general(kwin, qs, nt, preferred_element_type=_F32)
            bias = bias_ref[2 * kh + first] if r == 0 else bias_ref[2 * kh]
            s = s + bias
            sink = sink_ref[kh]
            m = jnp.maximum(jnp.max(s, axis=0, keepdims=True), sink)
            p = jnp.exp2(s - m)
            ot = jnp.dot(vwin, p.astype(_BF16), preferred_element_type=_F32)
            denom = ot[SW_HEAD_DIM:SW_HEAD_DIM + 1, :] + jnp.exp2(sink - m)
            ot = ot[:SW_HEAD_DIM, :] / denom
            for g in range(SW_GROUP):
                heads.append(ot[:, g * WINDOW:(g + 1) * WINDOW].T)
        o_ref[r * WINDOW:(r + 1) * WINDOW, :] = jnp.concatenate(heads, axis=1).astype(o_ref.dtype)


def _swa_bias():
    slopes = 2.0 ** (-8.0 * jnp.arange(1, SW_HEADS + 1, dtype=_F32) / SW_HEADS)
    kj = jnp.arange(2 * WINDOW, dtype=jnp.int32)[:, None]
    qi = jnp.arange(WINDOW, dtype=jnp.int32)[None, :]
    dist = qi + WINDOW - kj
    valid = (dist >= 0) & (dist < WINDOW)
    tables = []
    for kh in range(SW_KV_HEADS):
        for has_prev in (True, False):
            ok = valid if has_prev else valid & (kj >= WINDOW)
            per_head = [jnp.where(ok, -slopes[kh * SW_GROUP + g] * LOG2E * dist.astype(_F32), NEG_INF)
                        for g in range(SW_GROUP)]
            tables.append(jnp.concatenate(per_head, axis=1))
    return jnp.stack(tables)


def _sliding_window_attention(att, vtb, sink_rows, bias, batch, seq):
    tb = SWA_TB
    nb = seq // tb
    ratio = tb // WINDOW
    t = batch * seq
    kv_width = 2 * SW_KV_HEADS * SW_HEAD_DIM
    n_b = SW_KV_HEADS * VTB_ROWS
    q_blk = QB_OFF // D_MODEL
    kv_blk = KVB_OFF // kv_width
    prev = lambda b, i: jnp.maximum((b * nb + i) * ratio - 1, 0)
    return pl.pallas_call(
        _swa_kernel,
        grid=(batch, nb),
        in_specs=[
            pl.BlockSpec((SW_KV_HEADS, 1, SW_GROUP * WINDOW), lambda b, i: (0, 0, 0)),
            pl.BlockSpec((2 * SW_KV_HEADS, 2 * WINDOW, SW_GROUP * WINDOW), lambda b, i: (0, 0, 0)),
            pl.BlockSpec((tb, D_MODEL), lambda b, i: (b * nb + i, q_blk)),
            pl.BlockSpec((tb, kv_width), lambda b, i: (b * nb + i, kv_blk)),
            pl.BlockSpec((WINDOW, kv_width), lambda b, i: (prev(b, i), kv_blk)),
            pl.BlockSpec((n_b, tb), lambda b, i: (0, b * nb + i)),
            pl.BlockSpec((n_b, WINDOW), lambda b, i: (0, prev(b, i))),
        ],
        out_specs=pl.BlockSpec((tb, D_MODEL), lambda b, i: (b * nb + i, 0)),
        out_shape=jax.ShapeDtypeStruct((t, D_MODEL), _BF16),
        compiler_params=_params("parallel", "arbitrary"),
        name="swa",
    )(sink_rows, bias, att, att, att, vtb, vtb)


def _pack_rows(x):
    bits = lax.bitcast_convert_type(x.astype(_BF16).astype(_F32), jnp.uint32)
    return lax.bitcast_convert_type(bits[:, :HALF_D] | (bits[:, HALF_D:] >> 16), jnp.int32)


def _unpack_rows(words):
    bits = lax.bitcast_convert_type(words, jnp.uint32)
    hi = lax.bitcast_convert_type(bits & jnp.uint32(0xFFFF0000), _F32)
    lo = lax.bitcast_convert_type(bits << 16, _F32)
    return hi, lo


def _gather_rows(table, idx):
    n_idx = idx.shape[0]
    width = table.shape[1]
    workers = SC_CORES * SC_SUBCORES
    assert n_idx % (workers * 2 * GATHER_CHUNK) == 0
    per_worker = n_idx // workers
    n_chunks = per_worker // GATHER_CHUNK
    mesh = plsc.VectorSubcoreMesh(core_axis_name="c", subcore_axis_name="s",
                                  num_cores=SC_CORES, num_subcores=SC_SUBCORES)

    @functools.partial(
        pl.kernel, mesh=mesh,
        out_type=jax.ShapeDtypeStruct((n_idx, width), table.dtype),
        scratch_types=[
            pltpu.VMEM((per_worker,), jnp.int32),
            pltpu.VMEM((2, GATHER_CHUNK, width), table.dtype),
            pltpu.SemaphoreType.DMA((2,)),
            pltpu.SemaphoreType.DMA((2,)),
        ],
        name="gather_rows",
    )
    def gather(table_hbm, idx_hbm, out_hbm, idx_v, rows_v, fetch_sem, store_sem):
        base = (lax.axis_index("s") * SC_CORES + lax.axis_index("c")) * per_worker
        pltpu.sync_copy(idx_hbm.at[pl.ds(base, per_worker)], idx_v)

        def fetch(g, buf):
            chunk = idx_v.at[pl.ds(g * GATHER_CHUNK, GATHER_CHUNK)]
            return pltpu.make_async_copy(table_hbm.at[chunk], rows_v.at[buf], fetch_sem.at[buf])

        def store(g, buf):
            dst = out_hbm.at[pl.ds(base + g * GATHER_CHUNK, GATHER_CHUNK)]
            return pltpu.make_async_copy(rows_v.at[buf], dst, store_sem.at[buf])

        fetch(0, 0).start()

        @pl.loop(0, n_chunks, step=2)
        def _(g0):
            for buf in range(2):
                g = g0 + buf
                fetch(g, buf).wait()

                @pl.when(g >= 1)
                def _():
                    store(g - 1, 1 - buf).wait()

                @pl.when(g + 1 < n_chunks)
                def _():
                    fetch(g + 1, 1 - buf).start()

                store(g, buf).start()

        store(n_chunks - 1, 1).wait()

    return gather(table, idx)


def _layer_norm(z, g, b):
    mu = jnp.mean(z, axis=1, keepdims=True)
    d = z - mu
    var = jnp.mean(d * d, axis=1, keepdims=True)
    return d * lax.rsqrt(var + LN_EPS) * g + b


def _merge_out_kernel(ga_ref, gb_ref, oa_ref, ob_ref, x_ref, wo_ref, lng_ref, lnb_ref, wr_ref, br_ref,
                      h_ref, hp_ref, idx_ref, gate_ref, cnt_ref, run_ref):
    @pl.when(pl.program_id(0) == 0)
    def _():
        run_ref[...] = jnp.zeros(run_ref.shape, _F32)

    merged = (jax.nn.sigmoid(ga_ref[...].astype(_F32)) * oa_ref[...].astype(_F32)
              + jax.nn.sigmoid(gb_ref[...].astype(_F32)) * ob_ref[...].astype(_F32))
    y = jnp.dot(merged.astype(_BF16), wo_ref[0], preferred_element_type=_F32)
    hn = _layer_norm(DEEPNORM_ALPHA * x_ref[...] + y, lng_ref[0, 0:1, :], lnb_ref[0, 0:1, :])
    h_ref[...] = hn
    hb = hn.astype(_BF16)
    hp_ref[...] = _pack_rows(hn)
    logits = lax.dot_general(wr_ref[0], hb, (((1,), (1,)), ((), ())),
                             preferred_element_type=_F32) + br_ref[0]
    tm = logits.shape[1]
    erow = lax.broadcasted_iota(jnp.int32, (N_EXPERTS, tm), 0)
    cur = logits
    vals, picks, id_rows = [], [], []
    for k in range(TOP_K):
        mx = jnp.max(cur, axis=0, keepdims=True)
        ix = jnp.min(jnp.where(cur == mx, erow, N_EXPERTS), axis=0, keepdims=True)
        vals.append(mx)
        picks.append(erow == ix)
        id_rows.append(ix)
        cur = jnp.where(picks[k], -jnp.inf, cur)
    exps = [jnp.exp(v - vals[0]) for v in vals]
    tot = exps[0] + exps[1] + exps[2] + exps[3]
    gate_rows = [e / tot for e in exps] + [jnp.zeros((LANES - TOP_K, tm), _F32)]
    gate_ref[...] = jnp.concatenate(gate_rows, axis=0).T

    chosen = jnp.zeros((N_EXPERTS, tm), _F32)
    for k in range(TOP_K):
        chosen = chosen + jnp.where(picks[k], 1.0, 0.0)
    r = lax.broadcasted_iota(jnp.int32, (tm, tm), 0)
    c = lax.broadcasted_iota(jnp.int32, (tm, tm), 1)
    earlier = jnp.where(r < c, 1.0, 0.0).astype(_BF16)
    before = jnp.dot(chosen.astype(_BF16), earlier, preferred_element_type=_F32) + run_ref[...]
    rank_rows = [jnp.sum(jnp.where(picks[k], before, 0.0), axis=0, keepdims=True).astype(jnp.int32)
                 for k in range(TOP_K)]
    idx_ref[...] = jnp.concatenate(id_rows + rank_rows, axis=0)
    total = run_ref[...] + jnp.sum(chosen, axis=1, keepdims=True)
    run_ref[...] = total
    cnt_ref[...] = jnp.broadcast_to(total, cnt_ref.shape).astype(jnp.int32)


def _merge_out(gates, o_a, o_b, x, wo_all, ln_g, ln_b, wr_all, br_all, layer):
    t = x.shape[0]
    tm = OUT_TM
    row = lambda i: (i, 0)
    return pl.pallas_call(
        _merge_out_kernel,
        grid=(t // tm,),
        in_specs=[
            pl.BlockSpec((tm, D_MODEL), lambda i: (i, 0)),
            pl.BlockSpec((tm, D_MODEL), lambda i: (i, 1)),
            pl.BlockSpec((tm, D_MODEL), row),
            pl.BlockSpec((tm, D_MODEL), row),
            pl.BlockSpec((tm, D_MODEL), row),
            pl.BlockSpec((1, D_MODEL, D_MODEL), lambda i: (layer, 0, 0)),
            pl.BlockSpec((1, 2, D_MODEL), lambda i: (layer, 0, 0)),
            pl.BlockSpec((1, 2, D_MODEL), lambda i: (layer, 0, 0)),
            pl.BlockSpec((1, N_EXPERTS, D_MODEL), lambda i: (layer, 0, 0)),
            pl.BlockSpec((1, N_EXPERTS, 1), lambda i: (layer, 0, 0)),
        ],
        out_specs=[
            pl.BlockSpec((tm, D_MODEL), row),
            pl.BlockSpec((tm, HALF_D), row),
            pl.BlockSpec((2 * TOP_K, tm), lambda i: (0, i)),
            pl.BlockSpec((tm, LANES), row),
            pl.BlockSpec((N_EXPERTS, LANES), lambda i: (0, 0)),
        ],
        out_shape=[
            jax.ShapeDtypeStruct((t, D_MODEL), _F32),
            jax.ShapeDtypeStruct((t, HALF_D), jnp.int32),
            jax.ShapeDtypeStruct((2 * TOP_K, t), jnp.int32),
            jax.ShapeDtypeStruct((t, LANES), _F32),
            jax.ShapeDtypeStruct((N_EXPERTS, LANES), jnp.int32),
        ],
        scratch_shapes=[pltpu.VMEM((N_EXPERTS, 1), _F32)],
        compiler_params=_params("arbitrary"),
        name="merge_out_ln_router",
    )(gates, gates, o_a, o_b, x, wo_all, ln_g, ln_b, wr_all, br_all)


def _expert_kernel(be_ref, nused_ref, x_ref, wup_ref, bg_ref, bu_ref, wd_ref, bd_ref, *rest, block_off):
    y_ref, wg_scr, wu_scr, wd_scr = rest[-4:]
    i = pl.program_id(0)
    blk = i + block_off
    changed = jnp.logical_or(i == 0, be_ref[blk] != be_ref[jnp.maximum(blk - 1, 0)])
    active = blk < nused_ref[0]

    @pl.when(jnp.logical_and(changed, active))
    def _():
        half = DEINT_TILE // 2
        r = lax.broadcasted_iota(jnp.int32, (DEINT_TILE, DEINT_TILE), 0)
        c = lax.broadcasted_iota(jnp.int32, (DEINT_TILE, DEINT_TILE), 1)
        sel = jnp.where(r == jnp.where(c < half, 2 * c, 2 * (c - half) + 1), 1.0, 0.0).astype(_BF16)
        for t in range(2 * D_FF // DEINT_TILE):
            w = wup_ref[0, 0, :, t * DEINT_TILE:(t + 1) * DEINT_TILE].astype(_BF16)
            de = jnp.dot(w, sel, preferred_element_type=_F32).astype(_BF16)
            wg_scr[:, t * half:(t + 1) * half] = de[:, :half]
            wu_scr[:, t * half:(t + 1) * half] = de[:, half:]
        wd_scr[...] = wd_ref[0, 0].astype(_BF16)

    @pl.when(active)
    def _():
        hi, lo = _unpack_rows(x_ref[...])
        x = jnp.concatenate([hi.astype(_BF16), lo.astype(_BF16)], axis=1)
        gate = jnp.dot(x, wg_scr[...], preferred_element_type=_F32) + bg_ref[0, 0]
        up = jnp.dot(x, wu_scr[...], preferred_element_type=_F32) + bu_ref[0, 0]
        gate = jnp.minimum(gate, SWIGLU_LIMIT)
        up = jnp.clip(up, -SWIGLU_LIMIT, SWIGLU_LIMIT)
        act = (up + 1.0) * (gate * jax.nn.sigmoid(SWIGLU_ALPHA * gate))
        y = jnp.dot(act.astype(_BF16), wd_scr[...], preferred_element_type=_F32) + bd_ref[0, 0]
        y_ref[...] = _pack_rows(y)

    @pl.when(jnp.logical_not(active))
    def _():
        y_ref[...] = jnp.zeros(y_ref.shape, y_ref.dtype)


def _experts(block_expert, n_used, x_part, part, y_prev, w_up, bg_all, bu_all, w_down, bd_all, layer):
    part_blocks = x_part.shape[0] // MOE_BLOCK
    n_rows = block_expert.shape[0] * MOE_BLOCK
    off = part * part_blocks
    pick = lambda i, be, nu: (layer, be[i + off], 0, 0)
    bspec = pl.BlockSpec((1, 1, 1, D_FF), pick)
    in_specs = [
        pl.BlockSpec((MOE_BLOCK, HALF_D), lambda i, be, nu: (i, 0)),
        pl.BlockSpec((1, 1, D_MODEL, 2 * D_FF), pick),
        bspec, bspec,
        pl.BlockSpec((1, 1, D_FF, D_MODEL), pick),
        pl.BlockSpec((1, 1, 1, D_MODEL), pick),
    ]
    operands = [block_expert, n_used, x_part, w_up, bg_all, bu_all, w_down, bd_all]
    aliases = {}
    if y_prev is not None:
        in_specs.append(pl.BlockSpec(memory_space=pl.ANY))
        aliases = {len(operands): 0}
        operands.append(y_prev)
    grid_spec = pltpu.PrefetchScalarGridSpec(
        num_scalar_prefetch=2,
        grid=(part_blocks,),
        in_specs=in_specs,
        out_specs=pl.BlockSpec((MOE_BLOCK, HALF_D), lambda i, be, nu: (i + off, 0)),
        scratch_shapes=[
            pltpu.VMEM((D_MODEL, D_FF), _BF16),
            pltpu.VMEM((D_MODEL, D_FF), _BF16),
            pltpu.VMEM((D_FF, D_MODEL), _BF16),
        ],
    )
    return pl.pallas_call(
        functools.partial(_expert_kernel, block_off=off),
        grid_spec=grid_spec,
        out_shape=jax.ShapeDtypeStruct((n_rows, HALF_D), jnp.int32),
        input_output_aliases=aliases,
        compiler_params=_params("arbitrary"),
        name="experts",
    )(*operands)


def _combine_kernel(h_ref, yg_ref, gate_ref, lng_ref, lnb_ref, *rest):
    x_ref, xb_ref = rest[-2:]
    gates = gate_ref[...]
    y = None
    for k in range(TOP_K):
        hi, lo = _unpack_rows(yg_ref[k])
        yk = gates[:, k:k + 1] * jnp.concatenate([hi, lo], axis=1)
        y = yk if y is None else y + yk
    xn = _layer_norm(DEEPNORM_ALPHA * h_ref[...] + y, lng_ref[0, 1:2, :], lnb_ref[0, 1:2, :])
    x_ref[...] = xn
    xb_ref[...] = xn.astype(_BF16)


def _combine(h, yg_part, gates, ln_g, ln_b, layer, part, prev):
    t = h.shape[0]
    tm = LN2_TM
    off = part * (yg_part.shape[1] // tm)
    row = lambda i: (i + off, 0)
    in_specs = [
        pl.BlockSpec((tm, D_MODEL), row),
        pl.BlockSpec((TOP_K, tm, HALF_D), lambda i: (0, i, 0)),
        pl.BlockSpec((tm, LANES), row),
        pl.BlockSpec((1, 2, D_MODEL), lambda i: (layer, 0, 0)),
        pl.BlockSpec((1, 2, D_MODEL), lambda i: (layer, 0, 0)),
    ]
    operands = [h, yg_part, gates, ln_g, ln_b]
    aliases = {}
    if prev is not None:
        in_specs += [pl.BlockSpec(memory_space=pl.ANY), pl.BlockSpec(memory_space=pl.ANY)]
        aliases = {len(operands): 0, len(operands) + 1: 1}
        operands += list(prev)
    return pl.pallas_call(
        _combine_kernel,
        grid=(yg_part.shape[1] // tm,),
        in_specs=in_specs,
        out_specs=[pl.BlockSpec((tm, D_MODEL), row), pl.BlockSpec((tm, D_MODEL), row)],
        out_shape=[jax.ShapeDtypeStruct((t, D_MODEL), _F32), jax.ShapeDtypeStruct((t, D_MODEL), _BF16)],
        input_output_aliases=aliases,
        compiler_params=_params("parallel"),
        name="combine_ln",
    )(*operands)


def _dispatch_tables(route, counts, t):
    tk = t * TOP_K
    experts = route[:TOP_K]
    ranks = route[TOP_K:]
    padded = (counts + MOE_BLOCK - 1) // MOE_BLOCK * MOE_BLOCK
    pad_end = jnp.cumsum(padded)
    pad_start = pad_end - padded
    start = jnp.cumsum(counts) - counts
    expert_ids = jnp.arange(N_EXPERTS, dtype=jnp.int32)
    dest = jnp.sum(jnp.where(experts[..., None] == expert_ids, pad_start, 0), axis=-1) + ranks
    n_blocks = (tk + N_EXPERTS * (MOE_BLOCK - 1) + MOE_BLOCK - 1) // MOE_BLOCK
    block_start = jnp.arange(n_blocks, dtype=jnp.int32) * MOE_BLOCK
    block_expert = jnp.minimum(
        jnp.sum((pad_end[None, :] <= block_start[:, None]).astype(jnp.int32), axis=1), N_EXPERTS - 1)
    n_used = (pad_end[-1:] // MOE_BLOCK).astype(jnp.int32)
    order = jnp.argsort(experts.T.reshape(tk))
    in_block = jnp.arange(MOE_BLOCK, dtype=jnp.int32)[None, :]
    r = (block_start - pad_start[block_expert])[:, None] + in_block
    valid = r < counts[block_expert][:, None]
    src = jnp.where(valid, start[block_expert][:, None] + r, 0).reshape(n_blocks * MOE_BLOCK)
    filler = jnp.arange(n_blocks * MOE_BLOCK, dtype=jnp.int32) % t
    buf_tok = jnp.where(valid.reshape(-1), order[src] // TOP_K, filler)
    return buf_tok, dest, block_expert, n_used


def kernel(x, w_in, w_o, lambda_qk, subln_g, sinks, ln_g, ln_b, w_router, b_router, w_up, b_up, w_down, b_down):
    batch, seq, _ = x.shape
    t = batch * seq
    assert seq % ATT_TQ == 0 and seq % SWA_TB == 0 and t % PROJ_TM == 0

    va_lo, va_hi = 2 * D_MODEL, 3 * D_MODEL
    gates_lo = w_in.shape[-1] - GATE_COLS
    w_att_b = jnp.concatenate([w_in[..., :va_lo], w_in[..., va_hi:gates_lo]], axis=-1).astype(_BF16)
    vb_lo = gates_lo - SW_KV_HEADS * SW_HEAD_DIM
    w_vt_b = jnp.swapaxes(jnp.concatenate([w_in[..., va_lo:va_hi], w_in[..., vb_lo:gates_lo]], axis=-1),
                          1, 2).astype(_BF16)
    w_gates_b = w_in[..., gates_lo:].astype(_BF16)
    w_o_b = w_o.astype(_BF16)
    w_r_b = jnp.swapaxes(w_router, 1, 2).astype(_BF16)
    b_gate = b_up[..., 0::2].reshape(DEPTH, N_EXPERTS, 1, D_FF)
    b_upp = b_up[..., 1::2].reshape(DEPTH, N_EXPERTS, 1, D_FF)
    b_down4 = b_down.reshape(DEPTH, N_EXPERTS, 1, D_MODEL)
    b_router3 = b_router.astype(_F32).reshape(DEPTH, N_EXPERTS, 1)

    att_scale = jnp.concatenate([
        jnp.full((D_MODEL,), DA_HEAD_DIM ** -0.5 * LOG2E, _F32), jnp.ones((D_MODEL,), _F32),
        jnp.full((D_MODEL,), SW_HEAD_DIM ** -0.5 * LOG2E, _F32),
        jnp.ones((2 * SW_KV_HEADS * SW_HEAD_DIM,), _F32)]).reshape(1, ATT_COLS)
    gate_scale = jnp.ones((1, GATE_COLS), _F32)
    rest = 2.0 ** (-8.0 * jnp.arange(1, DA_HEADS + 1, dtype=_F32) / DA_HEADS) * LOG2E
    pieces = []
    for _ in range(N_SLOPE_PARTS):
        piece = rest.astype(_BF16).astype(_F32)
        pieces.append(piece)
        rest = rest - piece
    da_slopes = jnp.stack(pieces, axis=1)
    swa_bias = _swa_bias()

    xf = x.reshape(t, D_MODEL)
    xb = xf.astype(_BF16)
    for l in range(DEPTH):
        lam_init = 0.8 - 0.6 * math.exp(-0.3 * l)
        prm = jnp.concatenate([lambda_qk[l].astype(_F32), jnp.full((4, DA_HEAD_DIM), lam_init, _F32)], axis=0)
        att = _project(xb, w_att_b, l, ATT_COLS // 2, att_scale, _BF16)
        gates = _project(xb, w_gates_b, l, GATE_COLS // 2, gate_scale, _BF16)
        vt, vtb = _project_vt(xb, w_vt_b, l)
        o_a = _diff_attention(att, vt, da_slopes, prm, subln_g[l].reshape(DA_WIDTH, 1), batch, seq)
        sink_rows = jnp.repeat(sinks[l].astype(_F32).reshape(SW_KV_HEADS, SW_GROUP) * LOG2E, WINDOW, axis=1)
        o_b = _sliding_window_attention(att, vtb, sink_rows.reshape(SW_KV_HEADS, 1, SW_GROUP * WINDOW),
                                        swa_bias, batch, seq)
        h, hp, route, top_gate, counts = _merge_out(gates, o_a, o_b, xf, w_o_b, ln_g, ln_b, w_r_b, b_router3, l)
        buf_tok, pos, block_expert, n_used = _dispatch_tables(route, counts[:, 0], t)
        part_rows = buf_tok.shape[0] // MOE_PARTS
        yb = None
        for part in range(MOE_PARTS):
            x_rows = _gather_rows(hp, buf_tok[part * part_rows:(part + 1) * part_rows])
            yb = _experts(block_expert, n_used, x_rows, part, yb, w_up, b_gate, b_upp, w_down, b_down4, l)
        part_tok = t // MOE_PARTS
        outs = None
        for part in range(MOE_PARTS):
            rows = pos[:, part * part_tok:(part + 1) * part_tok].reshape(TOP_K * part_tok)
            yg = _gather_rows(yb, rows).reshape(TOP_K, part_tok, HALF_D)
            outs = _combine(h, yg, top_gate, ln_g, ln_b, l, part, outs)
        xf, xb = outs
    return xf.reshape(batch, seq, D_MODEL)
```

```python
import functools
import math

import jax
import jax.numpy as jnp
from jax import lax
from jax.experimental import pallas as pl
from jax.experimental.pallas import tpu as pltpu
from jax.experimental.pallas import tpu_sc as plsc

D_MODEL = 1024
DEPTH = 4
DA_HEAD_DIM = 128
DA_HEADS = 4
DA_WIDTH = 2 * DA_HEAD_DIM
SW_HEAD_DIM = 64
SW_HEADS = 16
SW_KV_HEADS = 2
SW_GROUP = SW_HEADS // SW_KV_HEADS
WINDOW = 128
N_EXPERTS = 32
TOP_K = 4
D_FF = D_MODEL
SWIGLU_LIMIT = 7.0
SWIGLU_ALPHA = 1.702
LN_EPS = 1e-5
DEEPNORM_ALPHA = (2.0 * DEPTH) ** 0.25
NEG_INF = -1e30

QA_OFF, KA_OFF, QB_OFF, KVB_OFF = 0, 1024, 2048, 3072
ATT_COLS = 3328
GATE_COLS = 2 * D_MODEL
LANES = 128
HALF_D = D_MODEL // 2
SC_CORES = 2
SC_SUBCORES = 16
GATHER_CHUNKS = (64, 32, 16)
SUBLANES_BF16 = 16
VT_ROWS = DA_WIDTH + SUBLANES_BF16
VTB_ROWS = SW_HEAD_DIM + SUBLANES_BF16
LOG2E = math.log2(math.e)
N_SLOPE_PARTS = 3

PROJ_TM = 1024
ATT_TQ = 1024
ATT_TK = 512
SWA_TB = 256
OUT_TM = 256
MOE_BLOCK = 256
MOE_PARTS = 4
DEINT_TILE = 256
LN2_TM = 256
VMEM_LIMIT = 56 * 1024 * 1024

_F32 = jnp.float32
_BF16 = jnp.bfloat16


def _params(*sem):
    return pltpu.CompilerParams(dimension_semantics=sem, vmem_limit_bytes=VMEM_LIMIT)


def _proj_kernel(x_ref, w_ref, s_ref, o_ref):
    acc = jnp.dot(x_ref[...], w_ref[0], preferred_element_type=_F32)
    o_ref[...] = (acc * s_ref[...]).astype(o_ref.dtype)


def _project(xb, w_all, layer, tn, scale, out_dtype):
    t = xb.shape[0]
    n_cols = w_all.shape[-1]
    assert n_cols % tn == 0 and t % PROJ_TM == 0
    return pl.pallas_call(
        _proj_kernel,
        grid=(n_cols // tn, t // PROJ_TM),
        in_specs=[
            pl.BlockSpec((PROJ_TM, D_MODEL), lambda j, i: (i, 0)),
            pl.BlockSpec((1, D_MODEL, tn), lambda j, i: (layer, 0, j)),
            pl.BlockSpec((1, tn), lambda j, i: (0, j)),
        ],
        out_specs=pl.BlockSpec((PROJ_TM, tn), lambda j, i: (i, j)),
        out_shape=jax.ShapeDtypeStruct((t, n_cols), out_dtype),
        compiler_params=_params("parallel", "parallel"),
        name="in_proj",
    )(xb, w_all, scale)


def _proj_vt_kernel(x_ref, wt_ref, oa_ref, ob_ref):
    x = x_ref[...]
    nt = (((1,), (1,)), ((), ()))
    ones = jnp.ones((SUBLANES_BF16, x.shape[0]), oa_ref.dtype)
    for h in range(DA_HEADS):
        vt = lax.dot_general(wt_ref[0, h * DA_WIDTH:(h + 1) * DA_WIDTH, :], x, nt, preferred_element_type=_F32)
        oa_ref[h * VT_ROWS:h * VT_ROWS + DA_WIDTH, :] = vt.astype(oa_ref.dtype)
        oa_ref[h * VT_ROWS + DA_WIDTH:(h + 1) * VT_ROWS, :] = ones
    vb = lax.dot_general(wt_ref[0, D_MODEL:, :], x, nt, preferred_element_type=_F32)
    for kh in range(SW_KV_HEADS):
        ob_ref[kh * VTB_ROWS:kh * VTB_ROWS + SW_HEAD_DIM, :] = (
            vb[kh * SW_HEAD_DIM:(kh + 1) * SW_HEAD_DIM, :].astype(ob_ref.dtype))
        ob_ref[kh * VTB_ROWS + SW_HEAD_DIM:(kh + 1) * VTB_ROWS, :] = ones


def _project_vt(xb, wt_all, layer):
    t = xb.shape[0]
    n_b = SW_KV_HEADS * SW_HEAD_DIM
    return pl.pallas_call(
        _proj_vt_kernel,
        grid=(t // PROJ_TM,),
        in_specs=[
            pl.BlockSpec((PROJ_TM, D_MODEL), lambda i: (i, 0)),
            pl.BlockSpec((1, D_MODEL + n_b, D_MODEL), lambda i: (layer, 0, 0)),
        ],
        out_specs=[pl.BlockSpec((DA_HEADS * VT_ROWS, PROJ_TM), lambda i: (0, i)),
                   pl.BlockSpec((SW_KV_HEADS * VTB_ROWS, PROJ_TM), lambda i: (0, i))],
        out_shape=[jax.ShapeDtypeStruct((DA_HEADS * VT_ROWS, t), _BF16),
                   jax.ShapeDtypeStruct((SW_KV_HEADS * VTB_ROWS, t), _BF16)],
        compiler_params=_params("parallel"),
        name="in_proj_vt",
    )(xb, wt_all)


def _diff_attn_kernel(slope_ref, prm_ref, g_ref, q_ref, k_ref, vt_ref, o_ref,
                      acc_ref, m_ref, qaug_ref, kaug_ref, s_ref, smax_ref, p_ref, alpha_ref, *, tq, tk):
    h = pl.program_id(1)
    i = pl.program_id(2)
    diag = tq // tk
    n_full = i * diag
    lane = lax.broadcasted_iota(jnp.int32, (tq, DA_HEAD_DIM), 1)
    slope_tile = jnp.zeros((tq, DA_HEAD_DIM), _F32)
    slope = 0.0
    for part in range(N_SLOPE_PARTS):
        piece = slope_ref[h, part]
        slope = slope + piece
        slope_tile = jnp.where(lane // 2 == part, piece, slope_tile)
    klane = lax.broadcasted_iota(jnp.int32, (tk, DA_HEAD_DIM), 1)
    kpos = lax.broadcasted_iota(jnp.int32, (tk, DA_HEAD_DIM), 0)
    pos_tile = jnp.where(klane < 2 * N_SLOPE_PARTS, jnp.where(klane % 2 == 0, kpos % 256, kpos // 256 * 256), 0)
    for c in range(2):
        kaug_ref[c, :, DA_HEAD_DIM:] = pos_tile.astype(_F32).astype(_BF16)
        qaug_ref[c, :, :DA_HEAD_DIM] = q_ref[:, c * DA_HEAD_DIM:(c + 1) * DA_HEAD_DIM]
        qaug_ref[c, :, DA_HEAD_DIM:] = slope_tile.astype(_BF16)
    m_ref[...] = jnp.full(m_ref.shape, NEG_INF, _F32)
    acc_ref[...] = jnp.zeros(acc_ref.shape, _F32)

    def scores(j):
        start = pl.multiple_of(j * tk, tk)
        for c in range(2):
            kaug_ref[c, :, :DA_HEAD_DIM] = k_ref[pl.ds(start, tk), c * DA_HEAD_DIM:(c + 1) * DA_HEAD_DIM]
            s = lax.dot_general(kaug_ref[c], qaug_ref[c], (((1,), (1,)), ((), ())),
                                preferred_element_type=_F32)
            s_ref[c] = s
            smax_ref[c] = jnp.max(s, axis=0, keepdims=True)

    def weights(j, boundary):
        off = (j * tk - i * tq).astype(_F32) * slope
        for c in range(2):
            s = s_ref[c]
            if boundary is None:
                smax = smax_ref[c]
            else:
                keys = lax.broadcasted_iota(jnp.int32, (tk, tq), 0) + boundary * tk
                queries = lax.broadcasted_iota(jnp.int32, (tk, tq), 1)
                s = jnp.where(keys <= queries, s, NEG_INF)
                smax = jnp.max(s, axis=0, keepdims=True)
            m_old = m_ref[c]
            m_new = jnp.maximum(m_old, smax + off)
            alpha_ref[c] = jnp.exp2(m_old - m_new)
            p_ref[c] = jnp.exp2(s - (m_new - off)).astype(_BF16)
            m_ref[c] = m_new

    def accumulate(j):
        start = pl.multiple_of(j * tk, tk)
        vt = vt_ref[:, pl.ds(start, tk)]
        for c in range(2):
            acc_ref[c] = alpha_ref[c] * acc_ref[c] + jnp.dot(vt, p_ref[c], preferred_element_type=_F32)

    def body(j, carry):
        accumulate(j)
        weights(j + 1, None)
        scores(j + 2)
        return carry

    def enter_boundary():
        weights(n_full, 0)
        if diag > 1:
            scores(n_full + 1)

    scores(0)

    @pl.when(i >= 1)
    def _():
        weights(0, None)
        scores(1)
        lax.fori_loop(0, n_full - 1, body, 0)
        accumulate(n_full - 1)
        enter_boundary()

    @pl.when(i == 0)
    def _():
        enter_boundary()

    for d in range(diag):
        accumulate(n_full + d)
        if d + 1 < diag:
            weights(n_full + d + 1, d + 1)
            if d + 2 < diag:
                scores(n_full + d + 2)

    prm = prm_ref[...]
    lam_init = prm[4:5, 0:1]
    lam = (jnp.exp(jnp.sum(prm[0:1] * prm[1:2], axis=1, keepdims=True))
           - jnp.exp(jnp.sum(prm[2:3] * prm[3:4], axis=1, keepdims=True)) + lam_init)
    o0 = acc_ref[0, :DA_WIDTH, :] / acc_ref[0, DA_WIDTH:DA_WIDTH + 1, :]
    o1 = acc_ref[1, :DA_WIDTH, :] / acc_ref[1, DA_WIDTH:DA_WIDTH + 1, :]
    a = o0 - lam * o1
    ms = jnp.mean(a * a, axis=0, keepdims=True)
    out = a * lax.rsqrt(ms + LN_EPS) * g_ref[...] * (1.0 - lam_init)
    o_ref[...] = out.T.astype(o_ref.dtype)


def _diff_attention(att, vt, slopes, prm, subln_g, batch, seq):
    tq, tk = ATT_TQ, ATT_TK
    assert tq % tk == 0 and seq % tq == 0
    nq = seq // tq
    t = batch * seq
    kernel = functools.partial(_diff_attn_kernel, tq=tq, tk=tk)
    return pl.pallas_call(
        kernel,
        grid=(batch, DA_HEADS, nq),
        in_specs=[
            pl.BlockSpec(memory_space=pltpu.SMEM),
            pl.BlockSpec((8, LANES), lambda b, h, i: (0, 0)),
            pl.BlockSpec((DA_WIDTH, 1), lambda b, h, i: (0, 0)),
            pl.BlockSpec((tq, DA_WIDTH), lambda b, h, i: (b * nq + i, QA_OFF // DA_WIDTH + h)),
            pl.BlockSpec((seq, DA_WIDTH), lambda b, h, i: (b, KA_OFF // DA_WIDTH + h)),
            pl.BlockSpec((VT_ROWS, seq), lambda b, h, i: (h, b)),
        ],
        out_specs=pl.BlockSpec((tq, DA_WIDTH), lambda b, h, i: (b * nq + i, h)),
        out_shape=jax.ShapeDtypeStruct((t, D_MODEL), _BF16),
        scratch_shapes=[
            pltpu.VMEM((2, VT_ROWS, tq), _F32),
            pltpu.VMEM((2, 1, tq), _F32),
            pltpu.VMEM((2, tq, 2 * DA_HEAD_DIM), _BF16),
            pltpu.VMEM((2, tk, 2 * DA_HEAD_DIM), _BF16),
            pltpu.VMEM((2, tk, tq), _F32),
            pltpu.VMEM((2, 1, tq), _F32),
            pltpu.VMEM((2, tk, tq), _BF16),
            pltpu.VMEM((2, 1, tq), _F32),
        ],
        compiler_params=_params("parallel", "parallel", "arbitrary"),
        name="diff_attn",
    )(slopes, prm, subln_g, att, att, vt)


def _swa_kernel(sink_ref, bias_ref, q_ref, kv_ref, kvp_ref, vt_ref, vtp_ref, o_ref):
    i = pl.program_id(1)
    first = (i == 0).astype(jnp.int32)
    nt = (((1,), (1,)), ((), ()))
    kv_all = jnp.concatenate([kvp_ref[...], kv_ref[...]], axis=0)
    vt_all = jnp.concatenate([vtp_ref[...], vt_ref[...]], axis=1)
    for r in range(SWA_TB // WINDOW):
        heads = []
        for kh in range(SW_KV_HEADS):
            kwin = kv_all[r * WINDOW:(r + 2) * WINDOW, kh * SW_HEAD_DIM:(kh + 1) * SW_HEAD_DIM]
            vwin = vt_all[kh * VTB_ROWS:(kh + 1) * VTB_ROWS, r * WINDOW:(r + 2) * WINDOW]
            qs = jnp.concatenate(
                [q_ref[r * WINDOW:(r + 1) * WINDOW, (kh * SW_GROUP + g) * SW_HEAD_DIM:(kh * SW_GROUP + g + 1) * SW_HEAD_DIM]
                 for g in range(SW_GROUP)], axis=0)
            s = lax.dot_general(kwin, qs, nt, preferred_element_type=_F32)
            bias = bias_ref[2 * kh + first] if r == 0 else bias_ref[2 * kh]
            s = s + bias
            sink = sink_ref[kh]
            m = jnp.maximum(jnp.max(s, axis=0, keepdims=True), sink)
            p = jnp.exp2(s - m)
            ot = jnp.dot(vwin, p.astype(_BF16), preferred_element_type=_F32)
            denom = ot[SW_HEAD_DIM:SW_HEAD_DIM + 1, :] + jnp.exp2(sink - m)
            ot = ot[:SW_HEAD_DIM, :] / denom
            for g in range(SW_GROUP):
                heads.append(ot[:, g * WINDOW:(g + 1) * WINDOW].T)
        o_ref[r * WINDOW:(r + 1) * WINDOW, :] = jnp.concatenate(heads, axis=1).astype(o_ref.dtype)


def _swa_bias():
    slopes = 2.0 ** (-8.0 * jnp.arange(1, SW_HEADS + 1, dtype=_F32) / SW_HEADS)
    kj = jnp.arange(2 * WINDOW, dtype=jnp.int32)[:, None]
    qi = jnp.arange(WINDOW, dtype=jnp.int32)[None, :]
    dist = qi + WINDOW - kj
    valid = (dist >= 0) & (dist < WINDOW)
    tables = []
    for kh in range(SW_KV_HEADS):
        for has_prev in (True, False):
            ok = valid if has_prev else valid & (kj >= WINDOW)
            per_head = [jnp.where(ok, -slopes[kh * SW_GROUP + g] * LOG2E * dist.astype(_F32), NEG_INF)
                        for g in range(SW_GROUP)]
            tables.append(jnp.concatenate(per_head, axis=1))
    return jnp.stack(tables)


def _sliding_window_attention(att, vtb, sink_rows, bias, batch, seq):
    tb = SWA_TB
    nb = seq // tb
    ratio = tb // WINDOW
    t = batch * seq
    kv_width = 2 * SW_KV_HEADS * SW_HEAD_DIM
    n_b = SW_KV_HEADS * VTB_ROWS
    q_blk = QB_OFF // D_MODEL
    kv_blk = KVB_OFF // kv_width
    prev = lambda b, i: jnp.maximum((b * nb + i) * ratio - 1, 0)
    return pl.pallas_call(
        _swa_kernel,
        grid=(batch, nb),
        in_specs=[
            pl.BlockSpec((SW_KV_HEADS, 1, SW_GROUP * WINDOW), lambda b, i: (0, 0, 0)),
            pl.BlockSpec((2 * SW_KV_HEADS, 2 * WINDOW, SW_GROUP * WINDOW), lambda b, i: (0, 0, 0)),
            pl.BlockSpec((tb, D_MODEL), lambda b, i: (b * nb + i, q_blk)),
            pl.BlockSpec((tb, kv_width), lambda b, i: (b * nb + i, kv_blk)),
            pl.BlockSpec((WINDOW, kv_width), lambda b, i: (prev(b, i), kv_blk)),
            pl.BlockSpec((n_b, tb), lambda b, i: (0, b * nb + i)),
            pl.BlockSpec((n_b, WINDOW), lambda b, i: (0, prev(b, i))),
        ],
        out_specs=pl.BlockSpec((tb, D_MODEL), lambda b, i: (b * nb + i, 0)),
        out_shape=jax.ShapeDtypeStruct((t, D_MODEL), _BF16),
        compiler_params=_params("parallel", "arbitrary"),
        name="swa",
    )(sink_rows, bias, att, att, att, vtb, vtb)


def _pack_rows(x):
    bits = lax.bitcast_convert_type(x.astype(_BF16).astype(_F32), jnp.uint32)
    return lax.bitcast_convert_type(bits[:, :HALF_D] | (bits[:, HALF_D:] >> 16), jnp.int32)


def _unpack_rows(words):
    bits = lax.bitcast_convert_type(words, jnp.uint32)
    hi = lax.bitcast_convert_type(bits & jnp.uint32(0xFFFF0000), _F32)
    lo = lax.bitcast_convert_type(bits << 16, _F32)
    return hi, lo


def _gather_rows(table, idx, n_groups, first, count):
    group_size = idx.shape[0] // n_groups
    n_out = n_groups * count
    width = table.shape[1]
    workers = SC_CORES * SC_SUBCORES
    per_worker = n_out // workers
    chunk_rows = next(c for c in GATHER_CHUNKS if per_worker % (2 * c) == 0)
    assert n_out % workers == 0 and count % per_worker == 0 and first % 8 == 0
    n_chunks = per_worker // chunk_rows
    mesh = plsc.VectorSubcoreMesh(core_axis_name="c", subcore_axis_name="s",
                                  num_cores=SC_CORES, num_subcores=SC_SUBCORES)

    @functools.partial(
        pl.kernel, mesh=mesh,
        out_type=jax.ShapeDtypeStruct((n_out, width), table.dtype),
        scratch_types=[
            pltpu.VMEM((per_worker,), jnp.int32),
            pltpu.VMEM((2, chunk_rows, width), table.dtype),
            pltpu.SemaphoreType.DMA((2,)),
            pltpu.SemaphoreType.DMA((2,)),
        ],
        name="gather_rows",
    )
    def gather(table_hbm, idx_hbm, out_hbm, idx_v, rows_v, fetch_sem, store_sem):
        base = (lax.axis_index("s") * SC_CORES + lax.axis_index("c")) * per_worker
        src = pl.multiple_of((base // count) * group_size + first + base % count, 8)
        pltpu.sync_copy(idx_hbm.at[pl.ds(src, per_worker)], idx_v)

        def fetch(g, buf):
            chunk = idx_v.at[pl.ds(g * chunk_rows, chunk_rows)]
            return pltpu.make_async_copy(table_hbm.at[chunk], rows_v.at[buf], fetch_sem.at[buf])

        def store(g, buf):
            dst = out_hbm.at[pl.ds(base + g * chunk_rows, chunk_rows)]
            return pltpu.make_async_copy(rows_v.at[buf], dst, store_sem.at[buf])

        fetch(0, 0).start()

        @pl.loop(0, n_chunks, step=2)
        def _(g0):
            for buf in range(2):
                g = g0 + buf
                fetch(g, buf).wait()

                @pl.when(g >= 1)
                def _():
                    store(g - 1, 1 - buf).wait()

                @pl.when(g + 1 < n_chunks)
                def _():
                    fetch(g + 1, 1 - buf).start()

                store(g, buf).start()

        store(n_chunks - 1, 1).wait()

    return gather(table, idx)


def _layer_norm(z, g, b):
    mu = jnp.mean(z, axis=1, keepdims=True)
    d = z - mu
    var = jnp.mean(d * d, axis=1, keepdims=True)
    return d * lax.rsqrt(var + LN_EPS) * g + b


def _merge_out_kernel(ga_ref, gb_ref, oa_ref, ob_ref, x_ref, wo_ref, lng_ref, lnb_ref, wr_ref, br_ref,
                      h_ref, hp_ref, idx_ref, gate_ref, cnt_ref, run_ref):
    @pl.when(pl.program_id(0) == 0)
    def _():
        run_ref[...] = jnp.zeros(run_ref.shape, _F32)

    merged = (jax.nn.sigmoid(ga_ref[...].astype(_F32)) * oa_ref[...].astype(_F32)
              + jax.nn.sigmoid(gb_ref[...].astype(_F32)) * ob_ref[...].astype(_F32))
    y = jnp.dot(merged.astype(_BF16), wo_ref[0], preferred_element_type=_F32)
    hn = _layer_norm(DEEPNORM_ALPHA * x_ref[...] + y, lng_ref[0, 0:1, :], lnb_ref[0, 0:1, :])
    h_ref[...] = hn
    hb = hn.astype(_BF16)
    hp_ref[...] = _pack_rows(hn)
    logits = lax.dot_general(wr_ref[0], hb, (((1,), (1,)), ((), ())),
                             preferred_element_type=_F32) + br_ref[0]
    tm = logits.shape[1]
    erow = lax.broadcasted_iota(jnp.int32, (N_EXPERTS, tm), 0)
    cur = logits
    vals, picks, id_rows = [], [], []
    for k in range(TOP_K):
        mx = jnp.max(cur, axis=0, keepdims=True)
        ix = jnp.min(jnp.where(cur == mx, erow, N_EXPERTS), axis=0, keepdims=True)
        vals.append(mx)
        picks.append(erow == ix)
        id_rows.append(ix)
        cur = jnp.where(picks[k], -jnp.inf, cur)
    exps = [jnp.exp(v - vals[0]) for v in vals]
    tot = exps[0] + exps[1] + exps[2] + exps[3]
    gate_rows = [e / tot for e in exps] + [jnp.zeros((LANES - TOP_K, tm), _F32)]
    gate_ref[...] = jnp.concatenate(gate_rows, axis=0).T

    chosen = jnp.zeros((N_EXPERTS, tm), _F32)
    for k in range(TOP_K):
        chosen = chosen + jnp.where(picks[k], 1.0, 0.0)
    r = lax.broadcasted_iota(jnp.int32, (tm, tm), 0)
    c = lax.broadcasted_iota(jnp.int32, (tm, tm), 1)
    earlier = jnp.where(r < c, 1.0, 0.0).astype(_BF16)
    before = jnp.dot(chosen.astype(_BF16), earlier, preferred_element_type=_F32) + run_ref[...]
    rank_rows = [jnp.sum(jnp.where(picks[k], before, 0.0), axis=0, keepdims=True).astype(jnp.int32)
                 for k in range(TOP_K)]
    idx_ref[...] = jnp.concatenate(id_rows + rank_rows, axis=0)
    total = run_ref[...] + jnp.sum(chosen, axis=1, keepdims=True)
    run_ref[...] = total
    cnt_ref[...] = jnp.broadcast_to(total, cnt_ref.shape).astype(jnp.int32)


def _merge_out(gates, o_a, o_b, x, wo_all, ln_g, ln_b, wr_all, br_all, layer):
    t = x.shape[0]
    tm = OUT_TM
    row = lambda i: (i, 0)
    return pl.pallas_call(
        _merge_out_kernel,
        grid=(t // tm,),
        in_specs=[
            pl.BlockSpec((tm, D_MODEL), lambda i: (i, 0)),
            pl.BlockSpec((tm, D_MODEL), lambda i: (i, 1)),
            pl.BlockSpec((tm, D_MODEL), row),
            pl.BlockSpec((tm, D_MODEL), row),
            pl.BlockSpec((tm, D_MODEL), row),
            pl.BlockSpec((1, D_MODEL, D_MODEL), lambda i: (layer, 0, 0)),
            pl.BlockSpec((1, 2, D_MODEL), lambda i: (layer, 0, 0)),
            pl.BlockSpec((1, 2, D_MODEL), lambda i: (layer, 0, 0)),
            pl.BlockSpec((1, N_EXPERTS, D_MODEL), lambda i: (layer, 0, 0)),
            pl.BlockSpec((1, N_EXPERTS, 1), lambda i: (layer, 0, 0)),
        ],
        out_specs=[
            pl.BlockSpec((tm, D_MODEL), row),
            pl.BlockSpec((tm, HALF_D), row),
            pl.BlockSpec((2 * TOP_K, tm), lambda i: (0, i)),
            pl.BlockSpec((tm, LANES), row),
            pl.BlockSpec((N_EXPERTS, LANES), lambda i: (0, 0)),
        ],
        out_shape=[
            jax.ShapeDtypeStruct((t, D_MODEL), _F32),
            jax.ShapeDtypeStruct((t, HALF_D), jnp.int32),
            jax.ShapeDtypeStruct((2 * TOP_K, t), jnp.int32),
            jax.ShapeDtypeStruct((t, LANES), _F32),
            jax.ShapeDtypeStruct((N_EXPERTS, LANES), jnp.int32),
        ],
        scratch_shapes=[pltpu.VMEM((N_EXPERTS, 1), _F32)],
        compiler_params=_params("arbitrary"),
        name="merge_out_ln_router",
    )(gates, gates, o_a, o_b, x, wo_all, ln_g, ln_b, wr_all, br_all)


def _expert_kernel(be_ref, nused_ref, x_ref, wup_ref, bg_ref, bu_ref, wd_ref, bd_ref, *rest, block_off):
    y_ref, wg_scr, wu_scr, wd_scr = rest[-4:]
    i = pl.program_id(0)
    blk = i + block_off
    changed = jnp.logical_or(i == 0, be_ref[blk] != be_ref[jnp.maximum(blk - 1, 0)])
    active = blk < nused_ref[0]

    @pl.when(jnp.logical_and(changed, active))
    def _():
        half = DEINT_TILE // 2
        r = lax.broadcasted_iota(jnp.int32, (DEINT_TILE, DEINT_TILE), 0)
        c = lax.broadcasted_iota(jnp.int32, (DEINT_TILE, DEINT_TILE), 1)
        sel = jnp.where(r == jnp.where(c < half, 2 * c, 2 * (c - half) + 1), 1.0, 0.0).astype(_BF16)
        for t in range(2 * D_FF // DEINT_TILE):
            w = wup_ref[0, 0, :, t * DEINT_TILE:(t + 1) * DEINT_TILE].astype(_BF16)
            de = jnp.dot(w, sel, preferred_element_type=_F32).astype(_BF16)
            wg_scr[:, t * half:(t + 1) * half] = de[:, :half]
            wu_scr[:, t * half:(t + 1) * half] = de[:, half:]
        wd_scr[...] = wd_ref[0, 0].astype(_BF16)

    @pl.when(active)
    def _():
        hi, lo = _unpack_rows(x_ref[...])
        x = jnp.concatenate([hi.astype(_BF16), lo.astype(_BF16)], axis=1)
        gate = jnp.dot(x, wg_scr[...], preferred_element_type=_F32) + bg_ref[0, 0]
        up = jnp.dot(x, wu_scr[...], preferred_element_type=_F32) + bu_ref[0, 0]
        gate = jnp.minimum(gate, SWIGLU_LIMIT)
        up = jnp.clip(up, -SWIGLU_LIMIT, SWIGLU_LIMIT)
        act = (up + 1.0) * (gate * jax.nn.sigmoid(SWIGLU_ALPHA * gate))
        y = jnp.dot(act.astype(_BF16), wd_scr[...], preferred_element_type=_F32) + bd_ref[0, 0]
        y_ref[...] = _pack_rows(y)

    @pl.when(jnp.logical_not(active))
    def _():
        y_ref[...] = jnp.zeros(y_ref.shape, y_ref.dtype)


def _experts(block_expert, n_used, x_part, part, y_prev, w_up, bg_all, bu_all, w_down, bd_all, layer):
    part_blocks = x_part.shape[0] // MOE_BLOCK
    n_rows = block_expert.shape[0] * MOE_BLOCK
    off = part * part_blocks
    pick = lambda i, be, nu: (layer, be[i + off], 0, 0)
    bspec = pl.BlockSpec((1, 1, 1, D_FF), pick)
    in_specs = [
        pl.BlockSpec((MOE_BLOCK, HALF_D), lambda i, be, nu: (i, 0)),
        pl.BlockSpec((1, 1, D_MODEL, 2 * D_FF), pick),
        bspec, bspec,
        pl.BlockSpec((1, 1, D_FF, D_MODEL), pick),
        pl.BlockSpec((1, 1, 1, D_MODEL), pick),
    ]
    operands = [block_expert, n_used, x_part, w_up, bg_all, bu_all, w_down, bd_all]
    aliases = {}
    if y_prev is not None:
        in_specs.append(pl.BlockSpec(memory_space=pl.ANY))
        aliases = {len(operands): 0}
        operands.append(y_prev)
    grid_spec = pltpu.PrefetchScalarGridSpec(
        num_scalar_prefetch=2,
        grid=(part_blocks,),
        in_specs=in_specs,
        out_specs=pl.BlockSpec((MOE_BLOCK, HALF_D), lambda i, be, nu: (i + off, 0)),
        scratch_shapes=[
            pltpu.VMEM((D_MODEL, D_FF), _BF16),
            pltpu.VMEM((D_MODEL, D_FF), _BF16),
            pltpu.VMEM((D_FF, D_MODEL), _BF16),
        ],
    )
    return pl.pallas_call(
        functools.partial(_expert_kernel, block_off=off),
        grid_spec=grid_spec,
        out_shape=jax.ShapeDtypeStruct((n_rows, HALF_D), jnp.int32),
        input_output_aliases=aliases,
        compiler_params=_params("arbitrary"),
        name="experts",
    )(*operands)


def _combine_kernel(h_ref, yg_ref, gate_ref, lng_ref, lnb_ref, *rest):
    x_ref, xb_ref = rest[-2:]
    gates = gate_ref[...]
    y = None
    for k in range(TOP_K):
        hi, lo = _unpack_rows(yg_ref[k])
        yk = gates[:, k:k + 1] * jnp.concatenate([hi, lo], axis=1)
        y = yk if y is None else y + yk
    xn = _layer_norm(DEEPNORM_ALPHA * h_ref[...] + y, lng_ref[0, 1:2, :], lnb_ref[0, 1:2, :])
    x_ref[...] = xn
    xb_ref[...] = xn.astype(_BF16)


def _combine(h, yg_part, gates, ln_g, ln_b, layer, part, prev):
    t = h.shape[0]
    tm = LN2_TM
    off = part * (yg_part.shape[1] // tm)
    row = lambda i: (i + off, 0)
    in_specs = [
        pl.BlockSpec((tm, D_MODEL), row),
        pl.BlockSpec((TOP_K, tm, HALF_D), lambda i: (0, i, 0)),
        pl.BlockSpec((tm, LANES), row),
        pl.BlockSpec((1, 2, D_MODEL), lambda i: (layer, 0, 0)),
        pl.BlockSpec((1, 2, D_MODEL), lambda i: (layer, 0, 0)),
    ]
    operands = [h, yg_part, gates, ln_g, ln_b]
    aliases = {}
    if prev is not None:
        in_specs += [pl.BlockSpec(memory_space=pl.ANY), pl.BlockSpec(memory_space=pl.ANY)]
        aliases = {len(operands): 0, len(operands) + 1: 1}
        operands += list(prev)
    return pl.pallas_call(
        _combine_kernel,
        grid=(yg_part.shape[1] // tm,),
        in_specs=in_specs,
        out_specs=[pl.BlockSpec((tm, D_MODEL), row), pl.BlockSpec((tm, D_MODEL), row)],
        out_shape=[jax.ShapeDtypeStruct((t, D_MODEL), _F32), jax.ShapeDtypeStruct((t, D_MODEL), _BF16)],
        input_output_aliases=aliases,
        compiler_params=_params("parallel"),
        name="combine_ln",
    )(*operands)


def _dispatch_tables(route, counts, t):
    tk = t * TOP_K
    experts = route[:TOP_K]
    ranks = route[TOP_K:]
    padded = (counts + MOE_BLOCK - 1) // MOE_BLOCK * MOE_BLOCK
    pad_end = jnp.cumsum(padded)
    pad_start = pad_end - padded
    start = jnp.cumsum(counts) - counts
    expert_ids = jnp.arange(N_EXPERTS, dtype=jnp.int32)
    dest = jnp.sum(jnp.where(experts[..., None] == expert_ids, pad_start, 0), axis=-1) + ranks
    n_blocks = (tk + N_EXPERTS * (MOE_BLOCK - 1) + MOE_BLOCK - 1) // MOE_BLOCK
    block_start = jnp.arange(n_blocks, dtype=jnp.int32) * MOE_BLOCK
    block_expert = jnp.minimum(
        jnp.sum((pad_end[None, :] <= block_start[:, None]).astype(jnp.int32), axis=1), N_EXPERTS - 1)
    n_used = (pad_end[-1:] // MOE_BLOCK).astype(jnp.int32)
    order = jnp.argsort(experts.T.reshape(tk))
    in_block = jnp.arange(MOE_BLOCK, dtype=jnp.int32)[None, :]
    r = (block_start - pad_start[block_expert])[:, None] + in_block
    valid = r < counts[block_expert][:, None]
    src = jnp.where(valid, start[block_expert][:, None] + r, 0).reshape(n_blocks * MOE_BLOCK)
    filler = jnp.arange(n_blocks * MOE_BLOCK, dtype=jnp.int32) % t
    buf_tok = jnp.where(valid.reshape(-1), order[src] // TOP_K, filler)
    return buf_tok, dest, block_expert, n_used


def kernel(x, w_in, w_o, lambda_qk, subln_g, sinks, ln_g, ln_b, w_router, b_router, w_up, b_up, w_down, b_down):
    batch, seq, _ = x.shape
    t = batch * seq
    assert seq % ATT_TQ == 0 and seq % SWA_TB == 0 and t % PROJ_TM == 0

    va_lo, va_hi = 2 * D_MODEL, 3 * D_MODEL
    gates_lo = w_in.shape[-1] - GATE_COLS
    w_att_b = jnp.concatenate([w_in[..., :va_lo], w_in[..., va_hi:gates_lo]], axis=-1).astype(_BF16)
    vb_lo = gates_lo - SW_KV_HEADS * SW_HEAD_DIM
    w_vt_b = jnp.swapaxes(jnp.concatenate([w_in[..., va_lo:va_hi], w_in[..., vb_lo:gates_lo]], axis=-1),
                          1, 2).astype(_BF16)
    w_gates_b = w_in[..., gates_lo:].astype(_BF16)
    w_o_b = w_o.astype(_BF16)
    w_r_b = jnp.swapaxes(w_router, 1, 2).astype(_BF16)
    b_gate = b_up[..., 0::2].reshape(DEPTH, N_EXPERTS, 1, D_FF)
    b_upp = b_up[..., 1::2].reshape(DEPTH, N_EXPERTS, 1, D_FF)
    b_down4 = b_down.reshape(DEPTH, N_EXPERTS, 1, D_MODEL)
    b_router3 = b_router.astype(_F32).reshape(DEPTH, N_EXPERTS, 1)

    att_scale = jnp.concatenate([
        jnp.full((D_MODEL,), DA_HEAD_DIM ** -0.5 * LOG2E, _F32), jnp.ones((D_MODEL,), _F32),
        jnp.full((D_MODEL,), SW_HEAD_DIM ** -0.5 * LOG2E, _F32),
        jnp.ones((2 * SW_KV_HEADS * SW_HEAD_DIM,), _F32)]).reshape(1, ATT_COLS)
    gate_scale = jnp.ones((1, GATE_COLS), _F32)
    rest = 2.0 ** (-8.0 * jnp.arange(1, DA_HEADS + 1, dtype=_F32) / DA_HEADS) * LOG2E
    pieces = []
    for _ in range(N_SLOPE_PARTS):
        piece = rest.astype(_BF16).astype(_F32)
        pieces.append(piece)
        rest = rest - piece
    da_slopes = jnp.stack(pieces, axis=1)
    swa_bias = _swa_bias()

    xf = x.reshape(t, D_MODEL)
    xb = xf.astype(_BF16)
    for l in range(DEPTH):
        lam_init = 0.8 - 0.6 * math.exp(-0.3 * l)
        prm = jnp.concatenate([lambda_qk[l].astype(_F32), jnp.full((4, DA_HEAD_DIM), lam_init, _F32)], axis=0)
        att = _project(xb, w_att_b, l, ATT_COLS // 2, att_scale, _BF16)
        gates = _project(xb, w_gates_b, l, GATE_COLS // 2, gate_scale, _BF16)
        vt, vtb = _project_vt(xb, w_vt_b, l)
        o_a = _diff_attention(att, vt, da_slopes, prm, subln_g[l].reshape(DA_WIDTH, 1), batch, seq)
        sink_rows = jnp.repeat(sinks[l].astype(_F32).reshape(SW_KV_HEADS, SW_GROUP) * LOG2E, WINDOW, axis=1)
        o_b = _sliding_window_attention(att, vtb, sink_rows.reshape(SW_KV_HEADS, 1, SW_GROUP * WINDOW),
                                        swa_bias, batch, seq)
        h, hp, route, top_gate, counts = _merge_out(gates, o_a, o_b, xf, w_o_b, ln_g, ln_b, w_r_b, b_router3, l)
        buf_tok, pos, block_expert, n_used = _dispatch_tables(route, counts[:, 0], t)
        part_rows = buf_tok.shape[0] // MOE_PARTS
        yb = None
        for part in range(MOE_PARTS):
            x_rows = _gather_rows(hp, buf_tok, 1, part * part_rows, part_rows)
            yb = _experts(block_expert, n_used, x_rows, part, yb, w_up, b_gate, b_upp, w_down, b_down4, l)
        part_tok = t // MOE_PARTS
        pos_flat = pos.reshape(TOP_K * t)
        outs = None
        for part in range(MOE_PARTS):
            yg = _gather_rows(yb, pos_flat, TOP_K, part * part_tok, part_tok).reshape(TOP_K, part_tok, HALF_D)
            outs = _combine(h, yg, top_gate, ln_g, ln_b, l, part, outs)
        xf, xb = outs
    return xf.reshape(batch, seq, D_MODEL)
```

```python
import functools
import math

import jax
import jax.numpy as jnp
from jax import lax
from jax.experimental import pallas as pl
from jax.experimental.pallas import tpu as pltpu
from jax.experimental.pallas import tpu_sc as plsc

D_MODEL = 1024
DEPTH = 4
DA_HEAD_DIM = 128
DA_HEADS = 4
DA_WIDTH = 2 * DA_HEAD_DIM
SW_HEAD_DIM = 64
SW_HEADS = 16
SW_KV_HEADS = 2
SW_GROUP = SW_HEADS // SW_KV_HEADS
WINDOW = 128
N_EXPERTS = 32
TOP_K = 4
D_FF = D_MODEL
SWIGLU_LIMIT = 7.0
SWIGLU_ALPHA = 1.702
LN_EPS = 1e-5
DEEPNORM_ALPHA = (2.0 * DEPTH) ** 0.25
NEG_INF = -1e30

QA_OFF, KA_OFF, QB_OFF, KVB_OFF = 0, 1024, 2048, 3072
ATT_COLS = 3328
GATE_COLS = 2 * D_MODEL
LANES = 128
HALF_D = D_MODEL // 2
SC_CORES = 2
SC_SUBCORES = 16
GATHER_CHUNKS = (64, 32, 16)
SUBLANES_BF16 = 16
VT_ROWS = DA_WIDTH + SUBLANES_BF16
VTB_ROWS = SW_HEAD_DIM + SUBLANES_BF16
LOG2E = math.log2(math.e)
N_SLOPE_PARTS = 3

PROJ_TM = 1024
ATT_TQ = 1024
ATT_TK = 512
SWA_TB = 256
OUT_TM = 512
MOE_BLOCK = 256
MOE_PARTS = 2
DEINT_TILE = 256
LN2_TM = 512
VMEM_LIMIT = 56 * 1024 * 1024

_F32 = jnp.float32
_BF16 = jnp.bfloat16


def _params(*sem):
    return pltpu.CompilerParams(dimension_semantics=sem, vmem_limit_bytes=VMEM_LIMIT)


def _proj_kernel(x_ref, w_ref, s_ref, o_ref):
    acc = jnp.dot(x_ref[...], w_ref[0], preferred_element_type=_F32)
    o_ref[...] = (acc * s_ref[...]).astype(o_ref.dtype)


def _project(xb, w_all, layer, tn, scale, out_dtype):
    t = xb.shape[0]
    n_cols = w_all.shape[-1]
    assert n_cols % tn == 0 and t % PROJ_TM == 0
    return pl.pallas_call(
        _proj_kernel,
        grid=(n_cols // tn, t // PROJ_TM),
        in_specs=[
            pl.BlockSpec((PROJ_TM, D_MODEL), lambda j, i: (i, 0)),
            pl.BlockSpec((1, D_MODEL, tn), lambda j, i: (layer, 0, j)),
            pl.BlockSpec((1, tn), lambda j, i: (0, j)),
        ],
        out_specs=pl.BlockSpec((PROJ_TM, tn), lambda j, i: (i, j)),
        out_shape=jax.ShapeDtypeStruct((t, n_cols), out_dtype),
        compiler_params=_params("parallel", "parallel"),
        name="in_proj",
    )(xb, w_all, scale)


def _proj_vt_kernel(x_ref, wt_ref, oa_ref, ob_ref):
    x = x_ref[...]
    nt = (((1,), (1,)), ((), ()))
    ones = jnp.ones((SUBLANES_BF16, x.shape[0]), oa_ref.dtype)
    for h in range(DA_HEADS):
        vt = lax.dot_general(wt_ref[0, h * DA_WIDTH:(h + 1) * DA_WIDTH, :], x, nt, preferred_element_type=_F32)
        oa_ref[h * VT_ROWS:h * VT_ROWS + DA_WIDTH, :] = vt.astype(oa_ref.dtype)
        oa_ref[h * VT_ROWS + DA_WIDTH:(h + 1) * VT_ROWS, :] = ones
    vb = lax.dot_general(wt_ref[0, D_MODEL:, :], x, nt, preferred_element_type=_F32)
    for kh in range(SW_KV_HEADS):
        ob_ref[kh * VTB_ROWS:kh * VTB_ROWS + SW_HEAD_DIM, :] = (
            vb[kh * SW_HEAD_DIM:(kh + 1) * SW_HEAD_DIM, :].astype(ob_ref.dtype))
        ob_ref[kh * VTB_ROWS + SW_HEAD_DIM:(kh + 1) * VTB_ROWS, :] = ones


def _project_vt(xb, wt_all, layer):
    t = xb.shape[0]
    n_b = SW_KV_HEADS * SW_HEAD_DIM
    return pl.pallas_call(
        _proj_vt_kernel,
        grid=(t // PROJ_TM,),
        in_specs=[
            pl.BlockSpec((PROJ_TM, D_MODEL), lambda i: (i, 0)),
            pl.BlockSpec((1, D_MODEL + n_b, D_MODEL), lambda i: (layer, 0, 0)),
        ],
        out_specs=[pl.BlockSpec((DA_HEADS * VT_ROWS, PROJ_TM), lambda i: (0, i)),
                   pl.BlockSpec((SW_KV_HEADS * VTB_ROWS, PROJ_TM), lambda i: (0, i))],
        out_shape=[jax.ShapeDtypeStruct((DA_HEADS * VT_ROWS, t), _BF16),
                   jax.ShapeDtypeStruct((SW_KV_HEADS * VTB_ROWS, t), _BF16)],
        compiler_params=_params("parallel"),
        name="in_proj_vt",
    )(xb, wt_all)


def _diff_attn_kernel(slope_ref, prm_ref, g_ref, q_ref, k_ref, vt_ref, o_ref,
                      acc_ref, m_ref, qaug_ref, kaug_ref, s_ref, smax_ref, p_ref, alpha_ref, *, tq, tk):
    h = pl.program_id(1)
    i = pl.program_id(2)
    diag = tq // tk
    n_full = i * diag
    lane = lax.broadcasted_iota(jnp.int32, (tq, DA_HEAD_DIM), 1)
    slope_tile = jnp.zeros((tq, DA_HEAD_DIM), _F32)
    slope = 0.0
    for part in range(N_SLOPE_PARTS):
        piece = slope_ref[h, part]
        slope = slope + piece
        slope_tile = jnp.where(lane // 2 == part, piece, slope_tile)
    klane = lax.broadcasted_iota(jnp.int32, (tk, DA_HEAD_DIM), 1)
    kpos = lax.broadcasted_iota(jnp.int32, (tk, DA_HEAD_DIM), 0)
    pos_tile = jnp.where(klane < 2 * N_SLOPE_PARTS, jnp.where(klane % 2 == 0, kpos % 256, kpos // 256 * 256), 0)
    for c in range(2):
        kaug_ref[c, :, DA_HEAD_DIM:] = pos_tile.astype(_F32).astype(_BF16)
        qaug_ref[c, :, :DA_HEAD_DIM] = q_ref[:, c * DA_HEAD_DIM:(c + 1) * DA_HEAD_DIM]
        qaug_ref[c, :, DA_HEAD_DIM:] = slope_tile.astype(_BF16)
    m_ref[...] = jnp.full(m_ref.shape, NEG_INF, _F32)
    acc_ref[...] = jnp.zeros(acc_ref.shape, _F32)

    def scores(j):
        start = pl.multiple_of(j * tk, tk)
        for c in range(2):
            kaug_ref[c, :, :DA_HEAD_DIM] = k_ref[pl.ds(start, tk), c * DA_HEAD_DIM:(c + 1) * DA_HEAD_DIM]
            s = lax.dot_general(kaug_ref[c], qaug_ref[c], (((1,), (1,)), ((), ())),
                                preferred_element_type=_F32)
            s_ref[c] = s
            smax_ref[c] = jnp.max(s, axis=0, keepdims=True)

    def weights(j, boundary):
        off = (j * tk - i * tq).astype(_F32) * slope
        for c in range(2):
            s = s_ref[c]
            if boundary is None:
                smax = smax_ref[c]
            else:
                keys = lax.broadcasted_iota(jnp.int32, (tk, tq), 0) + boundary * tk
                queries = lax.broadcasted_iota(jnp.int32, (tk, tq), 1)
                s = jnp.where(keys <= queries, s, NEG_INF)
                smax = jnp.max(s, axis=0, keepdims=True)
            m_old = m_ref[c]
            m_new = jnp.maximum(m_old, smax + off)
            alpha_ref[c] = jnp.exp2(m_old - m_new)
            p_ref[c] = jnp.exp2(s - (m_new - off)).astype(_BF16)
            m_ref[c] = m_new

    def accumulate(j):
        start = pl.multiple_of(j * tk, tk)
        vt = vt_ref[:, pl.ds(start, tk)]
        for c in range(2):
            acc_ref[c] = alpha_ref[c] * acc_ref[c] + jnp.dot(vt, p_ref[c], preferred_element_type=_F32)

    def body(j, carry):
        accumulate(j)
        weights(j + 1, None)
        scores(j + 2)
        return carry

    def enter_boundary():
        weights(n_full, 0)
        if diag > 1:
            scores(n_full + 1)

    scores(0)

    @pl.when(i >= 1)
    def _():
        weights(0, None)
        scores(1)
        lax.fori_loop(0, n_full - 1, body, 0)
        accumulate(n_full - 1)
        enter_boundary()

    @pl.when(i == 0)
    def _():
        enter_boundary()

    for d in range(diag):
        accumulate(n_full + d)
        if d + 1 < diag:
            weights(n_full + d + 1, d + 1)
            if d + 2 < diag:
                scores(n_full + d + 2)

    prm = prm_ref[...]
    lam_init = prm[4:5, 0:1]
    lam = (jnp.exp(jnp.sum(prm[0:1] * prm[1:2], axis=1, keepdims=True))
           - jnp.exp(jnp.sum(prm[2:3] * prm[3:4], axis=1, keepdims=True)) + lam_init)
    o0 = acc_ref[0, :DA_WIDTH, :] / acc_ref[0, DA_WIDTH:DA_WIDTH + 1, :]
    o1 = acc_ref[1, :DA_WIDTH, :] / acc_ref[1, DA_WIDTH:DA_WIDTH + 1, :]
    a = o0 - lam * o1
    ms = jnp.mean(a * a, axis=0, keepdims=True)
    out = a * lax.rsqrt(ms + LN_EPS) * g_ref[...] * (1.0 - lam_init)
    o_ref[...] = out.T.astype(o_ref.dtype)


def _diff_attention(att, vt, slopes, prm, subln_g, batch, seq):
    tq, tk = ATT_TQ, ATT_TK
    assert tq % tk == 0 and seq % tq == 0
    nq = seq // tq
    t = batch * seq
    kernel = functools.partial(_diff_attn_kernel, tq=tq, tk=tk)
    return pl.pallas_call(
        kernel,
        grid=(batch, DA_HEADS, nq),
        in_specs=[
            pl.BlockSpec(memory_space=pltpu.SMEM),
            pl.BlockSpec((8, LANES), lambda b, h, i: (0, 0)),
            pl.BlockSpec((DA_WIDTH, 1), lambda b, h, i: (0, 0)),
            pl.BlockSpec((tq, DA_WIDTH), lambda b, h, i: (b * nq + i, QA_OFF // DA_WIDTH + h)),
            pl.BlockSpec((seq, DA_WIDTH), lambda b, h, i: (b, KA_OFF // DA_WIDTH + h)),
            pl.BlockSpec((VT_ROWS, seq), lambda b, h, i: (h, b)),
        ],
        out_specs=pl.BlockSpec((tq, DA_WIDTH), lambda b, h, i: (b * nq + i, h)),
        out_shape=jax.ShapeDtypeStruct((t, D_MODEL), _BF16),
        scratch_shapes=[
            pltpu.VMEM((2, VT_ROWS, tq), _F32),
            pltpu.VMEM((2, 1, tq), _F32),
            pltpu.VMEM((2, tq, 2 * DA_HEAD_DIM), _BF16),
            pltpu.VMEM((2, tk, 2 * DA_HEAD_DIM), _BF16),
            pltpu.VMEM((2, tk, tq), _F32),
            pltpu.VMEM((2, 1, tq), _F32),
            pltpu.VMEM((2, tk, tq), _BF16),
            pltpu.VMEM((2, 1, tq), _F32),
        ],
        compiler_params=_params("parallel", "parallel", "arbitrary"),
        name="diff_attn",
    )(slopes, prm, subln_g, att, att, vt)


def _swa_kernel(sink_ref, bias_ref, q_ref, kv_ref, kvp_ref, vt_ref, vtp_ref, o_ref):
    i = pl.program_id(1)
    first = (i == 0).astype(jnp.int32)
    nt = (((1,), (1,)), ((), ()))
    kv_all = jnp.concatenate([kvp_ref[...], kv_ref[...]], axis=0)
    vt_all = jnp.concatenate([vtp_ref[...], vt_ref[...]], axis=1)
    for r in range(SWA_TB // WINDOW):
        heads = []
        for kh in range(SW_KV_HEADS):
            kwin = kv_all[r * WINDOW:(r + 2) * WINDOW, kh * SW_HEAD_DIM:(kh + 1) * SW_HEAD_DIM]
            vwin = vt_all[kh * VTB_ROWS:(kh + 1) * VTB_ROWS, r * WINDOW:(r + 2) * WINDOW]
            qs = jnp.concatenate(
                [q_ref[r * WINDOW:(r + 1) * WINDOW, (kh * SW_GROUP + g) * SW_HEAD_DIM:(kh * SW_GROUP + g + 1) * SW_HEAD_DIM]
                 for g in range(SW_GROUP)], axis=0)
            s = lax.dot_general(kwin, qs, nt, preferred_element_type=_F32)
            bias = bias_ref[2 * kh + first] if r == 0 else bias_ref[2 * kh]
            s = s + bias
            sink = sink_ref[kh]
            m = jnp.maximum(jnp.max(s, axis=0, keepdims=True), sink)
            p = jnp.exp2(s - m)
            ot = jnp.dot(vwin, p.astype(_BF16), preferred_element_type=_F32)
            denom = ot[SW_HEAD_DIM:SW_HEAD_DIM + 1, :] + jnp.exp2(sink - m)
            ot = ot[:SW_HEAD_DIM, :] / denom
            for g in range(SW_GROUP):
                heads.append(ot[:, g * WINDOW:(g + 1) * WINDOW].T)
        o_ref[r * WINDOW:(r + 1) * WINDOW, :] = jnp.concatenate(heads, axis=1).astype(o_ref.dtype)


def _swa_bias():
    slopes = 2.0 ** (-8.0 * jnp.arange(1, SW_HEADS + 1, dtype=_F32) / SW_HEADS)
    kj = jnp.arange(2 * WINDOW, dtype=jnp.int32)[:, None]
    qi = jnp.arange(WINDOW, dtype=jnp.int32)[None, :]
    dist = qi + WINDOW - kj
    valid = (dist >= 0) & (dist < WINDOW)
    tables = []
    for kh in range(SW_KV_HEADS):
        for has_prev in (True, False):
            ok = valid if has_prev else valid & (kj >= WINDOW)
            per_head = [jnp.where(ok, -slopes[kh * SW_GROUP + g] * LOG2E * dist.astype(_F32), NEG_INF)
                        for g in range(SW_GROUP)]
            tables.append(jnp.concatenate(per_head, axis=1))
    return jnp.stack(tables)


def _sliding_window_attention(att, vtb, sink_rows, bias, batch, seq):
    tb = SWA_TB
    nb = seq // tb
    ratio = tb // WINDOW
    t = batch * seq
    kv_width = 2 * SW_KV_HEADS * SW_HEAD_DIM
    n_b = SW_KV_HEADS * VTB_ROWS
    q_blk = QB_OFF // D_MODEL
    kv_blk = KVB_OFF // kv_width
    prev = lambda b, i: jnp.maximum((b * nb + i) * ratio - 1, 0)
    return pl.pallas_call(
        _swa_kernel,
        grid=(batch, nb),
        in_specs=[
            pl.BlockSpec((SW_KV_HEADS, 1, SW_GROUP * WINDOW), lambda b, i: (0, 0, 0)),
            pl.BlockSpec((2 * SW_KV_HEADS, 2 * WINDOW, SW_GROUP * WINDOW), lambda b, i: (0, 0, 0)),
            pl.BlockSpec((tb, D_MODEL), lambda b, i: (b * nb + i, q_blk)),
            pl.BlockSpec((tb, kv_width), lambda b, i: (b * nb + i, kv_blk)),
            pl.BlockSpec((WINDOW, kv_width), lambda b, i: (prev(b, i), kv_blk)),
            pl.BlockSpec((n_b, tb), lambda b, i: (0, b * nb + i)),
            pl.BlockSpec((n_b, WINDOW), lambda b, i: (0, prev(b, i))),
        ],
        out_specs=pl.BlockSpec((tb, D_MODEL), lambda b, i: (b * nb + i, 0)),
        out_shape=jax.ShapeDtypeStruct((t, D_MODEL), _BF16),
        compiler_params=_params("parallel", "arbitrary"),
        name="swa",
    )(sink_rows, bias, att, att, att, vtb, vtb)


def _pack_rows(x):
    bits = lax.bitcast_convert_type(x.astype(_BF16).astype(_F32), jnp.uint32)
    return lax.bitcast_convert_type(bits[:, :HALF_D] | (bits[:, HALF_D:] >> 16), jnp.int32)


def _unpack_rows(words):
    bits = lax.bitcast_convert_type(words, jnp.uint32)
    hi = lax.bitcast_convert_type(bits & jnp.uint32(0xFFFF0000), _F32)
    lo = lax.bitcast_convert_type(bits << 16, _F32)
    return hi, lo


def _gather_rows(table, idx, n_groups, first, count):
    group_size = idx.shape[0] // n_groups
    n_out = n_groups * count
    width = table.shape[1]
    workers = SC_CORES * SC_SUBCORES
    per_worker = n_out // workers
    chunk_rows = next(c for c in GATHER_CHUNKS if per_worker % (2 * c) == 0)
    assert n_out % workers == 0 and count % per_worker == 0 and first % 8 == 0
    n_chunks = per_worker // chunk_rows
    mesh = plsc.VectorSubcoreMesh(core_axis_name="c", subcore_axis_name="s",
                                  num_cores=SC_CORES, num_subcores=SC_SUBCORES)

    @functools.partial(
        pl.kernel, mesh=mesh,
        out_type=jax.ShapeDtypeStruct((n_out, width), table.dtype),
        scratch_types=[
            pltpu.VMEM((per_worker,), jnp.int32),
            pltpu.VMEM((2, chunk_rows, width), table.dtype),
            pltpu.SemaphoreType.DMA((2,)),
            pltpu.SemaphoreType.DMA((2,)),
        ],
        name="gather_rows",
    )
    def gather(table_hbm, idx_hbm, out_hbm, idx_v, rows_v, fetch_sem, store_sem):
        base = (lax.axis_index("s") * SC_CORES + lax.axis_index("c")) * per_worker
        src = pl.multiple_of((base // count) * group_size + first + base % count, 8)
        pltpu.sync_copy(idx_hbm.at[pl.ds(src, per_worker)], idx_v)

        def fetch(g, buf):
            chunk = idx_v.at[pl.ds(g * chunk_rows, chunk_rows)]
            return pltpu.make_async_copy(table_hbm.at[chunk], rows_v.at[buf], fetch_sem.at[buf])

        def store(g, buf):
            dst = out_hbm.at[pl.ds(base + g * chunk_rows, chunk_rows)]
            return pltpu.make_async_copy(rows_v.at[buf], dst, store_sem.at[buf])

        fetch(0, 0).start()

        @pl.loop(0, n_chunks, step=2)
        def _(g0):
            for buf in range(2):
                g = g0 + buf
                fetch(g, buf).wait()

                @pl.when(g >= 1)
                def _():
                    store(g - 1, 1 - buf).wait()

                @pl.when(g + 1 < n_chunks)
                def _():
                    fetch(g + 1, 1 - buf).start()

                store(g, buf).start()

        store(n_chunks - 1, 1).wait()

    return gather(table, idx)


def _layer_norm(z, g, b):
    mu = jnp.mean(z, axis=1, keepdims=True)
    d = z - mu
    var = jnp.mean(d * d, axis=1, keepdims=True)
    return d * lax.rsqrt(var + LN_EPS) * g + b


def _merge_out_kernel(ga_ref, gb_ref, oa_ref, ob_ref, x_ref, wo_ref, lng_ref, lnb_ref, wr_ref, br_ref,
                      h_ref, hp_ref, idx_ref, gate_ref, cnt_ref, run_ref):
    @pl.when(pl.program_id(0) == 0)
    def _():
        run_ref[...] = jnp.zeros(run_ref.shape, _F32)

    merged = (jax.nn.sigmoid(ga_ref[...].astype(_F32)) * oa_ref[...].astype(_F32)
              + jax.nn.sigmoid(gb_ref[...].astype(_F32)) * ob_ref[...].astype(_F32))
    y = jnp.dot(merged.astype(_BF16), wo_ref[0], preferred_element_type=_F32)
    hn = _layer_norm(DEEPNORM_ALPHA * x_ref[...] + y, lng_ref[0, 0:1, :], lnb_ref[0, 0:1, :])
    h_ref[...] = hn
    hb = hn.astype(_BF16)
    hp_ref[...] = _pack_rows(hn)
    logits = lax.dot_general(wr_ref[0], hb, (((1,), (1,)), ((), ())),
                             preferred_element_type=_F32) + br_ref[0]
    tm = logits.shape[1]
    erow = lax.broadcasted_iota(jnp.int32, (N_EXPERTS, tm), 0)
    cur = logits
    vals, picks, id_rows = [], [], []
    for k in range(TOP_K):
        mx = jnp.max(cur, axis=0, keepdims=True)
        ix = jnp.min(jnp.where(cur == mx, erow, N_EXPERTS), axis=0, keepdims=True)
        vals.append(mx)
        picks.append(erow == ix)
        id_rows.append(ix)
        cur = jnp.where(picks[k], -jnp.inf, cur)
    exps = [jnp.exp(v - vals[0]) for v in vals]
    tot = exps[0] + exps[1] + exps[2] + exps[3]
    gate_rows = [e / tot for e in exps] + [jnp.zeros((LANES - TOP_K, tm), _F32)]
    gate_ref[...] = jnp.concatenate(gate_rows, axis=0).T

    chosen = jnp.zeros((N_EXPERTS, tm), _F32)
    for k in range(TOP_K):
        chosen = chosen + jnp.where(picks[k], 1.0, 0.0)
    r = lax.broadcasted_iota(jnp.int32, (tm, tm), 0)
    c = lax.broadcasted_iota(jnp.int32, (tm, tm), 1)
    earlier = jnp.where(r < c, 1.0, 0.0).astype(_BF16)
    before = jnp.dot(chosen.astype(_BF16), earlier, preferred_element_type=_F32) + run_ref[...]
    rank_rows = [jnp.sum(jnp.where(picks[k], before, 0.0), axis=0, keepdims=True).astype(jnp.int32)
                 for k in range(TOP_K)]
    idx_ref[...] = jnp.concatenate(id_rows + rank_rows, axis=0)
    total = run_ref[...] + jnp.sum(chosen, axis=1, keepdims=True)
    run_ref[...] = total
    cnt_ref[...] = jnp.broadcast_to(total, cnt_ref.shape).astype(jnp.int32)


def _merge_out(gates, o_a, o_b, x, wo_all, ln_g, ln_b, wr_all, br_all, layer):
    t = x.shape[0]
    tm = OUT_TM
    row = lambda i: (i, 0)
    return pl.pallas_call(
        _merge_out_kernel,
        grid=(t // tm,),
        in_specs=[
            pl.BlockSpec((tm, D_MODEL), lambda i: (i, 0)),
            pl.BlockSpec((tm, D_MODEL), lambda i: (i, 1)),
            pl.BlockSpec((tm, D_MODEL), row),
            pl.BlockSpec((tm, D_MODEL), row),
            pl.BlockSpec((tm, D_MODEL), row),
            pl.BlockSpec((1, D_MODEL, D_MODEL), lambda i: (layer, 0, 0)),
            pl.BlockSpec((1, 2, D_MODEL), lambda i: (layer, 0, 0)),
            pl.BlockSpec((1, 2, D_MODEL), lambda i: (layer, 0, 0)),
            pl.BlockSpec((1, N_EXPERTS, D_MODEL), lambda i: (layer, 0, 0)),
            pl.BlockSpec((1, N_EXPERTS, 1), lambda i: (layer, 0, 0)),
        ],
        out_specs=[
            pl.BlockSpec((tm, D_MODEL), row),
            pl.BlockSpec((tm, HALF_D), row),
            pl.BlockSpec((2 * TOP_K, tm), lambda i: (0, i)),
            pl.BlockSpec((tm, LANES), row),
            pl.BlockSpec((N_EXPERTS, LANES), lambda i: (0, 0)),
        ],
        out_shape=[
            jax.ShapeDtypeStruct((t, D_MODEL), _F32),
            jax.ShapeDtypeStruct((t, HALF_D), jnp.int32),
            jax.ShapeDtypeStruct((2 * TOP_K, t), jnp.int32),
            jax.ShapeDtypeStruct((t, LANES), _F32),
            jax.ShapeDtypeStruct((N_EXPERTS, LANES), jnp.int32),
        ],
        scratch_shapes=[pltpu.VMEM((N_EXPERTS, 1), _F32)],
        compiler_params=_params("arbitrary"),
        name="merge_out_ln_router",
    )(gates, gates, o_a, o_b, x, wo_all, ln_g, ln_b, wr_all, br_all)


def _expert_kernel(be_ref, nused_ref, x_ref, wup_ref, bg_ref, bu_ref, wd_ref, bd_ref, *rest, block_off):
    y_ref, wg_scr, wu_scr, wd_scr = rest[-4:]
    i = pl.program_id(0)
    blk = i + block_off
    changed = jnp.logical_or(i == 0, be_ref[blk] != be_ref[jnp.maximum(blk - 1, 0)])
    active = blk < nused_ref[0]

    @pl.when(jnp.logical_and(changed, active))
    def _():
        half = DEINT_TILE // 2
        r = lax.broadcasted_iota(jnp.int32, (DEINT_TILE, DEINT_TILE), 0)
        c = lax.broadcasted_iota(jnp.int32, (DEINT_TILE, DEINT_TILE), 1)
        sel = jnp.where(r == jnp.where(c < half, 2 * c, 2 * (c - half) + 1), 1.0, 0.0).astype(_BF16)
        for t in range(2 * D_FF // DEINT_TILE):
            w = wup_ref[0, 0, :, t * DEINT_TILE:(t + 1) * DEINT_TILE].astype(_BF16)
            de = jnp.dot(w, sel, preferred_element_type=_F32).astype(_BF16)
            wg_scr[:, t * half:(t + 1) * half] = de[:, :half]
            wu_scr[:, t * half:(t + 1) * half] = de[:, half:]
        wd_scr[...] = wd_ref[0, 0].astype(_BF16)

    @pl.when(active)
    def _():
        hi, lo = _unpack_rows(x_ref[...])
        x = jnp.concatenate([hi.astype(_BF16), lo.astype(_BF16)], axis=1)
        gate = jnp.dot(x, wg_scr[...], preferred_element_type=_F32) + bg_ref[0, 0]
        up = jnp.dot(x, wu_scr[...], preferred_element_type=_F32) + bu_ref[0, 0]
        gate = jnp.minimum(gate, SWIGLU_LIMIT)
        up = jnp.clip(up, -SWIGLU_LIMIT, SWIGLU_LIMIT)
        act = (up + 1.0) * (gate * jax.nn.sigmoid(SWIGLU_ALPHA * gate))
        y = jnp.dot(act.astype(_BF16), wd_scr[...], preferred_element_type=_F32) + bd_ref[0, 0]
        y_ref[...] = _pack_rows(y)

    @pl.when(jnp.logical_not(active))
    def _():
        y_ref[...] = jnp.zeros(y_ref.shape, y_ref.dtype)


def _experts(block_expert, n_used, x_part, part, y_prev, w_up, bg_all, bu_all, w_down, bd_all, layer):
    part_blocks = x_part.shape[0] // MOE_BLOCK
    n_rows = block_expert.shape[0] * MOE_BLOCK
    off = part * part_blocks
    pick = lambda i, be, nu: (layer, be[i + off], 0, 0)
    bspec = pl.BlockSpec((1, 1, 1, D_FF), pick)
    in_specs = [
        pl.BlockSpec((MOE_BLOCK, HALF_D), lambda i, be, nu: (i, 0)),
        pl.BlockSpec((1, 1, D_MODEL, 2 * D_FF), pick),
        bspec, bspec,
        pl.BlockSpec((1, 1, D_FF, D_MODEL), pick),
        pl.BlockSpec((1, 1, 1, D_MODEL), pick),
    ]
    operands = [block_expert, n_used, x_part, w_up, bg_all, bu_all, w_down, bd_all]
    aliases = {}
    if y_prev is not None:
        in_specs.append(pl.BlockSpec(memory_space=pl.ANY))
        aliases = {len(operands): 0}
        operands.append(y_prev)
    grid_spec = pltpu.PrefetchScalarGridSpec(
        num_scalar_prefetch=2,
        grid=(part_blocks,),
        in_specs=in_specs,
        out_specs=pl.BlockSpec((MOE_BLOCK, HALF_D), lambda i, be, nu: (i + off, 0)),
        scratch_shapes=[
            pltpu.VMEM((D_MODEL, D_FF), _BF16),
            pltpu.VMEM((D_MODEL, D_FF), _BF16),
            pltpu.VMEM((D_FF, D_MODEL), _BF16),
        ],
    )
    return pl.pallas_call(
        functools.partial(_expert_kernel, block_off=off),
        grid_spec=grid_spec,
        out_shape=jax.ShapeDtypeStruct((n_rows, HALF_D), jnp.int32),
        input_output_aliases=aliases,
        compiler_params=_params("arbitrary"),
        name="experts",
    )(*operands)


def _combine_kernel(h_ref, yg_ref, gate_ref, lng_ref, lnb_ref, *rest):
    x_ref, xb_ref = rest[-2:]
    gates = gate_ref[...]
    y = None
    for k in range(TOP_K):
        hi, lo = _unpack_rows(yg_ref[k])
        yk = gates[:, k:k + 1] * jnp.concatenate([hi, lo], axis=1)
        y = yk if y is None else y + yk
    xn = _layer_norm(DEEPNORM_ALPHA * h_ref[...] + y, lng_ref[0, 1:2, :], lnb_ref[0, 1:2, :])
    x_ref[...] = xn
    xb_ref[...] = xn.astype(_BF16)


def _combine(h, yg_part, gates, ln_g, ln_b, layer, part, prev):
    t = h.shape[0]
    tm = LN2_TM
    off = part * (yg_part.shape[1] // tm)
    row = lambda i: (i + off, 0)
    in_specs = [
        pl.BlockSpec((tm, D_MODEL), row),
        pl.BlockSpec((TOP_K, tm, HALF_D), lambda i: (0, i, 0)),
        pl.BlockSpec((tm, LANES), row),
        pl.BlockSpec((1, 2, D_MODEL), lambda i: (layer, 0, 0)),
        pl.BlockSpec((1, 2, D_MODEL), lambda i: (layer, 0, 0)),
    ]
    operands = [h, yg_part, gates, ln_g, ln_b]
    aliases = {}
    if prev is not None:
        in_specs += [pl.BlockSpec(memory_space=pl.ANY), pl.BlockSpec(memory_space=pl.ANY)]
        aliases = {len(operands): 0, len(operands) + 1: 1}
        operands += list(prev)
    return pl.pallas_call(
        _combine_kernel,
        grid=(yg_part.shape[1] // tm,),
        in_specs=in_specs,
        out_specs=[pl.BlockSpec((tm, D_MODEL), row), pl.BlockSpec((tm, D_MODEL), row)],
        out_shape=[jax.ShapeDtypeStruct((t, D_MODEL), _F32), jax.ShapeDtypeStruct((t, D_MODEL), _BF16)],
        input_output_aliases=aliases,
        compiler_params=_params("parallel"),
        name="combine_ln",
    )(*operands)


def _dispatch_tables(route, counts, t):
    tk = t * TOP_K
    experts = route[:TOP_K]
    ranks = route[TOP_K:]
    padded = (counts + MOE_BLOCK - 1) // MOE_BLOCK * MOE_BLOCK
    pad_end = jnp.cumsum(padded)
    pad_start = pad_end - padded
    start = jnp.cumsum(counts) - counts
    expert_ids = jnp.arange(N_EXPERTS, dtype=jnp.int32)
    dest = jnp.sum(jnp.where(experts[..., None] == expert_ids, pad_start, 0), axis=-1) + ranks
    n_blocks = (tk + N_EXPERTS * (MOE_BLOCK - 1) + MOE_BLOCK - 1) // MOE_BLOCK
    block_start = jnp.arange(n_blocks, dtype=jnp.int32) * MOE_BLOCK
    block_expert = jnp.minimum(
        jnp.sum((pad_end[None, :] <= block_start[:, None]).astype(jnp.int32), axis=1), N_EXPERTS - 1)
    n_used = (pad_end[-1:] // MOE_BLOCK).astype(jnp.int32)
    order = jnp.argsort(experts.T.reshape(tk))
    in_block = jnp.arange(MOE_BLOCK, dtype=jnp.int32)[None, :]
    r = (block_start - pad_start[block_expert])[:, None] + in_block
    valid = r < counts[block_expert][:, None]
    src = jnp.where(valid, start[block_expert][:, None] + r, 0).reshape(n_blocks * MOE_BLOCK)
    filler = jnp.arange(n_blocks * MOE_BLOCK, dtype=jnp.int32) % t
    buf_tok = jnp.where(valid.reshape(-1), order[src] // TOP_K, filler)
    return buf_tok, dest, block_expert, n_used


def kernel(x, w_in, w_o, lambda_qk, subln_g, sinks, ln_g, ln_b, w_router, b_router, w_up, b_up, w_down, b_down):
    batch, seq, _ = x.shape
    t = batch * seq
    assert seq % ATT_TQ == 0 and seq % SWA_TB == 0 and t % PROJ_TM == 0

    va_lo, va_hi = 2 * D_MODEL, 3 * D_MODEL
    gates_lo = w_in.shape[-1] - GATE_COLS
    w_att_b = jnp.concatenate([w_in[..., :va_lo], w_in[..., va_hi:gates_lo]], axis=-1).astype(_BF16)
    vb_lo = gates_lo - SW_KV_HEADS * SW_HEAD_DIM
    w_vt_b = jnp.swapaxes(jnp.concatenate([w_in[..., va_lo:va_hi], w_in[..., vb_lo:gates_lo]], axis=-1),
                          1, 2).astype(_BF16)
    w_gates_b = w_in[..., gates_lo:].astype(_BF16)
    w_o_b = w_o.astype(_BF16)
    w_r_b = jnp.swapaxes(w_router, 1, 2).astype(_BF16)
    b_gate = b_up[..., 0::2].reshape(DEPTH, N_EXPERTS, 1, D_FF)
    b_upp = b_up[..., 1::2].reshape(DEPTH, N_EXPERTS, 1, D_FF)
    b_down4 = b_down.reshape(DEPTH, N_EXPERTS, 1, D_MODEL)
    b_router3 = b_router.astype(_F32).reshape(DEPTH, N_EXPERTS, 1)

    att_scale = jnp.concatenate([
        jnp.full((D_MODEL,), DA_HEAD_DIM ** -0.5 * LOG2E, _F32), jnp.ones((D_MODEL,), _F32),
        jnp.full((D_MODEL,), SW_HEAD_DIM ** -0.5 * LOG2E, _F32),
        jnp.ones((2 * SW_KV_HEADS * SW_HEAD_DIM,), _F32)]).reshape(1, ATT_COLS)
    gate_scale = jnp.ones((1, GATE_COLS), _F32)
    rest = 2.0 ** (-8.0 * jnp.arange(1, DA_HEADS + 1, dtype=_F32) / DA_HEADS) * LOG2E
    pieces = []
    for _ in range(N_SLOPE_PARTS):
        piece = rest.astype(_BF16).astype(_F32)
        pieces.append(piece)
        rest = rest - piece
    da_slopes = jnp.stack(pieces, axis=1)
    swa_bias = _swa_bias()

    xf = x.reshape(t, D_MODEL)
    xb = xf.astype(_BF16)
    for l in range(DEPTH):
        lam_init = 0.8 - 0.6 * math.exp(-0.3 * l)
        prm = jnp.concatenate([lambda_qk[l].astype(_F32), jnp.full((4, DA_HEAD_DIM), lam_init, _F32)], axis=0)
        att = _project(xb, w_att_b, l, ATT_COLS // 2, att_scale, _BF16)
        gates = _project(xb, w_gates_b, l, GATE_COLS // 2, gate_scale, _BF16)
        vt, vtb = _project_vt(xb, w_vt_b, l)
        o_a = _diff_attention(att, vt, da_slopes, prm, subln_g[l].reshape(DA_WIDTH, 1), batch, seq)
        sink_rows = jnp.repeat(sinks[l].astype(_F32).reshape(SW_KV_HEADS, SW_GROUP) * LOG2E, WINDOW, axis=1)
        o_b = _sliding_window_attention(att, vtb, sink_rows.reshape(SW_KV_HEADS, 1, SW_GROUP * WINDOW),
                                        swa_bias, batch, seq)
        h, hp, route, top_gate, counts = _merge_out(gates, o_a, o_b, xf, w_o_b, ln_g, ln_b, w_r_b, b_router3, l)
        buf_tok, pos, block_expert, n_used = _dispatch_tables(route, counts[:, 0], t)
        part_rows = buf_tok.shape[0] // MOE_PARTS
        yb = None
        for part in range(MOE_PARTS):
            x_rows = _gather_rows(hp, buf_tok, 1, part * part_rows, part_rows)
            yb = _experts(block_expert, n_used, x_rows, part, yb, w_up, b_gate, b_upp, w_down, b_down4, l)
        part_tok = t // MOE_PARTS
        pos_flat = pos.reshape(TOP_K * t)
        outs = None
        for part in range(MOE_PARTS):
            yg = _gather_rows(yb, pos_flat, TOP_K, part * part_tok, part_tok).reshape(TOP_K, part_tok, HALF_D)
            outs = _combine(h, yg, top_gate, ln_g, ln_b, l, part, outs)
        xf, xb = outs
    return xf.reshape(batch, seq, D_MODEL)
```

```python
import functools
import math

import jax
import jax.numpy as jnp
from jax import lax
from jax.experimental import pallas as pl
from jax.experimental.pallas import tpu as pltpu
from jax.experimental.pallas import tpu_sc as plsc

D_MODEL = 1024
DEPTH = 4
DA_HEAD_DIM = 128
DA_HEADS = 4
DA_WIDTH = 2 * DA_HEAD_DIM
SW_HEAD_DIM = 64
SW_HEADS = 16
SW_KV_HEADS = 2
SW_GROUP = SW_HEADS // SW_KV_HEADS
WINDOW = 128
N_EXPERTS = 32
TOP_K = 4
D_FF = D_MODEL
SWIGLU_LIMIT = 7.0
SWIGLU_ALPHA = 1.702
LN_EPS = 1e-5
DEEPNORM_ALPHA = (2.0 * DEPTH) ** 0.25
NEG_INF = -1e30

QA_OFF, KA_OFF, QB_OFF, KVB_OFF = 0, D_MODEL, 2 * D_MODEL, 3 * D_MODEL
ATT_COLS = KVB_OFF + 2 * SW_KV_HEADS * SW_HEAD_DIM
GATE_COLS = 2 * D_MODEL
HALF_D = D_MODEL // 2
HI16_MASK = 0xFFFF0000
LOG2E = math.log2(math.e)
N_SLOPE_PARTS = 3
BF16_EXACT_INTS = 256

LANES = 128
SUBLANES_BF16 = 16
MXU_TILE = 256
VMEM_BYTES = 64 * 1024 * 1024
SC_CORES = 2
SC_SUBCORES = 16
SC_SLICE_ALIGN = 8
GATHER_CHUNKS = (64, 32, 16)

VT_ROWS = DA_WIDTH + SUBLANES_BF16
VTB_ROWS = SW_HEAD_DIM + SUBLANES_BF16

PROJ_TM = 1024
ATT_TQ = 1024
ATT_TK = 512
SWA_TB = 256
OUT_TM = 512
MOE_BLOCK = 256
MOE_PARTS = 2
DEINT_TILE = MXU_TILE
LN2_TM = 512
VMEM_LIMIT = VMEM_BYTES * 7 // 8

_F32 = jnp.float32
_BF16 = jnp.bfloat16


def _params(*sem):
    return pltpu.CompilerParams(dimension_semantics=sem, vmem_limit_bytes=VMEM_LIMIT)


def _proj_kernel(x_ref, w_ref, s_ref, o_ref):
    acc = jnp.dot(x_ref[...], w_ref[0], preferred_element_type=_F32)
    o_ref[...] = (acc * s_ref[...]).astype(o_ref.dtype)


def _project(xb, w_all, layer, tn, scale, out_dtype):
    t = xb.shape[0]
    n_cols = w_all.shape[-1]
    assert n_cols % tn == 0 and t % PROJ_TM == 0
    return pl.pallas_call(
        _proj_kernel,
        grid=(n_cols // tn, t // PROJ_TM),
        in_specs=[
            pl.BlockSpec((PROJ_TM, D_MODEL), lambda j, i: (i, 0)),
            pl.BlockSpec((1, D_MODEL, tn), lambda j, i: (layer, 0, j)),
            pl.BlockSpec((1, tn), lambda j, i: (0, j)),
        ],
        out_specs=pl.BlockSpec((PROJ_TM, tn), lambda j, i: (i, j)),
        out_shape=jax.ShapeDtypeStruct((t, n_cols), out_dtype),
        compiler_params=_params("parallel", "parallel"),
        name="in_proj",
    )(xb, w_all, scale)


def _proj_vt_kernel(x_ref, wt_ref, oa_ref, ob_ref):
    x = x_ref[...]
    nt = (((1,), (1,)), ((), ()))
    ones = jnp.ones((SUBLANES_BF16, x.shape[0]), oa_ref.dtype)
    for h in range(DA_HEADS):
        vt = lax.dot_general(wt_ref[0, h * DA_WIDTH:(h + 1) * DA_WIDTH, :], x, nt, preferred_element_type=_F32)
        oa_ref[h * VT_ROWS:h * VT_ROWS + DA_WIDTH, :] = vt.astype(oa_ref.dtype)
        oa_ref[h * VT_ROWS + DA_WIDTH:(h + 1) * VT_ROWS, :] = ones
    vb = lax.dot_general(wt_ref[0, D_MODEL:, :], x, nt, preferred_element_type=_F32)
    for kh in range(SW_KV_HEADS):
        ob_ref[kh * VTB_ROWS:kh * VTB_ROWS + SW_HEAD_DIM, :] = (
            vb[kh * SW_HEAD_DIM:(kh + 1) * SW_HEAD_DIM, :].astype(ob_ref.dtype))
        ob_ref[kh * VTB_ROWS + SW_HEAD_DIM:(kh + 1) * VTB_ROWS, :] = ones


def _project_vt(xb, wt_all, layer):
    t = xb.shape[0]
    n_b = SW_KV_HEADS * SW_HEAD_DIM
    return pl.pallas_call(
        _proj_vt_kernel,
        grid=(t // PROJ_TM,),
        in_specs=[
            pl.BlockSpec((PROJ_TM, D_MODEL), lambda i: (i, 0)),
            pl.BlockSpec((1, D_MODEL + n_b, D_MODEL), lambda i: (layer, 0, 0)),
        ],
        out_specs=[pl.BlockSpec((DA_HEADS * VT_ROWS, PROJ_TM), lambda i: (0, i)),
                   pl.BlockSpec((SW_KV_HEADS * VTB_ROWS, PROJ_TM), lambda i: (0, i))],
        out_shape=[jax.ShapeDtypeStruct((DA_HEADS * VT_ROWS, t), _BF16),
                   jax.ShapeDtypeStruct((SW_KV_HEADS * VTB_ROWS, t), _BF16)],
        compiler_params=_params("parallel"),
        name="in_proj_vt",
    )(xb, wt_all)


def _diff_attn_kernel(slope_ref, prm_ref, g_ref, q_ref, k_ref, vt_ref, o_ref,
                      acc_ref, m_ref, qaug_ref, kaug_ref, s_ref, smax_ref, p_ref, alpha_ref, *, tq, tk):
    h = pl.program_id(1)
    i = pl.program_id(2)
    diag = tq // tk
    n_full = i * diag
    lane = lax.broadcasted_iota(jnp.int32, (tq, DA_HEAD_DIM), 1)
    slope_tile = jnp.zeros((tq, DA_HEAD_DIM), _F32)
    slope = 0.0
    for part in range(N_SLOPE_PARTS):
        piece = slope_ref[h, part]
        slope = slope + piece
        slope_tile = jnp.where(lane // 2 == part, piece, slope_tile)
    klane = lax.broadcasted_iota(jnp.int32, (tk, DA_HEAD_DIM), 1)
    kpos = lax.broadcasted_iota(jnp.int32, (tk, DA_HEAD_DIM), 0)
    pos_split = jnp.where(klane % 2 == 0, kpos % BF16_EXACT_INTS, kpos // BF16_EXACT_INTS * BF16_EXACT_INTS)
    pos_tile = jnp.where(klane < 2 * N_SLOPE_PARTS, pos_split, 0)
    for c in range(2):
        kaug_ref[c, :, DA_HEAD_DIM:] = pos_tile.astype(_F32).astype(_BF16)
        qaug_ref[c, :, :DA_HEAD_DIM] = q_ref[:, c * DA_HEAD_DIM:(c + 1) * DA_HEAD_DIM]
        qaug_ref[c, :, DA_HEAD_DIM:] = slope_tile.astype(_BF16)
    m_ref[...] = jnp.full(m_ref.shape, NEG_INF, _F32)
    acc_ref[...] = jnp.zeros(acc_ref.shape, _F32)

    def scores(j):
        start = pl.multiple_of(j * tk, tk)
        for c in range(2):
            kaug_ref[c, :, :DA_HEAD_DIM] = k_ref[pl.ds(start, tk), c * DA_HEAD_DIM:(c + 1) * DA_HEAD_DIM]
            s = lax.dot_general(kaug_ref[c], qaug_ref[c], (((1,), (1,)), ((), ())),
                                preferred_element_type=_F32)
            s_ref[c] = s
            smax_ref[c] = jnp.max(s, axis=0, keepdims=True)

    def weights(j, boundary):
        off = (j * tk - i * tq).astype(_F32) * slope
        for c in range(2):
            s = s_ref[c]
            if boundary is None:
                smax = smax_ref[c]
            else:
                keys = lax.broadcasted_iota(jnp.int32, (tk, tq), 0) + boundary * tk
                queries = lax.broadcasted_iota(jnp.int32, (tk, tq), 1)
                s = jnp.where(keys <= queries, s, NEG_INF)
                smax = jnp.max(s, axis=0, keepdims=True)
            m_old = m_ref[c]
            m_new = jnp.maximum(m_old, smax + off)
            alpha_ref[c] = jnp.exp2(m_old - m_new)
            p_ref[c] = jnp.exp2(s - (m_new - off)).astype(_BF16)
            m_ref[c] = m_new

    def accumulate(j):
        start = pl.multiple_of(j * tk, tk)
        vt = vt_ref[:, pl.ds(start, tk)]
        for c in range(2):
            acc_ref[c] = alpha_ref[c] * acc_ref[c] + jnp.dot(vt, p_ref[c], preferred_element_type=_F32)

    def body(j, carry):
        accumulate(j)
        weights(j + 1, None)
        scores(j + 2)
        return carry

    def enter_boundary():
        weights(n_full, 0)
        if diag > 1:
            scores(n_full + 1)

    scores(0)

    @pl.when(i >= 1)
    def _():
        weights(0, None)
        scores(1)
        lax.fori_loop(0, n_full - 1, body, 0)
        accumulate(n_full - 1)
        enter_boundary()

    @pl.when(i == 0)
    def _():
        enter_boundary()

    for d in range(diag):
        accumulate(n_full + d)
        if d + 1 < diag:
            weights(n_full + d + 1, d + 1)
            if d + 2 < diag:
                scores(n_full + d + 2)

    prm = prm_ref[...]
    lam_init = prm[4:5, 0:1]
    lam = (jnp.exp(jnp.sum(prm[0:1] * prm[1:2], axis=1, keepdims=True))
           - jnp.exp(jnp.sum(prm[2:3] * prm[3:4], axis=1, keepdims=True)) + lam_init)
    o0 = acc_ref[0, :DA_WIDTH, :] / acc_ref[0, DA_WIDTH:DA_WIDTH + 1, :]
    o1 = acc_ref[1, :DA_WIDTH, :] / acc_ref[1, DA_WIDTH:DA_WIDTH + 1, :]
    a = o0 - lam * o1
    ms = jnp.mean(a * a, axis=0, keepdims=True)
    out = a * lax.rsqrt(ms + LN_EPS) * g_ref[...] * (1.0 - lam_init)
    o_ref[...] = out.T.astype(o_ref.dtype)


def _diff_attention(att, vt, slopes, prm, subln_g, batch, seq):
    tq, tk = ATT_TQ, ATT_TK
    assert tq % tk == 0 and seq % tq == 0
    nq = seq // tq
    t = batch * seq
    kernel = functools.partial(_diff_attn_kernel, tq=tq, tk=tk)
    return pl.pallas_call(
        kernel,
        grid=(batch, DA_HEADS, nq),
        in_specs=[
            pl.BlockSpec(memory_space=pltpu.SMEM),
            pl.BlockSpec((8, LANES), lambda b, h, i: (0, 0)),
            pl.BlockSpec((DA_WIDTH, 1), lambda b, h, i: (0, 0)),
            pl.BlockSpec((tq, DA_WIDTH), lambda b, h, i: (b * nq + i, QA_OFF // DA_WIDTH + h)),
            pl.BlockSpec((seq, DA_WIDTH), lambda b, h, i: (b, KA_OFF // DA_WIDTH + h)),
            pl.BlockSpec((VT_ROWS, seq), lambda b, h, i: (h, b)),
        ],
        out_specs=pl.BlockSpec((tq, DA_WIDTH), lambda b, h, i: (b * nq + i, h)),
        out_shape=jax.ShapeDtypeStruct((t, D_MODEL), _BF16),
        scratch_shapes=[
            pltpu.VMEM((2, VT_ROWS, tq), _F32),
            pltpu.VMEM((2, 1, tq), _F32),
            pltpu.VMEM((2, tq, 2 * DA_HEAD_DIM), _BF16),
            pltpu.VMEM((2, tk, 2 * DA_HEAD_DIM), _BF16),
            pltpu.VMEM((2, tk, tq), _F32),
            pltpu.VMEM((2, 1, tq), _F32),
            pltpu.VMEM((2, tk, tq), _BF16),
            pltpu.VMEM((2, 1, tq), _F32),
        ],
        compiler_params=_params("parallel", "parallel", "arbitrary"),
        name="diff_attn",
    )(slopes, prm, subln_g, att, att, vt)


def _swa_kernel(sink_ref, bias_ref, q_ref, kv_ref, kvp_ref, vt_ref, vtp_ref, o_ref):
    i = pl.program_id(1)
    first = (i == 0).astype(jnp.int32)
    nt = (((1,), (1,)), ((), ()))
    kv_all = jnp.concatenate([kvp_ref[...], kv_ref[...]], axis=0)
    vt_all = jnp.concatenate([vtp_ref[...], vt_ref[...]], axis=1)
    for r in range(SWA_TB // WINDOW):
        heads = []
        for kh in range(SW_KV_HEADS):
            kwin = kv_all[r * WINDOW:(r + 2) * WINDOW, kh * SW_HEAD_DIM:(kh + 1) * SW_HEAD_DIM]
            vwin = vt_all[kh * VTB_ROWS:(kh + 1) * VTB_ROWS, r * WINDOW:(r + 2) * WINDOW]
            qs = jnp.concatenate(
                [q_ref[r * WINDOW:(r + 1) * WINDOW, (kh * SW_GROUP + g) * SW_HEAD_DIM:(kh * SW_GROUP + g + 1) * SW_HEAD_DIM]
                 for g in range(SW_GROUP)], axis=0)
            s = lax.dot_general(kwin, qs, nt, preferred_element_type=_F32)
            bias = bias_ref[2 * kh + first] if r == 0 else bias_ref[2 * kh]
            s = s + bias
            sink = sink_ref[kh]
            m = jnp.maximum(jnp.max(s, axis=0, keepdims=True), sink)
            p = jnp.exp2(s - m)
            ot = jnp.dot(vwin, p.astype(_BF16), preferred_element_type=_F32)
            denom = ot[SW_HEAD_DIM:SW_HEAD_DIM + 1, :] + jnp.exp2(sink - m)
            ot = ot[:SW_HEAD_DIM, :] / denom
            for g in range(SW_GROUP):
                heads.append(ot[:, g * WINDOW:(g + 1) * WINDOW].T)
        o_ref[r * WINDOW:(r + 1) * WINDOW, :] = jnp.concatenate(heads, axis=1).astype(o_ref.dtype)


def _swa_bias():
    slopes = 2.0 ** (-8.0 * jnp.arange(1, SW_HEADS + 1, dtype=_F32) / SW_HEADS)
    kj = jnp.arange(2 * WINDOW, dtype=jnp.int32)[:, None]
    qi = jnp.arange(WINDOW, dtype=jnp.int32)[None, :]
    dist = qi + WINDOW - kj
    valid = (dist >= 0) & (dist < WINDOW)
    tables = []
    for kh in range(SW_KV_HEADS):
        for has_prev in (True, False):
            ok = valid if has_prev else valid & (kj >= WINDOW)
            per_head = [jnp.where(ok, -slopes[kh * SW_GROUP + g] * LOG2E * dist.astype(_F32), NEG_INF)
                        for g in range(SW_GROUP)]
            tables.append(jnp.concatenate(per_head, axis=1))
    return jnp.stack(tables)


def _sliding_window_attention(att, vtb, sink_rows, bias, batch, seq):
    tb = SWA_TB
    nb = seq // tb
    ratio = tb // WINDOW
    t = batch * seq
    kv_width = 2 * SW_KV_HEADS * SW_HEAD_DIM
    n_b = SW_KV_HEADS * VTB_ROWS
    q_blk = QB_OFF // D_MODEL
    kv_blk = KVB_OFF // kv_width
    prev = lambda b, i: jnp.maximum((b * nb + i) * ratio - 1, 0)
    return pl.pallas_call(
        _swa_kernel,
        grid=(batch, nb),
        in_specs=[
            pl.BlockSpec((SW_KV_HEADS, 1, SW_GROUP * WINDOW), lambda b, i: (0, 0, 0)),
            pl.BlockSpec((2 * SW_KV_HEADS, 2 * WINDOW, SW_GROUP * WINDOW), lambda b, i: (0, 0, 0)),
            pl.BlockSpec((tb, D_MODEL), lambda b, i: (b * nb + i, q_blk)),
            pl.BlockSpec((tb, kv_width), lambda b, i: (b * nb + i, kv_blk)),
            pl.BlockSpec((WINDOW, kv_width), lambda b, i: (prev(b, i), kv_blk)),
            pl.BlockSpec((n_b, tb), lambda b, i: (0, b * nb + i)),
            pl.BlockSpec((n_b, WINDOW), lambda b, i: (0, prev(b, i))),
        ],
        out_specs=pl.BlockSpec((tb, D_MODEL), lambda b, i: (b * nb + i, 0)),
        out_shape=jax.ShapeDtypeStruct((t, D_MODEL), _BF16),
        compiler_params=_params("parallel", "arbitrary"),
        name="swa",
    )(sink_rows, bias, att, att, att, vtb, vtb)


def _pack_rows(x):
    bits = lax.bitcast_convert_type(x.astype(_BF16).astype(_F32), jnp.uint32)
    return lax.bitcast_convert_type(bits[:, :HALF_D] | (bits[:, HALF_D:] >> 16), jnp.int32)


def _unpack_rows(words):
    bits = lax.bitcast_convert_type(words, jnp.uint32)
    hi = lax.bitcast_convert_type(bits & jnp.uint32(HI16_MASK), _F32)
    lo = lax.bitcast_convert_type(bits << 16, _F32)
    return hi, lo


def _gather_rows(table, idx, n_groups, first, count):
    group_size = idx.shape[0] // n_groups
    n_out = n_groups * count
    width = table.shape[1]
    workers = SC_CORES * SC_SUBCORES
    per_worker = n_out // workers
    chunk_rows = next(c for c in GATHER_CHUNKS if per_worker % (2 * c) == 0)
    assert n_out % workers == 0 and count % per_worker == 0 and first % SC_SLICE_ALIGN == 0
    n_chunks = per_worker // chunk_rows
    mesh = plsc.VectorSubcoreMesh(core_axis_name="c", subcore_axis_name="s",
                                  num_cores=SC_CORES, num_subcores=SC_SUBCORES)

    @functools.partial(
        pl.kernel, mesh=mesh,
        out_type=jax.ShapeDtypeStruct((n_out, width), table.dtype),
        scratch_types=[
            pltpu.VMEM((per_worker,), jnp.int32),
            pltpu.VMEM((2, chunk_rows, width), table.dtype),
            pltpu.SemaphoreType.DMA((2,)),
            pltpu.SemaphoreType.DMA((2,)),
        ],
        name="gather_rows",
    )
    def gather(table_hbm, idx_hbm, out_hbm, idx_v, rows_v, fetch_sem, store_sem):
        base = (lax.axis_index("s") * SC_CORES + lax.axis_index("c")) * per_worker
        src = pl.multiple_of((base // count) * group_size + first + base % count, SC_SLICE_ALIGN)
        pltpu.sync_copy(idx_hbm.at[pl.ds(src, per_worker)], idx_v)

        def fetch(g, buf):
            chunk = idx_v.at[pl.ds(g * chunk_rows, chunk_rows)]
            return pltpu.make_async_copy(table_hbm.at[chunk], rows_v.at[buf], fetch_sem.at[buf])

        def store(g, buf):
            dst = out_hbm.at[pl.ds(base + g * chunk_rows, chunk_rows)]
            return pltpu.make_async_copy(rows_v.at[buf], dst, store_sem.at[buf])

        fetch(0, 0).start()

        @pl.loop(0, n_chunks, step=2)
        def _(g0):
            for buf in range(2):
                g = g0 + buf
                fetch(g, buf).wait()

                @pl.when(g >= 1)
                def _():
                    store(g - 1, 1 - buf).wait()

                @pl.when(g + 1 < n_chunks)
                def _():
                    fetch(g + 1, 1 - buf).start()

                store(g, buf).start()

        store(n_chunks - 1, 1).wait()

    return gather(table, idx)


def _layer_norm(z, g, b):
    mu = jnp.mean(z, axis=1, keepdims=True)
    d = z - mu
    var = jnp.mean(d * d, axis=1, keepdims=True)
    return d * lax.rsqrt(var + LN_EPS) * g + b


def _merge_out_kernel(ga_ref, gb_ref, oa_ref, ob_ref, x_ref, wo_ref, lng_ref, lnb_ref, wr_ref, br_ref,
                      h_ref, hp_ref, idx_ref, gate_ref, cnt_ref, run_ref):
    @pl.when(pl.program_id(0) == 0)
    def _():
        run_ref[...] = jnp.zeros(run_ref.shape, _F32)

    merged = (jax.nn.sigmoid(ga_ref[...].astype(_F32)) * oa_ref[...].astype(_F32)
              + jax.nn.sigmoid(gb_ref[...].astype(_F32)) * ob_ref[...].astype(_F32))
    y = jnp.dot(merged.astype(_BF16), wo_ref[0], preferred_element_type=_F32)
    hn = _layer_norm(DEEPNORM_ALPHA * x_ref[...] + y, lng_ref[0, 0:1, :], lnb_ref[0, 0:1, :])
    h_ref[...] = hn
    hb = hn.astype(_BF16)
    hp_ref[...] = _pack_rows(hn)
    logits = lax.dot_general(wr_ref[0], hb, (((1,), (1,)), ((), ())),
                             preferred_element_type=_F32) + br_ref[0]
    tm = logits.shape[1]
    erow = lax.broadcasted_iota(jnp.int32, (N_EXPERTS, tm), 0)
    cur = logits
    vals, picks, id_rows = [], [], []
    for k in range(TOP_K):
        mx = jnp.max(cur, axis=0, keepdims=True)
        ix = jnp.min(jnp.where(cur == mx, erow, N_EXPERTS), axis=0, keepdims=True)
        vals.append(mx)
        picks.append(erow == ix)
        id_rows.append(ix)
        cur = jnp.where(picks[k], -jnp.inf, cur)
    exps = [jnp.exp(v - vals[0]) for v in vals]
    tot = exps[0] + exps[1] + exps[2] + exps[3]
    gate_rows = [e / tot for e in exps] + [jnp.zeros((LANES - TOP_K, tm), _F32)]
    gate_ref[...] = jnp.concatenate(gate_rows, axis=0).T

    chosen = jnp.zeros((N_EXPERTS, tm), _F32)
    for k in range(TOP_K):
        chosen = chosen + jnp.where(picks[k], 1.0, 0.0)
    r = lax.broadcasted_iota(jnp.int32, (tm, tm), 0)
    c = lax.broadcasted_iota(jnp.int32, (tm, tm), 1)
    earlier = jnp.where(r < c, 1.0, 0.0).astype(_BF16)
    before = jnp.dot(chosen.astype(_BF16), earlier, preferred_element_type=_F32) + run_ref[...]
    rank_rows = [jnp.sum(jnp.where(picks[k], before, 0.0), axis=0, keepdims=True).astype(jnp.int32)
                 for k in range(TOP_K)]
    idx_ref[...] = jnp.concatenate(id_rows + rank_rows, axis=0)
    total = run_ref[...] + jnp.sum(chosen, axis=1, keepdims=True)
    run_ref[...] = total
    cnt_ref[...] = jnp.broadcast_to(total, cnt_ref.shape).astype(jnp.int32)


def _merge_out(gates, o_a, o_b, x, wo_all, ln_g, ln_b, wr_all, br_all, layer):
    t = x.shape[0]
    tm = OUT_TM
    row = lambda i: (i, 0)
    return pl.pallas_call(
        _merge_out_kernel,
        grid=(t // tm,),
        in_specs=[
            pl.BlockSpec((tm, D_MODEL), lambda i: (i, 0)),
            pl.BlockSpec((tm, D_MODEL), lambda i: (i, 1)),
            pl.BlockSpec((tm, D_MODEL), row),
            pl.BlockSpec((tm, D_MODEL), row),
            pl.BlockSpec((tm, D_MODEL), row),
            pl.BlockSpec((1, D_MODEL, D_MODEL), lambda i: (layer, 0, 0)),
            pl.BlockSpec((1, 2, D_MODEL), lambda i: (layer, 0, 0)),
            pl.BlockSpec((1, 2, D_MODEL), lambda i: (layer, 0, 0)),
            pl.BlockSpec((1, N_EXPERTS, D_MODEL), lambda i: (layer, 0, 0)),
            pl.BlockSpec((1, N_EXPERTS, 1), lambda i: (layer, 0, 0)),
        ],
        out_specs=[
            pl.BlockSpec((tm, D_MODEL), row),
            pl.BlockSpec((tm, HALF_D), row),
            pl.BlockSpec((2 * TOP_K, tm), lambda i: (0, i)),
            pl.BlockSpec((tm, LANES), row),
            pl.BlockSpec((N_EXPERTS, LANES), lambda i: (0, 0)),
        ],
        out_shape=[
            jax.ShapeDtypeStruct((t, D_MODEL), _F32),
            jax.ShapeDtypeStruct((t, HALF_D), jnp.int32),
            jax.ShapeDtypeStruct((2 * TOP_K, t), jnp.int32),
            jax.ShapeDtypeStruct((t, LANES), _F32),
            jax.ShapeDtypeStruct((N_EXPERTS, LANES), jnp.int32),
        ],
        scratch_shapes=[pltpu.VMEM((N_EXPERTS, 1), _F32)],
        compiler_params=_params("arbitrary"),
        name="merge_out_ln_router",
    )(gates, gates, o_a, o_b, x, wo_all, ln_g, ln_b, wr_all, br_all)


def _expert_kernel(be_ref, nused_ref, x_ref, wup_ref, bg_ref, bu_ref, wd_ref, bd_ref, *rest, block_off):
    y_ref, wg_scr, wu_scr, wd_scr = rest[-4:]
    i = pl.program_id(0)
    blk = i + block_off
    changed = jnp.logical_or(i == 0, be_ref[blk] != be_ref[jnp.maximum(blk - 1, 0)])
    active = blk < nused_ref[0]

    @pl.when(jnp.logical_and(changed, active))
    def _():
        half = DEINT_TILE // 2
        r = lax.broadcasted_iota(jnp.int32, (DEINT_TILE, DEINT_TILE), 0)
        c = lax.broadcasted_iota(jnp.int32, (DEINT_TILE, DEINT_TILE), 1)
        sel = jnp.where(r == jnp.where(c < half, 2 * c, 2 * (c - half) + 1), 1.0, 0.0).astype(_BF16)
        for t in range(2 * D_FF // DEINT_TILE):
            w = wup_ref[0, 0, :, t * DEINT_TILE:(t + 1) * DEINT_TILE].astype(_BF16)
            de = jnp.dot(w, sel, preferred_element_type=_F32).astype(_BF16)
            wg_scr[:, t * half:(t + 1) * half] = de[:, :half]
            wu_scr[:, t * half:(t + 1) * half] = de[:, half:]
        wd_scr[...] = wd_ref[0, 0].astype(_BF16)

    @pl.when(active)
    def _():
        hi, lo = _unpack_rows(x_ref[...])
        x = jnp.concatenate([hi.astype(_BF16), lo.astype(_BF16)], axis=1)
        gate = jnp.dot(x, wg_scr[...], preferred_element_type=_F32) + bg_ref[0, 0]
        up = jnp.dot(x, wu_scr[...], preferred_element_type=_F32) + bu_ref[0, 0]
        gate = jnp.minimum(gate, SWIGLU_LIMIT)
        up = jnp.clip(up, -SWIGLU_LIMIT, SWIGLU_LIMIT)
        act = (up + 1.0) * (gate * jax.nn.sigmoid(SWIGLU_ALPHA * gate))
        y = jnp.dot(act.astype(_BF16), wd_scr[...], preferred_element_type=_F32) + bd_ref[0, 0]
        y_ref[...] = _pack_rows(y)

    @pl.when(jnp.logical_not(active))
    def _():
        y_ref[...] = jnp.zeros(y_ref.shape, y_ref.dtype)


def _experts(block_expert, n_used, x_part, part, y_prev, w_up, bg_all, bu_all, w_down, bd_all, layer):
    part_blocks = x_part.shape[0] // MOE_BLOCK
    n_rows = block_expert.shape[0] * MOE_BLOCK
    off = part * part_blocks
    pick = lambda i, be, nu: (layer, be[i + off], 0, 0)
    bspec = pl.BlockSpec((1, 1, 1, D_FF), pick)
    in_specs = [
        pl.BlockSpec((MOE_BLOCK, HALF_D), lambda i, be, nu: (i, 0)),
        pl.BlockSpec((1, 1, D_MODEL, 2 * D_FF), pick),
        bspec, bspec,
        pl.BlockSpec((1, 1, D_FF, D_MODEL), pick),
        pl.BlockSpec((1, 1, 1, D_MODEL), pick),
    ]
    operands = [block_expert, n_used, x_part, w_up, bg_all, bu_all, w_down, bd_all]
    aliases = {}
    if y_prev is not None:
        in_specs.append(pl.BlockSpec(memory_space=pl.ANY))
        aliases = {len(operands): 0}
        operands.append(y_prev)
    grid_spec = pltpu.PrefetchScalarGridSpec(
        num_scalar_prefetch=2,
        grid=(part_blocks,),
        in_specs=in_specs,
        out_specs=pl.BlockSpec((MOE_BLOCK, HALF_D), lambda i, be, nu: (i + off, 0)),
        scratch_shapes=[
            pltpu.VMEM((D_MODEL, D_FF), _BF16),
            pltpu.VMEM((D_MODEL, D_FF), _BF16),
            pltpu.VMEM((D_FF, D_MODEL), _BF16),
        ],
    )
    return pl.pallas_call(
        functools.partial(_expert_kernel, block_off=off),
        grid_spec=grid_spec,
        out_shape=jax.ShapeDtypeStruct((n_rows, HALF_D), jnp.int32),
        input_output_aliases=aliases,
        compiler_params=_params("arbitrary"),
        name="experts",
    )(*operands)


def _combine_kernel(h_ref, yg_ref, gate_ref, lng_ref, lnb_ref, *rest):
    x_ref, xb_ref = rest[-2:]
    gates = gate_ref[...]
    y = None
    for k in range(TOP_K):
        hi, lo = _unpack_rows(yg_ref[k])
        yk = gates[:, k:k + 1] * jnp.concatenate([hi, lo], axis=1)
        y = yk if y is None else y + yk
    xn = _layer_norm(DEEPNORM_ALPHA * h_ref[...] + y, lng_ref[0, 1:2, :], lnb_ref[0, 1:2, :])
    x_ref[...] = xn
    xb_ref[...] = xn.astype(_BF16)


def _combine(h, yg_part, gates, ln_g, ln_b, layer, part, prev):
    t = h.shape[0]
    tm = LN2_TM
    off = part * (yg_part.shape[1] // tm)
    row = lambda i: (i + off, 0)
    in_specs = [
        pl.BlockSpec((tm, D_MODEL), row),
        pl.BlockSpec((TOP_K, tm, HALF_D), lambda i: (0, i, 0)),
        pl.BlockSpec((tm, LANES), row),
        pl.BlockSpec((1, 2, D_MODEL), lambda i: (layer, 0, 0)),
        pl.BlockSpec((1, 2, D_MODEL), lambda i: (layer, 0, 0)),
    ]
    operands = [h, yg_part, gates, ln_g, ln_b]
    aliases = {}
    if prev is not None:
        in_specs += [pl.BlockSpec(memory_space=pl.ANY), pl.BlockSpec(memory_space=pl.ANY)]
        aliases = {len(operands): 0, len(operands) + 1: 1}
        operands += list(prev)
    return pl.pallas_call(
        _combine_kernel,
        grid=(yg_part.shape[1] // tm,),
        in_specs=in_specs,
        out_specs=[pl.BlockSpec((tm, D_MODEL), row), pl.BlockSpec((tm, D_MODEL), row)],
        out_shape=[jax.ShapeDtypeStruct((t, D_MODEL), _F32), jax.ShapeDtypeStruct((t, D_MODEL), _BF16)],
        input_output_aliases=aliases,
        compiler_params=_params("parallel"),
        name="combine_ln",
    )(*operands)


def _dispatch_tables(route, counts, t):
    tk = t * TOP_K
    experts = route[:TOP_K]
    ranks = route[TOP_K:]
    padded = (counts + MOE_BLOCK - 1) // MOE_BLOCK * MOE_BLOCK
    pad_end = jnp.cumsum(padded)
    pad_start = pad_end - padded
    start = jnp.cumsum(counts) - counts
    expert_ids = jnp.arange(N_EXPERTS, dtype=jnp.int32)
    dest = jnp.sum(jnp.where(experts[..., None] == expert_ids, pad_start, 0), axis=-1) + ranks
    n_blocks = (tk + N_EXPERTS * (MOE_BLOCK - 1) + MOE_BLOCK - 1) // MOE_BLOCK
    block_start = jnp.arange(n_blocks, dtype=jnp.int32) * MOE_BLOCK
    block_expert = jnp.minimum(
        jnp.sum((pad_end[None, :] <= block_start[:, None]).astype(jnp.int32), axis=1), N_EXPERTS - 1)
    n_used = (pad_end[-1:] // MOE_BLOCK).astype(jnp.int32)
    order = jnp.argsort(experts.T.reshape(tk))
    in_block = jnp.arange(MOE_BLOCK, dtype=jnp.int32)[None, :]
    r = (block_start - pad_start[block_expert])[:, None] + in_block
    valid = r < counts[block_expert][:, None]
    src = jnp.where(valid, start[block_expert][:, None] + r, 0).reshape(n_blocks * MOE_BLOCK)
    filler = jnp.arange(n_blocks * MOE_BLOCK, dtype=jnp.int32) % t
    buf_tok = jnp.where(valid.reshape(-1), order[src] // TOP_K, filler)
    return buf_tok, dest, block_expert, n_used


def kernel(x, w_in, w_o, lambda_qk, subln_g, sinks, ln_g, ln_b, w_router, b_router, w_up, b_up, w_down, b_down):
    batch, seq, _ = x.shape
    t = batch * seq
    assert seq % ATT_TQ == 0 and seq % SWA_TB == 0 and t % PROJ_TM == 0

    va_lo, va_hi = 2 * D_MODEL, 3 * D_MODEL
    gates_lo = w_in.shape[-1] - GATE_COLS
    w_att_b = jnp.concatenate([w_in[..., :va_lo], w_in[..., va_hi:gates_lo]], axis=-1).astype(_BF16)
    vb_lo = gates_lo - SW_KV_HEADS * SW_HEAD_DIM
    w_vt_b = jnp.swapaxes(jnp.concatenate([w_in[..., va_lo:va_hi], w_in[..., vb_lo:gates_lo]], axis=-1),
                          1, 2).astype(_BF16)
    w_gates_b = w_in[..., gates_lo:].astype(_BF16)
    w_o_b = w_o.astype(_BF16)
    w_r_b = jnp.swapaxes(w_router, 1, 2).astype(_BF16)
    b_gate = b_up[..., 0::2].reshape(DEPTH, N_EXPERTS, 1, D_FF)
    b_upp = b_up[..., 1::2].reshape(DEPTH, N_EXPERTS, 1, D_FF)
    b_down4 = b_down.reshape(DEPTH, N_EXPERTS, 1, D_MODEL)
    b_router3 = b_router.astype(_F32).reshape(DEPTH, N_EXPERTS, 1)

    att_scale = jnp.concatenate([
        jnp.full((D_MODEL,), DA_HEAD_DIM ** -0.5 * LOG2E, _F32), jnp.ones((D_MODEL,), _F32),
        jnp.full((D_MODEL,), SW_HEAD_DIM ** -0.5 * LOG2E, _F32),
        jnp.ones((2 * SW_KV_HEADS * SW_HEAD_DIM,), _F32)]).reshape(1, ATT_COLS)
    gate_scale = jnp.ones((1, GATE_COLS), _F32)
    rest = 2.0 ** (-8.0 * jnp.arange(1, DA_HEADS + 1, dtype=_F32) / DA_HEADS) * LOG2E
    pieces = []
    for _ in range(N_SLOPE_PARTS):
        piece = rest.astype(_BF16).astype(_F32)
        pieces.append(piece)
        rest = rest - piece
    da_slopes = jnp.stack(pieces, axis=1)
    swa_bias = _swa_bias()

    xf = x.reshape(t, D_MODEL)
    xb = xf.astype(_BF16)
    for l in range(DEPTH):
        lam_init = 0.8 - 0.6 * math.exp(-0.3 * l)
        prm = jnp.concatenate([lambda_qk[l].astype(_F32), jnp.full((4, DA_HEAD_DIM), lam_init, _F32)], axis=0)
        att = _project(xb, w_att_b, l, ATT_COLS // 2, att_scale, _BF16)
        gates = _project(xb, w_gates_b, l, GATE_COLS // 2, gate_scale, _BF16)
        vt, vtb = _project_vt(xb, w_vt_b, l)
        o_a = _diff_attention(att, vt, da_slopes, prm, subln_g[l].reshape(DA_WIDTH, 1), batch, seq)
        sink_rows = jnp.repeat(sinks[l].astype(_F32).reshape(SW_KV_HEADS, SW_GROUP) * LOG2E, WINDOW, axis=1)
        o_b = _sliding_window_attention(att, vtb, sink_rows.reshape(SW_KV_HEADS, 1, SW_GROUP * WINDOW),
                                        swa_bias, batch, seq)
        h, hp, route, top_gate, counts = _merge_out(gates, o_a, o_b, xf, w_o_b, ln_g, ln_b, w_r_b, b_router3, l)
        buf_tok, pos, block_expert, n_used = _dispatch_tables(route, counts[:, 0], t)
        part_rows = buf_tok.shape[0] // MOE_PARTS
        yb = None
        for part in range(MOE_PARTS):
            x_rows = _gather_rows(hp, buf_tok, 1, part * part_rows, part_rows)
            yb = _experts(block_expert, n_used, x_rows, part, yb, w_up, b_gate, b_upp, w_down, b_down4, l)
        part_tok = t // MOE_PARTS
        pos_flat = pos.reshape(TOP_K * t)
        outs = None
        for part in range(MOE_PARTS):
            yg = _gather_rows(yb, pos_flat, TOP_K, part * part_tok, part_tok).reshape(TOP_K, part_tok, HALF_D)
            outs = _combine(h, yg, top_gate, ln_g, ln_b, l, part, outs)
        xf, xb = outs
    return xf.reshape(batch, seq, D_MODEL)
```

```python
import functools
import math

import jax
import jax.numpy as jnp
from jax import lax
from jax.experimental import pallas as pl
from jax.experimental.pallas import tpu as pltpu
from jax.experimental.pallas import tpu_sc as plsc

D_MODEL = 1024
DEPTH = 4
DA_HEAD_DIM = 128
DA_HEADS = 4
DA_WIDTH = 2 * DA_HEAD_DIM
SW_HEAD_DIM = 64
SW_HEADS = 16
SW_KV_HEADS = 2
SW_GROUP = SW_HEADS // SW_KV_HEADS
WINDOW = 128
N_EXPERTS = 32
TOP_K = 4
D_FF = D_MODEL
SWIGLU_LIMIT = 7.0
SWIGLU_ALPHA = 1.702
LN_EPS = 1e-5
DEEPNORM_ALPHA = (2.0 * DEPTH) ** 0.25
NEG_INF = -1e30

QA_OFF, KA_OFF, QB_OFF, KVB_OFF = 0, D_MODEL, 2 * D_MODEL, 3 * D_MODEL
ATT_COLS = KVB_OFF + 2 * SW_KV_HEADS * SW_HEAD_DIM
GATE_COLS = 2 * D_MODEL
HALF_D = D_MODEL // 2
HI16_MASK = 0xFFFF0000
LOG2E = math.log2(math.e)
N_SLOPE_PARTS = 3
BF16_EXACT_INTS = 256

LANES = 128
SUBLANES_BF16 = 16
MXU_TILE = 256
VMEM_BYTES = 64 * 1024 * 1024
SC_CORES = 2
SC_SUBCORES = 16
SC_SLICE_ALIGN = 8
GATHER_CHUNKS = (64, 32, 16)

VT_ROWS = DA_WIDTH + SUBLANES_BF16
VTB_ROWS = SW_HEAD_DIM + SUBLANES_BF16

PROJ_TM = 1024
ATT_TQ = 1024
ATT_TK = 512
SWA_TB = 512
OUT_TM = 512
MOE_BLOCK = 256
MOE_PARTS = 2
DEINT_TILE = MXU_TILE
LN2_TM = 512
VMEM_LIMIT = VMEM_BYTES * 7 // 8

_F32 = jnp.float32
_BF16 = jnp.bfloat16


def _params(*sem):
    return pltpu.CompilerParams(dimension_semantics=sem, vmem_limit_bytes=VMEM_LIMIT)


def _proj_kernel(x_ref, w_ref, s_ref, o_ref):
    acc = jnp.dot(x_ref[...], w_ref[0], preferred_element_type=_F32)
    o_ref[...] = (acc * s_ref[...]).astype(o_ref.dtype)


def _project(xb, w_all, layer, tn, scale, out_dtype):
    t = xb.shape[0]
    n_cols = w_all.shape[-1]
    assert n_cols % tn == 0 and t % PROJ_TM == 0
    return pl.pallas_call(
        _proj_kernel,
        grid=(n_cols // tn, t // PROJ_TM),
        in_specs=[
            pl.BlockSpec((PROJ_TM, D_MODEL), lambda j, i: (i, 0)),
            pl.BlockSpec((1, D_MODEL, tn), lambda j, i: (layer, 0, j)),
            pl.BlockSpec((1, tn), lambda j, i: (0, j)),
        ],
        out_specs=pl.BlockSpec((PROJ_TM, tn), lambda j, i: (i, j)),
        out_shape=jax.ShapeDtypeStruct((t, n_cols), out_dtype),
        compiler_params=_params("parallel", "parallel"),
        name="in_proj",
    )(xb, w_all, scale)


def _proj_vt_kernel(x_ref, wt_ref, oa_ref, ob_ref):
    x = x_ref[...]
    nt = (((1,), (1,)), ((), ()))
    ones = jnp.ones((SUBLANES_BF16, x.shape[0]), oa_ref.dtype)
    for h in range(DA_HEADS):
        vt = lax.dot_general(wt_ref[0, h * DA_WIDTH:(h + 1) * DA_WIDTH, :], x, nt, preferred_element_type=_F32)
        oa_ref[h * VT_ROWS:h * VT_ROWS + DA_WIDTH, :] = vt.astype(oa_ref.dtype)
        oa_ref[h * VT_ROWS + DA_WIDTH:(h + 1) * VT_ROWS, :] = ones
    vb = lax.dot_general(wt_ref[0, D_MODEL:, :], x, nt, preferred_element_type=_F32)
    for kh in range(SW_KV_HEADS):
        ob_ref[kh * VTB_ROWS:kh * VTB_ROWS + SW_HEAD_DIM, :] = (
            vb[kh * SW_HEAD_DIM:(kh + 1) * SW_HEAD_DIM, :].astype(ob_ref.dtype))
        ob_ref[kh * VTB_ROWS + SW_HEAD_DIM:(kh + 1) * VTB_ROWS, :] = ones


def _project_vt(xb, wt_all, layer):
    t = xb.shape[0]
    n_b = SW_KV_HEADS * SW_HEAD_DIM
    return pl.pallas_call(
        _proj_vt_kernel,
        grid=(t // PROJ_TM,),
        in_specs=[
            pl.BlockSpec((PROJ_TM, D_MODEL), lambda i: (i, 0)),
            pl.BlockSpec((1, D_MODEL + n_b, D_MODEL), lambda i: (layer, 0, 0)),
        ],
        out_specs=[pl.BlockSpec((DA_HEADS * VT_ROWS, PROJ_TM), lambda i: (0, i)),
                   pl.BlockSpec((SW_KV_HEADS * VTB_ROWS, PROJ_TM), lambda i: (0, i))],
        out_shape=[jax.ShapeDtypeStruct((DA_HEADS * VT_ROWS, t), _BF16),
                   jax.ShapeDtypeStruct((SW_KV_HEADS * VTB_ROWS, t), _BF16)],
        compiler_params=_params("parallel"),
        name="in_proj_vt",
    )(xb, wt_all)


def _diff_attn_kernel(slope_ref, prm_ref, g_ref, q_ref, k_ref, vt_ref, o_ref,
                      acc_ref, m_ref, qaug_ref, kaug_ref, s_ref, smax_ref, p_ref, alpha_ref, *, tq, tk):
    h = pl.program_id(1)
    i = pl.program_id(2)
    diag = tq // tk
    n_full = i * diag
    lane = lax.broadcasted_iota(jnp.int32, (tq, DA_HEAD_DIM), 1)
    slope_tile = jnp.zeros((tq, DA_HEAD_DIM), _F32)
    slope = 0.0
    for part in range(N_SLOPE_PARTS):
        piece = slope_ref[h, part]
        slope = slope + piece
        slope_tile = jnp.where(lane // 2 == part, piece, slope_tile)
    klane = lax.broadcasted_iota(jnp.int32, (tk, DA_HEAD_DIM), 1)
    kpos = lax.broadcasted_iota(jnp.int32, (tk, DA_HEAD_DIM), 0)
    pos_split = jnp.where(klane % 2 == 0, kpos % BF16_EXACT_INTS, kpos // BF16_EXACT_INTS * BF16_EXACT_INTS)
    pos_tile = jnp.where(klane < 2 * N_SLOPE_PARTS, pos_split, 0)
    for c in range(2):
        kaug_ref[c, :, DA_HEAD_DIM:] = pos_tile.astype(_F32).astype(_BF16)
        qaug_ref[c, :, :DA_HEAD_DIM] = q_ref[:, c * DA_HEAD_DIM:(c + 1) * DA_HEAD_DIM]
        qaug_ref[c, :, DA_HEAD_DIM:] = slope_tile.astype(_BF16)
    m_ref[...] = jnp.full(m_ref.shape, NEG_INF, _F32)
    acc_ref[...] = jnp.zeros(acc_ref.shape, _F32)

    def scores(j):
        start = pl.multiple_of(j * tk, tk)
        for c in range(2):
            kaug_ref[c, :, :DA_HEAD_DIM] = k_ref[pl.ds(start, tk), c * DA_HEAD_DIM:(c + 1) * DA_HEAD_DIM]
            s = lax.dot_general(kaug_ref[c], qaug_ref[c], (((1,), (1,)), ((), ())),
                                preferred_element_type=_F32)
            s_ref[c] = s
            smax_ref[c] = jnp.max(s, axis=0, keepdims=True)

    def weights(j, boundary):
        off = (j * tk - i * tq).astype(_F32) * slope
        for c in range(2):
            s = s_ref[c]
            if boundary is None:
                smax = smax_ref[c]
            else:
                keys = lax.broadcasted_iota(jnp.int32, (tk, tq), 0) + boundary * tk
                queries = lax.broadcasted_iota(jnp.int32, (tk, tq), 1)
                s = jnp.where(keys <= queries, s, NEG_INF)
                smax = jnp.max(s, axis=0, keepdims=True)
            m_old = m_ref[c]
            m_new = jnp.maximum(m_old, smax + off)
            alpha_ref[c] = jnp.exp2(m_old - m_new)
            p_ref[c] = jnp.exp2(s - (m_new - off)).astype(_BF16)
            m_ref[c] = m_new

    def accumulate(j):
        start = pl.multiple_of(j * tk, tk)
        vt = vt_ref[:, pl.ds(start, tk)]
        for c in range(2):
            acc_ref[c] = alpha_ref[c] * acc_ref[c] + jnp.dot(vt, p_ref[c], preferred_element_type=_F32)

    def body(j, carry):
        accumulate(j)
        weights(j + 1, None)
        scores(j + 2)
        return carry

    def enter_boundary():
        weights(n_full, 0)
        if diag > 1:
            scores(n_full + 1)

    scores(0)

    @pl.when(i >= 1)
    def _():
        weights(0, None)
        scores(1)
        lax.fori_loop(0, n_full - 1, body, 0)
        accumulate(n_full - 1)
        enter_boundary()

    @pl.when(i == 0)
    def _():
        enter_boundary()

    for d in range(diag):
        accumulate(n_full + d)
        if d + 1 < diag:
            weights(n_full + d + 1, d + 1)
            if d + 2 < diag:
                scores(n_full + d + 2)

    prm = prm_ref[...]
    lam_init = prm[4:5, 0:1]
    lam = (jnp.exp(jnp.sum(prm[0:1] * prm[1:2], axis=1, keepdims=True))
           - jnp.exp(jnp.sum(prm[2:3] * prm[3:4], axis=1, keepdims=True)) + lam_init)
    o0 = acc_ref[0, :DA_WIDTH, :] / acc_ref[0, DA_WIDTH:DA_WIDTH + 1, :]
    o1 = acc_ref[1, :DA_WIDTH, :] / acc_ref[1, DA_WIDTH:DA_WIDTH + 1, :]
    a = o0 - lam * o1
    ms = jnp.mean(a * a, axis=0, keepdims=True)
    out = a * lax.rsqrt(ms + LN_EPS) * g_ref[...] * (1.0 - lam_init)
    o_ref[...] = out.T.astype(o_ref.dtype)


def _diff_attention(att, vt, slopes, prm, subln_g, batch, seq):
    tq, tk = ATT_TQ, ATT_TK
    assert tq % tk == 0 and seq % tq == 0
    nq = seq // tq
    t = batch * seq
    kernel = functools.partial(_diff_attn_kernel, tq=tq, tk=tk)
    return pl.pallas_call(
        kernel,
        grid=(batch, DA_HEADS, nq),
        in_specs=[
            pl.BlockSpec(memory_space=pltpu.SMEM),
            pl.BlockSpec((8, LANES), lambda b, h, i: (0, 0)),
            pl.BlockSpec((DA_WIDTH, 1), lambda b, h, i: (0, 0)),
            pl.BlockSpec((tq, DA_WIDTH), lambda b, h, i: (b * nq + i, QA_OFF // DA_WIDTH + h)),
            pl.BlockSpec((seq, DA_WIDTH), lambda b, h, i: (b, KA_OFF // DA_WIDTH + h)),
            pl.BlockSpec((VT_ROWS, seq), lambda b, h, i: (h, b)),
        ],
        out_specs=pl.BlockSpec((tq, DA_WIDTH), lambda b, h, i: (b * nq + i, h)),
        out_shape=jax.ShapeDtypeStruct((t, D_MODEL), _BF16),
        scratch_shapes=[
            pltpu.VMEM((2, VT_ROWS, tq), _F32),
            pltpu.VMEM((2, 1, tq), _F32),
            pltpu.VMEM((2, tq, 2 * DA_HEAD_DIM), _BF16),
            pltpu.VMEM((2, tk, 2 * DA_HEAD_DIM), _BF16),
            pltpu.VMEM((2, tk, tq), _F32),
            pltpu.VMEM((2, 1, tq), _F32),
            pltpu.VMEM((2, tk, tq), _BF16),
            pltpu.VMEM((2, 1, tq), _F32),
        ],
        compiler_params=_params("parallel", "parallel", "arbitrary"),
        name="diff_attn",
    )(slopes, prm, subln_g, att, att, vt)


def _swa_kernel(sink_ref, bias_ref, q_ref, kv_ref, kvp_ref, vt_ref, vtp_ref, o_ref):
    i = pl.program_id(1)
    first = (i == 0).astype(jnp.int32)
    nt = (((1,), (1,)), ((), ()))
    kv_all = jnp.concatenate([kvp_ref[...], kv_ref[...]], axis=0)
    vt_all = jnp.concatenate([vtp_ref[...], vt_ref[...]], axis=1)
    def scores(r, kh):
        kwin = kv_all[r * WINDOW:(r + 2) * WINDOW, kh * SW_HEAD_DIM:(kh + 1) * SW_HEAD_DIM]
        qs = jnp.concatenate(
            [q_ref[r * WINDOW:(r + 1) * WINDOW, (kh * SW_GROUP + g) * SW_HEAD_DIM:(kh * SW_GROUP + g + 1) * SW_HEAD_DIM]
             for g in range(SW_GROUP)], axis=0)
        s = lax.dot_general(kwin, qs, nt, preferred_element_type=_F32)
        bias = bias_ref[2 * kh + first] if r == 0 else bias_ref[2 * kh]
        return s + bias

    pairs = [(r, kh) for r in range(SWA_TB // WINDOW) for kh in range(SW_KV_HEADS)]
    s_next = scores(*pairs[0])
    heads = []
    for n, (r, kh) in enumerate(pairs):
        s = s_next
        if n + 1 < len(pairs):
            s_next = scores(*pairs[n + 1])
        vwin = vt_all[kh * VTB_ROWS:(kh + 1) * VTB_ROWS, r * WINDOW:(r + 2) * WINDOW]
        sink = sink_ref[kh]
        m = jnp.maximum(jnp.max(s, axis=0, keepdims=True), sink)
        p = jnp.exp2(s - m)
        ot = jnp.dot(vwin, p.astype(_BF16), preferred_element_type=_F32)
        denom = ot[SW_HEAD_DIM:SW_HEAD_DIM + 1, :] + jnp.exp2(sink - m)
        ot = ot[:SW_HEAD_DIM, :] / denom
        for g in range(SW_GROUP):
            heads.append(ot[:, g * WINDOW:(g + 1) * WINDOW].T)
        if kh == SW_KV_HEADS - 1:
            o_ref[r * WINDOW:(r + 1) * WINDOW, :] = jnp.concatenate(heads, axis=1).astype(o_ref.dtype)
            heads = []


def _swa_bias():
    slopes = 2.0 ** (-8.0 * jnp.arange(1, SW_HEADS + 1, dtype=_F32) / SW_HEADS)
    kj = jnp.arange(2 * WINDOW, dtype=jnp.int32)[:, None]
    qi = jnp.arange(WINDOW, dtype=jnp.int32)[None, :]
    dist = qi + WINDOW - kj
    valid = (dist >= 0) & (dist < WINDOW)
    tables = []
    for kh in range(SW_KV_HEADS):
        for has_prev in (True, False):
            ok = valid if has_prev else valid & (kj >= WINDOW)
            per_head = [jnp.where(ok, -slopes[kh * SW_GROUP + g] * LOG2E * dist.astype(_F32), NEG_INF)
                        for g in range(SW_GROUP)]
            tables.append(jnp.concatenate(per_head, axis=1))
    return jnp.stack(tables)


def _sliding_window_attention(att, vtb, sink_rows, bias, batch, seq):
    tb = SWA_TB
    nb = seq // tb
    ratio = tb // WINDOW
    t = batch * seq
    kv_width = 2 * SW_KV_HEADS * SW_HEAD_DIM
    n_b = SW_KV_HEADS * VTB_ROWS
    q_blk = QB_OFF // D_MODEL
    kv_blk = KVB_OFF // kv_width
    prev = lambda b, i: jnp.maximum((b * nb + i) * ratio - 1, 0)
    return pl.pallas_call(
        _swa_kernel,
        grid=(batch, nb),
        in_specs=[
            pl.BlockSpec((SW_KV_HEADS, 1, SW_GROUP * WINDOW), lambda b, i: (0, 0, 0)),
            pl.BlockSpec((2 * SW_KV_HEADS, 2 * WINDOW, SW_GROUP * WINDOW), lambda b, i: (0, 0, 0)),
            pl.BlockSpec((tb, D_MODEL), lambda b, i: (b * nb + i, q_blk)),
            pl.BlockSpec((tb, kv_width), lambda b, i: (b * nb + i, kv_blk)),
            pl.BlockSpec((WINDOW, kv_width), lambda b, i: (prev(b, i), kv_blk)),
            pl.BlockSpec((n_b, tb), lambda b, i: (0, b * nb + i)),
            pl.BlockSpec((n_b, WINDOW), lambda b, i: (0, prev(b, i))),
        ],
        out_specs=pl.BlockSpec((tb, D_MODEL), lambda b, i: (b * nb + i, 0)),
        out_shape=jax.ShapeDtypeStruct((t, D_MODEL), _BF16),
        compiler_params=_params("parallel", "arbitrary"),
        name="swa",
    )(sink_rows, bias, att, att, att, vtb, vtb)


def _pack_rows(x):
    bits = lax.bitcast_convert_type(x.astype(_BF16).astype(_F32), jnp.uint32)
    return lax.bitcast_convert_type(bits[:, :HALF_D] | (bits[:, HALF_D:] >> 16), jnp.int32)


def _unpack_rows(words):
    bits = lax.bitcast_convert_type(words, jnp.uint32)
    hi = lax.bitcast_convert_type(bits & jnp.uint32(HI16_MASK), _F32)
    lo = lax.bitcast_convert_type(bits << 16, _F32)
    return hi, lo


def _gather_rows(table, idx, n_groups, first, count):
    group_size = idx.shape[0] // n_groups
    n_out = n_groups * count
    width = table.shape[1]
    workers = SC_CORES * SC_SUBCORES
    per_worker = n_out // workers
    chunk_rows = next(c for c in GATHER_CHUNKS if per_worker % (2 * c) == 0)
    assert n_out % workers == 0 and count % per_worker == 0 and first % SC_SLICE_ALIGN == 0
    n_chunks = per_worker // chunk_rows
    mesh = plsc.VectorSubcoreMesh(core_axis_name="c", subcore_axis_name="s",
                                  num_cores=SC_CORES, num_subcores=SC_SUBCORES)

    @functools.partial(
        pl.kernel, mesh=mesh,
        out_type=jax.ShapeDtypeStruct((n_out, width), table.dtype),
        scratch_types=[
            pltpu.VMEM((per_worker,), jnp.int32),
            pltpu.VMEM((2, chunk_rows, width), table.dtype),
            pltpu.SemaphoreType.DMA((2,)),
            pltpu.SemaphoreType.DMA((2,)),
        ],
        name="gather_rows",
    )
    def gather(table_hbm, idx_hbm, out_hbm, idx_v, rows_v, fetch_sem, store_sem):
        base = (lax.axis_index("s") * SC_CORES + lax.axis_index("c")) * per_worker
        src = pl.multiple_of((base // count) * group_size + first + base % count, SC_SLICE_ALIGN)
        pltpu.sync_copy(idx_hbm.at[pl.ds(src, per_worker)], idx_v)

        def fetch(g, buf):
            chunk = idx_v.at[pl.ds(g * chunk_rows, chunk_rows)]
            return pltpu.make_async_copy(table_hbm.at[chunk], rows_v.at[buf], fetch_sem.at[buf])

        def store(g, buf):
            dst = out_hbm.at[pl.ds(base + g * chunk_rows, chunk_rows)]
            return pltpu.make_async_copy(rows_v.at[buf], dst, store_sem.at[buf])

        fetch(0, 0).start()

        @pl.loop(0, n_chunks, step=2)
        def _(g0):
            for buf in range(2):
                g = g0 + buf
                fetch(g, buf).wait()

                @pl.when(g >= 1)
                def _():
                    store(g - 1, 1 - buf).wait()

                @pl.when(g + 1 < n_chunks)
                def _():
                    fetch(g + 1, 1 - buf).start()

                store(g, buf).start()

        store(n_chunks - 1, 1).wait()

    return gather(table, idx)


def _layer_norm(z, g, b):
    mu = jnp.mean(z, axis=1, keepdims=True)
    d = z - mu
    var = jnp.mean(d * d, axis=1, keepdims=True)
    return d * lax.rsqrt(var + LN_EPS) * g + b


def _merge_out_kernel(ga_ref, gb_ref, oa_ref, ob_ref, x_ref, wo_ref, lng_ref, lnb_ref, wr_ref, br_ref,
                      h_ref, hp_ref, idx_ref, gate_ref, cnt_ref, run_ref):
    @pl.when(pl.program_id(0) == 0)
    def _():
        run_ref[...] = jnp.zeros(run_ref.shape, _F32)

    merged = (jax.nn.sigmoid(ga_ref[...].astype(_F32)) * oa_ref[...].astype(_F32)
              + jax.nn.sigmoid(gb_ref[...].astype(_F32)) * ob_ref[...].astype(_F32))
    y = jnp.dot(merged.astype(_BF16), wo_ref[0], preferred_element_type=_F32)
    hn = _layer_norm(DEEPNORM_ALPHA * x_ref[...] + y, lng_ref[0, 0:1, :], lnb_ref[0, 0:1, :])
    h_ref[...] = hn
    hb = hn.astype(_BF16)
    hp_ref[...] = _pack_rows(hn)
    logits = lax.dot_general(wr_ref[0], hb, (((1,), (1,)), ((), ())),
                             preferred_element_type=_F32) + br_ref[0]
    tm = logits.shape[1]
    erow = lax.broadcasted_iota(jnp.int32, (N_EXPERTS, tm), 0)
    cur = logits
    vals, picks, id_rows = [], [], []
    for k in range(TOP_K):
        mx = jnp.max(cur, axis=0, keepdims=True)
        ix = jnp.min(jnp.where(cur == mx, erow, N_EXPERTS), axis=0, keepdims=True)
        vals.append(mx)
        picks.append(erow == ix)
        id_rows.append(ix)
        cur = jnp.where(picks[k], -jnp.inf, cur)
    exps = [jnp.exp(v - vals[0]) for v in vals]
    tot = exps[0] + exps[1] + exps[2] + exps[3]
    gate_rows = [e / tot for e in exps] + [jnp.zeros((LANES - TOP_K, tm), _F32)]
    gate_ref[...] = jnp.concatenate(gate_rows, axis=0).T

    chosen = jnp.zeros((N_EXPERTS, tm), _F32)
    for k in range(TOP_K):
        chosen = chosen + jnp.where(picks[k], 1.0, 0.0)
    r = lax.broadcasted_iota(jnp.int32, (tm, tm), 0)
    c = lax.broadcasted_iota(jnp.int32, (tm, tm), 1)
    earlier = jnp.where(r < c, 1.0, 0.0).astype(_BF16)
    before = jnp.dot(chosen.astype(_BF16), earlier, preferred_element_type=_F32) + run_ref[...]
    rank_rows = [jnp.sum(jnp.where(picks[k], before, 0.0), axis=0, keepdims=True).astype(jnp.int32)
                 for k in range(TOP_K)]
    idx_ref[...] = jnp.concatenate(id_rows + rank_rows, axis=0)
    total = run_ref[...] + jnp.sum(chosen, axis=1, keepdims=True)
    run_ref[...] = total
    cnt_ref[...] = jnp.broadcast_to(total, cnt_ref.shape).astype(jnp.int32)


def _merge_out(gates, o_a, o_b, x, wo_all, ln_g, ln_b, wr_all, br_all, layer):
    t = x.shape[0]
    tm = OUT_TM
    row = lambda i: (i, 0)
    return pl.pallas_call(
        _merge_out_kernel,
        grid=(t // tm,),
        in_specs=[
            pl.BlockSpec((tm, D_MODEL), lambda i: (i, 0)),
            pl.BlockSpec((tm, D_MODEL), lambda i: (i, 1)),
            pl.BlockSpec((tm, D_MODEL), row),
            pl.BlockSpec((tm, D_MODEL), row),
            pl.BlockSpec((tm, D_MODEL), row),
            pl.BlockSpec((1, D_MODEL, D_MODEL), lambda i: (layer, 0, 0)),
            pl.BlockSpec((1, 2, D_MODEL), lambda i: (layer, 0, 0)),
            pl.BlockSpec((1, 2, D_MODEL), lambda i: (layer, 0, 0)),
            pl.BlockSpec((1, N_EXPERTS, D_MODEL), lambda i: (layer, 0, 0)),
            pl.BlockSpec((1, N_EXPERTS, 1), lambda i: (layer, 0, 0)),
        ],
        out_specs=[
            pl.BlockSpec((tm, D_MODEL), row),
            pl.BlockSpec((tm, HALF_D), row),
            pl.BlockSpec((2 * TOP_K, tm), lambda i: (0, i)),
            pl.BlockSpec((tm, LANES), row),
            pl.BlockSpec((N_EXPERTS, LANES), lambda i: (0, 0)),
        ],
        out_shape=[
            jax.ShapeDtypeStruct((t, D_MODEL), _F32),
            jax.ShapeDtypeStruct((t, HALF_D), jnp.int32),
            jax.ShapeDtypeStruct((2 * TOP_K, t), jnp.int32),
            jax.ShapeDtypeStruct((t, LANES), _F32),
            jax.ShapeDtypeStruct((N_EXPERTS, LANES), jnp.int32),
        ],
        scratch_shapes=[pltpu.VMEM((N_EXPERTS, 1), _F32)],
        compiler_params=_params("arbitrary"),
        name="merge_out_ln_router",
    )(gates, gates, o_a, o_b, x, wo_all, ln_g, ln_b, wr_all, br_all)


def _expert_kernel(be_ref, nused_ref, x_ref, wup_ref, bg_ref, bu_ref, wd_ref, bd_ref, *rest, block_off):
    y_ref, wg_scr, wu_scr, wd_scr = rest[-4:]
    i = pl.program_id(0)
    blk = i + block_off
    changed = jnp.logical_or(i == 0, be_ref[blk] != be_ref[jnp.maximum(blk - 1, 0)])
    active = blk < nused_ref[0]

    @pl.when(jnp.logical_and(changed, active))
    def _():
        half = DEINT_TILE // 2
        r = lax.broadcasted_iota(jnp.int32, (DEINT_TILE, DEINT_TILE), 0)
        c = lax.broadcasted_iota(jnp.int32, (DEINT_TILE, DEINT_TILE), 1)
        sel = jnp.where(r == jnp.where(c < half, 2 * c, 2 * (c - half) + 1), 1.0, 0.0).astype(_BF16)
        for t in range(2 * D_FF // DEINT_TILE):
            w = wup_ref[0, 0, :, t * DEINT_TILE:(t + 1) * DEINT_TILE].astype(_BF16)
            de = jnp.dot(w, sel, preferred_element_type=_F32).astype(_BF16)
            wg_scr[:, t * half:(t + 1) * half] = de[:, :half]
            wu_scr[:, t * half:(t + 1) * half] = de[:, half:]
        wd_scr[...] = wd_ref[0, 0].astype(_BF16)

    @pl.when(active)
    def _():
        hi, lo = _unpack_rows(x_ref[...])
        x = jnp.concatenate([hi.astype(_BF16), lo.astype(_BF16)], axis=1)
        gate = jnp.dot(x, wg_scr[...], preferred_element_type=_F32) + bg_ref[0, 0]
        up = jnp.dot(x, wu_scr[...], preferred_element_type=_F32) + bu_ref[0, 0]
        gate = jnp.minimum(gate, SWIGLU_LIMIT)
        up = jnp.clip(up, -SWIGLU_LIMIT, SWIGLU_LIMIT)
        act = (up + 1.0) * (gate * jax.nn.sigmoid(SWIGLU_ALPHA * gate))
        y = jnp.dot(act.astype(_BF16), wd_scr[...], preferred_element_type=_F32) + bd_ref[0, 0]
        y_ref[...] = _pack_rows(y)

    @pl.when(jnp.logical_not(active))
    def _():
        y_ref[...] = jnp.zeros(y_ref.shape, y_ref.dtype)


def _experts(block_expert, n_used, x_part, part, y_prev, w_up, bg_all, bu_all, w_down, bd_all, layer):
    part_blocks = x_part.shape[0] // MOE_BLOCK
    n_rows = block_expert.shape[0] * MOE_BLOCK
    off = part * part_blocks
    pick = lambda i, be, nu: (layer, be[i + off], 0, 0)
    bspec = pl.BlockSpec((1, 1, 1, D_FF), pick)
    in_specs = [
        pl.BlockSpec((MOE_BLOCK, HALF_D), lambda i, be, nu: (i, 0)),
        pl.BlockSpec((1, 1, D_MODEL, 2 * D_FF), pick),
        bspec, bspec,
        pl.BlockSpec((1, 1, D_FF, D_MODEL), pick),
        pl.BlockSpec((1, 1, 1, D_MODEL), pick),
    ]
    operands = [block_expert, n_used, x_part, w_up, bg_all, bu_all, w_down, bd_all]
    aliases = {}
    if y_prev is not None:
        in_specs.append(pl.BlockSpec(memory_space=pl.ANY))
        aliases = {len(operands): 0}
        operands.append(y_prev)
    grid_spec = pltpu.PrefetchScalarGridSpec(
        num_scalar_prefetch=2,
        grid=(part_blocks,),
        in_specs=in_specs,
        out_specs=pl.BlockSpec((MOE_BLOCK, HALF_D), lambda i, be, nu: (i + off, 0)),
        scratch_shapes=[
            pltpu.VMEM((D_MODEL, D_FF), _BF16),
            pltpu.VMEM((D_MODEL, D_FF), _BF16),
            pltpu.VMEM((D_FF, D_MODEL), _BF16),
        ],
    )
    return pl.pallas_call(
        functools.partial(_expert_kernel, block_off=off),
        grid_spec=grid_spec,
        out_shape=jax.ShapeDtypeStruct((n_rows, HALF_D), jnp.int32),
        input_output_aliases=aliases,
        compiler_params=_params("arbitrary"),
        name="experts",
    )(*operands)


def _combine_kernel(h_ref, yg_ref, gate_ref, lng_ref, lnb_ref, *rest):
    x_ref, xb_ref = rest[-2:]
    gates = gate_ref[...]
    y = None
    for k in range(TOP_K):
        hi, lo = _unpack_rows(yg_ref[k])
        yk = gates[:, k:k + 1] * jnp.concatenate([hi, lo], axis=1)
        y = yk if y is None else y + yk
    xn = _layer_norm(DEEPNORM_ALPHA * h_ref[...] + y, lng_ref[0, 1:2, :], lnb_ref[0, 1:2, :])
    x_ref[...] = xn
    xb_ref[...] = xn.astype(_BF16)


def _combine(h, yg_part, gates, ln_g, ln_b, layer, part, prev):
    t = h.shape[0]
    tm = LN2_TM
    off = part * (yg_part.shape[1] // tm)
    row = lambda i: (i + off, 0)
    in_specs = [
        pl.BlockSpec((tm, D_MODEL), row),
        pl.BlockSpec((TOP_K, tm, HALF_D), lambda i: (0, i, 0)),
        pl.BlockSpec((tm, LANES), row),
        pl.BlockSpec((1, 2, D_MODEL), lambda i: (layer, 0, 0)),
        pl.BlockSpec((1, 2, D_MODEL), lambda i: (layer, 0, 0)),
    ]
    operands = [h, yg_part, gates, ln_g, ln_b]
    aliases = {}
    if prev is not None:
        in_specs += [pl.BlockSpec(memory_space=pl.ANY), pl.BlockSpec(memory_space=pl.ANY)]
        aliases = {len(operands): 0, len(operands) + 1: 1}
        operands += list(prev)
    return pl.pallas_call(
        _combine_kernel,
        grid=(yg_part.shape[1] // tm,),
        in_specs=in_specs,
        out_specs=[pl.BlockSpec((tm, D_MODEL), row), pl.BlockSpec((tm, D_MODEL), row)],
        out_shape=[jax.ShapeDtypeStruct((t, D_MODEL), _F32), jax.ShapeDtypeStruct((t, D_MODEL), _BF16)],
        input_output_aliases=aliases,
        compiler_params=_params("parallel"),
        name="combine_ln",
    )(*operands)


def _dispatch_tables(route, counts, t):
    tk = t * TOP_K
    experts = route[:TOP_K]
    ranks = route[TOP_K:]
    padded = (counts + MOE_BLOCK - 1) // MOE_BLOCK * MOE_BLOCK
    pad_end = jnp.cumsum(padded)
    pad_start = pad_end - padded
    start = jnp.cumsum(counts) - counts
    expert_ids = jnp.arange(N_EXPERTS, dtype=jnp.int32)
    dest = jnp.sum(jnp.where(experts[..., None] == expert_ids, pad_start, 0), axis=-1) + ranks
    n_blocks = (tk + N_EXPERTS * (MOE_BLOCK - 1) + MOE_BLOCK - 1) // MOE_BLOCK
    block_start = jnp.arange(n_blocks, dtype=jnp.int32) * MOE_BLOCK
    block_expert = jnp.minimum(
        jnp.sum((pad_end[None, :] <= block_start[:, None]).astype(jnp.int32), axis=1), N_EXPERTS - 1)
    n_used = (pad_end[-1:] // MOE_BLOCK).astype(jnp.int32)
    order = jnp.argsort(experts.T.reshape(tk))
    in_block = jnp.arange(MOE_BLOCK, dtype=jnp.int32)[None, :]
    r = (block_start - pad_start[block_expert])[:, None] + in_block
    valid = r < counts[block_expert][:, None]
    src = jnp.where(valid, start[block_expert][:, None] + r, 0).reshape(n_blocks * MOE_BLOCK)
    filler = jnp.arange(n_blocks * MOE_BLOCK, dtype=jnp.int32) % t
    buf_tok = jnp.where(valid.reshape(-1), order[src] // TOP_K, filler)
    return buf_tok, dest, block_expert, n_used


def kernel(x, w_in, w_o, lambda_qk, subln_g, sinks, ln_g, ln_b, w_router, b_router, w_up, b_up, w_down, b_down):
    batch, seq, _ = x.shape
    t = batch * seq
    assert seq % ATT_TQ == 0 and seq % SWA_TB == 0 and t % PROJ_TM == 0

    va_lo, va_hi = 2 * D_MODEL, 3 * D_MODEL
    gates_lo = w_in.shape[-1] - GATE_COLS
    w_att_b = jnp.concatenate([w_in[..., :va_lo], w_in[..., va_hi:gates_lo]], axis=-1).astype(_BF16)
    vb_lo = gates_lo - SW_KV_HEADS * SW_HEAD_DIM
    w_vt_b = jnp.swapaxes(jnp.concatenate([w_in[..., va_lo:va_hi], w_in[..., vb_lo:gates_lo]], axis=-1),
                          1, 2).astype(_BF16)
    w_gates_b = w_in[..., gates_lo:].astype(_BF16)
    w_o_b = w_o.astype(_BF16)
    w_r_b = jnp.swapaxes(w_router, 1, 2).astype(_BF16)
    b_gate = b_up[..., 0::2].reshape(DEPTH, N_EXPERTS, 1, D_FF)
    b_upp = b_up[..., 1::2].reshape(DEPTH, N_EXPERTS, 1, D_FF)
    b_down4 = b_down.reshape(DEPTH, N_EXPERTS, 1, D_MODEL)
    b_router3 = b_router.astype(_F32).reshape(DEPTH, N_EXPERTS, 1)

    att_scale = jnp.concatenate([
        jnp.full((D_MODEL,), DA_HEAD_DIM ** -0.5 * LOG2E, _F32), jnp.ones((D_MODEL,), _F32),
        jnp.full((D_MODEL,), SW_HEAD_DIM ** -0.5 * LOG2E, _F32),
        jnp.ones((2 * SW_KV_HEADS * SW_HEAD_DIM,), _F32)]).reshape(1, ATT_COLS)
    gate_scale = jnp.ones((1, GATE_COLS), _F32)
    rest = 2.0 ** (-8.0 * jnp.arange(1, DA_HEADS + 1, dtype=_F32) / DA_HEADS) * LOG2E
    pieces = []
    for _ in range(N_SLOPE_PARTS):
        piece = rest.astype(_BF16).astype(_F32)
        pieces.append(piece)
        rest = rest - piece
    da_slopes = jnp.stack(pieces, axis=1)
    swa_bias = _swa_bias()

    xf = x.reshape(t, D_MODEL)
    xb = xf.astype(_BF16)
    for l in range(DEPTH):
        lam_init = 0.8 - 0.6 * math.exp(-0.3 * l)
        prm = jnp.concatenate([lambda_qk[l].astype(_F32), jnp.full((4, DA_HEAD_DIM), lam_init, _F32)], axis=0)
        att = _project(xb, w_att_b, l, ATT_COLS // 2, att_scale, _BF16)
        gates = _project(xb, w_gates_b, l, GATE_COLS // 2, gate_scale, _BF16)
        vt, vtb = _project_vt(xb, w_vt_b, l)
        o_a = _diff_attention(att, vt, da_slopes, prm, subln_g[l].reshape(DA_WIDTH, 1), batch, seq)
        sink_rows = jnp.repeat(sinks[l].astype(_F32).reshape(SW_KV_HEADS, SW_GROUP) * LOG2E, WINDOW, axis=1)
        o_b = _sliding_window_attention(att, vtb, sink_rows.reshape(SW_KV_HEADS, 1, SW_GROUP * WINDOW),
                                        swa_bias, batch, seq)
        h, hp, route, top_gate, counts = _merge_out(gates, o_a, o_b, xf, w_o_b, ln_g, ln_b, w_r_b, b_router3, l)
        buf_tok, pos, block_expert, n_used = _dispatch_tables(route, counts[:, 0], t)
        part_rows = buf_tok.shape[0] // MOE_PARTS
        yb = None
        for part in range(MOE_PARTS):
            x_rows = _gather_rows(hp, buf_tok, 1, part * part_rows, part_rows)
            yb = _experts(block_expert, n_used, x_rows, part, yb, w_up, b_gate, b_upp, w_down, b_down4, l)
        part_tok = t // MOE_PARTS
        pos_flat = pos.reshape(TOP_K * t)
        outs = None
        for part in range(MOE_PARTS):
            yg = _gather_rows(yb, pos_flat, TOP_K, part * part_tok, part_tok).reshape(TOP_K, part_tok, HALF_D)
            outs = _combine(h, yg, top_gate, ln_g, ln_b, l, part, outs)
        xf, xb = outs
    return xf.reshape(batch, seq, D_MODEL)
```

```python
import functools
import math

import jax
import jax.numpy as jnp
from jax import lax
from jax.experimental import pallas as pl
from jax.experimental.pallas import tpu as pltpu
from jax.experimental.pallas import tpu_sc as plsc

D_MODEL = 1024
DEPTH = 4
DA_HEAD_DIM = 128
DA_HEADS = 4
DA_WIDTH = 2 * DA_HEAD_DIM
SW_HEAD_DIM = 64
SW_HEADS = 16
SW_KV_HEADS = 2
SW_GROUP = SW_HEADS // SW_KV_HEADS
WINDOW = 128
N_EXPERTS = 32
TOP_K = 4
D_FF = D_MODEL
SWIGLU_LIMIT = 7.0
SWIGLU_ALPHA = 1.702
LN_EPS = 1e-5
DEEPNORM_ALPHA = (2.0 * DEPTH) ** 0.25
NEG_INF = -1e30

QA_OFF, KA_OFF, QB_OFF, KVB_OFF = 0, D_MODEL, 2 * D_MODEL, 3 * D_MODEL
ATT_COLS = KVB_OFF + 2 * SW_KV_HEADS * SW_HEAD_DIM
GATE_COLS = 2 * D_MODEL
HALF_D = D_MODEL // 2
HI16_MASK = 0xFFFF0000
LOG2E = math.log2(math.e)
N_SLOPE_PARTS = 3
BF16_EXACT_INTS = 256

LANES = 128
SUBLANES_F32 = 8
SUBLANES_BF16 = 16
MXU_TILE = 256
VMEM_BYTES = 64 * 1024 * 1024
SC_CORES = 2
SC_SUBCORES = 16
SC_SLICE_ALIGN = 8
GATHER_CHUNKS = (64, 32, 16)

VT_ROWS = DA_WIDTH + SUBLANES_BF16
VTB_ROWS = SW_HEAD_DIM + SUBLANES_BF16

PROJ_TM = 1024
ATT_TQ = 1024
ATT_TK = 512
SWA_TB = 512
OUT_TM = 512
MOE_BLOCK = 256
MOE_SLICE_DENOM = 4
MOE_SLICES = ((0, 1), (1, 4))
DEINT_TILE = MXU_TILE
LN2_TM = 512
VMEM_LIMIT = VMEM_BYTES * 7 // 8

_F32 = jnp.float32
_BF16 = jnp.bfloat16


def _params(*sem):
    return pltpu.CompilerParams(dimension_semantics=sem, vmem_limit_bytes=VMEM_LIMIT)


def _proj_kernel(x_ref, w_ref, s_ref, o_ref):
    acc = jnp.dot(x_ref[...], w_ref[0], preferred_element_type=_F32)
    o_ref[...] = (acc * s_ref[...]).astype(o_ref.dtype)


def _project(xb, w_all, layer, tn, scale, out_dtype):
    t = xb.shape[0]
    n_cols = w_all.shape[-1]
    assert n_cols % tn == 0 and t % PROJ_TM == 0
    return pl.pallas_call(
        _proj_kernel,
        grid=(n_cols // tn, t // PROJ_TM),
        in_specs=[
            pl.BlockSpec((PROJ_TM, D_MODEL), lambda j, i: (i, 0)),
            pl.BlockSpec((1, D_MODEL, tn), lambda j, i: (layer, 0, j)),
            pl.BlockSpec((1, tn), lambda j, i: (0, j)),
        ],
        out_specs=pl.BlockSpec((PROJ_TM, tn), lambda j, i: (i, j)),
        out_shape=jax.ShapeDtypeStruct((t, n_cols), out_dtype),
        compiler_params=_params("parallel", "parallel"),
        name="in_proj",
    )(xb, w_all, scale)


def _proj_vt_kernel(x_ref, wt_ref, oa_ref, ob_ref):
    x = x_ref[...]
    nt = (((1,), (1,)), ((), ()))
    ones = jnp.ones((SUBLANES_BF16, x.shape[0]), oa_ref.dtype)
    for h in range(DA_HEADS):
        vt = lax.dot_general(wt_ref[0, h * DA_WIDTH:(h + 1) * DA_WIDTH, :], x, nt, preferred_element_type=_F32)
        oa_ref[h * VT_ROWS:h * VT_ROWS + DA_WIDTH, :] = vt.astype(oa_ref.dtype)
        oa_ref[h * VT_ROWS + DA_WIDTH:(h + 1) * VT_ROWS, :] = ones
    vb = lax.dot_general(wt_ref[0, D_MODEL:, :], x, nt, preferred_element_type=_F32)
    for kh in range(SW_KV_HEADS):
        ob_ref[kh * VTB_ROWS:kh * VTB_ROWS + SW_HEAD_DIM, :] = (
            vb[kh * SW_HEAD_DIM:(kh + 1) * SW_HEAD_DIM, :].astype(ob_ref.dtype))
        ob_ref[kh * VTB_ROWS + SW_HEAD_DIM:(kh + 1) * VTB_ROWS, :] = ones


def _project_vt(xb, wt_all, layer):
    t = xb.shape[0]
    n_b = SW_KV_HEADS * SW_HEAD_DIM
    return pl.pallas_call(
        _proj_vt_kernel,
        grid=(t // PROJ_TM,),
        in_specs=[
            pl.BlockSpec((PROJ_TM, D_MODEL), lambda i: (i, 0)),
            pl.BlockSpec((1, D_MODEL + n_b, D_MODEL), lambda i: (layer, 0, 0)),
        ],
        out_specs=[pl.BlockSpec((DA_HEADS * VT_ROWS, PROJ_TM), lambda i: (0, i)),
                   pl.BlockSpec((SW_KV_HEADS * VTB_ROWS, PROJ_TM), lambda i: (0, i))],
        out_shape=[jax.ShapeDtypeStruct((DA_HEADS * VT_ROWS, t), _BF16),
                   jax.ShapeDtypeStruct((SW_KV_HEADS * VTB_ROWS, t), _BF16)],
        compiler_params=_params("parallel"),
        name="in_proj_vt",
    )(xb, wt_all)


def _diff_attn_kernel(slope_ref, prm_ref, g_ref, q_ref, k_ref, vt_ref, o_ref,
                      acc_ref, m_ref, qaug_ref, kaug_ref, s_ref, smax_ref, p_ref, alpha_ref, *, tq, tk):
    h = pl.program_id(1)
    i = pl.program_id(2)
    diag = tq // tk
    n_full = i * diag
    lane = lax.broadcasted_iota(jnp.int32, (tq, DA_HEAD_DIM), 1)
    slope_tile = jnp.zeros((tq, DA_HEAD_DIM), _F32)
    slope = 0.0
    for part in range(N_SLOPE_PARTS):
        piece = slope_ref[h, part]
        slope = slope + piece
        slope_tile = jnp.where(lane // 2 == part, piece, slope_tile)
    klane = lax.broadcasted_iota(jnp.int32, (tk, DA_HEAD_DIM), 1)
    kpos = lax.broadcasted_iota(jnp.int32, (tk, DA_HEAD_DIM), 0)
    pos_split = jnp.where(klane % 2 == 0, kpos % BF16_EXACT_INTS, kpos // BF16_EXACT_INTS * BF16_EXACT_INTS)
    pos_tile = jnp.where(klane < 2 * N_SLOPE_PARTS, pos_split, 0)
    for c in range(2):
        kaug_ref[c, :, DA_HEAD_DIM:] = pos_tile.astype(_F32).astype(_BF16)
        qaug_ref[c, :, :DA_HEAD_DIM] = q_ref[:, c * DA_HEAD_DIM:(c + 1) * DA_HEAD_DIM]
        qaug_ref[c, :, DA_HEAD_DIM:] = slope_tile.astype(_BF16)
    m_ref[...] = jnp.full(m_ref.shape, NEG_INF, _F32)
    acc_ref[...] = jnp.zeros(acc_ref.shape, _F32)

    def scores(j):
        start = pl.multiple_of(j * tk, tk)
        for c in range(2):
            kaug_ref[c, :, :DA_HEAD_DIM] = k_ref[pl.ds(start, tk), c * DA_HEAD_DIM:(c + 1) * DA_HEAD_DIM]
            s = lax.dot_general(kaug_ref[c], qaug_ref[c], (((1,), (1,)), ((), ())),
                                preferred_element_type=_F32)
            s_ref[c] = s
            smax_ref[c] = jnp.max(s, axis=0, keepdims=True)

    def weights(j, boundary):
        off = (j * tk - i * tq).astype(_F32) * slope
        for c in range(2):
            s = s_ref[c]
            if boundary is None:
                smax = smax_ref[c]
            else:
                keys = lax.broadcasted_iota(jnp.int32, (tk, tq), 0) + boundary * tk
                queries = lax.broadcasted_iota(jnp.int32, (tk, tq), 1)
                s = jnp.where(keys <= queries, s, NEG_INF)
                smax = jnp.max(s, axis=0, keepdims=True)
            m_old = m_ref[c]
            m_new = jnp.maximum(m_old, smax + off)
            alpha_ref[c] = jnp.exp2(m_old - m_new)
            p_ref[c] = jnp.exp2(s - (m_new - off)).astype(_BF16)
            m_ref[c] = m_new

    def accumulate(j):
        start = pl.multiple_of(j * tk, tk)
        vt = vt_ref[:, pl.ds(start, tk)]
        for c in range(2):
            acc_ref[c] = alpha_ref[c] * acc_ref[c] + jnp.dot(vt, p_ref[c], preferred_element_type=_F32)

    def body(j, carry):
        accumulate(j)
        weights(j + 1, None)
        scores(j + 2)
        return carry

    def enter_boundary():
        weights(n_full, 0)
        if diag > 1:
            scores(n_full + 1)

    scores(0)

    @pl.when(i >= 1)
    def _():
        weights(0, None)
        scores(1)
        lax.fori_loop(0, n_full - 1, body, 0)
        accumulate(n_full - 1)
        enter_boundary()

    @pl.when(i == 0)
    def _():
        enter_boundary()

    for d in range(diag):
        accumulate(n_full + d)
        if d + 1 < diag:
            weights(n_full + d + 1, d + 1)
            if d + 2 < diag:
                scores(n_full + d + 2)

    prm = prm_ref[...]
    lam_init = prm[4:5, 0:1]
    lam = (jnp.exp(jnp.sum(prm[0:1] * prm[1:2], axis=1, keepdims=True))
           - jnp.exp(jnp.sum(prm[2:3] * prm[3:4], axis=1, keepdims=True)) + lam_init)
    o0 = acc_ref[0, :DA_WIDTH, :] / acc_ref[0, DA_WIDTH:DA_WIDTH + 1, :]
    o1 = acc_ref[1, :DA_WIDTH, :] / acc_ref[1, DA_WIDTH:DA_WIDTH + 1, :]
    a = o0 - lam * o1
    ms = jnp.mean(a * a, axis=0, keepdims=True)
    out = a * lax.rsqrt(ms + LN_EPS) * g_ref[...] * (1.0 - lam_init)
    o_ref[...] = out.T.astype(o_ref.dtype)


def _diff_attention(att, vt, slopes, prm, subln_g, batch, seq):
    tq, tk = ATT_TQ, ATT_TK
    assert tq % tk == 0 and seq % tq == 0
    nq = seq // tq
    t = batch * seq
    kernel = functools.partial(_diff_attn_kernel, tq=tq, tk=tk)
    return pl.pallas_call(
        kernel,
        grid=(batch, DA_HEADS, nq),
        in_specs=[
            pl.BlockSpec(memory_space=pltpu.SMEM),
            pl.BlockSpec((SUBLANES_F32, LANES), lambda b, h, i: (0, 0)),
            pl.BlockSpec((DA_WIDTH, 1), lambda b, h, i: (0, 0)),
            pl.BlockSpec((tq, DA_WIDTH), lambda b, h, i: (b * nq + i, QA_OFF // DA_WIDTH + h)),
            pl.BlockSpec((seq, DA_WIDTH), lambda b, h, i: (b, KA_OFF // DA_WIDTH + h)),
            pl.BlockSpec((VT_ROWS, seq), lambda b, h, i: (h, b)),
        ],
        out_specs=pl.BlockSpec((tq, DA_WIDTH), lambda b, h, i: (b * nq + i, h)),
        out_shape=jax.ShapeDtypeStruct((t, D_MODEL), _BF16),
        scratch_shapes=[
            pltpu.VMEM((2, VT_ROWS, tq), _F32),
            pltpu.VMEM((2, 1, tq), _F32),
            pltpu.VMEM((2, tq, 2 * DA_HEAD_DIM), _BF16),
            pltpu.VMEM((2, tk, 2 * DA_HEAD_DIM), _BF16),
            pltpu.VMEM((2, tk, tq), _F32),
            pltpu.VMEM((2, 1, tq), _F32),
            pltpu.VMEM((2, tk, tq), _BF16),
            pltpu.VMEM((2, 1, tq), _F32),
        ],
        compiler_params=_params("parallel", "parallel", "arbitrary"),
        name="diff_attn",
    )(slopes, prm, subln_g, att, att, vt)


def _swa_kernel(sink_ref, bias_ref, q_ref, kv_ref, kvp_ref, vt_ref, vtp_ref, o_ref):
    i = pl.program_id(1)
    first = (i == 0).astype(jnp.int32)
    nt = (((1,), (1,)), ((), ()))
    kv_all = jnp.concatenate([kvp_ref[...], kv_ref[...]], axis=0)
    vt_all = jnp.concatenate([vtp_ref[...], vt_ref[...]], axis=1)
    def scores(r, kh):
        kwin = kv_all[r * WINDOW:(r + 2) * WINDOW, kh * SW_HEAD_DIM:(kh + 1) * SW_HEAD_DIM]
        qs = jnp.concatenate(
            [q_ref[r * WINDOW:(r + 1) * WINDOW, (kh * SW_GROUP + g) * SW_HEAD_DIM:(kh * SW_GROUP + g + 1) * SW_HEAD_DIM]
             for g in range(SW_GROUP)], axis=0)
        s = lax.dot_general(kwin, qs, nt, preferred_element_type=_F32)
        bias = bias_ref[2 * kh + first] if r == 0 else bias_ref[2 * kh]
        return s + bias

    pairs = [(r, kh) for r in range(SWA_TB // WINDOW) for kh in range(SW_KV_HEADS)]
    s_next = scores(*pairs[0])
    heads = []
    for n, (r, kh) in enumerate(pairs):
        s = s_next
        if n + 1 < len(pairs):
            s_next = scores(*pairs[n + 1])
        vwin = vt_all[kh * VTB_ROWS:(kh + 1) * VTB_ROWS, r * WINDOW:(r + 2) * WINDOW]
        sink = sink_ref[kh]
        m = jnp.maximum(jnp.max(s, axis=0, keepdims=True), sink)
        p = jnp.exp2(s - m)
        ot = jnp.dot(vwin, p.astype(_BF16), preferred_element_type=_F32)
        denom = ot[SW_HEAD_DIM:SW_HEAD_DIM + 1, :] + jnp.exp2(sink - m)
        ot = ot[:SW_HEAD_DIM, :] / denom
        for g in range(SW_GROUP):
            heads.append(ot[:, g * WINDOW:(g + 1) * WINDOW].T)
        if kh == SW_KV_HEADS - 1:
            o_ref[r * WINDOW:(r + 1) * WINDOW, :] = jnp.concatenate(heads, axis=1).astype(o_ref.dtype)
            heads = []


def _swa_bias():
    slopes = 2.0 ** (-8.0 * jnp.arange(1, SW_HEADS + 1, dtype=_F32) / SW_HEADS)
    kj = jnp.arange(2 * WINDOW, dtype=jnp.int32)[:, None]
    qi = jnp.arange(WINDOW, dtype=jnp.int32)[None, :]
    dist = qi + WINDOW - kj
    valid = (dist >= 0) & (dist < WINDOW)
    tables = []
    for kh in range(SW_KV_HEADS):
        for has_prev in (True, False):
            ok = valid if has_prev else valid & (kj >= WINDOW)
            per_head = [jnp.where(ok, -slopes[kh * SW_GROUP + g] * LOG2E * dist.astype(_F32), NEG_INF)
                        for g in range(SW_GROUP)]
            tables.append(jnp.concatenate(per_head, axis=1))
    return jnp.stack(tables)


def _sliding_window_attention(att, vtb, sink_rows, bias, batch, seq):
    tb = SWA_TB
    nb = seq // tb
    ratio = tb // WINDOW
    t = batch * seq
    kv_width = 2 * SW_KV_HEADS * SW_HEAD_DIM
    n_b = SW_KV_HEADS * VTB_ROWS
    q_blk = QB_OFF // D_MODEL
    kv_blk = KVB_OFF // kv_width
    prev = lambda b, i: jnp.maximum((b * nb + i) * ratio - 1, 0)
    return pl.pallas_call(
        _swa_kernel,
        grid=(batch, nb),
        in_specs=[
            pl.BlockSpec((SW_KV_HEADS, 1, SW_GROUP * WINDOW), lambda b, i: (0, 0, 0)),
            pl.BlockSpec((2 * SW_KV_HEADS, 2 * WINDOW, SW_GROUP * WINDOW), lambda b, i: (0, 0, 0)),
            pl.BlockSpec((tb, D_MODEL), lambda b, i: (b * nb + i, q_blk)),
            pl.BlockSpec((tb, kv_width), lambda b, i: (b * nb + i, kv_blk)),
            pl.BlockSpec((WINDOW, kv_width), lambda b, i: (prev(b, i), kv_blk)),
            pl.BlockSpec((n_b, tb), lambda b, i: (0, b * nb + i)),
            pl.BlockSpec((n_b, WINDOW), lambda b, i: (0, prev(b, i))),
        ],
        out_specs=pl.BlockSpec((tb, D_MODEL), lambda b, i: (b * nb + i, 0)),
        out_shape=jax.ShapeDtypeStruct((t, D_MODEL), _BF16),
        compiler_params=_params("parallel", "arbitrary"),
        name="swa",
    )(sink_rows, bias, att, att, att, vtb, vtb)


def _pack_rows(x):
    bits = lax.bitcast_convert_type(x.astype(_BF16).astype(_F32), jnp.uint32)
    return lax.bitcast_convert_type(bits[:, :HALF_D] | (bits[:, HALF_D:] >> 16), jnp.int32)


def _unpack_rows(words):
    bits = lax.bitcast_convert_type(words, jnp.uint32)
    hi = lax.bitcast_convert_type(bits & jnp.uint32(HI16_MASK), _F32)
    lo = lax.bitcast_convert_type(bits << 16, _F32)
    return hi, lo


def _gather_rows(table, idx, n_groups, first, count):
    group_size = idx.shape[0] // n_groups
    n_out = n_groups * count
    width = table.shape[1]
    workers = SC_CORES * SC_SUBCORES
    per_worker = n_out // workers
    chunk_rows = next(c for c in GATHER_CHUNKS if per_worker % (2 * c) == 0)
    assert n_out % workers == 0 and count % per_worker == 0 and first % SC_SLICE_ALIGN == 0
    n_chunks = per_worker // chunk_rows
    mesh = plsc.VectorSubcoreMesh(core_axis_name="c", subcore_axis_name="s",
                                  num_cores=SC_CORES, num_subcores=SC_SUBCORES)

    @functools.partial(
        pl.kernel, mesh=mesh,
        out_type=jax.ShapeDtypeStruct((n_out, width), table.dtype),
        scratch_types=[
            pltpu.VMEM((per_worker,), jnp.int32),
            pltpu.VMEM((2, chunk_rows, width), table.dtype),
            pltpu.SemaphoreType.DMA((2,)),
            pltpu.SemaphoreType.DMA((2,)),
        ],
        name="gather_rows",
    )
    def gather(table_hbm, idx_hbm, out_hbm, idx_v, rows_v, fetch_sem, store_sem):
        base = (lax.axis_index("s") * SC_CORES + lax.axis_index("c")) * per_worker
        src = pl.multiple_of((base // count) * group_size + first + base % count, SC_SLICE_ALIGN)
        pltpu.sync_copy(idx_hbm.at[pl.ds(src, per_worker)], idx_v)

        def fetch(g, buf):
            chunk = idx_v.at[pl.ds(g * chunk_rows, chunk_rows)]
            return pltpu.make_async_copy(table_hbm.at[chunk], rows_v.at[buf], fetch_sem.at[buf])

        def store(g, buf):
            dst = out_hbm.at[pl.ds(base + g * chunk_rows, chunk_rows)]
            return pltpu.make_async_copy(rows_v.at[buf], dst, store_sem.at[buf])

        fetch(0, 0).start()

        @pl.loop(0, n_chunks, step=2)
        def _(g0):
            for buf in range(2):
                g = g0 + buf
                fetch(g, buf).wait()

                @pl.when(g >= 1)
                def _():
                    store(g - 1, 1 - buf).wait()

                @pl.when(g + 1 < n_chunks)
                def _():
                    fetch(g + 1, 1 - buf).start()

                store(g, buf).start()

        store(n_chunks - 1, 1).wait()

    return gather(table, idx)


def _layer_norm(z, g, b):
    mu = jnp.mean(z, axis=1, keepdims=True)
    d = z - mu
    var = jnp.mean(d * d, axis=1, keepdims=True)
    return d * lax.rsqrt(var + LN_EPS) * g + b


def _merge_out_kernel(ga_ref, gb_ref, oa_ref, ob_ref, x_ref, wo_ref, lng_ref, lnb_ref, wr_ref, br_ref,
                      h_ref, hp_ref, idx_ref, gate_ref, cnt_ref, run_ref):
    @pl.when(pl.program_id(0) == 0)
    def _():
        run_ref[...] = jnp.zeros(run_ref.shape, _F32)

    merged = (jax.nn.sigmoid(ga_ref[...].astype(_F32)) * oa_ref[...].astype(_F32)
              + jax.nn.sigmoid(gb_ref[...].astype(_F32)) * ob_ref[...].astype(_F32))
    y = jnp.dot(merged.astype(_BF16), wo_ref[0], preferred_element_type=_F32)
    hn = _layer_norm(DEEPNORM_ALPHA * x_ref[...] + y, lng_ref[0, 0:1, :], lnb_ref[0, 0:1, :])
    h_ref[...] = hn
    hb = hn.astype(_BF16)
    hp_ref[...] = _pack_rows(hn)
    logits = lax.dot_general(wr_ref[0], hb, (((1,), (1,)), ((), ())),
                             preferred_element_type=_F32) + br_ref[0]
    tm = logits.shape[1]
    erow = lax.broadcasted_iota(jnp.int32, (N_EXPERTS, tm), 0)
    cur = logits
    vals, picks, id_rows = [], [], []
    for k in range(TOP_K):
        mx = jnp.max(cur, axis=0, keepdims=True)
        ix = jnp.min(jnp.where(cur == mx, erow, N_EXPERTS), axis=0, keepdims=True)
        vals.append(mx)
        picks.append(erow == ix)
        id_rows.append(ix)
        cur = jnp.where(picks[k], -jnp.inf, cur)
    exps = [jnp.exp(v - vals[0]) for v in vals]
    tot = exps[0] + exps[1] + exps[2] + exps[3]
    gate_rows = [e / tot for e in exps] + [jnp.zeros((LANES - TOP_K, tm), _F32)]
    gate_ref[...] = jnp.concatenate(gate_rows, axis=0).T

    chosen = jnp.zeros((N_EXPERTS, tm), _F32)
    for k in range(TOP_K):
        chosen = chosen + jnp.where(picks[k], 1.0, 0.0)
    r = lax.broadcasted_iota(jnp.int32, (tm, tm), 0)
    c = lax.broadcasted_iota(jnp.int32, (tm, tm), 1)
    earlier = jnp.where(r < c, 1.0, 0.0).astype(_BF16)
    before = jnp.dot(chosen.astype(_BF16), earlier, preferred_element_type=_F32) + run_ref[...]
    rank_rows = [jnp.sum(jnp.where(picks[k], before, 0.0), axis=0, keepdims=True).astype(jnp.int32)
                 for k in range(TOP_K)]
    idx_ref[...] = jnp.concatenate(id_rows + rank_rows, axis=0)
    total = run_ref[...] + jnp.sum(chosen, axis=1, keepdims=True)
    run_ref[...] = total
    cnt_ref[...] = jnp.broadcast_to(total, cnt_ref.shape).astype(jnp.int32)


def _merge_out(gates, o_a, o_b, x, wo_all, ln_g, ln_b, wr_all, br_all, layer):
    t = x.shape[0]
    tm = OUT_TM
    row = lambda i: (i, 0)
    return pl.pallas_call(
        _merge_out_kernel,
        grid=(t // tm,),
        in_specs=[
            pl.BlockSpec((tm, D_MODEL), lambda i: (i, 0)),
            pl.BlockSpec((tm, D_MODEL), lambda i: (i, 1)),
            pl.BlockSpec((tm, D_MODEL), row),
            pl.BlockSpec((tm, D_MODEL), row),
            pl.BlockSpec((tm, D_MODEL), row),
            pl.BlockSpec((1, D_MODEL, D_MODEL), lambda i: (layer, 0, 0)),
            pl.BlockSpec((1, 2, D_MODEL), lambda i: (layer, 0, 0)),
            pl.BlockSpec((1, 2, D_MODEL), lambda i: (layer, 0, 0)),
            pl.BlockSpec((1, N_EXPERTS, D_MODEL), lambda i: (layer, 0, 0)),
            pl.BlockSpec((1, N_EXPERTS, 1), lambda i: (layer, 0, 0)),
        ],
        out_specs=[
            pl.BlockSpec((tm, D_MODEL), row),
            pl.BlockSpec((tm, HALF_D), row),
            pl.BlockSpec((2 * TOP_K, tm), lambda i: (0, i)),
            pl.BlockSpec((tm, LANES), row),
            pl.BlockSpec((N_EXPERTS, LANES), lambda i: (0, 0)),
        ],
        out_shape=[
            jax.ShapeDtypeStruct((t, D_MODEL), _F32),
            jax.ShapeDtypeStruct((t, HALF_D), jnp.int32),
            jax.ShapeDtypeStruct((2 * TOP_K, t), jnp.int32),
            jax.ShapeDtypeStruct((t, LANES), _F32),
            jax.ShapeDtypeStruct((N_EXPERTS, LANES), jnp.int32),
        ],
        scratch_shapes=[pltpu.VMEM((N_EXPERTS, 1), _F32)],
        compiler_params=_params("arbitrary"),
        name="merge_out_ln_router",
    )(gates, gates, o_a, o_b, x, wo_all, ln_g, ln_b, wr_all, br_all)


def _expert_kernel(be_ref, nused_ref, x_ref, wup_ref, bg_ref, bu_ref, wd_ref, bd_ref, *rest, block_off):
    y_ref, wg_scr, wu_scr, wd_scr = rest[-4:]
    i = pl.program_id(0)
    blk = i + block_off
    changed = jnp.logical_or(i == 0, be_ref[blk] != be_ref[jnp.maximum(blk - 1, 0)])
    active = blk < nused_ref[0]

    @pl.when(jnp.logical_and(changed, active))
    def _():
        half = DEINT_TILE // 2
        r = lax.broadcasted_iota(jnp.int32, (DEINT_TILE, DEINT_TILE), 0)
        c = lax.broadcasted_iota(jnp.int32, (DEINT_TILE, DEINT_TILE), 1)
        sel = jnp.where(r == jnp.where(c < half, 2 * c, 2 * (c - half) + 1), 1.0, 0.0).astype(_BF16)
        for t in range(2 * D_FF // DEINT_TILE):
            w = wup_ref[0, 0, :, t * DEINT_TILE:(t + 1) * DEINT_TILE].astype(_BF16)
            de = jnp.dot(w, sel, preferred_element_type=_F32).astype(_BF16)
            wg_scr[:, t * half:(t + 1) * half] = de[:, :half]
            wu_scr[:, t * half:(t + 1) * half] = de[:, half:]
        wd_scr[...] = wd_ref[0, 0].astype(_BF16)

    @pl.when(active)
    def _():
        hi, lo = _unpack_rows(x_ref[...])
        x = jnp.concatenate([hi.astype(_BF16), lo.astype(_BF16)], axis=1)
        gate = jnp.dot(x, wg_scr[...], preferred_element_type=_F32) + bg_ref[0, 0]
        up = jnp.dot(x, wu_scr[...], preferred_element_type=_F32) + bu_ref[0, 0]
        gate = jnp.minimum(gate, SWIGLU_LIMIT)
        up = jnp.clip(up, -SWIGLU_LIMIT, SWIGLU_LIMIT)
        act = (up + 1.0) * (gate * jax.nn.sigmoid(SWIGLU_ALPHA * gate))
        y = jnp.dot(act.astype(_BF16), wd_scr[...], preferred_element_type=_F32) + bd_ref[0, 0]
        y_ref[...] = _pack_rows(y)

    @pl.when(jnp.logical_not(active))
    def _():
        y_ref[...] = jnp.zeros(y_ref.shape, y_ref.dtype)


def _experts(block_expert, n_used, x_part, first_row, y_prev, w_up, bg_all, bu_all, w_down, bd_all, layer):
    part_blocks = x_part.shape[0] // MOE_BLOCK
    n_rows = block_expert.shape[0] * MOE_BLOCK
    off = first_row // MOE_BLOCK
    pick = lambda i, be, nu: (layer, be[i + off], 0, 0)
    bspec = pl.BlockSpec((1, 1, 1, D_FF), pick)
    in_specs = [
        pl.BlockSpec((MOE_BLOCK, HALF_D), lambda i, be, nu: (i, 0)),
        pl.BlockSpec((1, 1, D_MODEL, 2 * D_FF), pick),
        bspec, bspec,
        pl.BlockSpec((1, 1, D_FF, D_MODEL), pick),
        pl.BlockSpec((1, 1, 1, D_MODEL), pick),
    ]
    operands = [block_expert, n_used, x_part, w_up, bg_all, bu_all, w_down, bd_all]
    aliases = {}
    if y_prev is not None:
        in_specs.append(pl.BlockSpec(memory_space=pl.ANY))
        aliases = {len(operands): 0}
        operands.append(y_prev)
    grid_spec = pltpu.PrefetchScalarGridSpec(
        num_scalar_prefetch=2,
        grid=(part_blocks,),
        in_specs=in_specs,
        out_specs=pl.BlockSpec((MOE_BLOCK, HALF_D), lambda i, be, nu: (i + off, 0)),
        scratch_shapes=[
            pltpu.VMEM((D_MODEL, D_FF), _BF16),
            pltpu.VMEM((D_MODEL, D_FF), _BF16),
            pltpu.VMEM((D_FF, D_MODEL), _BF16),
        ],
    )
    return pl.pallas_call(
        functools.partial(_expert_kernel, block_off=off),
        grid_spec=grid_spec,
        out_shape=jax.ShapeDtypeStruct((n_rows, HALF_D), jnp.int32),
        input_output_aliases=aliases,
        compiler_params=_params("arbitrary"),
        name="experts",
    )(*operands)


def _combine_kernel(h_ref, yg_ref, gate_ref, lng_ref, lnb_ref, *rest):
    x_ref, xb_ref = rest[-2:]
    gates = gate_ref[...]
    y = None
    for k in range(TOP_K):
        hi, lo = _unpack_rows(yg_ref[k])
        yk = gates[:, k:k + 1] * jnp.concatenate([hi, lo], axis=1)
        y = yk if y is None else y + yk
    xn = _layer_norm(DEEPNORM_ALPHA * h_ref[...] + y, lng_ref[0, 1:2, :], lnb_ref[0, 1:2, :])
    x_ref[...] = xn
    xb_ref[...] = xn.astype(_BF16)


def _combine(h, yg_part, gates, ln_g, ln_b, layer, first_tok, prev):
    t = h.shape[0]
    tm = LN2_TM
    off = first_tok // tm
    row = lambda i: (i + off, 0)
    in_specs = [
        pl.BlockSpec((tm, D_MODEL), row),
        pl.BlockSpec((TOP_K, tm, HALF_D), lambda i: (0, i, 0)),
        pl.BlockSpec((tm, LANES), row),
        pl.BlockSpec((1, 2, D_MODEL), lambda i: (layer, 0, 0)),
        pl.BlockSpec((1, 2, D_MODEL), lambda i: (layer, 0, 0)),
    ]
    operands = [h, yg_part, gates, ln_g, ln_b]
    aliases = {}
    if prev is not None:
        in_specs += [pl.BlockSpec(memory_space=pl.ANY), pl.BlockSpec(memory_space=pl.ANY)]
        aliases = {len(operands): 0, len(operands) + 1: 1}
        operands += list(prev)
    return pl.pallas_call(
        _combine_kernel,
        grid=(yg_part.shape[1] // tm,),
        in_specs=in_specs,
        out_specs=[pl.BlockSpec((tm, D_MODEL), row), pl.BlockSpec((tm, D_MODEL), row)],
        out_shape=[jax.ShapeDtypeStruct((t, D_MODEL), _F32), jax.ShapeDtypeStruct((t, D_MODEL), _BF16)],
        input_output_aliases=aliases,
        compiler_params=_params("parallel"),
        name="combine_ln",
    )(*operands)


def _dispatch_tables(route, counts, t):
    tk = t * TOP_K
    experts = route[:TOP_K]
    ranks = route[TOP_K:]
    padded = (counts + MOE_BLOCK - 1) // MOE_BLOCK * MOE_BLOCK
    pad_end = jnp.cumsum(padded)
    pad_start = pad_end - padded
    start = jnp.cumsum(counts) - counts
    expert_ids = jnp.arange(N_EXPERTS, dtype=jnp.int32)
    dest = jnp.sum(jnp.where(experts[..., None] == expert_ids, pad_start, 0), axis=-1) + ranks
    n_blocks = (tk + N_EXPERTS * (MOE_BLOCK - 1) + MOE_BLOCK - 1) // MOE_BLOCK
    block_start = jnp.arange(n_blocks, dtype=jnp.int32) * MOE_BLOCK
    block_expert = jnp.minimum(
        jnp.sum((pad_end[None, :] <= block_start[:, None]).astype(jnp.int32), axis=1), N_EXPERTS - 1)
    n_used = (pad_end[-1:] // MOE_BLOCK).astype(jnp.int32)
    order = jnp.argsort(experts.T.reshape(tk))
    in_block = jnp.arange(MOE_BLOCK, dtype=jnp.int32)[None, :]
    r = (block_start - pad_start[block_expert])[:, None] + in_block
    valid = r < counts[block_expert][:, None]
    src = jnp.where(valid, start[block_expert][:, None] + r, 0).reshape(n_blocks * MOE_BLOCK)
    filler = jnp.arange(n_blocks * MOE_BLOCK, dtype=jnp.int32) % t
    buf_tok = jnp.where(valid.reshape(-1), order[src] // TOP_K, filler)
    return buf_tok, dest, block_expert, n_used


def kernel(x, w_in, w_o, lambda_qk, subln_g, sinks, ln_g, ln_b, w_router, b_router, w_up, b_up, w_down, b_down):
    batch, seq, _ = x.shape
    t = batch * seq
    assert seq % ATT_TQ == 0 and seq % SWA_TB == 0 and t % PROJ_TM == 0

    va_lo, va_hi = 2 * D_MODEL, 3 * D_MODEL
    gates_lo = w_in.shape[-1] - GATE_COLS
    w_att_b = jnp.concatenate([w_in[..., :va_lo], w_in[..., va_hi:gates_lo]], axis=-1).astype(_BF16)
    vb_lo = gates_lo - SW_KV_HEADS * SW_HEAD_DIM
    w_vt_b = jnp.swapaxes(jnp.concatenate([w_in[..., va_lo:va_hi], w_in[..., vb_lo:gates_lo]], axis=-1),
                          1, 2).astype(_BF16)
    w_gates_b = w_in[..., gates_lo:].astype(_BF16)
    w_o_b = w_o.astype(_BF16)
    w_r_b = jnp.swapaxes(w_router, 1, 2).astype(_BF16)
    b_gate = b_up[..., 0::2].reshape(DEPTH, N_EXPERTS, 1, D_FF)
    b_upp = b_up[..., 1::2].reshape(DEPTH, N_EXPERTS, 1, D_FF)
    b_down4 = b_down.reshape(DEPTH, N_EXPERTS, 1, D_MODEL)
    b_router3 = b_router.astype(_F32).reshape(DEPTH, N_EXPERTS, 1)

    att_scale = jnp.concatenate([
        jnp.full((D_MODEL,), DA_HEAD_DIM ** -0.5 * LOG2E, _F32), jnp.ones((D_MODEL,), _F32),
        jnp.full((D_MODEL,), SW_HEAD_DIM ** -0.5 * LOG2E, _F32),
        jnp.ones((2 * SW_KV_HEADS * SW_HEAD_DIM,), _F32)]).reshape(1, ATT_COLS)
    gate_scale = jnp.ones((1, GATE_COLS), _F32)
    rest = 2.0 ** (-8.0 * jnp.arange(1, DA_HEADS + 1, dtype=_F32) / DA_HEADS) * LOG2E
    pieces = []
    for _ in range(N_SLOPE_PARTS):
        piece = rest.astype(_BF16).astype(_F32)
        pieces.append(piece)
        rest = rest - piece
    da_slopes = jnp.stack(pieces, axis=1)
    swa_bias = _swa_bias()

    xf = x.reshape(t, D_MODEL)
    xb = xf.astype(_BF16)
    for l in range(DEPTH):
        lam_init = 0.8 - 0.6 * math.exp(-0.3 * l)
        prm = jnp.concatenate([lambda_qk[l].astype(_F32),
                               jnp.full((SUBLANES_F32 - lambda_qk.shape[1], DA_HEAD_DIM), lam_init, _F32)], axis=0)
        att = _project(xb, w_att_b, l, ATT_COLS // 2, att_scale, _BF16)
        gates = _project(xb, w_gates_b, l, GATE_COLS // 2, gate_scale, _BF16)
        vt, vtb = _project_vt(xb, w_vt_b, l)
        o_a = _diff_attention(att, vt, da_slopes, prm, subln_g[l].reshape(DA_WIDTH, 1), batch, seq)
        sink_rows = jnp.repeat(sinks[l].astype(_F32).reshape(SW_KV_HEADS, SW_GROUP) * LOG2E, WINDOW, axis=1)
        o_b = _sliding_window_attention(att, vtb, sink_rows.reshape(SW_KV_HEADS, 1, SW_GROUP * WINDOW),
                                        swa_bias, batch, seq)
        h, hp, route, top_gate, counts = _merge_out(gates, o_a, o_b, xf, w_o_b, ln_g, ln_b, w_r_b, b_router3, l)
        buf_tok, pos, block_expert, n_used = _dispatch_tables(route, counts[:, 0], t)
        yb = None
        for lo, hi in MOE_SLICES:
            first, count = buf_tok.shape[0] * lo // MOE_SLICE_DENOM, buf_tok.shape[0] * (hi - lo) // MOE_SLICE_DENOM
            x_rows = _gather_rows(hp, buf_tok, 1, first, count)
            yb = _experts(block_expert, n_used, x_rows, first, yb, w_up, b_gate, b_upp, w_down, b_down4, l)
        pos_flat = pos.reshape(TOP_K * t)
        outs = None
        for lo, hi in MOE_SLICES:
            first, count = t * lo // MOE_SLICE_DENOM, t * (hi - lo) // MOE_SLICE_DENOM
            yg = _gather_rows(yb, pos_flat, TOP_K, first, count).reshape(TOP_K, count, HALF_D)
            outs = _combine(h, yg, top_gate, ln_g, ln_b, l, first, outs)
        xf, xb = outs
    return xf.reshape(batch, seq, D_MODEL)
```

```python
import functools
import math

import jax
import jax.numpy as jnp
from jax import lax
from jax.experimental import pallas as pl
from jax.experimental.pallas import tpu as pltpu
from jax.experimental.pallas import tpu_sc as plsc

D_MODEL = 1024
DEPTH = 4
DA_HEAD_DIM = 128
DA_HEADS = 4
DA_WIDTH = 2 * DA_HEAD_DIM
SW_HEAD_DIM = 64
SW_HEADS = 16
SW_KV_HEADS = 2
SW_GROUP = SW_HEADS // SW_KV_HEADS
WINDOW = 128
N_EXPERTS = 32
TOP_K = 4
D_FF = D_MODEL
SWIGLU_LIMIT = 7.0
SWIGLU_ALPHA = 1.702
LN_EPS = 1e-5
DEEPNORM_ALPHA = (2.0 * DEPTH) ** 0.25
NEG_INF = -1e30

QA_OFF, KA_OFF, QB_OFF, KVB_OFF = 0, D_MODEL, 2 * D_MODEL, 3 * D_MODEL
ATT_COLS = KVB_OFF + 2 * SW_KV_HEADS * SW_HEAD_DIM
GATE_COLS = 2 * D_MODEL
HALF_D = D_MODEL // 2
HI16_MASK = 0xFFFF0000
LOG2E = math.log2(math.e)
N_SLOPE_PARTS = 3
BF16_EXACT_INTS = 256

LANES = 128
SUBLANES_BF16 = 16
MXU_TILE = 256
VMEM_BYTES = 64 * 1024 * 1024
SC_CORES = 2
SC_SUBCORES = 16
SC_SLICE_ALIGN = 8
GATHER_CHUNKS = (64, 32, 16)

VT_ROWS = DA_WIDTH + SUBLANES_BF16
VTB_ROWS = SW_HEAD_DIM + SUBLANES_BF16

PROJ_TM = 1024
ATT_TQ = 1024
ATT_TK = 512
SWA_TB = 512
OUT_TM = 512
MOE_BLOCK = 256
MOE_PARTS = 2
DEINT_TILE = MXU_TILE
LN2_TM = 512
VMEM_LIMIT = VMEM_BYTES * 7 // 8

_F32 = jnp.float32
_BF16 = jnp.bfloat16


def _params(*sem):
    return pltpu.CompilerParams(dimension_semantics=sem, vmem_limit_bytes=VMEM_LIMIT)


def _proj_kernel(x_ref, w_ref, s_ref, o_ref):
    acc = jnp.dot(x_ref[...], w_ref[0], preferred_element_type=_F32)
    o_ref[...] = (acc * s_ref[...]).astype(o_ref.dtype)


def _project(xb, w_all, layer, tn, scale, out_dtype):
    t = xb.shape[0]
    n_cols = w_all.shape[-1]
    assert n_cols % tn == 0 and t % PROJ_TM == 0
    return pl.pallas_call(
        _proj_kernel,
        grid=(n_cols // tn, t // PROJ_TM),
        in_specs=[
            pl.BlockSpec((PROJ_TM, D_MODEL), lambda j, i: (i, 0)),
            pl.BlockSpec((1, D_MODEL, tn), lambda j, i: (layer, 0, j)),
            pl.BlockSpec((1, tn), lambda j, i: (0, j)),
        ],
        out_specs=pl.BlockSpec((PROJ_TM, tn), lambda j, i: (i, j)),
        out_shape=jax.ShapeDtypeStruct((t, n_cols), out_dtype),
        compiler_params=_params("parallel", "parallel"),
        name="in_proj",
    )(xb, w_all, scale)


def _proj_vt_kernel(x_ref, wt_ref, oa_ref, ob_ref):
    x = x_ref[...]
    nt = (((1,), (1,)), ((), ()))
    ones = jnp.ones((SUBLANES_BF16, x.shape[0]), oa_ref.dtype)
    for h in range(DA_HEADS):
        vt = lax.dot_general(wt_ref[0, h * DA_WIDTH:(h + 1) * DA_WIDTH, :], x, nt, preferred_element_type=_F32)
        oa_ref[h * VT_ROWS:h * VT_ROWS + DA_WIDTH, :] = vt.astype(oa_ref.dtype)
        oa_ref[h * VT_ROWS + DA_WIDTH:(h + 1) * VT_ROWS, :] = ones
    vb = lax.dot_general(wt_ref[0, D_MODEL:, :], x, nt, preferred_element_type=_F32)
    for kh in range(SW_KV_HEADS):
        ob_ref[kh * VTB_ROWS:kh * VTB_ROWS + SW_HEAD_DIM, :] = (
            vb[kh * SW_HEAD_DIM:(kh + 1) * SW_HEAD_DIM, :].astype(ob_ref.dtype))
        ob_ref[kh * VTB_ROWS + SW_HEAD_DIM:(kh + 1) * VTB_ROWS, :] = ones


def _project_vt(xb, wt_all, layer):
    t = xb.shape[0]
    n_b = SW_KV_HEADS * SW_HEAD_DIM
    return pl.pallas_call(
        _proj_vt_kernel,
        grid=(t // PROJ_TM,),
        in_specs=[
            pl.BlockSpec((PROJ_TM, D_MODEL), lambda i: (i, 0)),
            pl.BlockSpec((1, D_MODEL + n_b, D_MODEL), lambda i: (layer, 0, 0)),
        ],
        out_specs=[pl.BlockSpec((DA_HEADS * VT_ROWS, PROJ_TM), lambda i: (0, i)),
                   pl.BlockSpec((SW_KV_HEADS * VTB_ROWS, PROJ_TM), lambda i: (0, i))],
        out_shape=[jax.ShapeDtypeStruct((DA_HEADS * VT_ROWS, t), _BF16),
                   jax.ShapeDtypeStruct((SW_KV_HEADS * VTB_ROWS, t), _BF16)],
        compiler_params=_params("parallel"),
        name="in_proj_vt",
    )(xb, wt_all)


def _diff_attn_kernel(slope_ref, prm_ref, g_ref, q_ref, k_ref, vt_ref, o_ref,
                      acc_ref, m_ref, qaug_ref, kaug_ref, s_ref, smax_ref, p_ref, alpha_ref, *, tq, tk):
    h = pl.program_id(1)
    i = pl.program_id(2)
    diag = tq // tk
    n_full = i * diag
    lane = lax.broadcasted_iota(jnp.int32, (tq, DA_HEAD_DIM), 1)
    slope_tile = jnp.zeros((tq, DA_HEAD_DIM), _F32)
    slope = 0.0
    for part in range(N_SLOPE_PARTS):
        piece = slope_ref[h, part]
        slope = slope + piece
        slope_tile = jnp.where(lane // 2 == part, piece, slope_tile)
    klane = lax.broadcasted_iota(jnp.int32, (tk, DA_HEAD_DIM), 1)
    kpos = lax.broadcasted_iota(jnp.int32, (tk, DA_HEAD_DIM), 0)
    pos_split = jnp.where(klane % 2 == 0, kpos % BF16_EXACT_INTS, kpos // BF16_EXACT_INTS * BF16_EXACT_INTS)
    pos_tile = jnp.where(klane < 2 * N_SLOPE_PARTS, pos_split, 0)
    for c in range(2):
        kaug_ref[c, :, DA_HEAD_DIM:] = pos_tile.astype(_F32).astype(_BF16)
        qaug_ref[c, :, :DA_HEAD_DIM] = q_ref[:, c * DA_HEAD_DIM:(c + 1) * DA_HEAD_DIM]
        qaug_ref[c, :, DA_HEAD_DIM:] = slope_tile.astype(_BF16)
    m_ref[...] = jnp.full(m_ref.shape, NEG_INF, _F32)
    acc_ref[...] = jnp.zeros(acc_ref.shape, _F32)

    def scores(j):
        start = pl.multiple_of(j * tk, tk)
        for c in range(2):
            kaug_ref[c, :, :DA_HEAD_DIM] = k_ref[pl.ds(start, tk), c * DA_HEAD_DIM:(c + 1) * DA_HEAD_DIM]
            s = lax.dot_general(kaug_ref[c], qaug_ref[c], (((1,), (1,)), ((), ())),
                                preferred_element_type=_F32)
            s_ref[c] = s
            smax_ref[c] = jnp.max(s, axis=0, keepdims=True)

    def weights(j, boundary):
        off = (j * tk - i * tq).astype(_F32) * slope
        for c in range(2):
            s = s_ref[c]
            if boundary is None:
                smax = smax_ref[c]
            else:
                keys = lax.broadcasted_iota(jnp.int32, (tk, tq), 0) + boundary * tk
                queries = lax.broadcasted_iota(jnp.int32, (tk, tq), 1)
                s = jnp.where(keys <= queries, s, NEG_INF)
                smax = jnp.max(s, axis=0, keepdims=True)
            m_old = m_ref[c]
            m_new = jnp.maximum(m_old, smax + off)
            alpha_ref[c] = jnp.exp2(m_old - m_new)
            p_ref[c] = jnp.exp2(s - (m_new - off)).astype(_BF16)
            m_ref[c] = m_new

    def accumulate(j):
        start = pl.multiple_of(j * tk, tk)
        vt = vt_ref[:, pl.ds(start, tk)]
        for c in range(2):
            acc_ref[c] = alpha_ref[c] * acc_ref[c] + jnp.dot(vt, p_ref[c], preferred_element_type=_F32)

    def body(j, carry):
        accumulate(j)
        weights(j + 1, None)
        scores(j + 2)
        return carry

    def enter_boundary():
        weights(n_full, 0)
        if diag > 1:
            scores(n_full + 1)

    scores(0)

    @pl.when(i >= 1)
    def _():
        weights(0, None)
        scores(1)
        lax.fori_loop(0, n_full - 1, body, 0)
        accumulate(n_full - 1)
        enter_boundary()

    @pl.when(i == 0)
    def _():
        enter_boundary()

    for d in range(diag):
        accumulate(n_full + d)
        if d + 1 < diag:
            weights(n_full + d + 1, d + 1)
            if d + 2 < diag:
                scores(n_full + d + 2)

    prm = prm_ref[...]
    lam_init = prm[4:5, 0:1]
    lam = (jnp.exp(jnp.sum(prm[0:1] * prm[1:2], axis=1, keepdims=True))
           - jnp.exp(jnp.sum(prm[2:3] * prm[3:4], axis=1, keepdims=True)) + lam_init)
    o0 = acc_ref[0, :DA_WIDTH, :] / acc_ref[0, DA_WIDTH:DA_WIDTH + 1, :]
    o1 = acc_ref[1, :DA_WIDTH, :] / acc_ref[1, DA_WIDTH:DA_WIDTH + 1, :]
    a = o0 - lam * o1
    ms = jnp.mean(a * a, axis=0, keepdims=True)
    out = a * lax.rsqrt(ms + LN_EPS) * g_ref[...] * (1.0 - lam_init)
    o_ref[...] = out.T.astype(o_ref.dtype)


def _diff_attention(att, vt, slopes, prm, subln_g, batch, seq):
    tq, tk = ATT_TQ, ATT_TK
    assert tq % tk == 0 and seq % tq == 0
    nq = seq // tq
    t = batch * seq
    kernel = functools.partial(_diff_attn_kernel, tq=tq, tk=tk)
    return pl.pallas_call(
        kernel,
        grid=(batch, DA_HEADS, nq),
        in_specs=[
            pl.BlockSpec(memory_space=pltpu.SMEM),
            pl.BlockSpec((8, LANES), lambda b, h, i: (0, 0)),
            pl.BlockSpec((DA_WIDTH, 1), lambda b, h, i: (0, 0)),
            pl.BlockSpec((tq, DA_WIDTH), lambda b, h, i: (b * nq + i, QA_OFF // DA_WIDTH + h)),
            pl.BlockSpec((seq, DA_WIDTH), lambda b, h, i: (b, KA_OFF // DA_WIDTH + h)),
            pl.BlockSpec((VT_ROWS, seq), lambda b, h, i: (h, b)),
        ],
        out_specs=pl.BlockSpec((tq, DA_WIDTH), lambda b, h, i: (b * nq + i, h)),
        out_shape=jax.ShapeDtypeStruct((t, D_MODEL), _BF16),
        scratch_shapes=[
            pltpu.VMEM((2, VT_ROWS, tq), _F32),
            pltpu.VMEM((2, 1, tq), _F32),
            pltpu.VMEM((2, tq, 2 * DA_HEAD_DIM), _BF16),
            pltpu.VMEM((2, tk, 2 * DA_HEAD_DIM), _BF16),
            pltpu.VMEM((2, tk, tq), _F32),
            pltpu.VMEM((2, 1, tq), _F32),
            pltpu.VMEM((2, tk, tq), _BF16),
            pltpu.VMEM((2, 1, tq), _F32),
        ],
        compiler_params=_params("parallel", "parallel", "arbitrary"),
        name="diff_attn",
    )(slopes, prm, subln_g, att, att, vt)


def _swa_kernel(sink_ref, bias_ref, q_ref, kv_ref, kvp_ref, vt_ref, vtp_ref, o_ref):
    i = pl.program_id(1)
    first = (i == 0).astype(jnp.int32)
    nt = (((1,), (1,)), ((), ()))
    kv_all = jnp.concatenate([kvp_ref[...], kv_ref[...]], axis=0)
    vt_all = jnp.concatenate([vtp_ref[...], vt_ref[...]], axis=1)
    def scores(r, kh):
        kwin = kv_all[r * WINDOW:(r + 2) * WINDOW, kh * SW_HEAD_DIM:(kh + 1) * SW_HEAD_DIM]
        qs = jnp.concatenate(
            [q_ref[r * WINDOW:(r + 1) * WINDOW, (kh * SW_GROUP + g) * SW_HEAD_DIM:(kh * SW_GROUP + g + 1) * SW_HEAD_DIM]
             for g in range(SW_GROUP)], axis=0)
        s = lax.dot_general(kwin, qs, nt, preferred_element_type=_F32)
        bias = bias_ref[2 * kh + first] if r == 0 else bias_ref[2 * kh]
        return s + bias

    pairs = [(r, kh) for r in range(SWA_TB // WINDOW) for kh in range(SW_KV_HEADS)]
    s_next = scores(*pairs[0])
    heads = []
    for n, (r, kh) in enumerate(pairs):
        s = s_next
        if n + 1 < len(pairs):
            s_next = scores(*pairs[n + 1])
        vwin = vt_all[kh * VTB_ROWS:(kh + 1) * VTB_ROWS, r * WINDOW:(r + 2) * WINDOW]
        sink = sink_ref[kh]
        m = jnp.maximum(jnp.max(s, axis=0, keepdims=True), sink)
        p = jnp.exp2(s - m)
        ot = jnp.dot(vwin, p.astype(_BF16), preferred_element_type=_F32)
        denom = ot[SW_HEAD_DIM:SW_HEAD_DIM + 1, :] + jnp.exp2(sink - m)
        ot = ot[:SW_HEAD_DIM, :] / denom
        for g in range(SW_GROUP):
            heads.append(ot[:, g * WINDOW:(g + 1) * WINDOW].T)
        if kh == SW_KV_HEADS - 1:
            o_ref[r * WINDOW:(r + 1) * WINDOW, :] = jnp.concatenate(heads, axis=1).astype(o_ref.dtype)
            heads = []


def _swa_bias():
    slopes = 2.0 ** (-8.0 * jnp.arange(1, SW_HEADS + 1, dtype=_F32) / SW_HEADS)
    kj = jnp.arange(2 * WINDOW, dtype=jnp.int32)[:, None]
    qi = jnp.arange(WINDOW, dtype=jnp.int32)[None, :]
    dist = qi + WINDOW - kj
    valid = (dist >= 0) & (dist < WINDOW)
    tables = []
    for kh in range(SW_KV_HEADS):
        for has_prev in (True, False):
            ok = valid if has_prev else valid & (kj >= WINDOW)
            per_head = [jnp.where(ok, -slopes[kh * SW_GROUP + g] * LOG2E * dist.astype(_F32), NEG_INF)
                        for g in range(SW_GROUP)]
            tables.append(jnp.concatenate(per_head, axis=1))
    return jnp.stack(tables)


def _sliding_window_attention(att, vtb, sink_rows, bias, batch, seq):
    tb = SWA_TB
    nb = seq // tb
    ratio = tb // WINDOW
    t = batch * seq
    kv_width = 2 * SW_KV_HEADS * SW_HEAD_DIM
    n_b = SW_KV_HEADS * VTB_ROWS
    q_blk = QB_OFF // D_MODEL
    kv_blk = KVB_OFF // kv_width
    prev = lambda b, i: jnp.maximum((b * nb + i) * ratio - 1, 0)
    return pl.pallas_call(
        _swa_kernel,
        grid=(batch, nb),
        in_specs=[
            pl.BlockSpec((SW_KV_HEADS, 1, SW_GROUP * WINDOW), lambda b, i: (0, 0, 0)),
            pl.BlockSpec((2 * SW_KV_HEADS, 2 * WINDOW, SW_GROUP * WINDOW), lambda b, i: (0, 0, 0)),
            pl.BlockSpec((tb, D_MODEL), lambda b, i: (b * nb + i, q_blk)),
            pl.BlockSpec((tb, kv_width), lambda b, i: (b * nb + i, kv_blk)),
            pl.BlockSpec((WINDOW, kv_width), lambda b, i: (prev(b, i), kv_blk)),
            pl.BlockSpec((n_b, tb), lambda b, i: (0, b * nb + i)),
            pl.BlockSpec((n_b, WINDOW), lambda b, i: (0, prev(b, i))),
        ],
        out_specs=pl.BlockSpec((tb, D_MODEL), lambda b, i: (b * nb + i, 0)),
        out_shape=jax.ShapeDtypeStruct((t, D_MODEL), _BF16),
        compiler_params=_params("parallel", "arbitrary"),
        name="swa",
    )(sink_rows, bias, att, att, att, vtb, vtb)


def _pack_rows(x):
    bits = lax.bitcast_convert_type(x.astype(_BF16).astype(_F32), jnp.uint32)
    return lax.bitcast_convert_type(bits[:, :HALF_D] | (bits[:, HALF_D:] >> 16), jnp.int32)


def _unpack_rows(words):
    bits = lax.bitcast_convert_type(words, jnp.uint32)
    hi = lax.bitcast_convert_type(bits & jnp.uint32(HI16_MASK), _F32)
    lo = lax.bitcast_convert_type(bits << 16, _F32)
    return hi, lo


def _gather_rows(table, idx, n_groups, first, count):
    group_size = idx.shape[0] // n_groups
    n_out = n_groups * count
    width = table.shape[1]
    workers = SC_CORES * SC_SUBCORES
    per_worker = n_out // workers
    chunk_rows = next(c for c in GATHER_CHUNKS if per_worker % (2 * c) == 0)
    assert n_out % workers == 0 and count % per_worker == 0 and first % SC_SLICE_ALIGN == 0
    n_chunks = per_worker // chunk_rows
    mesh = plsc.VectorSubcoreMesh(core_axis_name="c", subcore_axis_name="s",
                                  num_cores=SC_CORES, num_subcores=SC_SUBCORES)

    @functools.partial(
        pl.kernel, mesh=mesh,
        out_type=jax.ShapeDtypeStruct((n_out, width), table.dtype),
        scratch_types=[
            pltpu.VMEM((per_worker,), jnp.int32),
            pltpu.VMEM((2, chunk_rows, width), table.dtype),
            pltpu.SemaphoreType.DMA((2,)),
            pltpu.SemaphoreType.DMA((2,)),
        ],
        name="gather_rows",
    )
    def gather(table_hbm, idx_hbm, out_hbm, idx_v, rows_v, fetch_sem, store_sem):
        base = (lax.axis_index("s") * SC_CORES + lax.axis_index("c")) * per_worker
        src = pl.multiple_of((base // count) * group_size + first + base % count, SC_SLICE_ALIGN)
        pltpu.sync_copy(idx_hbm.at[pl.ds(src, per_worker)], idx_v)

        def fetch(g, buf):
            chunk = idx_v.at[pl.ds(g * chunk_rows, chunk_rows)]
            return pltpu.make_async_copy(table_hbm.at[chunk], rows_v.at[buf], fetch_sem.at[buf])

        def store(g, buf):
            dst = out_hbm.at[pl.ds(base + g * chunk_rows, chunk_rows)]
            return pltpu.make_async_copy(rows_v.at[buf], dst, store_sem.at[buf])

        fetch(0, 0).start()

        @pl.loop(0, n_chunks, step=2)
        def _(g0):
            for buf in range(2):
                g = g0 + buf
                fetch(g, buf).wait()

                @pl.when(g >= 1)
                def _():
                    store(g - 1, 1 - buf).wait()

                @pl.when(g + 1 < n_chunks)
                def _():
                    fetch(g + 1, 1 - buf).start()

                store(g, buf).start()

        store(n_chunks - 1, 1).wait()

    return gather(table, idx)


def _layer_norm(z, g, b):
    mu = jnp.mean(z, axis=1, keepdims=True)
    d = z - mu
    var = jnp.mean(d * d, axis=1, keepdims=True)
    return d * lax.rsqrt(var + LN_EPS) * g + b


def _merge_out_kernel(xb_ref, wg_ref, oa_ref, ob_ref, x_ref, wo_ref, lng_ref, lnb_ref, wr_ref, br_ref,
                      h_ref, hp_ref, idx_ref, gate_ref, cnt_ref, run_ref):
    @pl.when(pl.program_id(0) == 0)
    def _():
        run_ref[...] = jnp.zeros(run_ref.shape, _F32)

    branch_gates = jnp.dot(xb_ref[...], wg_ref[0], preferred_element_type=_F32)
    merged = (jax.nn.sigmoid(branch_gates[:, :D_MODEL]) * oa_ref[...].astype(_F32)
              + jax.nn.sigmoid(branch_gates[:, D_MODEL:]) * ob_ref[...].astype(_F32))
    y = jnp.dot(merged.astype(_BF16), wo_ref[0], preferred_element_type=_F32)
    hn = _layer_norm(DEEPNORM_ALPHA * x_ref[...] + y, lng_ref[0, 0:1, :], lnb_ref[0, 0:1, :])
    h_ref[...] = hn
    hb = hn.astype(_BF16)
    hp_ref[...] = _pack_rows(hn)
    logits = lax.dot_general(wr_ref[0], hb, (((1,), (1,)), ((), ())),
                             preferred_element_type=_F32) + br_ref[0]
    tm = logits.shape[1]
    erow = lax.broadcasted_iota(jnp.int32, (N_EXPERTS, tm), 0)
    cur = logits
    vals, picks, id_rows = [], [], []
    for k in range(TOP_K):
        mx = jnp.max(cur, axis=0, keepdims=True)
        ix = jnp.min(jnp.where(cur == mx, erow, N_EXPERTS), axis=0, keepdims=True)
        vals.append(mx)
        picks.append(erow == ix)
        id_rows.append(ix)
        cur = jnp.where(picks[k], -jnp.inf, cur)
    exps = [jnp.exp(v - vals[0]) for v in vals]
    tot = exps[0] + exps[1] + exps[2] + exps[3]
    gate_rows = [e / tot for e in exps] + [jnp.zeros((LANES - TOP_K, tm), _F32)]
    gate_ref[...] = jnp.concatenate(gate_rows, axis=0).T

    chosen = jnp.zeros((N_EXPERTS, tm), _F32)
    for k in range(TOP_K):
        chosen = chosen + jnp.where(picks[k], 1.0, 0.0)
    r = lax.broadcasted_iota(jnp.int32, (tm, tm), 0)
    c = lax.broadcasted_iota(jnp.int32, (tm, tm), 1)
    earlier = jnp.where(r < c, 1.0, 0.0).astype(_BF16)
    before = jnp.dot(chosen.astype(_BF16), earlier, preferred_element_type=_F32) + run_ref[...]
    rank_rows = [jnp.sum(jnp.where(picks[k], before, 0.0), axis=0, keepdims=True).astype(jnp.int32)
                 for k in range(TOP_K)]
    idx_ref[...] = jnp.concatenate(id_rows + rank_rows, axis=0)
    total = run_ref[...] + jnp.sum(chosen, axis=1, keepdims=True)
    run_ref[...] = total
    cnt_ref[...] = jnp.broadcast_to(total, cnt_ref.shape).astype(jnp.int32)


def _merge_out(xb, wg_all, o_a, o_b, x, wo_all, ln_g, ln_b, wr_all, br_all, layer):
    t = x.shape[0]
    tm = OUT_TM
    row = lambda i: (i, 0)
    return pl.pallas_call(
        _merge_out_kernel,
        grid=(t // tm,),
        in_specs=[
            pl.BlockSpec((tm, D_MODEL), row),
            pl.BlockSpec((1, D_MODEL, GATE_COLS), lambda i: (layer, 0, 0)),
            pl.BlockSpec((tm, D_MODEL), row),
            pl.BlockSpec((tm, D_MODEL), row),
            pl.BlockSpec((tm, D_MODEL), row),
            pl.BlockSpec((1, D_MODEL, D_MODEL), lambda i: (layer, 0, 0)),
            pl.BlockSpec((1, 2, D_MODEL), lambda i: (layer, 0, 0)),
            pl.BlockSpec((1, 2, D_MODEL), lambda i: (layer, 0, 0)),
            pl.BlockSpec((1, N_EXPERTS, D_MODEL), lambda i: (layer, 0, 0)),
            pl.BlockSpec((1, N_EXPERTS, 1), lambda i: (layer, 0, 0)),
        ],
        out_specs=[
            pl.BlockSpec((tm, D_MODEL), row),
            pl.BlockSpec((tm, HALF_D), row),
            pl.BlockSpec((2 * TOP_K, tm), lambda i: (0, i)),
            pl.BlockSpec((tm, LANES), row),
            pl.BlockSpec((N_EXPERTS, LANES), lambda i: (0, 0)),
        ],
        out_shape=[
            jax.ShapeDtypeStruct((t, D_MODEL), _F32),
            jax.ShapeDtypeStruct((t, HALF_D), jnp.int32),
            jax.ShapeDtypeStruct((2 * TOP_K, t), jnp.int32),
            jax.ShapeDtypeStruct((t, LANES), _F32),
            jax.ShapeDtypeStruct((N_EXPERTS, LANES), jnp.int32),
        ],
        scratch_shapes=[pltpu.VMEM((N_EXPERTS, 1), _F32)],
        compiler_params=_params("arbitrary"),
        name="merge_out_ln_router",
    )(xb, wg_all, o_a, o_b, x, wo_all, ln_g, ln_b, wr_all, br_all)


def _expert_kernel(be_ref, nused_ref, x_ref, wup_ref, bg_ref, bu_ref, wd_ref, bd_ref, *rest, block_off):
    y_ref, wg_scr, wu_scr, wd_scr = rest[-4:]
    i = pl.program_id(0)
    blk = i + block_off
    changed = jnp.logical_or(i == 0, be_ref[blk] != be_ref[jnp.maximum(blk - 1, 0)])
    active = blk < nused_ref[0]

    @pl.when(jnp.logical_and(changed, active))
    def _():
        half = DEINT_TILE // 2
        r = lax.broadcasted_iota(jnp.int32, (DEINT_TILE, DEINT_TILE), 0)
        c = lax.broadcasted_iota(jnp.int32, (DEINT_TILE, DEINT_TILE), 1)
        sel = jnp.where(r == jnp.where(c < half, 2 * c, 2 * (c - half) + 1), 1.0, 0.0).astype(_BF16)
        for t in range(2 * D_FF // DEINT_TILE):
            w = wup_ref[0, 0, :, t * DEINT_TILE:(t + 1) * DEINT_TILE].astype(_BF16)
            de = jnp.dot(w, sel, preferred_element_type=_F32).astype(_BF16)
            wg_scr[:, t * half:(t + 1) * half] = de[:, :half]
            wu_scr[:, t * half:(t + 1) * half] = de[:, half:]
        wd_scr[...] = wd_ref[0, 0].astype(_BF16)

    @pl.when(active)
    def _():
        hi, lo = _unpack_rows(x_ref[...])
        x = jnp.concatenate([hi.astype(_BF16), lo.astype(_BF16)], axis=1)
        gate = jnp.dot(x, wg_scr[...], preferred_element_type=_F32) + bg_ref[0, 0]
        up = jnp.dot(x, wu_scr[...], preferred_element_type=_F32) + bu_ref[0, 0]
        gate = jnp.minimum(gate, SWIGLU_LIMIT)
        up = jnp.clip(up, -SWIGLU_LIMIT, SWIGLU_LIMIT)
        act = (up + 1.0) * (gate * jax.nn.sigmoid(SWIGLU_ALPHA * gate))
        y = jnp.dot(act.astype(_BF16), wd_scr[...], preferred_element_type=_F32) + bd_ref[0, 0]
        y_ref[...] = _pack_rows(y)

    @pl.when(jnp.logical_not(active))
    def _():
        y_ref[...] = jnp.zeros(y_ref.shape, y_ref.dtype)


def _experts(block_expert, n_used, x_part, part, y_prev, w_up, bg_all, bu_all, w_down, bd_all, layer):
    part_blocks = x_part.shape[0] // MOE_BLOCK
    n_rows = block_expert.shape[0] * MOE_BLOCK
    off = part * part_blocks
    pick = lambda i, be, nu: (layer, be[i + off], 0, 0)
    bspec = pl.BlockSpec((1, 1, 1, D_FF), pick)
    in_specs = [
        pl.BlockSpec((MOE_BLOCK, HALF_D), lambda i, be, nu: (i, 0)),
        pl.BlockSpec((1, 1, D_MODEL, 2 * D_FF), pick),
        bspec, bspec,
        pl.BlockSpec((1, 1, D_FF, D_MODEL), pick),
        pl.BlockSpec((1, 1, 1, D_MODEL), pick),
    ]
    operands = [block_expert, n_used, x_part, w_up, bg_all, bu_all, w_down, bd_all]
    aliases = {}
    if y_prev is not None:
        in_specs.append(pl.BlockSpec(memory_space=pl.ANY))
        aliases = {len(operands): 0}
        operands.append(y_prev)
    grid_spec = pltpu.PrefetchScalarGridSpec(
        num_scalar_prefetch=2,
        grid=(part_blocks,),
        in_specs=in_specs,
        out_specs=pl.BlockSpec((MOE_BLOCK, HALF_D), lambda i, be, nu: (i + off, 0)),
        scratch_shapes=[
            pltpu.VMEM((D_MODEL, D_FF), _BF16),
            pltpu.VMEM((D_MODEL, D_FF), _BF16),
            pltpu.VMEM((D_FF, D_MODEL), _BF16),
        ],
    )
    return pl.pallas_call(
        functools.partial(_expert_kernel, block_off=off),
        grid_spec=grid_spec,
        out_shape=jax.ShapeDtypeStruct((n_rows, HALF_D), jnp.int32),
        input_output_aliases=aliases,
        compiler_params=_params("arbitrary"),
        name="experts",
    )(*operands)


def _combine_kernel(h_ref, yg_ref, gate_ref, lng_ref, lnb_ref, *rest):
    x_ref, xb_ref = rest[-2:]
    gates = gate_ref[...]
    y = None
    for k in range(TOP_K):
        hi, lo = _unpack_rows(yg_ref[k])
        yk = gates[:, k:k + 1] * jnp.concatenate([hi, lo], axis=1)
        y = yk if y is None else y + yk
    xn = _layer_norm(DEEPNORM_ALPHA * h_ref[...] + y, lng_ref[0, 1:2, :], lnb_ref[0, 1:2, :])
    x_ref[...] = xn
    xb_ref[...] = xn.astype(_BF16)


def _combine(h, yg_part, gates, ln_g, ln_b, layer, part, prev):
    t = h.shape[0]
    tm = LN2_TM
    off = part * (yg_part.shape[1] // tm)
    row = lambda i: (i + off, 0)
    in_specs = [
        pl.BlockSpec((tm, D_MODEL), row),
        pl.BlockSpec((TOP_K, tm, HALF_D), lambda i: (0, i, 0)),
        pl.BlockSpec((tm, LANES), row),
        pl.BlockSpec((1, 2, D_MODEL), lambda i: (layer, 0, 0)),
        pl.BlockSpec((1, 2, D_MODEL), lambda i: (layer, 0, 0)),
    ]
    operands = [h, yg_part, gates, ln_g, ln_b]
    aliases = {}
    if prev is not None:
        in_specs += [pl.BlockSpec(memory_space=pl.ANY), pl.BlockSpec(memory_space=pl.ANY)]
        aliases = {len(operands): 0, len(operands) + 1: 1}
        operands += list(prev)
    return pl.pallas_call(
        _combine_kernel,
        grid=(yg_part.shape[1] // tm,),
        in_specs=in_specs,
        out_specs=[pl.BlockSpec((tm, D_MODEL), row), pl.BlockSpec((tm, D_MODEL), row)],
        out_shape=[jax.ShapeDtypeStruct((t, D_MODEL), _F32), jax.ShapeDtypeStruct((t, D_MODEL), _BF16)],
        input_output_aliases=aliases,
        compiler_params=_params("parallel"),
        name="combine_ln",
    )(*operands)


def _dispatch_tables(route, counts, t):
    tk = t * TOP_K
    experts = route[:TOP_K]
    ranks = route[TOP_K:]
    padded = (counts + MOE_BLOCK - 1) // MOE_BLOCK * MOE_BLOCK
    pad_end = jnp.cumsum(padded)
    pad_start = pad_end - padded
    start = jnp.cumsum(counts) - counts
    expert_ids = jnp.arange(N_EXPERTS, dtype=jnp.int32)
    dest = jnp.sum(jnp.where(experts[..., None] == expert_ids, pad_start, 0), axis=-1) + ranks
    n_blocks = (tk + N_EXPERTS * (MOE_BLOCK - 1) + MOE_BLOCK - 1) // MOE_BLOCK
    block_start = jnp.arange(n_blocks, dtype=jnp.int32) * MOE_BLOCK
    block_expert = jnp.minimum(
        jnp.sum((pad_end[None, :] <= block_start[:, None]).astype(jnp.int32), axis=1), N_EXPERTS - 1)
    n_used = (pad_end[-1:] // MOE_BLOCK).astype(jnp.int32)
    order = jnp.argsort(experts.T.reshape(tk))
    in_block = jnp.arange(MOE_BLOCK, dtype=jnp.int32)[None, :]
    r = (block_start - pad_start[block_expert])[:, None] + in_block
    valid = r < counts[block_expert][:, None]
    src = jnp.where(valid, start[block_expert][:, None] + r, 0).reshape(n_blocks * MOE_BLOCK)
    filler = jnp.arange(n_blocks * MOE_BLOCK, dtype=jnp.int32) % t
    buf_tok = jnp.where(valid.reshape(-1), order[src] // TOP_K, filler)
    return buf_tok, dest, block_expert, n_used


def kernel(x, w_in, w_o, lambda_qk, subln_g, sinks, ln_g, ln_b, w_router, b_router, w_up, b_up, w_down, b_down):
    batch, seq, _ = x.shape
    t = batch * seq
    assert seq % ATT_TQ == 0 and seq % SWA_TB == 0 and t % PROJ_TM == 0

    va_lo, va_hi = 2 * D_MODEL, 3 * D_MODEL
    gates_lo = w_in.shape[-1] - GATE_COLS
    w_att_b = jnp.concatenate([w_in[..., :va_lo], w_in[..., va_hi:gates_lo]], axis=-1).astype(_BF16)
    vb_lo = gates_lo - SW_KV_HEADS * SW_HEAD_DIM
    w_vt_b = jnp.swapaxes(jnp.concatenate([w_in[..., va_lo:va_hi], w_in[..., vb_lo:gates_lo]], axis=-1),
                          1, 2).astype(_BF16)
    w_gates_b = w_in[..., gates_lo:].astype(_BF16)
    w_o_b = w_o.astype(_BF16)
    w_r_b = jnp.swapaxes(w_router, 1, 2).astype(_BF16)
    b_gate = b_up[..., 0::2].reshape(DEPTH, N_EXPERTS, 1, D_FF)
    b_upp = b_up[..., 1::2].reshape(DEPTH, N_EXPERTS, 1, D_FF)
    b_down4 = b_down.reshape(DEPTH, N_EXPERTS, 1, D_MODEL)
    b_router3 = b_router.astype(_F32).reshape(DEPTH, N_EXPERTS, 1)

    att_scale = jnp.concatenate([
        jnp.full((D_MODEL,), DA_HEAD_DIM ** -0.5 * LOG2E, _F32), jnp.ones((D_MODEL,), _F32),
        jnp.full((D_MODEL,), SW_HEAD_DIM ** -0.5 * LOG2E, _F32),
        jnp.ones((2 * SW_KV_HEADS * SW_HEAD_DIM,), _F32)]).reshape(1, ATT_COLS)
    rest = 2.0 ** (-8.0 * jnp.arange(1, DA_HEADS + 1, dtype=_F32) / DA_HEADS) * LOG2E
    pieces = []
    for _ in range(N_SLOPE_PARTS):
        piece = rest.astype(_BF16).astype(_F32)
        pieces.append(piece)
        rest = rest - piece
    da_slopes = jnp.stack(pieces, axis=1)
    swa_bias = _swa_bias()

    xf = x.reshape(t, D_MODEL)
    xb = xf.astype(_BF16)
    for l in range(DEPTH):
        lam_init = 0.8 - 0.6 * math.exp(-0.3 * l)
        prm = jnp.concatenate([lambda_qk[l].astype(_F32), jnp.full((4, DA_HEAD_DIM), lam_init, _F32)], axis=0)
        att = _project(xb, w_att_b, l, ATT_COLS // 2, att_scale, _BF16)
        vt, vtb = _project_vt(xb, w_vt_b, l)
        o_a = _diff_attention(att, vt, da_slopes, prm, subln_g[l].reshape(DA_WIDTH, 1), batch, seq)
        sink_rows = jnp.repeat(sinks[l].astype(_F32).reshape(SW_KV_HEADS, SW_GROUP) * LOG2E, WINDOW, axis=1)
        o_b = _sliding_window_attention(att, vtb, sink_rows.reshape(SW_KV_HEADS, 1, SW_GROUP * WINDOW),
                                        swa_bias, batch, seq)
        h, hp, route, top_gate, counts = _merge_out(xb, w_gates_b, o_a, o_b, xf, w_o_b, ln_g, ln_b, w_r_b,
                                                    b_router3, l)
        buf_tok, pos, block_expert, n_used = _dispatch_tables(route, counts[:, 0], t)
        part_rows = buf_tok.shape[0] // MOE_PARTS
        yb = None
        for part in range(MOE_PARTS):
            x_rows = _gather_rows(hp, buf_tok, 1, part * part_rows, part_rows)
            yb = _experts(block_expert, n_used, x_rows, part, yb, w_up, b_gate, b_upp, w_down, b_down4, l)
        part_tok = t // MOE_PARTS
        pos_flat = pos.reshape(TOP_K * t)
        outs = None
        for part in range(MOE_PARTS):
            yg = _gather_rows(yb, pos_flat, TOP_K, part * part_tok, part_tok).reshape(TOP_K, part_tok, HALF_D)
            outs = _combine(h, yg, top_gate, ln_g, ln_b, l, part, outs)
        xf, xb = outs
    return xf.reshape(batch, seq, D_MODEL)
```

```python
import functools
import math

import jax
import jax.numpy as jnp
from jax import lax
from jax.experimental import pallas as pl
from jax.experimental.pallas import tpu as pltpu
from jax.experimental.pallas import tpu_sc as plsc

D_MODEL = 1024
DEPTH = 4
DA_HEAD_DIM = 128
DA_HEADS = 4
DA_WIDTH = 2 * DA_HEAD_DIM
SW_HEAD_DIM = 64
SW_HEADS = 16
SW_KV_HEADS = 2
SW_GROUP = SW_HEADS // SW_KV_HEADS
WINDOW = 128
N_EXPERTS = 32
TOP_K = 4
D_FF = D_MODEL
SWIGLU_LIMIT = 7.0
SWIGLU_ALPHA = 1.702
LN_EPS = 1e-5
DEEPNORM_ALPHA = (2.0 * DEPTH) ** 0.25
NEG_INF = -1e30

QA_OFF, KA_OFF, QB_OFF, KVB_OFF = 0, D_MODEL, 2 * D_MODEL, 3 * D_MODEL
ATT_COLS = KVB_OFF + 2 * SW_KV_HEADS * SW_HEAD_DIM
GATE_COLS = 2 * D_MODEL
HALF_D = D_MODEL // 2
HI16_MASK = 0xFFFF0000
LOG2E = math.log2(math.e)
N_SLOPE_PARTS = 3
BF16_EXACT_INTS = 256

LANES = 128
SUBLANES_BF16 = 16
MXU_TILE = 256
VMEM_BYTES = 64 * 1024 * 1024
SC_CORES = 2
SC_SUBCORES = 16
SC_SLICE_ALIGN = 8
GATHER_CHUNKS = (64, 32, 16)

VT_ROWS = DA_WIDTH + SUBLANES_BF16
VTB_ROWS = SW_HEAD_DIM + SUBLANES_BF16

PROJ_TM = 1024
ATT_TQ = 1024
ATT_TK = 512
SWA_TB = 512
OUT_TM = 512
MOE_BLOCK = 256
MOE_PARTS = 2
DEINT_TILE = MXU_TILE
LN2_TM = 512
VMEM_LIMIT = VMEM_BYTES * 7 // 8

_F32 = jnp.float32
_BF16 = jnp.bfloat16


def _params(*sem):
    return pltpu.CompilerParams(dimension_semantics=sem, vmem_limit_bytes=VMEM_LIMIT)


def _proj_vt_kernel(x_ref, w_ref, s_ref, wt_ref, o_ref, oa_ref, ob_ref):
    x = x_ref[...]
    half = ATT_COLS // 2
    for j in range(2):
        acc = jnp.dot(x, w_ref[0, :, j * half:(j + 1) * half], preferred_element_type=_F32)
        o_ref[:, j * half:(j + 1) * half] = (acc * s_ref[:, j * half:(j + 1) * half]).astype(o_ref.dtype)
    nt = (((1,), (1,)), ((), ()))
    ones = jnp.ones((SUBLANES_BF16, x.shape[0]), oa_ref.dtype)
    for h in range(DA_HEADS):
        vt = lax.dot_general(wt_ref[0, h * DA_WIDTH:(h + 1) * DA_WIDTH, :], x, nt, preferred_element_type=_F32)
        oa_ref[h * VT_ROWS:h * VT_ROWS + DA_WIDTH, :] = vt.astype(oa_ref.dtype)
        oa_ref[h * VT_ROWS + DA_WIDTH:(h + 1) * VT_ROWS, :] = ones
    vb = lax.dot_general(wt_ref[0, D_MODEL:, :], x, nt, preferred_element_type=_F32)
    for kh in range(SW_KV_HEADS):
        ob_ref[kh * VTB_ROWS:kh * VTB_ROWS + SW_HEAD_DIM, :] = (
            vb[kh * SW_HEAD_DIM:(kh + 1) * SW_HEAD_DIM, :].astype(ob_ref.dtype))
        ob_ref[kh * VTB_ROWS + SW_HEAD_DIM:(kh + 1) * VTB_ROWS, :] = ones


def _project_vt(xb, w_all, scale, wt_all, layer):
    t = xb.shape[0]
    n_b = SW_KV_HEADS * SW_HEAD_DIM
    return pl.pallas_call(
        _proj_vt_kernel,
        grid=(t // PROJ_TM,),
        in_specs=[
            pl.BlockSpec((PROJ_TM, D_MODEL), lambda i: (i, 0)),
            pl.BlockSpec((1, D_MODEL, ATT_COLS), lambda i: (layer, 0, 0)),
            pl.BlockSpec((1, ATT_COLS), lambda i: (0, 0)),
            pl.BlockSpec((1, D_MODEL + n_b, D_MODEL), lambda i: (layer, 0, 0)),
        ],
        out_specs=[pl.BlockSpec((PROJ_TM, ATT_COLS), lambda i: (i, 0)),
                   pl.BlockSpec((DA_HEADS * VT_ROWS, PROJ_TM), lambda i: (0, i)),
                   pl.BlockSpec((SW_KV_HEADS * VTB_ROWS, PROJ_TM), lambda i: (0, i))],
        out_shape=[jax.ShapeDtypeStruct((t, ATT_COLS), _BF16),
                   jax.ShapeDtypeStruct((DA_HEADS * VT_ROWS, t), _BF16),
                   jax.ShapeDtypeStruct((SW_KV_HEADS * VTB_ROWS, t), _BF16)],
        compiler_params=_params("parallel"),
        name="in_proj_vt",
    )(xb, w_all, scale, wt_all)


def _diff_attn_kernel(slope_ref, prm_ref, g_ref, q_ref, k_ref, vt_ref, o_ref,
                      acc_ref, m_ref, qaug_ref, kaug_ref, s_ref, smax_ref, p_ref, alpha_ref, *, tq, tk):
    h = pl.program_id(1)
    i = pl.program_id(2)
    diag = tq // tk
    n_full = i * diag
    lane = lax.broadcasted_iota(jnp.int32, (tq, DA_HEAD_DIM), 1)
    slope_tile = jnp.zeros((tq, DA_HEAD_DIM), _F32)
    slope = 0.0
    for part in range(N_SLOPE_PARTS):
        piece = slope_ref[h, part]
        slope = slope + piece
        slope_tile = jnp.where(lane // 2 == part, piece, slope_tile)
    klane = lax.broadcasted_iota(jnp.int32, (tk, DA_HEAD_DIM), 1)
    kpos = lax.broadcasted_iota(jnp.int32, (tk, DA_HEAD_DIM), 0)
    pos_split = jnp.where(klane % 2 == 0, kpos % BF16_EXACT_INTS, kpos // BF16_EXACT_INTS * BF16_EXACT_INTS)
    pos_tile = jnp.where(klane < 2 * N_SLOPE_PARTS, pos_split, 0)
    for c in range(2):
        kaug_ref[c, :, DA_HEAD_DIM:] = pos_tile.astype(_F32).astype(_BF16)
        qaug_ref[c, :, :DA_HEAD_DIM] = q_ref[:, c * DA_HEAD_DIM:(c + 1) * DA_HEAD_DIM]
        qaug_ref[c, :, DA_HEAD_DIM:] = slope_tile.astype(_BF16)
    m_ref[...] = jnp.full(m_ref.shape, NEG_INF, _F32)
    acc_ref[...] = jnp.zeros(acc_ref.shape, _F32)

    def scores(j):
        start = pl.multiple_of(j * tk, tk)
        for c in range(2):
            kaug_ref[c, :, :DA_HEAD_DIM] = k_ref[pl.ds(start, tk), c * DA_HEAD_DIM:(c + 1) * DA_HEAD_DIM]
            s = lax.dot_general(kaug_ref[c], qaug_ref[c], (((1,), (1,)), ((), ())),
                                preferred_element_type=_F32)
            s_ref[c] = s
            smax_ref[c] = jnp.max(s, axis=0, keepdims=True)

    def weights(j, boundary):
        off = (j * tk - i * tq).astype(_F32) * slope
        for c in range(2):
            s = s_ref[c]
            if boundary is None:
                smax = smax_ref[c]
            else:
                keys = lax.broadcasted_iota(jnp.int32, (tk, tq), 0) + boundary * tk
                queries = lax.broadcasted_iota(jnp.int32, (tk, tq), 1)
                s = jnp.where(keys <= queries, s, NEG_INF)
                smax = jnp.max(s, axis=0, keepdims=True)
            m_old = m_ref[c]
            m_new = jnp.maximum(m_old, smax + off)
            alpha_ref[c] = jnp.exp2(m_old - m_new)
            p_ref[c] = jnp.exp2(s - (m_new - off)).astype(_BF16)
            m_ref[c] = m_new

    def accumulate(j):
        start = pl.multiple_of(j * tk, tk)
        vt = vt_ref[:, pl.ds(start, tk)]
        for c in range(2):
            acc_ref[c] = alpha_ref[c] * acc_ref[c] + jnp.dot(vt, p_ref[c], preferred_element_type=_F32)

    def body(j, carry):
        accumulate(j)
        weights(j + 1, None)
        scores(j + 2)
        return carry

    def enter_boundary():
        weights(n_full, 0)
        if diag > 1:
            scores(n_full + 1)

    scores(0)

    @pl.when(i >= 1)
    def _():
        weights(0, None)
        scores(1)
        lax.fori_loop(0, n_full - 1, body, 0)
        accumulate(n_full - 1)
        enter_boundary()

    @pl.when(i == 0)
    def _():
        enter_boundary()

    for d in range(diag):
        accumulate(n_full + d)
        if d + 1 < diag:
            weights(n_full + d + 1, d + 1)
            if d + 2 < diag:
                scores(n_full + d + 2)

    prm = prm_ref[...]
    lam_init = prm[4:5, 0:1]
    lam = (jnp.exp(jnp.sum(prm[0:1] * prm[1:2], axis=1, keepdims=True))
           - jnp.exp(jnp.sum(prm[2:3] * prm[3:4], axis=1, keepdims=True)) + lam_init)
    o0 = acc_ref[0, :DA_WIDTH, :] / acc_ref[0, DA_WIDTH:DA_WIDTH + 1, :]
    o1 = acc_ref[1, :DA_WIDTH, :] / acc_ref[1, DA_WIDTH:DA_WIDTH + 1, :]
    a = o0 - lam * o1
    ms = jnp.mean(a * a, axis=0, keepdims=True)
    out = a * lax.rsqrt(ms + LN_EPS) * g_ref[...] * (1.0 - lam_init)
    o_ref[...] = out.T.astype(o_ref.dtype)


def _diff_attention(att, vt, slopes, prm, subln_g, batch, seq):
    tq, tk = ATT_TQ, ATT_TK
    assert tq % tk == 0 and seq % tq == 0
    nq = seq // tq
    t = batch * seq
    kernel = functools.partial(_diff_attn_kernel, tq=tq, tk=tk)
    return pl.pallas_call(
        kernel,
        grid=(batch, DA_HEADS, nq),
        in_specs=[
            pl.BlockSpec(memory_space=pltpu.SMEM),
            pl.BlockSpec((8, LANES), lambda b, h, i: (0, 0)),
            pl.BlockSpec((DA_WIDTH, 1), lambda b, h, i: (0, 0)),
            pl.BlockSpec((tq, DA_WIDTH), lambda b, h, i: (b * nq + i, QA_OFF // DA_WIDTH + h)),
            pl.BlockSpec((seq, DA_WIDTH), lambda b, h, i: (b, KA_OFF // DA_WIDTH + h)),
            pl.BlockSpec((VT_ROWS, seq), lambda b, h, i: (h, b)),
        ],
        out_specs=pl.BlockSpec((tq, DA_WIDTH), lambda b, h, i: (b * nq + i, h)),
        out_shape=jax.ShapeDtypeStruct((t, D_MODEL), _BF16),
        scratch_shapes=[
            pltpu.VMEM((2, VT_ROWS, tq), _F32),
            pltpu.VMEM((2, 1, tq), _F32),
            pltpu.VMEM((2, tq, 2 * DA_HEAD_DIM), _BF16),
            pltpu.VMEM((2, tk, 2 * DA_HEAD_DIM), _BF16),
            pltpu.VMEM((2, tk, tq), _F32),
            pltpu.VMEM((2, 1, tq), _F32),
            pltpu.VMEM((2, tk, tq), _BF16),
            pltpu.VMEM((2, 1, tq), _F32),
        ],
        compiler_params=_params("parallel", "parallel", "arbitrary"),
        name="diff_attn",
    )(slopes, prm, subln_g, att, att, vt)


def _swa_kernel(sink_ref, bias_ref, q_ref, kv_ref, kvp_ref, vt_ref, vtp_ref, o_ref):
    i = pl.program_id(1)
    first = (i == 0).astype(jnp.int32)
    nt = (((1,), (1,)), ((), ()))
    kv_all = jnp.concatenate([kvp_ref[...], kv_ref[...]], axis=0)
    vt_all = jnp.concatenate([vtp_ref[...], vt_ref[...]], axis=1)
    def scores(r, kh):
        kwin = kv_all[r * WINDOW:(r + 2) * WINDOW, kh * SW_HEAD_DIM:(kh + 1) * SW_HEAD_DIM]
        qs = jnp.concatenate(
            [q_ref[r * WINDOW:(r + 1) * WINDOW, (kh * SW_GROUP + g) * SW_HEAD_DIM:(kh * SW_GROUP + g + 1) * SW_HEAD_DIM]
             for g in range(SW_GROUP)], axis=0)
        s = lax.dot_general(kwin, qs, nt, preferred_element_type=_F32)
        bias = bias_ref[2 * kh + first] if r == 0 else bias_ref[2 * kh]
        return s + bias

    pairs = [(r, kh) for r in range(SWA_TB // WINDOW) for kh in range(SW_KV_HEADS)]
    s_next = scores(*pairs[0])
    heads = []
    for n, (r, kh) in enumerate(pairs):
        s = s_next
        if n + 1 < len(pairs):
            s_next = scores(*pairs[n + 1])
        vwin = vt_all[kh * VTB_ROWS:(kh + 1) * VTB_ROWS, r * WINDOW:(r + 2) * WINDOW]
        sink = sink_ref[kh]
        m = jnp.maximum(jnp.max(s, axis=0, keepdims=True), sink)
        p = jnp.exp2(s - m)
        ot = jnp.dot(vwin, p.astype(_BF16), preferred_element_type=_F32)
        denom = ot[SW_HEAD_DIM:SW_HEAD_DIM + 1, :] + jnp.exp2(sink - m)
        ot = ot[:SW_HEAD_DIM, :] / denom
        for g in range(SW_GROUP):
            heads.append(ot[:, g * WINDOW:(g + 1) * WINDOW].T)
        if kh == SW_KV_HEADS - 1:
            o_ref[r * WINDOW:(r + 1) * WINDOW, :] = jnp.concatenate(heads, axis=1).astype(o_ref.dtype)
            heads = []


def _swa_bias():
    slopes = 2.0 ** (-8.0 * jnp.arange(1, SW_HEADS + 1, dtype=_F32) / SW_HEADS)
    kj = jnp.arange(2 * WINDOW, dtype=jnp.int32)[:, None]
    qi = jnp.arange(WINDOW, dtype=jnp.int32)[None, :]
    dist = qi + WINDOW - kj
    valid = (dist >= 0) & (dist < WINDOW)
    tables = []
    for kh in range(SW_KV_HEADS):
        for has_prev in (True, False):
            ok = valid if has_prev else valid & (kj >= WINDOW)
            per_head = [jnp.where(ok, -slopes[kh * SW_GROUP + g] * LOG2E * dist.astype(_F32), NEG_INF)
                        for g in range(SW_GROUP)]
            tables.append(jnp.concatenate(per_head, axis=1))
    return jnp.stack(tables)


def _sliding_window_attention(att, vtb, sink_rows, bias, batch, seq):
    tb = SWA_TB
    nb = seq // tb
    ratio = tb // WINDOW
    t = batch * seq
    kv_width = 2 * SW_KV_HEADS * SW_HEAD_DIM
    n_b = SW_KV_HEADS * VTB_ROWS
    q_blk = QB_OFF // D_MODEL
    kv_blk = KVB_OFF // kv_width
    prev = lambda b, i: jnp.maximum((b * nb + i) * ratio - 1, 0)
    return pl.pallas_call(
        _swa_kernel,
        grid=(batch, nb),
        in_specs=[
            pl.BlockSpec((SW_KV_HEADS, 1, SW_GROUP * WINDOW), lambda b, i: (0, 0, 0)),
            pl.BlockSpec((2 * SW_KV_HEADS, 2 * WINDOW, SW_GROUP * WINDOW), lambda b, i: (0, 0, 0)),
            pl.BlockSpec((tb, D_MODEL), lambda b, i: (b * nb + i, q_blk)),
            pl.BlockSpec((tb, kv_width), lambda b, i: (b * nb + i, kv_blk)),
            pl.BlockSpec((WINDOW, kv_width), lambda b, i: (prev(b, i), kv_blk)),
            pl.BlockSpec((n_b, tb), lambda b, i: (0, b * nb + i)),
            pl.BlockSpec((n_b, WINDOW), lambda b, i: (0, prev(b, i))),
        ],
        out_specs=pl.BlockSpec((tb, D_MODEL), lambda b, i: (b * nb + i, 0)),
        out_shape=jax.ShapeDtypeStruct((t, D_MODEL), _BF16),
        compiler_params=_params("parallel", "arbitrary"),
        name="swa",
    )(sink_rows, bias, att, att, att, vtb, vtb)


def _pack_rows(x):
    bits = lax.bitcast_convert_type(x.astype(_BF16).astype(_F32), jnp.uint32)
    return lax.bitcast_convert_type(bits[:, :HALF_D] | (bits[:, HALF_D:] >> 16), jnp.int32)


def _unpack_rows(words):
    bits = lax.bitcast_convert_type(words, jnp.uint32)
    hi = lax.bitcast_convert_type(bits & jnp.uint32(HI16_MASK), _F32)
    lo = lax.bitcast_convert_type(bits << 16, _F32)
    return hi, lo


def _gather_rows(table, idx, n_groups, first, count):
    group_size = idx.shape[0] // n_groups
    n_out = n_groups * count
    width = table.shape[1]
    workers = SC_CORES * SC_SUBCORES
    per_worker = n_out // workers
    chunk_rows = next(c for c in GATHER_CHUNKS if per_worker % (2 * c) == 0)
    assert n_out % workers == 0 and count % per_worker == 0 and first % SC_SLICE_ALIGN == 0
    n_chunks = per_worker // chunk_rows
    mesh = plsc.VectorSubcoreMesh(core_axis_name="c", subcore_axis_name="s",
                                  num_cores=SC_CORES, num_subcores=SC_SUBCORES)

    @functools.partial(
        pl.kernel, mesh=mesh,
        out_type=jax.ShapeDtypeStruct((n_out, width), table.dtype),
        scratch_types=[
            pltpu.VMEM((per_worker,), jnp.int32),
            pltpu.VMEM((2, chunk_rows, width), table.dtype),
            pltpu.SemaphoreType.DMA((2,)),
            pltpu.SemaphoreType.DMA((2,)),
        ],
        name="gather_rows",
    )
    def gather(table_hbm, idx_hbm, out_hbm, idx_v, rows_v, fetch_sem, store_sem):
        base = (lax.axis_index("s") * SC_CORES + lax.axis_index("c")) * per_worker
        src = pl.multiple_of((base // count) * group_size + first + base % count, SC_SLICE_ALIGN)
        pltpu.sync_copy(idx_hbm.at[pl.ds(src, per_worker)], idx_v)

        def fetch(g, buf):
            chunk = idx_v.at[pl.ds(g * chunk_rows, chunk_rows)]
            return pltpu.make_async_copy(table_hbm.at[chunk], rows_v.at[buf], fetch_sem.at[buf])

        def store(g, buf):
            dst = out_hbm.at[pl.ds(base + g * chunk_rows, chunk_rows)]
            return pltpu.make_async_copy(rows_v.at[buf], dst, store_sem.at[buf])

        fetch(0, 0).start()

        @pl.loop(0, n_chunks, step=2)
        def _(g0):
            for buf in range(2):
                g = g0 + buf
                fetch(g, buf).wait()

                @pl.when(g >= 1)
                def _():
                    store(g - 1, 1 - buf).wait()

                @pl.when(g + 1 < n_chunks)
                def _():
                    fetch(g + 1, 1 - buf).start()

                store(g, buf).start()

        store(n_chunks - 1, 1).wait()

    return gather(table, idx)


def _layer_norm(z, g, b):
    mu = jnp.mean(z, axis=1, keepdims=True)
    d = z - mu
    var = jnp.mean(d * d, axis=1, keepdims=True)
    return d * lax.rsqrt(var + LN_EPS) * g + b


def _merge_out_kernel(xb_ref, wg_ref, oa_ref, ob_ref, x_ref, wo_ref, lng_ref, lnb_ref, wr_ref, br_ref,
                      h_ref, hp_ref, idx_ref, gate_ref, cnt_ref, run_ref):
    @pl.when(pl.program_id(0) == 0)
    def _():
        run_ref[...] = jnp.zeros(run_ref.shape, _F32)

    branch_gates = jnp.dot(xb_ref[...], wg_ref[0], preferred_element_type=_F32)
    merged = (jax.nn.sigmoid(branch_gates[:, :D_MODEL]) * oa_ref[...].astype(_F32)
              + jax.nn.sigmoid(branch_gates[:, D_MODEL:]) * ob_ref[...].astype(_F32))
    y = jnp.dot(merged.astype(_BF16), wo_ref[0], preferred_element_type=_F32)
    hn = _layer_norm(DEEPNORM_ALPHA * x_ref[...] + y, lng_ref[0, 0:1, :], lnb_ref[0, 0:1, :])
    h_ref[...] = hn
    hb = hn.astype(_BF16)
    hp_ref[...] = _pack_rows(hn)
    logits = lax.dot_general(wr_ref[0], hb, (((1,), (1,)), ((), ())),
                             preferred_element_type=_F32) + br_ref[0]
    tm = logits.shape[1]
    erow = lax.broadcasted_iota(jnp.int32, (N_EXPERTS, tm), 0)
    cur = logits
    vals, picks, id_rows = [], [], []
    for k in range(TOP_K):
        mx = jnp.max(cur, axis=0, keepdims=True)
        ix = jnp.min(jnp.where(cur == mx, erow, N_EXPERTS), axis=0, keepdims=True)
        vals.append(mx)
        picks.append(erow == ix)
        id_rows.append(ix)
        cur = jnp.where(picks[k], -jnp.inf, cur)
    exps = [jnp.exp(v - vals[0]) for v in vals]
    tot = exps[0] + exps[1] + exps[2] + exps[3]
    gate_rows = [e / tot for e in exps] + [jnp.zeros((LANES - TOP_K, tm), _F32)]
    gate_ref[...] = jnp.concatenate(gate_rows, axis=0).T

    chosen = jnp.zeros((N_EXPERTS, tm), _F32)
    for k in range(TOP_K):
        chosen = chosen + jnp.where(picks[k], 1.0, 0.0)
    r = lax.broadcasted_iota(jnp.int32, (tm, tm), 0)
    c = lax.broadcasted_iota(jnp.int32, (tm, tm), 1)
    earlier = jnp.where(r < c, 1.0, 0.0).astype(_BF16)
    before = jnp.dot(chosen.astype(_BF16), earlier, preferred_element_type=_F32) + run_ref[...]
    rank_rows = [jnp.sum(jnp.where(picks[k], before, 0.0), axis=0, keepdims=True).astype(jnp.int32)
                 for k in range(TOP_K)]
    idx_ref[...] = jnp.concatenate(id_rows + rank_rows, axis=0)
    total = run_ref[...] + jnp.sum(chosen, axis=1, keepdims=True)
    run_ref[...] = total
    cnt_ref[...] = jnp.broadcast_to(total, cnt_ref.shape).astype(jnp.int32)


def _merge_out(xb, wg_all, o_a, o_b, x, wo_all, ln_g, ln_b, wr_all, br_all, layer):
    t = x.shape[0]
    tm = OUT_TM
    row = lambda i: (i, 0)
    return pl.pallas_call(
        _merge_out_kernel,
        grid=(t // tm,),
        in_specs=[
            pl.BlockSpec((tm, D_MODEL), row),
            pl.BlockSpec((1, D_MODEL, GATE_COLS), lambda i: (layer, 0, 0)),
            pl.BlockSpec((tm, D_MODEL), row),
            pl.BlockSpec((tm, D_MODEL), row),
            pl.BlockSpec((tm, D_MODEL), row),
            pl.BlockSpec((1, D_MODEL, D_MODEL), lambda i: (layer, 0, 0)),
            pl.BlockSpec((1, 2, D_MODEL), lambda i: (layer, 0, 0)),
            pl.BlockSpec((1, 2, D_MODEL), lambda i: (layer, 0, 0)),
            pl.BlockSpec((1, N_EXPERTS, D_MODEL), lambda i: (layer, 0, 0)),
            pl.BlockSpec((1, N_EXPERTS, 1), lambda i: (layer, 0, 0)),
        ],
        out_specs=[
            pl.BlockSpec((tm, D_MODEL), row),
            pl.BlockSpec((tm, HALF_D), row),
            pl.BlockSpec((2 * TOP_K, tm), lambda i: (0, i)),
            pl.BlockSpec((tm, LANES), row),
            pl.BlockSpec((N_EXPERTS, LANES), lambda i: (0, 0)),
        ],
        out_shape=[
            jax.ShapeDtypeStruct((t, D_MODEL), _F32),
            jax.ShapeDtypeStruct((t, HALF_D), jnp.int32),
            jax.ShapeDtypeStruct((2 * TOP_K, t), jnp.int32),
            jax.ShapeDtypeStruct((t, LANES), _F32),
            jax.ShapeDtypeStruct((N_EXPERTS, LANES), jnp.int32),
        ],
        scratch_shapes=[pltpu.VMEM((N_EXPERTS, 1), _F32)],
        compiler_params=_params("arbitrary"),
        name="merge_out_ln_router",
    )(xb, wg_all, o_a, o_b, x, wo_all, ln_g, ln_b, wr_all, br_all)


def _expert_kernel(be_ref, nused_ref, x_ref, wup_ref, bg_ref, bu_ref, wd_ref, bd_ref, *rest, block_off):
    y_ref, wg_scr, wu_scr, wd_scr = rest[-4:]
    i = pl.program_id(0)
    blk = i + block_off
    changed = jnp.logical_or(i == 0, be_ref[blk] != be_ref[jnp.maximum(blk - 1, 0)])
    active = blk < nused_ref[0]

    @pl.when(jnp.logical_and(changed, active))
    def _():
        half = DEINT_TILE // 2
        r = lax.broadcasted_iota(jnp.int32, (DEINT_TILE, DEINT_TILE), 0)
        c = lax.broadcasted_iota(jnp.int32, (DEINT_TILE, DEINT_TILE), 1)
        sel = jnp.where(r == jnp.where(c < half, 2 * c, 2 * (c - half) + 1), 1.0, 0.0).astype(_BF16)
        for t in range(2 * D_FF // DEINT_TILE):
            w = wup_ref[0, 0, :, t * DEINT_TILE:(t + 1) * DEINT_TILE].astype(_BF16)
            de = jnp.dot(w, sel, preferred_element_type=_F32).astype(_BF16)
            wg_scr[:, t * half:(t + 1) * half] = de[:, :half]
            wu_scr[:, t * half:(t + 1) * half] = de[:, half:]
        wd_scr[...] = wd_ref[0, 0].astype(_BF16)

    @pl.when(active)
    def _():
        hi, lo = _unpack_rows(x_ref[...])
        x = jnp.concatenate([hi.astype(_BF16), lo.astype(_BF16)], axis=1)
        gate = jnp.dot(x, wg_scr[...], preferred_element_type=_F32) + bg_ref[0, 0]
        up = jnp.dot(x, wu_scr[...], preferred_element_type=_F32) + bu_ref[0, 0]
        gate = jnp.minimum(gate, SWIGLU_LIMIT)
        up = jnp.clip(up, -SWIGLU_LIMIT, SWIGLU_LIMIT)
        act = (up + 1.0) * (gate * jax.nn.sigmoid(SWIGLU_ALPHA * gate))
        y = jnp.dot(act.astype(_BF16), wd_scr[...], preferred_element_type=_F32) + bd_ref[0, 0]
        y_ref[...] = _pack_rows(y)

    @pl.when(jnp.logical_not(active))
    def _():
        y_ref[...] = jnp.zeros(y_ref.shape, y_ref.dtype)


def _experts(block_expert, n_used, x_part, part, y_prev, w_up, bg_all, bu_all, w_down, bd_all, layer):
    part_blocks = x_part.shape[0] // MOE_BLOCK
    n_rows = block_expert.shape[0] * MOE_BLOCK
    off = part * part_blocks
    pick = lambda i, be, nu: (layer, be[i + off], 0, 0)
    bspec = pl.BlockSpec((1, 1, 1, D_FF), pick)
    in_specs = [
        pl.BlockSpec((MOE_BLOCK, HALF_D), lambda i, be, nu: (i, 0)),
        pl.BlockSpec((1, 1, D_MODEL, 2 * D_FF), pick),
        bspec, bspec,
        pl.BlockSpec((1, 1, D_FF, D_MODEL), pick),
        pl.BlockSpec((1, 1, 1, D_MODEL), pick),
    ]
    operands = [block_expert, n_used, x_part, w_up, bg_all, bu_all, w_down, bd_all]
    aliases = {}
    if y_prev is not None:
        in_specs.append(pl.BlockSpec(memory_space=pl.ANY))
        aliases = {len(operands): 0}
        operands.append(y_prev)
    grid_spec = pltpu.PrefetchScalarGridSpec(
        num_scalar_prefetch=2,
        grid=(part_blocks,),
        in_specs=in_specs,
        out_specs=pl.BlockSpec((MOE_BLOCK, HALF_D), lambda i, be, nu: (i + off, 0)),
        scratch_shapes=[
            pltpu.VMEM((D_MODEL, D_FF), _BF16),
            pltpu.VMEM((D_MODEL, D_FF), _BF16),
            pltpu.VMEM((D_FF, D_MODEL), _BF16),
        ],
    )
    return pl.pallas_call(
        functools.partial(_expert_kernel, block_off=off),
        grid_spec=grid_spec,
        out_shape=jax.ShapeDtypeStruct((n_rows, HALF_D), jnp.int32),
        input_output_aliases=aliases,
        compiler_params=_params("arbitrary"),
        name="experts",
    )(*operands)


def _combine_kernel(h_ref, yg_ref, gate_ref, lng_ref, lnb_ref, *rest):
    x_ref, xb_ref = rest[-2:]
    gates = gate_ref[...]
    y = None
    for k in range(TOP_K):
        hi, lo = _unpack_rows(yg_ref[k])
        yk = gates[:, k:k + 1] * jnp.concatenate([hi, lo], axis=1)
        y = yk if y is None else y + yk
    xn = _layer_norm(DEEPNORM_ALPHA * h_ref[...] + y, lng_ref[0, 1:2, :], lnb_ref[0, 1:2, :])
    x_ref[...] = xn
    xb_ref[...] = xn.astype(_BF16)


def _combine(h, yg_part, gates, ln_g, ln_b, layer, part, prev):
    t = h.shape[0]
    tm = LN2_TM
    off = part * (yg_part.shape[1] // tm)
    row = lambda i: (i + off, 0)
    in_specs = [
        pl.BlockSpec((tm, D_MODEL), row),
        pl.BlockSpec((TOP_K, tm, HALF_D), lambda i: (0, i, 0)),
        pl.BlockSpec((tm, LANES), row),
        pl.BlockSpec((1, 2, D_MODEL), lambda i: (layer, 0, 0)),
        pl.BlockSpec((1, 2, D_MODEL), lambda i: (layer, 0, 0)),
    ]
    operands = [h, yg_part, gates, ln_g, ln_b]
    aliases = {}
    if prev is not None:
        in_specs += [pl.BlockSpec(memory_space=pl.ANY), pl.BlockSpec(memory_space=pl.ANY)]
        aliases = {len(operands): 0, len(operands) + 1: 1}
        operands += list(prev)
    return pl.pallas_call(
        _combine_kernel,
        grid=(yg_part.shape[1] // tm,),
        in_specs=in_specs,
        out_specs=[pl.BlockSpec((tm, D_MODEL), row), pl.BlockSpec((tm, D_MODEL), row)],
        out_shape=[jax.ShapeDtypeStruct((t, D_MODEL), _F32), jax.ShapeDtypeStruct((t, D_MODEL), _BF16)],
        input_output_aliases=aliases,
        compiler_params=_params("parallel"),
        name="combine_ln",
    )(*operands)


def _dispatch_tables(route, counts, t):
    tk = t * TOP_K
    experts = route[:TOP_K]
    ranks = route[TOP_K:]
    padded = (counts + MOE_BLOCK - 1) // MOE_BLOCK * MOE_BLOCK
    pad_end = jnp.cumsum(padded)
    pad_start = pad_end - padded
    start = jnp.cumsum(counts) - counts
    expert_ids = jnp.arange(N_EXPERTS, dtype=jnp.int32)
    dest = jnp.sum(jnp.where(experts[..., None] == expert_ids, pad_start, 0), axis=-1) + ranks
    n_blocks = (tk + N_EXPERTS * (MOE_BLOCK - 1) + MOE_BLOCK - 1) // MOE_BLOCK
    block_start = jnp.arange(n_blocks, dtype=jnp.int32) * MOE_BLOCK
    block_expert = jnp.minimum(
        jnp.sum((pad_end[None, :] <= block_start[:, None]).astype(jnp.int32), axis=1), N_EXPERTS - 1)
    n_used = (pad_end[-1:] // MOE_BLOCK).astype(jnp.int32)
    order = jnp.argsort(experts.T.reshape(tk))
    in_block = jnp.arange(MOE_BLOCK, dtype=jnp.int32)[None, :]
    r = (block_start - pad_start[block_expert])[:, None] + in_block
    valid = r < counts[block_expert][:, None]
    src = jnp.where(valid, start[block_expert][:, None] + r, 0).reshape(n_blocks * MOE_BLOCK)
    filler = jnp.arange(n_blocks * MOE_BLOCK, dtype=jnp.int32) % t
    buf_tok = jnp.where(valid.reshape(-1), order[src] // TOP_K, filler)
    return buf_tok, dest, block_expert, n_used


def kernel(x, w_in, w_o, lambda_qk, subln_g, sinks, ln_g, ln_b, w_router, b_router, w_up, b_up, w_down, b_down):
    batch, seq, _ = x.shape
    t = batch * seq
    assert seq % ATT_TQ == 0 and seq % SWA_TB == 0 and t % PROJ_TM == 0

    va_lo, va_hi = 2 * D_MODEL, 3 * D_MODEL
    gates_lo = w_in.shape[-1] - GATE_COLS
    w_att_b = jnp.concatenate([w_in[..., :va_lo], w_in[..., va_hi:gates_lo]], axis=-1).astype(_BF16)
    vb_lo = gates_lo - SW_KV_HEADS * SW_HEAD_DIM
    w_vt_b = jnp.swapaxes(jnp.concatenate([w_in[..., va_lo:va_hi], w_in[..., vb_lo:gates_lo]], axis=-1),
                          1, 2).astype(_BF16)
    w_gates_b = w_in[..., gates_lo:].astype(_BF16)
    w_o_b = w_o.astype(_BF16)
    w_r_b = jnp.swapaxes(w_router, 1, 2).astype(_BF16)
    b_gate = b_up[..., 0::2].reshape(DEPTH, N_EXPERTS, 1, D_FF)
    b_upp = b_up[..., 1::2].reshape(DEPTH, N_EXPERTS, 1, D_FF)
    b_down4 = b_down.reshape(DEPTH, N_EXPERTS, 1, D_MODEL)
    b_router3 = b_router.astype(_F32).reshape(DEPTH, N_EXPERTS, 1)

    att_scale = jnp.concatenate([
        jnp.full((D_MODEL,), DA_HEAD_DIM ** -0.5 * LOG2E, _F32), jnp.ones((D_MODEL,), _F32),
        jnp.full((D_MODEL,), SW_HEAD_DIM ** -0.5 * LOG2E, _F32),
        jnp.ones((2 * SW_KV_HEADS * SW_HEAD_DIM,), _F32)]).reshape(1, ATT_COLS)
    rest = 2.0 ** (-8.0 * jnp.arange(1, DA_HEADS + 1, dtype=_F32) / DA_HEADS) * LOG2E
    pieces = []
    for _ in range(N_SLOPE_PARTS):
        piece = rest.astype(_BF16).astype(_F32)
        pieces.append(piece)
        rest = rest - piece
    da_slopes = jnp.stack(pieces, axis=1)
    swa_bias = _swa_bias()

    xf = x.reshape(t, D_MODEL)
    xb = xf.astype(_BF16)
    for l in range(DEPTH):
        lam_init = 0.8 - 0.6 * math.exp(-0.3 * l)
        prm = jnp.concatenate([lambda_qk[l].astype(_F32), jnp.full((4, DA_HEAD_DIM), lam_init, _F32)], axis=0)
        att, vt, vtb = _project_vt(xb, w_att_b, att_scale, w_vt_b, l)
        o_a = _diff_attention(att, vt, da_slopes, prm, subln_g[l].reshape(DA_WIDTH, 1), batch, seq)
        sink_rows = jnp.repeat(sinks[l].astype(_F32).reshape(SW_KV_HEADS, SW_GROUP) * LOG2E, WINDOW, axis=1)
        o_b = _sliding_window_attention(att, vtb, sink_rows.reshape(SW_KV_HEADS, 1, SW_GROUP * WINDOW),
                                        swa_bias, batch, seq)
        h, hp, route, top_gate, counts = _merge_out(xb, w_gates_b, o_a, o_b, xf, w_o_b, ln_g, ln_b, w_r_b,
                                                    b_router3, l)
        buf_tok, pos, block_expert, n_used = _dispatch_tables(route, counts[:, 0], t)
        part_rows = buf_tok.shape[0] // MOE_PARTS
        yb = None
        for part in range(MOE_PARTS):
            x_rows = _gather_rows(hp, buf_tok, 1, part * part_rows, part_rows)
            yb = _experts(block_expert, n_used, x_rows, part, yb, w_up, b_gate, b_upp, w_down, b_down4, l)
        part_tok = t // MOE_PARTS
        pos_flat = pos.reshape(TOP_K * t)
        outs = None
        for part in range(MOE_PARTS):
            yg = _gather_rows(yb, pos_flat, TOP_K, part * part_tok, part_tok).reshape(TOP_K, part_tok, HALF_D)
            outs = _combine(h, yg, top_gate, ln_g, ln_b, l, part, outs)
        xf, xb = outs
    return xf.reshape(batch, seq, D_MODEL)
```
